```python
import jax, jax.numpy as jnp
from jax import lax
import numpy as np

D_MODEL = 1024
BATCH = 8
SEQ = 4096
DEPTH = 2

HEAD_DIM = 64
N_SB_HEADS = 12
SB_WIDTH = N_SB_HEADS * HEAD_DIM
N_MEM_HEADS = 4
MEM_WIDTH = N_MEM_HEADS * HEAD_DIM
MEM_TOKENS = 256
MIX_WIDTH = SB_WIDTH + MEM_WIDTH
POOL_WINDOWS = (2, 4, 8, 16)
N_POOL_GROUPS = len(POOL_WINDOWS)
POOL_WIDTH = SB_WIDTH
POOL_GROUP = POOL_WIDTH // N_POOL_GROUPS
D_FF = -(-8 * D_MODEL // (3 * 256)) * 256
SB_BLOCK = 128
N_A_LAYERS = DEPTH // 2
N_B_LAYERS = DEPTH - N_A_LAYERS
EPS = 1e-6

kernel_name = "yoco_pool_stickbreak_hybrid"


def rmsnorm(x, g):
    xf = x.astype(jnp.float32)
    y = xf * lax.rsqrt(jnp.mean(xf * xf, axis=-1, keepdims=True) + EPS)
    return (y * g.astype(jnp.float32)).astype(x.dtype)


def swiglu(h, w_gu, w_down):
    gate, up = jnp.split(h @ w_gu, 2, axis=-1)
    return (jax.nn.silu(gate) * up) @ w_down


def multiscale_pool(u):
    s = u.shape[1]
    uf = u.astype(jnp.float32)
    cs = jnp.cumsum(uf, axis=1)
    pos = jnp.arange(s)
    outs = []
    for g, w in enumerate(POOL_WINDOWS):
        c = cs[..., g * POOL_GROUP:(g + 1) * POOL_GROUP]
        prev = jnp.pad(c, ((0, 0), (w, 0), (0, 0)))[:, :s]
        cnt = jnp.minimum(pos + 1, w).astype(jnp.float32)[None, :, None]
        outs.append((c - prev) / cnt)
    pooled = jnp.concatenate(outs, axis=-1)
    return (pooled - uf).astype(u.dtype)


def memory_kv(mem, mem_norm, w_mem_kv):
    b, m, _ = mem.shape
    k, v = jnp.split(rmsnorm(mem, mem_norm) @ w_mem_kv, 2, axis=-1)
    return (k.reshape(b, m, N_MEM_HEADS, HEAD_DIM), v.reshape(b, m, N_MEM_HEADS, HEAD_DIM))


def memory_attention(q, mk, mv):
    b, s, _ = q.shape
    qh = q.reshape(b, s, N_MEM_HEADS, HEAD_DIM)
    logits = jnp.einsum("bshd,bmhd->bhsm", qh, mk).astype(jnp.float32) * (HEAD_DIM ** -0.5)
    p = jax.nn.softmax(logits, axis=-1).astype(mv.dtype)
    return jnp.einsum("bhsm,bmhd->bshd", p, mv).reshape(b, s, MEM_WIDTH)


def stick_breaking_attention(q, k, v):
    s = q.shape[2]
    scale = HEAD_DIM ** -0.5
    outs = []
    for i in range(s // SB_BLOCK):
        q0 = i * SB_BLOCK
        end = q0 + SB_BLOCK
        qb = q[:, :, q0:end]
        kb = k[:, :, :end]
        vb = v[:, :, :end]
        z = jnp.einsum("bhqd,bhkd->bhqk", qb, kb).astype(jnp.float32) * scale
        tpos = q0 + jnp.arange(SB_BLOCK)
        spos = jnp.arange(end)
        mask = spos[None, :] < tpos[:, None]
        log_not = jnp.where(mask, jax.nn.log_sigmoid(-z), 0.0)
        later = lax.cumsum(log_not, axis=3, reverse=True) - log_not
        wts = jnp.where(mask, jnp.exp(jax.nn.log_sigmoid(z) + later), 0.0)
        outs.append(jnp.einsum("bhqk,bhkd->bhqd", wts.astype(vb.dtype), vb))
    return jnp.concatenate(outs, axis=2)


def pool_layer(x, mem, mem_norm, norm_mix, w_in, w_group, scale, w_mem_kv, w_out, norm_ffn, w_gu, w_down):
    b, s, _ = x.shape
    proj = rmsnorm(x, norm_mix) @ w_in
    u_pool, q_mem = proj[..., :POOL_WIDTH], proj[..., POOL_WIDTH:]
    pooled = multiscale_pool(u_pool).reshape(b, s, N_POOL_GROUPS, POOL_GROUP)
    grouped = jnp.einsum("bsgc,gcd->bsgd", pooled, w_group).reshape(b, s, POOL_WIDTH) * scale
    mk, mv = memory_kv(mem, mem_norm, w_mem_kv)
    mem_out = memory_attention(q_mem, mk, mv)
    x = x + jnp.concatenate([grouped, mem_out], axis=-1) @ w_out
    return x + swiglu(rmsnorm(x, norm_ffn), w_gu, w_down)


def sb_layer(x, mem, k_sh, v_sh, mem_norm, norm_mix, w_q, w_mem_kv, w_out, norm_ffn, w_gu, w_down):
    b, s, _ = x.shape
    proj = rmsnorm(x, norm_mix) @ w_q
    q_sb = proj[..., :SB_WIDTH].reshape(b, s, N_SB_HEADS, HEAD_DIM).transpose(0, 2, 1, 3)
    q_mem = proj[..., SB_WIDTH:]
    sb_out = stick_breaking_attention(q_sb, k_sh, v_sh).transpose(0, 2, 1, 3).reshape(b, s, SB_WIDTH)
    mk, mv = memory_kv(mem, mem_norm, w_mem_kv)
    mem_out = memory_attention(q_mem, mk, mv)
    x = x + jnp.concatenate([sb_out, mem_out], axis=-1) @ w_out
    return x + swiglu(rmsnorm(x, norm_ffn), w_gu, w_down)


def _fwd_setup_inputs(seed: int = 0) -> dict:
    key = jax.random.key(seed)
    ks = jax.random.split(key, 24)
    f32 = jnp.float32
    out_gain = (2.0 * DEPTH) ** -0.5

    def w(k, shape, fan_in, gain=1.0):
        return jax.random.normal(k, shape, f32) * (gain * fan_in ** -0.5)

    def g(k, shape):
        return 1.0 + 0.05 * jax.random.normal(k, shape, f32)

    na, nb = N_A_LAYERS, N_B_LAYERS
    return {
        "x": jax.random.normal(ks[0], (BATCH, SEQ, D_MODEL), f32),
        "mem": jax.random.normal(ks[1], (BATCH, MEM_TOKENS, D_MODEL), f32),
        "mem_norm": g(ks[2], (D_MODEL,)),
        "a_norm_mix": g(ks[3], (na, D_MODEL)),
        "a_w_in": w(ks[4], (na, D_MODEL, MIX_WIDTH), D_MODEL),
        "a_w_group": w(ks[5], (na, N_POOL_GROUPS, POOL_GROUP, POOL_GROUP), POOL_GROUP),
        "a_scale": g(ks[6], (na, POOL_WIDTH)),
        "a_w_mem_kv": w(ks[7], (na, D_MODEL, 2 * MEM_WIDTH), D_MODEL),
        "a_w_out": w(ks[8], (na, MIX_WIDTH, D_MODEL), MIX_WIDTH, out_gain),
        "a_norm_ffn": g(ks[9], (na, D_MODEL)),
        "a_w_gu": w(ks[10], (na, D_MODEL, 2 * D_FF), D_MODEL),
        "a_w_down": w(ks[11], (na, D_FF, D_MODEL), D_FF, out_gain),
        "kv_norm": g(ks[12], (D_MODEL,)),
        "w_kv": w(ks[13], (D_MODEL, 2 * SB_WIDTH), D_MODEL),
        "b_norm_mix": g(ks[14], (nb, D_MODEL)),
        "b_w_q": w(ks[15], (nb, D_MODEL, MIX_WIDTH), D_MODEL),
        "b_w_mem_kv": w(ks[16], (nb, D_MODEL, 2 * MEM_WIDTH), D_MODEL),
        "b_w_out": w(ks[17], (nb, MIX_WIDTH, D_MODEL), MIX_WIDTH, out_gain),
        "b_norm_ffn": g(ks[18], (nb, D_MODEL)),
        "b_w_gu": w(ks[19], (nb, D_MODEL, 2 * D_FF), D_MODEL),
        "b_w_down": w(ks[20], (nb, D_FF, D_MODEL), D_FF, out_gain),
        "final_norm": g(ks[21], (D_MODEL,)),
    }


def _fwd_reference(x, mem, mem_norm, a_norm_mix, a_w_in, a_w_group, a_scale, a_w_mem_kv, a_w_out,
              a_norm_ffn, a_w_gu, a_w_down, kv_norm, w_kv, b_norm_mix, b_w_q, b_w_mem_kv,
              b_w_out, b_norm_ffn, b_w_gu, b_w_down, final_norm):
    b, s, _ = x.shape
    k_sh = v_sh = None
    for layer in range(DEPTH):
        if layer < N_A_LAYERS:
            i = layer
            x = pool_layer(x, mem, mem_norm, a_norm_mix[i], a_w_in[i], a_w_group[i], a_scale[i],
                           a_w_mem_kv[i], a_w_out[i], a_norm_ffn[i], a_w_gu[i], a_w_down[i])
        else:
            if layer == N_A_LAYERS:
                kv = rmsnorm(x, kv_norm) @ w_kv
                k_sh = kv[..., :SB_WIDTH].reshape(b, s, N_SB_HEADS, HEAD_DIM).transpose(0, 2, 1, 3)
                v_sh = kv[..., SB_WIDTH:].reshape(b, s, N_SB_HEADS, HEAD_DIM).transpose(0, 2, 1, 3)
            j = layer - N_A_LAYERS
            x = sb_layer(x, mem, k_sh, v_sh, mem_norm, b_norm_mix[j], b_w_q[j], b_w_mem_kv[j],
                         b_w_out[j], b_norm_ffn[j], b_w_gu[j], b_w_down[j])
    return rmsnorm(x, final_norm)


import jax as _jax
import jax.numpy as _jnp

TWIN_FORMAT = 'train_step'
FWD_PARAMS = ['x', 'mem', 'mem_norm', 'a_norm_mix', 'a_w_in', 'a_w_group', 'a_scale', 'a_w_mem_kv', 'a_w_out', 'a_norm_ffn', 'a_w_gu', 'a_w_down', 'kv_norm', 'w_kv', 'b_norm_mix', 'b_w_q', 'b_w_mem_kv', 'b_w_out', 'b_norm_ffn', 'b_w_gu', 'b_w_down', 'final_norm']
TWIN_WEIGHTS = ['mem_norm', 'a_norm_mix', 'a_w_in', 'a_w_group', 'a_scale', 'a_w_mem_kv', 'a_w_out', 'a_norm_ffn', 'a_w_gu', 'a_w_down', 'kv_norm', 'w_kv', 'b_norm_mix', 'b_w_q', 'b_w_mem_kv', 'b_w_out', 'b_norm_ffn', 'b_w_gu', 'b_w_down', 'final_norm']
TWIN_DIFF_INPUT = 'x'
TWIN_INPUTS = ['x', 'mem', 'mem_norm', 'a_norm_mix', 'a_w_in', 'a_w_group', 'a_scale', 'a_w_mem_kv', 'a_w_out', 'a_norm_ffn', 'a_w_gu', 'a_w_down', 'kv_norm', 'w_kv', 'b_norm_mix', 'b_w_q', 'b_w_mem_kv', 'b_w_out', 'b_norm_ffn', 'b_w_gu', 'b_w_down', 'final_norm', 'loss_target', 'm_mem_norm', 'm_a_norm_mix', 'm_a_w_in', 'm_a_w_group', 'm_a_scale', 'm_a_w_mem_kv', 'm_a_w_out', 'm_a_norm_ffn', 'm_a_w_gu', 'm_a_w_down', 'm_kv_norm', 'm_w_kv', 'm_b_norm_mix', 'm_b_w_q', 'm_b_w_mem_kv', 'm_b_w_out', 'm_b_norm_ffn', 'm_b_w_gu', 'm_b_w_down', 'm_final_norm', 'v_mem_norm', 'v_a_norm_mix', 'v_a_w_in', 'v_a_w_group', 'v_a_scale', 'v_a_w_mem_kv', 'v_a_w_out', 'v_a_norm_ffn', 'v_a_w_gu', 'v_a_w_down', 'v_kv_norm', 'v_w_kv', 'v_b_norm_mix', 'v_b_w_q', 'v_b_w_mem_kv', 'v_b_w_out', 'v_b_norm_ffn', 'v_b_w_gu', 'v_b_w_down', 'v_final_norm']
TWIN_OUTPUTS = ['loss', 'grad_x', 'grad_mem_norm', 'grad_a_norm_mix', 'grad_a_w_in', 'grad_a_w_group', 'grad_a_scale', 'grad_a_w_mem_kv', 'grad_a_w_out', 'grad_a_norm_ffn', 'grad_a_w_gu', 'grad_a_w_down', 'grad_kv_norm', 'grad_w_kv', 'grad_b_norm_mix', 'grad_b_w_q', 'grad_b_w_mem_kv', 'grad_b_w_out', 'grad_b_norm_ffn', 'grad_b_w_gu', 'grad_b_w_down', 'grad_final_norm', 'delta_mem_norm', 'delta_a_norm_mix', 'delta_a_w_in', 'delta_a_w_group', 'delta_a_scale', 'delta_a_w_mem_kv', 'delta_a_w_out', 'delta_a_norm_ffn', 'delta_a_w_gu', 'delta_a_w_down', 'delta_kv_norm', 'delta_w_kv', 'delta_b_norm_mix', 'delta_b_w_q', 'delta_b_w_mem_kv', 'delta_b_w_out', 'delta_b_norm_ffn', 'delta_b_w_gu', 'delta_b_w_down', 'delta_final_norm', 'new_m_mem_norm', 'new_m_a_norm_mix', 'new_m_a_w_in', 'new_m_a_w_group', 'new_m_a_scale', 'new_m_a_w_mem_kv', 'new_m_a_w_out', 'new_m_a_norm_ffn', 'new_m_a_w_gu', 'new_m_a_w_down', 'new_m_kv_norm', 'new_m_w_kv', 'new_m_b_norm_mix', 'new_m_b_w_q', 'new_m_b_w_mem_kv', 'new_m_b_w_out', 'new_m_b_norm_ffn', 'new_m_b_w_gu', 'new_m_b_w_down', 'new_m_final_norm', 'new_v_mem_norm', 'new_v_a_norm_mix', 'new_v_a_w_in', 'new_v_a_w_group', 'new_v_a_scale', 'new_v_a_w_mem_kv', 'new_v_a_w_out', 'new_v_a_norm_ffn', 'new_v_a_w_gu', 'new_v_a_w_down', 'new_v_kv_norm', 'new_v_w_kv', 'new_v_b_norm_mix', 'new_v_b_w_q', 'new_v_b_w_mem_kv', 'new_v_b_w_out', 'new_v_b_norm_ffn', 'new_v_b_w_gu', 'new_v_b_w_down', 'new_v_final_norm']
TWIN_LEAF_KINDS = {'loss': 'loss', 'grad_x': 'grad_x', 'grad_mem_norm': 'grad_w', 'grad_a_norm_mix': 'grad_w', 'grad_a_w_in': 'grad_w', 'grad_a_w_group': 'grad_w', 'grad_a_scale': 'grad_w', 'grad_a_w_mem_kv': 'grad_w', 'grad_a_w_out': 'grad_w', 'grad_a_norm_ffn': 'grad_w', 'grad_a_w_gu': 'grad_w', 'grad_a_w_down': 'grad_w', 'grad_kv_norm': 'grad_w', 'grad_w_kv': 'grad_w', 'grad_b_norm_mix': 'grad_w', 'grad_b_w_q': 'grad_w', 'grad_b_w_mem_kv': 'grad_w', 'grad_b_w_out': 'grad_w', 'grad_b_norm_ffn': 'grad_w', 'grad_b_w_gu': 'grad_w', 'grad_b_w_down': 'grad_w', 'grad_final_norm': 'grad_w', 'delta_mem_norm': 'delta_w', 'delta_a_norm_mix': 'delta_w', 'delta_a_w_in': 'delta_w', 'delta_a_w_group': 'delta_w', 'delta_a_scale': 'delta_w', 'delta_a_w_mem_kv': 'delta_w', 'delta_a_w_out': 'delta_w', 'delta_a_norm_ffn': 'delta_w', 'delta_a_w_gu': 'delta_w', 'delta_a_w_down': 'delta_w', 'delta_kv_norm': 'delta_w', 'delta_w_kv': 'delta_w', 'delta_b_norm_mix': 'delta_w', 'delta_b_w_q': 'delta_w', 'delta_b_w_mem_kv': 'delta_w', 'delta_b_w_out': 'delta_w', 'delta_b_norm_ffn': 'delta_w', 'delta_b_w_gu': 'delta_w', 'delta_b_w_down': 'delta_w', 'delta_final_norm': 'delta_w', 'new_m_mem_norm': 'new_m', 'new_m_a_norm_mix': 'new_m', 'new_m_a_w_in': 'new_m', 'new_m_a_w_group': 'new_m', 'new_m_a_scale': 'new_m', 'new_m_a_w_mem_kv': 'new_m', 'new_m_a_w_out': 'new_m', 'new_m_a_norm_ffn': 'new_m', 'new_m_a_w_gu': 'new_m', 'new_m_a_w_down': 'new_m', 'new_m_kv_norm': 'new_m', 'new_m_w_kv': 'new_m', 'new_m_b_norm_mix': 'new_m', 'new_m_b_w_q': 'new_m', 'new_m_b_w_mem_kv': 'new_m', 'new_m_b_w_out': 'new_m', 'new_m_b_norm_ffn': 'new_m', 'new_m_b_w_gu': 'new_m', 'new_m_b_w_down': 'new_m', 'new_m_final_norm': 'new_m', 'new_v_mem_norm': 'new_v', 'new_v_a_norm_mix': 'new_v', 'new_v_a_w_in': 'new_v', 'new_v_a_w_group': 'new_v', 'new_v_a_scale': 'new_v', 'new_v_a_w_mem_kv': 'new_v', 'new_v_a_w_out': 'new_v', 'new_v_a_norm_ffn': 'new_v', 'new_v_a_w_gu': 'new_v', 'new_v_a_w_down': 'new_v', 'new_v_kv_norm': 'new_v', 'new_v_w_kv': 'new_v', 'new_v_b_norm_mix': 'new_v', 'new_v_b_w_q': 'new_v', 'new_v_b_w_mem_kv': 'new_v', 'new_v_b_w_out': 'new_v', 'new_v_b_norm_ffn': 'new_v', 'new_v_b_w_gu': 'new_v', 'new_v_b_w_down': 'new_v', 'new_v_final_norm': 'new_v'}


def _forward(args):
    return _fwd_reference(*[args[k] for k in FWD_PARAMS])


def _output_shape():
    out = _jax.eval_shape(lambda: _forward(_fwd_setup_inputs(0)))
    return out.shape, out.dtype

N_MICROBATCH = 1
ADAM_LR = 0.001
ADAM_B1 = 0.9
ADAM_B2 = 0.999
ADAM_EPS = 1e-08
ADAM_WD = 0.01
ADAM_STEP = 10
PER_EXAMPLE_BATCH_AXIS = {'x': 0, 'mem': 0, 'loss_target': 0}
SHARED_INPUTS = []
_WEIGHT_DTYPES = {'mem_norm': _jnp.float32, 'a_norm_mix': _jnp.float32, 'a_w_in': _jnp.float32, 'a_w_group': _jnp.float32, 'a_scale': _jnp.float32, 'a_w_mem_kv': _jnp.float32, 'a_w_out': _jnp.float32, 'a_norm_ffn': _jnp.float32, 'a_w_gu': _jnp.float32, 'a_w_down': _jnp.float32, 'kv_norm': _jnp.float32, 'w_kv': _jnp.float32, 'b_norm_mix': _jnp.float32, 'b_w_q': _jnp.float32, 'b_w_mem_kv': _jnp.float32, 'b_w_out': _jnp.float32, 'b_norm_ffn': _jnp.float32, 'b_w_gu': _jnp.float32, 'b_w_down': _jnp.float32, 'final_norm': _jnp.float32}
MOMENT_SCALE = {'mem_norm': 9.533825e-03, 'a_norm_mix': 7.113713e-02, 'a_w_in': 7.024356e-02, 'a_w_group': 8.085408e-02, 'a_scale': 7.876568e-02, 'a_w_mem_kv': 9.893811e-03, 'a_w_out': 1.402370e-01, 'a_norm_ffn': 7.735656e-02, 'a_w_gu': 3.085532e-02, 'a_w_down': 1.012179e-01, 'kv_norm': 4.844509e-02, 'w_kv': 4.059372e-02, 'b_norm_mix': 2.132920e-02, 'b_w_q': 2.172153e-02, 'b_w_mem_kv': 8.665243e-03, 'b_w_out': 9.085147e-02, 'b_norm_ffn': 6.654533e-02, 'b_w_gu': 2.773242e-02, 'b_w_down': 9.089270e-02, 'final_norm': 3.207183e+01}


def _to_microbatches(a, axis):
    t = _jnp.moveaxis(a, axis, 0)
    t = t.reshape((N_MICROBATCH, t.shape[0] // N_MICROBATCH) + t.shape[1:])
    return _jnp.moveaxis(t, 1, axis + 1)


def setup_inputs(seed: int = 0) -> dict:
    inp = _fwd_setup_inputs(seed)
    key = _jax.random.fold_in(_jax.random.key(seed), 7919)
    shape, _ = _output_shape()
    out = dict(inp)
    out["loss_target"] = _jax.random.normal(_jax.random.fold_in(key, 0), shape, _jnp.float32)
    for i, name in enumerate(TWIN_WEIGHTS):
        w = inp[name].astype(_jnp.float32)
        if MOMENT_SCALE is None:
            s = _jnp.sqrt(_jnp.mean(_jnp.square(w)) + 1e-30)
        else:
            s = MOMENT_SCALE[name]
        km, kv = _jax.random.split(_jax.random.fold_in(key, i + 1))
        out[name] = w
        out["m_" + name] = s * _jax.random.normal(km, w.shape, _jnp.float32)
        out["v_" + name] = (s * s) * _jax.random.uniform(kv, w.shape, _jnp.float32, 0.5, 1.5)
    if N_MICROBATCH > 1:
        for name, axis in PER_EXAMPLE_BATCH_AXIS.items():
            out[name] = _to_microbatches(out[name], axis)
    return {'x': out['x'], 'mem': out['mem'], 'mem_norm': out['mem_norm'], 'a_norm_mix': out['a_norm_mix'], 'a_w_in': out['a_w_in'], 'a_w_group': out['a_w_group'], 'a_scale': out['a_scale'], 'a_w_mem_kv': out['a_w_mem_kv'], 'a_w_out': out['a_w_out'], 'a_norm_ffn': out['a_norm_ffn'], 'a_w_gu': out['a_w_gu'], 'a_w_down': out['a_w_down'], 'kv_norm': out['kv_norm'], 'w_kv': out['w_kv'], 'b_norm_mix': out['b_norm_mix'], 'b_w_q': out['b_w_q'], 'b_w_mem_kv': out['b_w_mem_kv'], 'b_w_out': out['b_w_out'], 'b_norm_ffn': out['b_norm_ffn'], 'b_w_gu': out['b_w_gu'], 'b_w_down': out['b_w_down'], 'final_norm': out['final_norm'], 'loss_target': out['loss_target'], 'm_mem_norm': out['m_mem_norm'], 'm_a_norm_mix': out['m_a_norm_mix'], 'm_a_w_in': out['m_a_w_in'], 'm_a_w_group': out['m_a_w_group'], 'm_a_scale': out['m_a_scale'], 'm_a_w_mem_kv': out['m_a_w_mem_kv'], 'm_a_w_out': out['m_a_w_out'], 'm_a_norm_ffn': out['m_a_norm_ffn'], 'm_a_w_gu': out['m_a_w_gu'], 'm_a_w_down': out['m_a_w_down'], 'm_kv_norm': out['m_kv_norm'], 'm_w_kv': out['m_w_kv'], 'm_b_norm_mix': out['m_b_norm_mix'], 'm_b_w_q': out['m_b_w_q'], 'm_b_w_mem_kv': out['m_b_w_mem_kv'], 'm_b_w_out': out['m_b_w_out'], 'm_b_norm_ffn': out['m_b_norm_ffn'], 'm_b_w_gu': out['m_b_w_gu'], 'm_b_w_down': out['m_b_w_down'], 'm_final_norm': out['m_final_norm'], 'v_mem_norm': out['v_mem_norm'], 'v_a_norm_mix': out['v_a_norm_mix'], 'v_a_w_in': out['v_a_w_in'], 'v_a_w_group': out['v_a_w_group'], 'v_a_scale': out['v_a_scale'], 'v_a_w_mem_kv': out['v_a_w_mem_kv'], 'v_a_w_out': out['v_a_w_out'], 'v_a_norm_ffn': out['v_a_norm_ffn'], 'v_a_w_gu': out['v_a_w_gu'], 'v_a_w_down': out['v_a_w_down'], 'v_kv_norm': out['v_kv_norm'], 'v_w_kv': out['v_w_kv'], 'v_b_norm_mix': out['v_b_norm_mix'], 'v_b_w_q': out['v_b_w_q'], 'v_b_w_mem_kv': out['v_b_w_mem_kv'], 'v_b_w_out': out['v_b_w_out'], 'v_b_norm_ffn': out['v_b_norm_ffn'], 'v_b_w_gu': out['v_b_w_gu'], 'v_b_w_down': out['v_b_w_down'], 'v_final_norm': out['v_final_norm']}


def _loss(weights, diff, rest, loss_target):
    with _jax.named_scope("forward"):
        args = {**rest, TWIN_DIFF_INPUT: diff, **{k: w.astype(_WEIGHT_DTYPES[k]) for k, w in weights.items()}}
        y = _forward(args)
    with _jax.named_scope("loss_head"):
        err = _jnp.square(y.astype(_jnp.float32) - loss_target)
        return 0.5 * _jnp.sum(_jnp.mean(err, axis=-1)) if err.ndim else 0.5 * err


def _adamw(w, g, m, v):
    m = ADAM_B1 * m + (1.0 - ADAM_B1) * g
    v = ADAM_B2 * v + (1.0 - ADAM_B2) * _jnp.square(g)
    m_hat = m / (1.0 - ADAM_B1 ** ADAM_STEP)
    v_hat = v / (1.0 - ADAM_B2 ** ADAM_STEP)
    delta = -ADAM_LR * (m_hat / (_jnp.sqrt(v_hat) + ADAM_EPS) + ADAM_WD * w)
    return delta, m, v


def reference(x, mem, mem_norm, a_norm_mix, a_w_in, a_w_group, a_scale, a_w_mem_kv, a_w_out, a_norm_ffn, a_w_gu, a_w_down, kv_norm, w_kv, b_norm_mix, b_w_q, b_w_mem_kv, b_w_out, b_norm_ffn, b_w_gu, b_w_down, final_norm, loss_target, m_mem_norm, m_a_norm_mix, m_a_w_in, m_a_w_group, m_a_scale, m_a_w_mem_kv, m_a_w_out, m_a_norm_ffn, m_a_w_gu, m_a_w_down, m_kv_norm, m_w_kv, m_b_norm_mix, m_b_w_q, m_b_w_mem_kv, m_b_w_out, m_b_norm_ffn, m_b_w_gu, m_b_w_down, m_final_norm, v_mem_norm, v_a_norm_mix, v_a_w_in, v_a_w_group, v_a_scale, v_a_w_mem_kv, v_a_w_out, v_a_norm_ffn, v_a_w_gu, v_a_w_down, v_kv_norm, v_w_kv, v_b_norm_mix, v_b_w_q, v_b_w_mem_kv, v_b_w_out, v_b_norm_ffn, v_b_w_gu, v_b_w_down, v_final_norm):
    given = dict(x=x, mem=mem, mem_norm=mem_norm, a_norm_mix=a_norm_mix, a_w_in=a_w_in, a_w_group=a_w_group, a_scale=a_scale, a_w_mem_kv=a_w_mem_kv, a_w_out=a_w_out, a_norm_ffn=a_norm_ffn, a_w_gu=a_w_gu, a_w_down=a_w_down, kv_norm=kv_norm, w_kv=w_kv, b_norm_mix=b_norm_mix, b_w_q=b_w_q, b_w_mem_kv=b_w_mem_kv, b_w_out=b_w_out, b_norm_ffn=b_norm_ffn, b_w_gu=b_w_gu, b_w_down=b_w_down, final_norm=final_norm, loss_target=loss_target, m_mem_norm=m_mem_norm, m_a_norm_mix=m_a_norm_mix, m_a_w_in=m_a_w_in, m_a_w_group=m_a_w_group, m_a_scale=m_a_scale, m_a_w_mem_kv=m_a_w_mem_kv, m_a_w_out=m_a_w_out, m_a_norm_ffn=m_a_norm_ffn, m_a_w_gu=m_a_w_gu, m_a_w_down=m_a_w_down, m_kv_norm=m_kv_norm, m_w_kv=m_w_kv, m_b_norm_mix=m_b_norm_mix, m_b_w_q=m_b_w_q, m_b_w_mem_kv=m_b_w_mem_kv, m_b_w_out=m_b_w_out, m_b_norm_ffn=m_b_norm_ffn, m_b_w_gu=m_b_w_gu, m_b_w_down=m_b_w_down, m_final_norm=m_final_norm, v_mem_norm=v_mem_norm, v_a_norm_mix=v_a_norm_mix, v_a_w_in=v_a_w_in, v_a_w_group=v_a_w_group, v_a_scale=v_a_scale, v_a_w_mem_kv=v_a_w_mem_kv, v_a_w_out=v_a_w_out, v_a_norm_ffn=v_a_norm_ffn, v_a_w_gu=v_a_w_gu, v_a_w_down=v_a_w_down, v_kv_norm=v_kv_norm, v_w_kv=v_w_kv, v_b_norm_mix=v_b_norm_mix, v_b_w_q=v_b_w_q, v_b_w_mem_kv=v_b_w_mem_kv, v_b_w_out=v_b_w_out, v_b_norm_ffn=v_b_norm_ffn, v_b_w_gu=v_b_w_gu, v_b_w_down=v_b_w_down, v_final_norm=v_final_norm)
    weights = {n: given[n] for n in TWIN_WEIGHTS}
    shared = {n: given[n] for n in SHARED_INPUTS}
    per_example = {n: given[n] for n in ['x', 'mem']}
    grad_fn = _jax.value_and_grad(_loss, argnums=(0, 1))

    def one_microbatch(ex, loss_target):
        ex = dict(ex)
        diff = ex.pop(TWIN_DIFF_INPUT)
        return grad_fn(weights, diff, {**shared, **ex}, loss_target)

    if N_MICROBATCH == 1:
        loss, (grad_w, grad_x) = one_microbatch(per_example, given["loss_target"])
    else:
        def body(carry, xs):
            loss_sum, grad_sum = carry
            l_k, (gw_k, gx_k) = one_microbatch(xs[0], xs[1])
            with _jax.named_scope("update"):
                return (loss_sum + l_k, _jax.tree.map(_jnp.add, grad_sum, gw_k)), gx_k

        init = (_jnp.zeros((), _jnp.float32), _jax.tree.map(_jnp.zeros_like, weights))
        (loss, grad_w), grad_x = _jax.lax.scan(body, init, (per_example, given["loss_target"]))
    with _jax.named_scope("update"):
        delta_w, new_m, new_v = {}, {}, {}
        for n in TWIN_WEIGHTS:
            delta_w[n], new_m[n], new_v[n] = _adamw(weights[n], grad_w[n], given["m_" + n], given["v_" + n])
    return (loss, grad_x, *[grad_w[n] for n in TWIN_WEIGHTS], *[delta_w[n] for n in TWIN_WEIGHTS],
            *[new_m[n] for n in TWIN_WEIGHTS], *[new_v[n] for n in TWIN_WEIGHTS])
```

```python
import functools

import jax
import jax.numpy as jnp
from jax import lax
from jax.experimental import pallas as pl
from jax.experimental.pallas import tpu as pltpu

F32 = jnp.float32
BF16 = jnp.bfloat16

HEAD_DIM = 64
SB_WIDTH = 768
MEM_WIDTH = 256
POOL_WINDOWS = (2, 4, 8, 16)
POOL_GROUP = 192
POOL_HALO = 16
EPS = 1e-6
ATT_SCALE = HEAD_DIM ** -0.5
ADAM_LR, ADAM_B1, ADAM_B2, ADAM_EPS, ADAM_WD, ADAM_STEP = 0.001, 0.9, 0.999, 1e-08, 0.01, 10

LANES = 128
VMEM_LIMIT = 56 * 1024 * 1024
MESH = pl.DeviceIdType.MESH
ANY = pl.BlockSpec(memory_space=pl.ANY)


def _params(*sem):
    return pltpu.CompilerParams(dimension_semantics=sem, vmem_limit_bytes=VMEM_LIMIT)


def _tile(n, pref):
    if n <= pref:
        return n
    best = None
    for t in range(LANES, pref + 1, LANES):
        if n % t == 0:
            best = t
    assert best is not None, (n, pref)
    return best


def _row_tile(t, pref):
    if t <= pref:
        return t
    for tb in range(pref - pref % 16, 0, -16):
        if t % tb == 0:
            return tb
    raise ValueError((t, pref))


def _rowwise(name, fn, rows, vecs, row_outs, sum_outs=(), tb=512):
    norm_rows = []
    for r in rows:
        if isinstance(r, tuple):
            arr, (bc, cb) = r
        else:
            arr, (bc, cb) = r, (r.shape[1], 0)
        norm_rows.append((arr, bc, cb))
    t = norm_rows[0][0].shape[0]
    tb = _row_tile(t, tb)
    n_in, n_ro = len(norm_rows) + len(vecs), len(row_outs)

    def body(*refs):
        ins = [r[...] for r in refs[:n_in]]
        outs = fn(*ins)
        if not isinstance(outs, tuple):
            outs = (outs,)
        for o_ref, o in zip(refs[n_in:n_in + n_ro], outs[:n_ro]):
            o_ref[...] = o.astype(o_ref.dtype)
        for s_ref, s in zip(refs[n_in + n_ro:], outs[n_ro:]):
            @pl.when(pl.program_id(0) == 0)
            def _():
                s_ref[...] = jnp.zeros_like(s_ref)
            s_ref[...] += s

    in_specs = [pl.BlockSpec((tb, bc), functools.partial(lambda i, cb: (i, cb), cb=cb)) for _, bc, cb in norm_rows]
    in_specs += [pl.BlockSpec(v.shape, lambda i: (0, 0)) for v in vecs]
    out_specs = [pl.BlockSpec((tb, c), lambda i: (i, 0)) for c, _ in row_outs]
    out_specs += [pl.BlockSpec((1, c), lambda i: (0, 0)) for c in sum_outs]
    out_shape = [jax.ShapeDtypeStruct((t, c), d) for c, d in row_outs]
    out_shape += [jax.ShapeDtypeStruct((1, c), F32) for c in sum_outs]
    res = pl.pallas_call(
        body, name=name, grid=(t // tb,), in_specs=in_specs, out_specs=out_specs, out_shape=out_shape,
        compiler_params=_params("arbitrary"),
    )(*[a for a, _, _ in norm_rows], *vecs)
    return res


def _rms_fwd(name, x, g):
    def fn(xt, gt):
        rstd = lax.rsqrt(jnp.mean(xt * xt, axis=-1, keepdims=True) + EPS)
        return xt * rstd * gt
    return _rowwise(name, fn, [x], [g], [(x.shape[1], BF16)])[0]


def _rms_bwd(name, x, g, dh, dres=None, want_dx=True):
    has_res = dres is not None

    def fn(*a):
        if has_res:
            xt, dht, drt, gt = a
        else:
            xt, dht, gt = a
        rstd = lax.rsqrt(jnp.mean(xt * xt, axis=-1, keepdims=True) + EPS)
        xhat = xt * rstd
        dht = dht.astype(F32)
        dg = jnp.sum(dht * xhat, axis=0, keepdims=True)
        if not want_dx:
            return (dg,)
        dxhat = dht * gt
        dx = rstd * (dxhat - xhat * jnp.mean(dxhat * xhat, axis=-1, keepdims=True))
        if has_res:
            dx = dx + drt
        return dx, dx, dg

    d = x.shape[1]
    rows = [x, dh] + ([dres] if has_res else [])
    outs = [(d, F32), (d, BF16)] if want_dx else []
    return _rowwise(name, fn, rows, [g], outs, [d])


_DOT_DIMS = {"nn": ((1,), (0,)), "nt": ((1,), (1,)), "tn": ((0,), (0,))}


def _mm(name, a, b, mode, out_dtype, res=None, tm=512, tn=512):
    if mode == "nn":
        (m, k), (k2, n) = a.shape, b.shape
    elif mode == "nt":
        (m, k), (n, k2) = a.shape, b.shape
    else:
        (k, m), (k2, n) = a.shape, b.shape
    assert k == k2, (name, a.shape, b.shape)
    tm, tn = _tile(m, tm), _tile(n, tn)
    dims = (_DOT_DIMS[mode], ((), ()))
    has_res = res is not None

    def body(a_ref, b_ref, *rest):
        acc = lax.dot_general(a_ref[...], b_ref[...], dims, preferred_element_type=F32)
        if has_res:
            acc = acc + rest[0][...]
        rest[-1][...] = acc.astype(out_dtype)

    a_spec = pl.BlockSpec((k, tm), lambda i, j: (0, i)) if mode == "tn" else pl.BlockSpec((tm, k), lambda i, j: (i, 0))
    b_spec = pl.BlockSpec((tn, k), lambda i, j: (j, 0)) if mode == "nt" else pl.BlockSpec((k, tn), lambda i, j: (0, j))
    o_spec = pl.BlockSpec((tm, tn), lambda i, j: (i, j))
    in_specs, args = [a_spec, b_spec], [a, b]
    if has_res:
        in_specs.append(o_spec)
        args.append(res)
    return pl.pallas_call(
        body, name=name, grid=(m // tm, n // tn), in_specs=in_specs, out_specs=o_spec,
        out_shape=jax.ShapeDtypeStruct((m, n), out_dtype), compiler_params=_params("parallel", "arbitrary"),
    )(*args)


def _pool(name, u, reverse, tb=512):
    t = u.shape[0]
    tb = min(tb, t)
    nt = t // tb
    c = SB_WIDTH
    hpb = tb // POOL_HALO

    def body(cur_ref, halo_ref, o_ref):
        i = pl.program_id(0)
        cur = cur_ref[...].astype(F32)
        edge = (i == nt - 1) if reverse else (i == 0)
        halo = jnp.where(edge, 0.0, halo_ref[...].astype(F32))
        col = lax.broadcasted_iota(jnp.int32, (tb + POOL_HALO, c), 1)
        row = lax.broadcasted_iota(jnp.int32, (tb + POOL_HALO, c), 0)
        wcol = jnp.where(col < POOL_GROUP, 2, jnp.where(col < 2 * POOL_GROUP, 4, jnp.where(col < 3 * POOL_GROUP, 8, 16)))
        n = tb + POOL_HALO
        if reverse:
            ext = jnp.concatenate([cur, halo], axis=0)
            tpos = i * tb + row
            ext = ext / jnp.minimum(tpos + 1, wcol).astype(F32)
            shift = lambda a, k: pltpu.roll(a, n - k, 0)
        else:
            ext = jnp.concatenate([halo, cur], axis=0)
            shift = lambda a, k: pltpu.roll(a, k, 0)
        s2 = ext + shift(ext, 1)
        s4 = s2 + shift(s2, 2)
        s8 = s4 + shift(s4, 4)
        s16 = s8 + shift(s8, 8)
        win = jnp.where(wcol == 2, s2, jnp.where(wcol == 4, s4, jnp.where(wcol == 8, s8, s16)))
        if reverse:
            out = win[:tb] - cur
        else:
            tpos = i * tb + row[POOL_HALO:] - POOL_HALO
            out = win[POOL_HALO:] / jnp.minimum(tpos + 1, wcol[POOL_HALO:]).astype(F32) - cur
        o_ref[...] = out.astype(o_ref.dtype)

    if reverse:
        halo_map = lambda i: (jnp.minimum((i + 1) * hpb, t // POOL_HALO - 1), 0)
    else:
        halo_map = lambda i: (jnp.maximum(i * hpb - 1, 0), 0)
    return pl.pallas_call(
        body, name=name, grid=(nt,),
        in_specs=[pl.BlockSpec((tb, c), lambda i: (i, 0)), pl.BlockSpec((POOL_HALO, c), halo_map)],
        out_specs=pl.BlockSpec((tb, c), lambda i: (i, 0)),
        out_shape=jax.ShapeDtypeStruct((t, c), BF16), compiler_params=_params("arbitrary"),
    )(u, u)


def _head_masks(shape):
    lane = lax.broadcasted_iota(jnp.int32, shape, 1)
    return lane < HEAD_DIM, lane >= HEAD_DIM


def _pick(mask, a):
    return jnp.where(mask, a, jnp.zeros_like(a))


def _dot(a, b, mode):
    return lax.dot_general(a, b, (_DOT_DIMS[mode], ((), ())), preferred_element_type=F32)


def _dot_split(a, tri):
    hi = a.astype(BF16)
    lo = (a - hi.astype(F32)).astype(BF16)
    return _dot(hi, tri, "nn") + _dot(lo, tri, "nn")


def _neg_softplus(z):
    return -(jnp.maximum(z, 0.0) + jnp.log(1.0 + jnp.exp(-jnp.abs(z))))


def _sb_fwd(proj, kv, blk=128):
    s = proj.shape[0]
    nb = s // blk
    npair = SB_WIDTH // LANES

    def body(q_ref, k_ref, v_ref, o_ref, tot_ref):
        r = lax.broadcasted_iota(jnp.int32, (blk, blk), 0)
        cidx = lax.broadcasted_iota(jnp.int32, (blk, blk), 1)
        causal = cidx < r
        tri_gt = (r > cidx).astype(BF16)
        m_a, m_b = _head_masks((blk, LANES))

        def block(qh, k2, vh, carry, acc, mask):
            z = _dot(qh, k2, "nt") * ATT_SCALE
            ln_full = _neg_softplus(z)
            ln = ln_full if mask is None else jnp.where(mask, ln_full, 0.0)
            later = _dot_split(ln, tri_gt) + carry
            w = jnp.exp(z + ln_full + later)
            if mask is not None:
                w = jnp.where(mask, w, 0.0)
            acc = acc + _dot(w.astype(BF16), vh, "nn")
            return carry + jnp.sum(ln, axis=1, keepdims=True), acc

        def q_block(qi, _):
            q0 = pl.multiple_of(qi * blk, blk)
            q2 = q_ref[pl.ds(q0, blk), :]
            qa, qb = _pick(m_a, q2), _pick(m_b, q2)
            k2 = k_ref[pl.ds(q0, blk), :]
            v2 = v_ref[pl.ds(q0, blk), :]
            zero_c = jnp.zeros((blk, 1), F32)
            zero_o = jnp.zeros((blk, LANES), F32)
            ca, acc = block(qa, k2, _pick(m_a, v2), zero_c, zero_o, causal)
            cb, acc = block(qb, k2, _pick(m_b, v2), zero_c, acc, causal)

            def k_block(step, carry):
                ca, cb, acc = carry
                k0 = pl.multiple_of((qi - 1 - step) * blk, blk)
                k2 = k_ref[pl.ds(k0, blk), :]
                v2 = v_ref[pl.ds(k0, blk), :]
                ca, acc = block(qa, k2, _pick(m_a, v2), ca, acc, None)
                cb, acc = block(qb, k2, _pick(m_b, v2), cb, acc, None)
                return ca, cb, acc

            ca, cb, acc = lax.fori_loop(0, qi, k_block, (ca, cb, acc))
            o_ref[pl.ds(q0, blk), :] = acc.astype(o_ref.dtype)
            tot_ref[0, pl.ds(q0, blk), :] = jnp.broadcast_to(ca, (blk, LANES))
            tot_ref[1, pl.ds(q0, blk), :] = jnp.broadcast_to(cb, (blk, LANES))
            return 0

        lax.fori_loop(0, nb, q_block, 0)

    return pl.pallas_call(
        body, name="sb_fwd", grid=(npair,),
        in_specs=[pl.BlockSpec((s, LANES), lambda p: (0, p)), pl.BlockSpec((s, LANES), lambda p: (0, p)),
                  pl.BlockSpec((s, LANES), lambda p: (0, npair + p))],
        out_specs=[pl.BlockSpec((s, LANES), lambda p: (0, p)), pl.BlockSpec((None, 2, s, LANES), lambda p: (p, 0, 0, 0))],
        out_shape=[jax.ShapeDtypeStruct((s, SB_WIDTH), BF16), jax.ShapeDtypeStruct((npair, 2, s, LANES), F32)],
        compiler_params=_params("arbitrary"),
    )(proj, kv, kv)


def _sb_bwd(proj, kv, dcat, tot, blk=128):
    s = proj.shape[0]
    nb = s // blk
    npair = SB_WIDTH // LANES

    def body(q_ref, k_ref, v_ref, do_ref, tot_ref, dq_ref, dk_ref, dv_ref, dk_acc, dv_acc):
        r = lax.broadcasted_iota(jnp.int32, (blk, blk), 0)
        cidx = lax.broadcasted_iota(jnp.int32, (blk, blk), 1)
        causal = cidx < r
        tri_le = (r <= cidx).astype(BF16)
        tri_lt = (r < cidx).astype(BF16)
        m_a, m_b = _head_masks((blk, LANES))
        dk_acc[...] = jnp.zeros_like(dk_acc)
        dv_acc[...] = jnp.zeros_like(dv_acc)

        def block(qh, doh, k2, v2, kh, tot_h, carry, mask):
            c_ln, c_d, dq = carry
            z = _dot(qh, k2, "nt") * ATT_SCALE
            ln_full = _neg_softplus(z)
            ln = ln_full if mask is None else jnp.where(mask, ln_full, 0.0)
            later = tot_h - (_dot_split(ln, tri_le) + c_ln)
            lsz = z + ln_full
            w = jnp.exp(lsz + later)
            if mask is not None:
                w = jnp.where(mask, w, 0.0)
            dlw = _dot(doh, v2, "nt") * w
            before = _dot_split(dlw, tri_lt) + c_d
            dz = dlw * jnp.exp(ln_full) - before * jnp.exp(lsz)
            if mask is not None:
                dz = jnp.where(mask, dz, 0.0)
            dz = (dz * ATT_SCALE).astype(BF16)
            dq = dq + _dot(dz, kh, "nn")
            dk = _dot(dz, qh, "tn")
            dv = _dot(w.astype(BF16), doh, "tn")
            carry = (c_ln + jnp.sum(ln, axis=1, keepdims=True), c_d + jnp.sum(dlw, axis=1, keepdims=True), dq)
            return carry, dk, dv

        def q_block(qi, _):
            q0 = pl.multiple_of(qi * blk, blk)
            q2 = q_ref[pl.ds(q0, blk), :]
            do2 = do_ref[pl.ds(q0, blk), :]
            qa, qb = _pick(m_a, q2), _pick(m_b, q2)
            doa, dob = _pick(m_a, do2), _pick(m_b, do2)
            tot_a = tot_ref[0, pl.ds(q0, blk), 0:1]
            tot_b = tot_ref[1, pl.ds(q0, blk), 0:1]
            zero_c = jnp.zeros((blk, 1), F32)
            zero_q = jnp.zeros((blk, LANES), F32)

            def both(k0, ca, cb, mask):
                k2 = k_ref[pl.ds(k0, blk), :]
                v2 = v_ref[pl.ds(k0, blk), :]
                ca, dka, dva = block(qa, doa, k2, v2, _pick(m_a, k2), tot_a, ca, mask)
                cb, dkb, dvb = block(qb, dob, k2, v2, _pick(m_b, k2), tot_b, cb, mask)
                dk_acc[pl.ds(k0, blk), :] += dka + dkb
                dv_acc[pl.ds(k0, blk), :] += dva + dvb
                return ca, cb

            def k_block(kj, carry):
                return both(pl.multiple_of(kj * blk, blk), carry[0], carry[1], None)

            init = ((zero_c, zero_c, zero_q), (zero_c, zero_c, zero_q))
            ca, cb = lax.fori_loop(0, qi, k_block, init)
            ca, cb = both(q0, ca, cb, causal)
            dq_ref[pl.ds(q0, blk), :] = (ca[2] + cb[2]).astype(dq_ref.dtype)
            return 0

        lax.fori_loop(0, nb, q_block, 0)
        dk_ref[...] = dk_acc[...].astype(dk_ref.dtype)
        dv_ref[...] = dv_acc[...].astype(dv_ref.dtype)

    col = lambda off: pl.BlockSpec((s, LANES), functools.partial(lambda p, off: (0, off + p), off=off))
    dq, dk, dv = pl.pallas_call(
        body, name="sb_bwd", grid=(npair,),
        in_specs=[col(0), col(0), col(npair), col(0), pl.BlockSpec((None, 2, s, LANES), lambda p: (p, 0, 0, 0))],
        out_specs=[col(0), col(0), col(0)],
        out_shape=[jax.ShapeDtypeStruct((s, SB_WIDTH), BF16)] * 3,
        scratch_shapes=[pltpu.VMEM((s, LANES), F32), pltpu.VMEM((s, LANES), F32)],
        compiler_params=_params("arbitrary"),
    )(proj, kv, kv, dcat, tot)
    return dq, jnp.concatenate([dk, dv], axis=1)


def _mem_fwd(name, proj, mkv, tq=512):
    s = proj.shape[0]
    tq = min(tq, s)
    qblk = SB_WIDTH // MEM_WIDTH

    def body(q_ref, kv_ref, o_ref):
        m_a, m_b = _head_masks((tq, LANES))
        mk_a, mk_b = _head_masks((kv_ref.shape[0], LANES))
        for p in range(MEM_WIDTH // LANES):
            q2 = q_ref[:, p * LANES:(p + 1) * LANES]
            k2 = kv_ref[:, p * LANES:(p + 1) * LANES]
            v2 = kv_ref[:, MEM_WIDTH + p * LANES:MEM_WIDTH + (p + 1) * LANES]
            acc = jnp.zeros((tq, LANES), F32)
            for mq, mk in ((m_a, mk_a), (m_b, mk_b)):
                logits = _dot(_pick(mq, q2), k2, "nt") * ATT_SCALE
                e = jnp.exp(logits - jnp.max(logits, axis=-1, keepdims=True))
                prob = e / jnp.sum(e, axis=-1, keepdims=True)
                acc = acc + _dot(prob.astype(BF16), _pick(mk, v2), "nn")
            o_ref[:, p * LANES:(p + 1) * LANES] = acc.astype(o_ref.dtype)

    return pl.pallas_call(
        body, name=name, grid=(s // tq,),
        in_specs=[pl.BlockSpec((tq, MEM_WIDTH), lambda i: (i, qblk)), pl.BlockSpec(mkv.shape, lambda i: (0, 0))],
        out_specs=pl.BlockSpec((tq, MEM_WIDTH), lambda i: (i, 0)),
        out_shape=jax.ShapeDtypeStruct((s, MEM_WIDTH), BF16), compiler_params=_params("arbitrary"),
    )(proj, mkv)


def _mem_bwd(name, proj, mkv, dcat, tq=512):
    s = proj.shape[0]
    tq = min(tq, s)
    qblk = SB_WIDTH // MEM_WIDTH

    def body(q_ref, kv_ref, do_ref, dq_ref, dkv_ref):
        @pl.when(pl.program_id(0) == 0)
        def _():
            dkv_ref[...] = jnp.zeros_like(dkv_ref)

        m_a, m_b = _head_masks((tq, LANES))
        for p in range(MEM_WIDTH // LANES):
            ksl = slice(p * LANES, (p + 1) * LANES)
            vsl = slice(MEM_WIDTH + p * LANES, MEM_WIDTH + (p + 1) * LANES)
            q2, do2 = q_ref[:, ksl], do_ref[:, ksl]
            k2, v2 = kv_ref[:, ksl], kv_ref[:, vsl]
            mk_a, mk_b = _head_masks(k2.shape)
            dq = jnp.zeros((tq, LANES), F32)
            dk = jnp.zeros(k2.shape, F32)
            dv = jnp.zeros(k2.shape, F32)
            for mq, mk in ((m_a, mk_a), (m_b, mk_b)):
                qh, doh = _pick(mq, q2), _pick(mq, do2)
                logits = _dot(qh, k2, "nt") * ATT_SCALE
                e = jnp.exp(logits - jnp.max(logits, axis=-1, keepdims=True))
                prob = e / jnp.sum(e, axis=-1, keepdims=True)
                dp = _dot(doh, v2, "nt")
                ds = prob * (dp - jnp.sum(dp * prob, axis=-1, keepdims=True)) * ATT_SCALE
                ds = ds.astype(BF16)
                dq = dq + _dot(ds, _pick(mk, k2), "nn")
                dk = dk + _dot(ds, qh, "tn")
                dv = dv + _dot(prob.astype(BF16), doh, "tn")
            dq_ref[:, ksl] = dq.astype(dq_ref.dtype)
            dkv_ref[:, ksl] += dk
            dkv_ref[:, vsl] += dv

    return pl.pallas_call(
        body, name=name, grid=(s // tq,),
        in_specs=[pl.BlockSpec((tq, MEM_WIDTH), lambda i: (i, qblk)), pl.BlockSpec(mkv.shape, lambda i: (0, 0)),
                  pl.BlockSpec((tq, MEM_WIDTH), lambda i: (i, qblk))],
        out_specs=[pl.BlockSpec((tq, MEM_WIDTH), lambda i: (i, 0)), pl.BlockSpec(mkv.shape, lambda i: (0, 0))],
        out_shape=[jax.ShapeDtypeStruct((s, MEM_WIDTH), BF16), jax.ShapeDtypeStruct(mkv.shape, F32)],
        compiler_params=_params("arbitrary"),
    )(proj, mkv, dcat)


def _swiglu_fwd(name, gu):
    f = gu.shape[1] // 2

    def fn(g, u):
        g, u = g.astype(F32), u.astype(F32)
        return g * jax.nn.sigmoid(g) * u
    return _rowwise(name, fn, [(gu, (f, 0)), (gu, (f, 1))], [], [(f, BF16)], tb=256)[0]


def _swiglu_bwd(name, gu, dact):
    f = gu.shape[1] // 2

    def fn(g, u, d):
        g, u, d = g.astype(F32), u.astype(F32), d.astype(F32)
        sg = jax.nn.sigmoid(g)
        silu = g * sg
        return jnp.concatenate([d * u * (sg + silu * (1.0 - sg)), d * silu], axis=1)
    return _rowwise(name, fn, [(gu, (f, 0)), (gu, (f, 1)), dact], [], [(2 * f, BF16)], tb=256)[0]


def _ffn_fwd(tag, x, norm, w_gu, w_down):
    h = _rms_fwd(tag + "_ffn_norm", x, norm)
    gu = _mm(tag + "_gu", h, w_gu, "nn", BF16)
    act = _swiglu_fwd(tag + "_swiglu", gu)
    out = _mm(tag + "_down", act, w_down, "nn", F32, res=x, tn=1024)
    return out, (h, gu, act)


def _ffn_bwd(tag, x, norm, w_gu, w_down, saved, dout, dout_bf):
    h, gu, act = saved
    dact = _mm(tag + "_down_dx", dout_bf, w_down, "nt", BF16, tn=1408)
    g_down = _mm(tag + "_down_dw", act, dout_bf, "tn", BF16, tm=256, tn=1024)
    dgu = _swiglu_bwd(tag + "_swiglu_bwd", gu, dact)
    g_gu = _mm(tag + "_gu_dw", h, dgu, "tn", BF16, tm=1024)
    dh = _mm(tag + "_gu_dx", dgu, w_gu, "nt", F32)
    dx, dx_bf, g_norm = _rms_bwd(tag + "_ffn_norm_bwd", x, norm, dh, dres=dout)
    return dx, dx_bf, g_gu, g_down, g_norm


def _mem_kv(tag, mem_n, w_mem_kv):
    return _mm(tag + "_memkv", mem_n, w_mem_kv, "nn", BF16)


def _mem_kv_bwd(tag, mem, mem_norm, mem_n, w_mem_kv, dmkv):
    dmkv = dmkv.astype(BF16)
    g_w = _mm(tag + "_memkv_dw", mem_n, dmkv, "tn", BF16)
    dmem_n = _mm(tag + "_memkv_dx", dmkv, w_mem_kv, "nt", F32)
    (g_norm,) = _rms_bwd(tag + "_memnorm_bwd", mem, mem_norm, dmem_n, want_dx=False)
    return g_w, g_norm


def _block_diag(w_group):
    z = jnp.zeros((POOL_GROUP, POOL_GROUP), w_group.dtype)
    return jnp.concatenate(
        [jnp.concatenate([w_group[g] if h == g else z for h in range(4)], axis=1) for g in range(4)], axis=0)


def _local_step(x, mem, target, w):
    g = {}
    mem_n = _rms_fwd("mem_norm", mem, w["mem_norm"])
    h_a = _rms_fwd("a_mix_norm", x, w["a_norm_mix"])
    proj_a = _mm("a_in", h_a, w["a_w_in"], "nn", F32)
    pooled = _pool("a_pool", proj_a, reverse=False)
    w_bd = _block_diag(w["a_w_group"])
    g_pre = _mm("a_group", pooled, w_bd, "nn", BF16, tn=768)
    mkv_a = _mem_kv("a", mem_n, w["a_w_mem_kv"])
    proj_a_bf = proj_a.astype(BF16)
    mem_a = _mem_fwd("a_mem_attn", proj_a_bf, mkv_a)
    cat_a = _rowwise("a_cat", lambda gp, mo, sc: jnp.concatenate([gp.astype(F32) * sc, mo.astype(F32)], axis=1),
                     [g_pre, mem_a], [w["a_scale"]], [(1024, BF16)])[0]
    x1 = _mm("a_out", cat_a, w["a_w_out"], "nn", F32, res=x, tn=1024)
    x2, ffn_a = _ffn_fwd("a", x1, w["a_norm_ffn"], w["a_w_gu"], w["a_w_down"])
    h_k = _rms_fwd("kv_norm", x2, w["kv_norm"])
    kv = _mm("kv_proj", h_k, w["w_kv"], "nn", BF16)
    h_b = _rms_fwd("b_mix_norm", x2, w["b_norm_mix"])
    proj_b = _mm("b_q", h_b, w["b_w_q"], "nn", BF16)
    sb_out, tot = _sb_fwd(proj_b, kv)
    mkv_b = _mem_kv("b", mem_n, w["b_w_mem_kv"])
    mem_b = _mem_fwd("b_mem_attn", proj_b, mkv_b)
    cat_b = jnp.concatenate([sb_out, mem_b], axis=1)
    x3 = _mm("b_out", cat_b, w["b_w_out"], "nn", F32, res=x2, tn=1024)
    x4, ffn_b = _ffn_fwd("b", x3, w["b_norm_ffn"], w["b_w_gu"], w["b_w_down"])

    d = x.shape[1]

    def head(xt, tt, gt):
        rstd = lax.rsqrt(jnp.mean(xt * xt, axis=-1, keepdims=True) + EPS)
        xhat = xt * rstd
        err = xhat * gt - tt
        loss = 0.5 * jnp.sum(jnp.sum(err * err, axis=1, keepdims=True), axis=0, keepdims=True) / d
        dy = err / d
        dxhat = dy * gt
        dx = rstd * (dxhat - xhat * jnp.mean(dxhat * xhat, axis=-1, keepdims=True))
        return dx, dx, jnp.sum(dy * xhat, axis=0, keepdims=True), jnp.broadcast_to(loss, (1, LANES))

    dx4, dx4_bf, g["final_norm"], loss = _rowwise(
        "loss_head", head, [x4, target], [w["final_norm"]], [(d, F32), (d, BF16)], [d, LANES])

    dx3, dx3_bf, g["b_w_gu"], g["b_w_down"], g["b_norm_ffn"] = _ffn_bwd(
        "b", x3, w["b_norm_ffn"], w["b_w_gu"], w["b_w_down"], ffn_b, dx4, dx4_bf)
    dcat_b = _mm("b_out_dx", dx3_bf, w["b_w_out"], "nt", BF16, tn=1024)
    g["b_w_out"] = _mm("b_out_dw", cat_b, dx3_bf, "tn", BF16, tm=1024, tn=1024)
    dq_sb, dkv = _sb_bwd(proj_b, kv, dcat_b, tot)
    dq_mem_b, dmkv_b = _mem_bwd("b_mem_attn_bwd", proj_b, mkv_b, dcat_b)
    dproj_b = jnp.concatenate([dq_sb, dq_mem_b], axis=1)
    g["b_w_q"] = _mm("b_q_dw", h_b, dproj_b, "tn", BF16, tm=1024, tn=1024)
    dh_b = _mm("b_q_dx", dproj_b, w["b_w_q"], "nt", F32, tn=1024)
    dx2, _, g["b_norm_mix"] = _rms_bwd("b_mix_norm_bwd", x2, w["b_norm_mix"], dh_b, dres=dx3)
    g["b_w_mem_kv"], g_memnorm_b = _mem_kv_bwd("b", mem, w["mem_norm"], mem_n, w["b_w_mem_kv"], dmkv_b)
    g["w_kv"] = _mm("kv_proj_dw", h_k, dkv, "tn", BF16, tm=1024)
    dh_k = _mm("kv_proj_dx", dkv, w["w_kv"], "nt", F32, tn=1024)
    dx2, dx2_bf, g["kv_norm"] = _rms_bwd("kv_norm_bwd", x2, w["kv_norm"], dh_k, dres=dx2)

    dx1, dx1_bf, g["a_w_gu"], g["a_w_down"], g["a_norm_ffn"] = _ffn_bwd(
        "a", x1, w["a_norm_ffn"], w["a_w_gu"], w["a_w_down"], ffn_a, dx2, dx2_bf)
    dcat_a = _mm("a_out_dx", dx1_bf, w["a_w_out"], "nt", BF16, tn=1024)
    g["a_w_out"] = _mm("a_out_dw", cat_a, dx1_bf, "tn", BF16, tm=1024, tn=1024)

    def scale_bwd(dc, gp, sc):
        dc, gp = dc.astype(F32), gp.astype(F32)
        return dc * sc, jnp.sum(dc * gp, axis=0, keepdims=True)

    dg_pre, g["a_scale"] = _rowwise("a_scale_bwd", scale_bwd, [(dcat_a, (SB_WIDTH, 0)), g_pre], [w["a_scale"]],
                                    [(SB_WIDTH, BF16)], [SB_WIDTH])
    g_bd = _mm("a_group_dw", pooled, dg_pre, "tn", F32, tm=768, tn=768)
    g["a_w_group"] = jnp.stack([g_bd[i * POOL_GROUP:(i + 1) * POOL_GROUP, i * POOL_GROUP:(i + 1) * POOL_GROUP]
                                for i in range(4)])
    dpooled = _mm("a_group_dx", dg_pre, w_bd, "nt", F32, tn=768)
    du_pool = _pool("a_pool_bwd", dpooled, reverse=True)
    dq_mem_a, dmkv_a = _mem_bwd("a_mem_attn_bwd", proj_a_bf, mkv_a, dcat_a)
    dproj_a = jnp.concatenate([du_pool, dq_mem_a], axis=1)
    g["a_w_in"] = _mm("a_in_dw", h_a, dproj_a, "tn", BF16, tm=1024, tn=1024)
    dh_a = _mm("a_in_dx", dproj_a, w["a_w_in"], "nt", F32, tn=1024)
    grad_x, _, g["a_norm_mix"] = _rms_bwd("a_mix_norm_bwd", x, w["a_norm_mix"], dh_a, dres=dx1)
    g["a_w_mem_kv"], g_memnorm_a = _mem_kv_bwd("a", mem, w["mem_norm"], mem_n, w["a_w_mem_kv"], dmkv_a)
    g["mem_norm"] = g_memnorm_a + g_memnorm_b
    return loss, grad_x, g


ROW_SHARDED = ("a_w_in", "a_w_mem_kv", "a_w_out", "a_w_down", "b_w_q", "b_w_mem_kv", "b_w_out", "b_w_down")
COL_SHARDED = ("a_w_gu", "w_kv", "b_w_gu")
BIG = ("a_w_in", "a_w_mem_kv", "a_w_out", "a_w_gu", "a_w_down", "w_kv", "b_w_q", "b_w_mem_kv", "b_w_out", "b_w_gu",
       "b_w_down")
N_CHIPS = 4
N_DEV = 8


def _position():
    x, y, c = lax.axis_index("x"), lax.axis_index("y"), lax.axis_index("c")
    other_chips = [(1 - x, y), (x, 1 - y), (1 - x, 1 - y)]
    return x, y, c, other_chips


def _remote(src, dst, send_sem, recv_sem, device):
    return pltpu.make_async_remote_copy(src_ref=src, dst_ref=dst, send_sem=send_sem, recv_sem=recv_sem,
                                        device_id=device, device_id_type=MESH)


def _comm_call(name, body, args, out_shape, n_remote, n_local):
    return pl.pallas_call(
        body, name=name, in_specs=[ANY] * len(args), out_specs=[ANY] * len(out_shape), out_shape=out_shape,
        scratch_shapes=[pltpu.SemaphoreType.DMA((n_remote,)), pltpu.SemaphoreType.DMA((n_remote,)),
                        pltpu.SemaphoreType.DMA((n_local,))],
    )(*args)


def _shard_window(ref, row_sharded, shard_shape, chip, half):
    r, cdim = shard_shape
    if row_sharded:
        return ref.at[pl.ds(pl.multiple_of(chip * r + half * (r // 2), 16), r // 2), :]
    return ref.at[pl.ds(pl.multiple_of(half * (r // 2), 16), r // 2), pl.ds(pl.multiple_of(chip * cdim, LANES), cdim)]


def _half_rows(ref, half):
    r = ref.shape[0] // 2
    return ref.at[pl.ds(pl.multiple_of(half * r, 16), r), :]


def _gather_weights(shards):
    names = list(shards)
    n = len(names)
    args = [shards[name] for name in names]
    out_shape = []
    for name in names:
        r, cdim = shards[name].shape
        full = (N_CHIPS * r, cdim) if name in ROW_SHARDED else (r, N_CHIPS * cdim)
        out_shape.append(jax.ShapeDtypeStruct(full, BF16))

    def body(*refs):
        src, dst = refs[:n], refs[n:2 * n]
        send, recv, loc = refs[2 * n:]
        x, y, c, chips = _position()
        me = 2 * x + y

        def window(i, chip, half):
            return _shard_window(dst[i], names[i] in ROW_SHARDED, src[i].shape, chip, half)

        local, sends = [], []
        for i in range(n):
            for half in range(2):
                cp = pltpu.make_async_copy(_half_rows(src[i], half), window(i, me, half), loc.at[2 * i + half])
                cp.start()
                local.append(cp)
            for k, (px, py) in enumerate(chips):
                cp = _remote(_half_rows(src[i], c), window(i, me, c), send.at[6 * i + k], recv.at[6 * i + k], (px, py, c))
                cp.start()
                sends.append(cp)
        for i in range(n):
            for k, (px, py) in enumerate(chips):
                chip = 2 * px + py
                _remote(_half_rows(src[i], c), window(i, chip, c), send.at[6 * i + k], recv.at[6 * i + k],
                        (px, py, c)).wait_recv()
                cp = _remote(window(i, chip, c), window(i, chip, c), send.at[6 * i + 3 + k], recv.at[6 * i + 3 + k],
                             (x, y, 1 - c))
                cp.start()
                sends.append(cp)
        for i in range(n):
            for k, (px, py) in enumerate(chips):
                chip = 2 * px + py
                _remote(_half_rows(src[i], c), window(i, chip, 1 - c), send.at[6 * i + 3 + k], recv.at[6 * i + 3 + k],
                        (x, y, 1 - c)).wait_recv()
        for cp in sends:
            cp.wait_send()
        for cp in local:
            cp.wait()

    outs = _comm_call("gather_weights", body, args, out_shape, 6 * n, 2 * n)
    return dict(zip(names, outs))


def _add2(name, a, b):
    return _rowwise(name, lambda p, q: p.astype(F32) + q.astype(F32), [a, b], [], [(a.shape[1], BF16)], tb=256)[0]


def _sum_slots(name, b, tb=256):
    _, r, ncol = b.shape
    tb = _row_tile(r, tb)

    def body(b_ref, o_ref):
        acc = b_ref[0].astype(F32)
        for k in range(1, N_CHIPS):
            acc = acc + b_ref[k].astype(F32)
        o_ref[...] = acc

    return pl.pallas_call(
        body, name=name, grid=(r // tb,), in_specs=[pl.BlockSpec((N_CHIPS, tb, ncol), lambda i: (0, i, 0))],
        out_specs=pl.BlockSpec((tb, ncol), lambda i: (i, 0)), out_shape=jax.ShapeDtypeStruct((r, ncol), F32),
        compiler_params=_params("arbitrary"),
    )(b)


def _reduce_scatter(grads):
    names = list(grads)
    n = len(names)
    row_sharded = [name in ROW_SHARDED for name in names]
    shard_shape = []
    for name in names:
        rows, cols = grads[name].shape
        shard_shape.append((rows // N_CHIPS, cols) if name in ROW_SHARDED else (rows, cols // N_CHIPS))

    args, shape_a = [], []
    for i, name in enumerate(names):
        r, cdim = shard_shape[i]
        if row_sharded[i]:
            args.append(grads[name].reshape(N_CHIPS, 2, r // 2, cdim))
            shape_a.append((N_CHIPS, r // 2, cdim))
        else:
            args.append(grads[name])
            shape_a.append((r // 2, N_CHIPS * cdim))
    out_shape = [jax.ShapeDtypeStruct(s, BF16) for s in shape_a] * 2

    def body_a(*refs):
        src, own, sib = refs[:n], refs[n:2 * n], refs[2 * n:3 * n]
        send, recv, loc = refs[3 * n:]
        x, y, c, _ = _position()

        def half_of(i, half):
            return src[i].at[:, half] if row_sharded[i] else _half_rows(src[i], half)

        copies = []
        for i in range(n):
            lc = pltpu.make_async_copy(half_of(i, c), own[i], loc.at[i])
            lc.start()
            cp = _remote(half_of(i, 1 - c), sib[i], send.at[i], recv.at[i], (x, y, 1 - c))
            cp.start()
            copies.extend([lc, cp])
        for cp in copies:
            cp.wait()

    outs = _comm_call("grads_pair_exchange", body_a, args, out_shape, n, n)
    partial = []
    for i, name in enumerate(names):
        s = shape_a[i]
        flat = (s[0] * s[1], s[2]) if row_sharded[i] else s
        partial.append(_add2(name + "_pair_sum", outs[i].reshape(flat), outs[n + i].reshape(flat)).reshape(s))

    half_shape = [(r // 2, cdim) for r, cdim in shard_shape]
    out_shape = [jax.ShapeDtypeStruct((N_CHIPS,) + s, BF16) for s in half_shape]

    def body_b(*refs):
        src, dst = refs[:n], refs[n:2 * n]
        send, recv, loc = refs[2 * n:]
        x, y, c, chips = _position()
        me = 2 * x + y

        def shard(i, chip):
            if row_sharded[i]:
                return src[i].at[chip]
            cdim = half_shape[i][1]
            return src[i].at[:, pl.ds(pl.multiple_of(chip * cdim, LANES), cdim)]

        copies = []
        for i in range(n):
            lc = pltpu.make_async_copy(shard(i, me), dst[i].at[3], loc.at[i])
            lc.start()
            copies.append(lc)
            for k, (px, py) in enumerate(chips):
                cp = _remote(shard(i, 2 * px + py), dst[i].at[k], send.at[3 * i + k], recv.at[3 * i + k], (px, py, c))
                cp.start()
                copies.append(cp)
        for cp in copies:
            cp.wait()

    outs = _comm_call("grads_chip_exchange", body_b, partial, out_shape, 3 * n, n)
    halves = [_sum_slots(name + "_chip_sum", o) for name, o in zip(names, outs)]

    out_shape = [jax.ShapeDtypeStruct((2,) + s, F32) for s in half_shape]

    def body_c(*refs):
        src, dst = refs[:n], refs[n:2 * n]
        send, recv, loc = refs[2 * n:]
        x, y, c, _ = _position()
        copies = []
        for i in range(n):
            lc = pltpu.make_async_copy(src[i], dst[i].at[c], loc.at[i])
            lc.start()
            cp = _remote(src[i], dst[i].at[c], send.at[i], recv.at[i], (x, y, 1 - c))
            cp.start()
            copies.extend([lc, cp])
        for cp in copies:
            cp.wait()

    outs = _comm_call("grads_pair_share", body_c, halves, out_shape, n, n)
    return {name: o.reshape(2 * o.shape[1], o.shape[2]) for name, o in zip(names, outs)}


def _all_reduce_small(name, v):
    rows, cols = v.shape

    def body(v_ref, o_ref, buf, send, recv):
        x, y, c, _ = _position()
        me = 4 * x + 2 * y + c
        buf[me] = v_ref[...]
        peers = []
        for k in range(1, N_DEV):
            fx, fy, fc = (k >> 2) & 1, (k >> 1) & 1, k & 1
            peers.append((x + fx - 2 * fx * x, y + fy - 2 * fy * y, c + fc - 2 * fc * c))
        sends = []
        for k, peer in enumerate(peers):
            cp = _remote(v_ref, buf.at[me], send.at[k], recv.at[k], peer)
            cp.start()
            sends.append(cp)
        for k, (px, py, pc) in enumerate(peers):
            _remote(v_ref, buf.at[4 * px + 2 * py + pc], send.at[k], recv.at[k], (px, py, pc)).wait_recv()
        for cp in sends:
            cp.wait_send()
        acc = buf[0]
        for d in range(1, N_DEV):
            acc = acc + buf[d]
        o_ref[...] = acc

    vm = pl.BlockSpec(memory_space=pltpu.VMEM)
    return pl.pallas_call(
        body, name=name, in_specs=[vm], out_specs=vm, out_shape=jax.ShapeDtypeStruct(v.shape, F32),
        scratch_shapes=[pltpu.VMEM((N_DEV, rows, cols), F32), pltpu.SemaphoreType.DMA((N_DEV - 1,)),
                        pltpu.SemaphoreType.DMA((N_DEV - 1,))],
        compiler_params=pltpu.CompilerParams(vmem_limit_bytes=VMEM_LIMIT),
    )(v)


def _adamw(name, w, g, m, v):
    def fn(wt, gt, mt, vt):
        mt = ADAM_B1 * mt + (1.0 - ADAM_B1) * gt
        vt = ADAM_B2 * vt + (1.0 - ADAM_B2) * (gt * gt)
        m_hat = mt / (1.0 - ADAM_B1 ** ADAM_STEP)
        v_hat = vt / (1.0 - ADAM_B2 ** ADAM_STEP)
        delta = -ADAM_LR * (m_hat / (jnp.sqrt(v_hat) + ADAM_EPS) + ADAM_WD * wt)
        return delta, mt, vt
    n = w.shape[1]
    return _rowwise(name, fn, [w, g, m, v], [], [(n, F32)] * 3, tb=256)


WEIGHTS = ("mem_norm", "a_norm_mix", "a_w_in", "a_w_group", "a_scale", "a_w_mem_kv", "a_w_out", "a_norm_ffn", "a_w_gu",
           "a_w_down", "kv_norm", "w_kv", "b_norm_mix", "b_w_q", "b_w_mem_kv", "b_w_out", "b_norm_ffn", "b_w_gu",
           "b_w_down", "final_norm")
REPLICATED_VECS = ("mem_norm", "kv_norm", "b_norm_mix", "b_norm_ffn", "final_norm")
SHARDED_VECS = ("a_norm_mix", "a_norm_ffn", "a_scale")
D_MODEL = 1024
GROUP_ROWS = 4 * POOL_GROUP * POOL_GROUP // D_MODEL


def _row(v):
    v = v.reshape(1, -1).astype(F32)
    return jnp.pad(v, ((0, 0), (0, D_MODEL - v.shape[1])))


def _pack_small(t):
    rows = [_row(t[k]) for k in REPLICATED_VECS]
    rows.append(_row(jnp.concatenate([t[k].reshape(-1) for k in SHARDED_VECS])))
    rows.append(jnp.zeros((2, D_MODEL), F32))
    rows.append(t["a_w_group"].reshape(GROUP_ROWS, D_MODEL).astype(F32))
    return jnp.concatenate(rows, axis=0)


def _unpack_small(p, like):
    out = {k: p[i, :].reshape(like[k].shape) for i, k in enumerate(REPLICATED_VECS)}
    off = 0
    for k in SHARDED_VECS:
        size = like[k].size
        out[k] = p[len(REPLICATED_VECS), off:off + size].reshape(like[k].shape)
        off += size
    out["a_w_group"] = p[len(REPLICATED_VECS) + 3:, :].reshape(like["a_w_group"].shape)
    return out


def kernel(x, mem, mem_norm, a_norm_mix, a_w_in, a_w_group, a_scale, a_w_mem_kv, a_w_out, a_norm_ffn, a_w_gu, a_w_down, kv_norm, w_kv, b_norm_mix, b_w_q, b_w_mem_kv, b_w_out, b_norm_ffn, b_w_gu, b_w_down, final_norm, loss_target, m_mem_norm, m_a_norm_mix, m_a_w_in, m_a_w_group, m_a_scale, m_a_w_mem_kv, m_a_w_out, m_a_norm_ffn, m_a_w_gu, m_a_w_down, m_kv_norm, m_w_kv, m_b_norm_mix, m_b_w_q, m_b_w_mem_kv, m_b_w_out, m_b_norm_ffn, m_b_w_gu, m_b_w_down, m_final_norm, v_mem_norm, v_a_norm_mix, v_a_w_in, v_a_w_group, v_a_scale, v_a_w_mem_kv, v_a_w_out, v_a_norm_ffn, v_a_w_gu, v_a_w_down, v_kv_norm, v_w_kv, v_b_norm_mix, v_b_w_q, v_b_w_mem_kv, v_b_w_out, v_b_norm_ffn, v_b_w_gu, v_b_w_down, v_final_norm):
    given = dict(locals())
    wl = {k: given[k] for k in WEIGHTS}
    ml = {k: given["m_" + k] for k in WEIGHTS}
    vl = {k: given["v_" + k] for k in WEIGHTS}
    chip = 2 * lax.axis_index("x") + lax.axis_index("y")

    def mat(a):
        return a.reshape(a.shape[-2], a.shape[-1])

    full = _gather_weights({k: mat(wl[k]).astype(BF16) for k in BIG})
    gains = jnp.zeros((8, D_MODEL), F32)
    for i, k in enumerate(SHARDED_VECS):
        part = wl[k].reshape(1, -1)
        width = part.shape[1]
        gains = lax.dynamic_update_slice(gains, part, (i, chip * width))
    gains = _all_reduce_small("gains_all_gather", gains) * 0.5
    w = dict(full)
    for k in REPLICATED_VECS:
        w[k] = wl[k].reshape(1, D_MODEL)
    w["a_norm_mix"], w["a_norm_ffn"] = gains[0:1], gains[1:2]
    w["a_scale"] = gains[2:3, :SB_WIDTH]
    w["a_w_group"] = wl["a_w_group"][0].astype(BF16)

    loss, grad_x, g = _local_step(x[0], mem[0], loss_target[0], w)

    red = _reduce_scatter({k: g[k] for k in BIG})
    small = jnp.concatenate(
        [_row(g[k]) for k in REPLICATED_VECS] + [_row(g[k]) for k in SHARDED_VECS] + [_row(loss)]
        + [jnp.zeros((7, D_MODEL), F32), g["a_w_group"].reshape(GROUP_ROWS, D_MODEL)], axis=0)
    small = _all_reduce_small("small_grads_all_reduce", small)
    gs = {k: small[i] for i, k in enumerate(REPLICATED_VECS)}
    for i, k in enumerate(SHARDED_VECS):
        width = wl[k].shape[-1]
        gs[k] = lax.dynamic_slice(small[len(REPLICATED_VECS) + i], (chip * width,), (width,))
    gs["a_w_group"] = small[16:]
    total_loss = small[8, 0]

    out_g, out_d, out_m, out_v = {}, {}, {}, {}
    for k in BIG:
        shape = wl[k].shape
        out_g[k] = red[k].reshape(shape)
        d, nm, nv = _adamw(k + "_adamw", mat(wl[k]), red[k], mat(ml[k]), mat(vl[k]))
        out_d[k], out_m[k], out_v[k] = d.reshape(shape), nm.reshape(shape), nv.reshape(shape)
    small_names = REPLICATED_VECS + SHARDED_VECS + ("a_w_group",)
    d, nm, nv = _adamw("small_adamw", _pack_small(wl), _pack_small(gs), _pack_small(ml), _pack_small(vl))
    like = {k: wl[k] for k in small_names}
    for dst, p in ((out_d, d), (out_m, nm), (out_v, nv)):
        dst.update(_unpack_small(p, like))
    for k in small_names:
        out_g[k] = gs[k].reshape(wl[k].shape)

    return (total_loss, grad_x[None], *[out_g[k] for k in WEIGHTS], *[out_d[k] for k in WEIGHTS],
            *[out_m[k] for k in WEIGHTS], *[out_v[k] for k in WEIGHTS])
```

```python
import functools

import jax
import jax.numpy as jnp
from jax import lax
from jax.experimental import pallas as pl
from jax.experimental.pallas import tpu as pltpu

F32 = jnp.float32
BF16 = jnp.bfloat16

HEAD_DIM = 64
SB_WIDTH = 768
MEM_WIDTH = 256
POOL_WINDOWS = (2, 4, 8, 16)
POOL_GROUP = 192
POOL_HALO = 16
EPS = 1e-6
ATT_SCALE = HEAD_DIM ** -0.5
ADAM_LR, ADAM_B1, ADAM_B2, ADAM_EPS, ADAM_WD, ADAM_STEP = 0.001, 0.9, 0.999, 1e-08, 0.01, 10

LANES = 128
SB_TQ, SB_TK = 512, 512
VMEM_LIMIT = 56 * 1024 * 1024
MESH = pl.DeviceIdType.MESH
COPY_BYTES = 512 * 1024
ANY = pl.BlockSpec(memory_space=pl.ANY)


def _params(*sem):
    return pltpu.CompilerParams(dimension_semantics=sem, vmem_limit_bytes=VMEM_LIMIT)


def _tile(n, pref):
    if n <= pref:
        return n
    best = None
    for t in range(LANES, pref + 1, LANES):
        if n % t == 0:
            best = t
    assert best is not None, (n, pref)
    return best


def _row_tile(t, pref):
    if t <= pref:
        return t
    for tb in range(pref - pref % 16, 0, -16):
        if t % tb == 0:
            return tb
    raise ValueError((t, pref))


def _rowwise(name, fn, rows, vecs, row_outs, sum_outs=(), tb=512):
    norm_rows = []
    for r in rows:
        if isinstance(r, tuple):
            arr, (bc, cb) = r
        else:
            arr, (bc, cb) = r, (r.shape[1], 0)
        norm_rows.append((arr, bc, cb))
    t = norm_rows[0][0].shape[0]
    tb = _row_tile(t, tb)
    n_in, n_ro = len(norm_rows) + len(vecs), len(row_outs)

    def body(*refs):
        ins = [r[...] for r in refs[:n_in]]
        outs = fn(*ins)
        if not isinstance(outs, tuple):
            outs = (outs,)
        for o_ref, o in zip(refs[n_in:n_in + n_ro], outs[:n_ro]):
            o_ref[...] = o.astype(o_ref.dtype)
        for s_ref, s in zip(refs[n_in + n_ro:], outs[n_ro:]):
            @pl.when(pl.program_id(0) == 0)
            def _():
                s_ref[...] = jnp.zeros_like(s_ref)
            s_ref[...] += s

    in_specs = [pl.BlockSpec((tb, bc), functools.partial(lambda i, cb: (i, cb), cb=cb)) for _, bc, cb in norm_rows]
    in_specs += [pl.BlockSpec(v.shape, lambda i: (0, 0)) for v in vecs]
    out_specs = [pl.BlockSpec((tb, c), lambda i: (i, 0)) for c, _ in row_outs]
    out_specs += [pl.BlockSpec((1, c), lambda i: (0, 0)) for c in sum_outs]
    out_shape = [jax.ShapeDtypeStruct((t, c), d) for c, d in row_outs]
    out_shape += [jax.ShapeDtypeStruct((1, c), F32) for c in sum_outs]
    res = pl.pallas_call(
        body, name=name, grid=(t // tb,), in_specs=in_specs, out_specs=out_specs, out_shape=out_shape,
        compiler_params=_params("arbitrary"),
    )(*[a for a, _, _ in norm_rows], *vecs)
    return res


def _rms_fwd(name, x, g):
    def fn(xt, gt):
        rstd = lax.rsqrt(jnp.mean(xt * xt, axis=-1, keepdims=True) + EPS)
        return xt * rstd * gt
    return _rowwise(name, fn, [x], [g], [(x.shape[1], BF16)])[0]


def _rms_bwd(name, x, g, dh, dres=None, want_dx=True):
    has_res = dres is not None

    def fn(*a):
        if has_res:
            xt, dht, drt, gt = a
        else:
            xt, dht, gt = a
        rstd = lax.rsqrt(jnp.mean(xt * xt, axis=-1, keepdims=True) + EPS)
        xhat = xt * rstd
        dht = dht.astype(F32)
        dg = jnp.sum(dht * xhat, axis=0, keepdims=True)
        if not want_dx:
            return (dg,)
        dxhat = dht * gt
        dx = rstd * (dxhat - xhat * jnp.mean(dxhat * xhat, axis=-1, keepdims=True))
        if has_res:
            dx = dx + drt
        return dx, dx, dg

    d = x.shape[1]
    rows = [x, dh] + ([dres] if has_res else [])
    outs = [(d, F32), (d, BF16)] if want_dx else []
    return _rowwise(name, fn, rows, [g], outs, [d])


_DOT_DIMS = {"nn": ((1,), (0,)), "nt": ((1,), (1,)), "tn": ((0,), (0,))}


def _mm(name, a, b, mode, out_dtype, res=None, tm=512, tn=512):
    if mode == "nn":
        (m, k), (k2, n) = a.shape, b.shape
    elif mode == "nt":
        (m, k), (n, k2) = a.shape, b.shape
    else:
        (k, m), (k2, n) = a.shape, b.shape
    assert k == k2, (name, a.shape, b.shape)
    tm, tn = _tile(m, tm), _tile(n, tn)
    dims = (_DOT_DIMS[mode], ((), ()))
    has_res = res is not None

    def body(a_ref, b_ref, *rest):
        acc = lax.dot_general(a_ref[...], b_ref[...], dims, preferred_element_type=F32)
        if has_res:
            acc = acc + rest[0][...]
        rest[-1][...] = acc.astype(out_dtype)

    a_spec = pl.BlockSpec((k, tm), lambda i, j: (0, i)) if mode == "tn" else pl.BlockSpec((tm, k), lambda i, j: (i, 0))
    b_spec = pl.BlockSpec((tn, k), lambda i, j: (j, 0)) if mode == "nt" else pl.BlockSpec((k, tn), lambda i, j: (0, j))
    o_spec = pl.BlockSpec((tm, tn), lambda i, j: (i, j))
    in_specs, args = [a_spec, b_spec], [a, b]
    if has_res:
        in_specs.append(o_spec)
        args.append(res)
    return pl.pallas_call(
        body, name=name, grid=(m // tm, n // tn), in_specs=in_specs, out_specs=o_spec,
        out_shape=jax.ShapeDtypeStruct((m, n), out_dtype), compiler_params=_params("parallel", "arbitrary"),
    )(*args)


def _pool(name, u, reverse, tb=512):
    t = u.shape[0]
    tb = min(tb, t)
    nt = t // tb
    c = SB_WIDTH
    hpb = tb // POOL_HALO

    def body(cur_ref, halo_ref, o_ref):
        i = pl.program_id(0)
        cur = cur_ref[...].astype(F32)
        edge = (i == nt - 1) if reverse else (i == 0)
        halo = jnp.where(edge, 0.0, halo_ref[...].astype(F32))
        col = lax.broadcasted_iota(jnp.int32, (tb + POOL_HALO, c), 1)
        row = lax.broadcasted_iota(jnp.int32, (tb + POOL_HALO, c), 0)
        wcol = jnp.where(col < POOL_GROUP, 2, jnp.where(col < 2 * POOL_GROUP, 4, jnp.where(col < 3 * POOL_GROUP, 8, 16)))
        n = tb + POOL_HALO
        if reverse:
            ext = jnp.concatenate([cur, halo], axis=0)
            tpos = i * tb + row
            ext = ext / jnp.minimum(tpos + 1, wcol).astype(F32)
            shift = lambda a, k: pltpu.roll(a, n - k, 0)
        else:
            ext = jnp.concatenate([halo, cur], axis=0)
            shift = lambda a, k: pltpu.roll(a, k, 0)
        s2 = ext + shift(ext, 1)
        s4 = s2 + shift(s2, 2)
        s8 = s4 + shift(s4, 4)
        s16 = s8 + shift(s8, 8)
        win = jnp.where(wcol == 2, s2, jnp.where(wcol == 4, s4, jnp.where(wcol == 8, s8, s16)))
        if reverse:
            out = win[:tb] - cur
        else:
            tpos = i * tb + row[POOL_HALO:] - POOL_HALO
            out = win[POOL_HALO:] / jnp.minimum(tpos + 1, wcol[POOL_HALO:]).astype(F32) - cur
        o_ref[...] = out.astype(o_ref.dtype)

    if reverse:
        halo_map = lambda i: (jnp.minimum((i + 1) * hpb, t // POOL_HALO - 1), 0)
    else:
        halo_map = lambda i: (jnp.maximum(i * hpb - 1, 0), 0)
    return pl.pallas_call(
        body, name=name, grid=(nt,),
        in_specs=[pl.BlockSpec((tb, c), lambda i: (i, 0)), pl.BlockSpec((POOL_HALO, c), halo_map)],
        out_specs=pl.BlockSpec((tb, c), lambda i: (i, 0)),
        out_shape=jax.ShapeDtypeStruct((t, c), BF16), compiler_params=_params("arbitrary"),
    )(u, u)


def _head_masks(shape):
    lane = lax.broadcasted_iota(jnp.int32, shape, 1)
    return lane < HEAD_DIM, lane >= HEAD_DIM


def _pick(mask, a):
    return jnp.where(mask, a, jnp.zeros_like(a))


def _dot(a, b, mode):
    return lax.dot_general(a, b, (_DOT_DIMS[mode], ((), ())), preferred_element_type=F32)


def _dot_tri(a, tri):
    return _dot(a.astype(BF16), tri, "nn")


def _log_gates(z):
    nz = -z
    l = jnp.log(1.0 + jnp.exp(jnp.minimum(z, nz)))
    ln = jnp.minimum(nz, 0.0) - l
    return ln, z + ln


def _sb_blocks(s, tq, tk):
    tq, tk = min(tq, s), min(tk, s)
    assert tk % tq == 0 and s % tk == 0, (s, tq, tk)
    return tq, tk


def _sb_fwd(proj, kv, tq=SB_TQ, tk=SB_TK):
    s = proj.shape[0]
    tq, tk = _sb_blocks(s, tq, tk)
    npair = SB_WIDTH // LANES

    def body(q_ref, k_ref, v_ref, o_ref, tot_ref):
        r = lax.broadcasted_iota(jnp.int32, (tq, tk), 0)
        cidx = lax.broadcasted_iota(jnp.int32, (tq, tk), 1)
        ahead = cidx - r
        rk = lax.broadcasted_iota(jnp.int32, (tk, tk), 0)
        ck = lax.broadcasted_iota(jnp.int32, (tk, tk), 1)
        tri_gt = (rk > ck).astype(BF16)
        m_a, m_b = _head_masks((tq, LANES))
        mk_a, mk_b = _head_masks((tk, LANES))

        def block(qh, k2, vh, carry, acc, mask):
            ln_full, lsz = _log_gates(_dot(qh, k2, "nt"))
            ln = ln_full if mask is None else jnp.where(mask, ln_full, 0.0)
            w = jnp.exp(lsz + (_dot_tri(ln, tri_gt) + carry))
            if mask is not None:
                w = jnp.where(mask, w, 0.0)
            acc = acc + _dot(w.astype(BF16), vh, "nn")
            return carry + jnp.sum(ln, axis=1, keepdims=True), acc

        def q_block(qi, _):
            q0 = pl.multiple_of(qi * tq, tq)
            q2 = q_ref[pl.ds(q0, tq), :] * ATT_SCALE
            qa, qb = _pick(m_a, q2), _pick(m_b, q2)
            n_full = q0 // tk

            def both(k0, ca, cb, acc, mask):
                k2 = k_ref[pl.ds(k0, tk), :]
                v2 = v_ref[pl.ds(k0, tk), :]
                ca, acc = block(qa, k2, _pick(mk_a, v2), ca, acc, mask)
                cb, acc = block(qb, k2, _pick(mk_b, v2), cb, acc, mask)
                return ca, cb, acc

            kd = pl.multiple_of(n_full * tk, tk)
            zero_c = jnp.zeros((tq, 1), F32)
            carry = both(kd, zero_c, zero_c, jnp.zeros((tq, LANES), F32), ahead < q0 - kd)

            def k_block(step, carry):
                return both(pl.multiple_of((n_full - 1 - step) * tk, tk), *carry, None)

            ca, cb, acc = lax.fori_loop(0, n_full, k_block, carry)
            o_ref[pl.ds(q0, tq), :] = acc.astype(o_ref.dtype)
            tot_ref[0, pl.ds(q0, tq), :] = jnp.broadcast_to(ca, (tq, LANES))
            tot_ref[1, pl.ds(q0, tq), :] = jnp.broadcast_to(cb, (tq, LANES))
            return 0

        lax.fori_loop(0, s // tq, q_block, 0)

    return pl.pallas_call(
        body, name="sb_fwd", grid=(npair,),
        in_specs=[pl.BlockSpec((s, LANES), lambda p: (0, p)), pl.BlockSpec((s, LANES), lambda p: (0, p)),
                  pl.BlockSpec((s, LANES), lambda p: (0, npair + p))],
        out_specs=[pl.BlockSpec((s, LANES), lambda p: (0, p)), pl.BlockSpec((None, 2, s, LANES), lambda p: (p, 0, 0, 0))],
        out_shape=[jax.ShapeDtypeStruct((s, SB_WIDTH), BF16), jax.ShapeDtypeStruct((npair, 2, s, LANES), F32)],
        compiler_params=_params("arbitrary"),
    )(proj, kv, kv)


def _sb_bwd(proj, kv, dcat, tot, tq=SB_TQ, tk=SB_TK):
    s = proj.shape[0]
    tq, tk = _sb_blocks(s, tq, tk)
    npair = SB_WIDTH // LANES

    def body(q_ref, k_ref, v_ref, do_ref, tot_ref, dq_ref, dk_ref, dv_ref, dk_acc, dv_acc):
        r = lax.broadcasted_iota(jnp.int32, (tq, tk), 0)
        cidx = lax.broadcasted_iota(jnp.int32, (tq, tk), 1)
        ahead = cidx - r
        rk = lax.broadcasted_iota(jnp.int32, (tk, tk), 0)
        ck = lax.broadcasted_iota(jnp.int32, (tk, tk), 1)
        tri_gt = (rk > ck).astype(BF16)
        tri_lt = (rk < ck).astype(BF16)
        m_a, m_b = _head_masks((tq, LANES))
        mk_a, mk_b = _head_masks((tk, LANES))
        dk_acc[...] = jnp.zeros_like(dk_acc)
        dv_acc[...] = jnp.zeros_like(dv_acc)

        def block(qh, doh, k2, v2, kh, tot_h, carry, mask):
            c_ln, c_d, dq = carry
            ln_full, lsz = _log_gates(_dot(qh, k2, "nt"))
            ln = ln_full if mask is None else jnp.where(mask, ln_full, 0.0)
            c_ln = c_ln + jnp.sum(ln, axis=1, keepdims=True)
            later = (tot_h - c_ln) + _dot_tri(ln, tri_gt)
            w = jnp.exp(lsz + later)
            if mask is not None:
                w = jnp.where(mask, w, 0.0)
            dlw = _dot(doh, v2, "nt") * w
            before = _dot_tri(dlw, tri_lt) + c_d
            dz = dlw * jnp.exp(ln_full) - before * jnp.exp(lsz)
            if mask is not None:
                dz = jnp.where(mask, dz, 0.0)
            dz = dz.astype(BF16)
            dq = dq + _dot(dz, kh, "nn")
            dk = _dot(dz, qh, "tn")
            dv = _dot(w.astype(BF16), doh, "tn")
            carry = (c_ln, c_d + jnp.sum(dlw, axis=1, keepdims=True), dq)
            return carry, dk, dv

        def q_block(qi, _):
            q0 = pl.multiple_of(qi * tq, tq)
            q2 = q_ref[pl.ds(q0, tq), :] * ATT_SCALE
            do2 = do_ref[pl.ds(q0, tq), :]
            qa, qb = _pick(m_a, q2), _pick(m_b, q2)
            doa, dob = _pick(m_a, do2), _pick(m_b, do2)
            tot_a = tot_ref[0, pl.ds(q0, tq), 0:1]
            tot_b = tot_ref[1, pl.ds(q0, tq), 0:1]
            zero_c = jnp.zeros((tq, 1), F32)
            zero_q = jnp.zeros((tq, LANES), F32)
            n_full = q0 // tk

            def both(k0, ca, cb, mask):
                k2 = k_ref[pl.ds(k0, tk), :]
                v2 = v_ref[pl.ds(k0, tk), :]
                ca, dka, dva = block(qa, doa, k2, v2, _pick(mk_a, k2), tot_a, ca, mask)
                cb, dkb, dvb = block(qb, dob, k2, v2, _pick(mk_b, k2), tot_b, cb, mask)
                dk_acc[pl.ds(k0, tk), :] += dka + dkb
                dv_acc[pl.ds(k0, tk), :] += dva + dvb
                return ca, cb

            def k_block(kj, carry):
                return both(pl.multiple_of(kj * tk, tk), carry[0], carry[1], None)

            init = ((zero_c, zero_c, zero_q), (zero_c, zero_c, zero_q))
            ca, cb = lax.fori_loop(0, n_full, k_block, init)
            kd = pl.multiple_of(n_full * tk, tk)
            ca, cb = both(kd, ca, cb, ahead < q0 - kd)
            dq_ref[pl.ds(q0, tq), :] = ((ca[2] + cb[2]) * ATT_SCALE).astype(dq_ref.dtype)
            return 0

        lax.fori_loop(0, s // tq, q_block, 0)
        dk_ref[...] = dk_acc[...].astype(dk_ref.dtype)
        dv_ref[...] = dv_acc[...].astype(dv_ref.dtype)

    col = lambda off: pl.BlockSpec((s, LANES), functools.partial(lambda p, off: (0, off + p), off=off))
    dq, dk, dv = pl.pallas_call(
        body, name="sb_bwd", grid=(npair,),
        in_specs=[col(0), col(0), col(npair), col(0), pl.BlockSpec((None, 2, s, LANES), lambda p: (p, 0, 0, 0))],
        out_specs=[col(0), col(0), col(0)],
        out_shape=[jax.ShapeDtypeStruct((s, SB_WIDTH), BF16)] * 3,
        scratch_shapes=[pltpu.VMEM((s, LANES), F32), pltpu.VMEM((s, LANES), F32)],
        compiler_params=_params("arbitrary"),
    )(proj, kv, kv, dcat, tot)
    return dq, jnp.concatenate([dk, dv], axis=1)


def _mem_fwd(name, proj, mkv, tq=512):
    s = proj.shape[0]
    tq = min(tq, s)
    qblk = SB_WIDTH // MEM_WIDTH

    def body(q_ref, kv_ref, o_ref):
        m_a, m_b = _head_masks((tq, LANES))
        mk_a, mk_b = _head_masks((kv_ref.shape[0], LANES))
        for p in range(MEM_WIDTH // LANES):
            q2 = q_ref[:, p * LANES:(p + 1) * LANES]
            k2 = kv_ref[:, p * LANES:(p + 1) * LANES]
            v2 = kv_ref[:, MEM_WIDTH + p * LANES:MEM_WIDTH + (p + 1) * LANES]
            acc = jnp.zeros((tq, LANES), F32)
            for mq, mk in ((m_a, mk_a), (m_b, mk_b)):
                logits = _dot(_pick(mq, q2), k2, "nt") * ATT_SCALE
                e = jnp.exp(logits - jnp.max(logits, axis=-1, keepdims=True))
                prob = e / jnp.sum(e, axis=-1, keepdims=True)
                acc = acc + _dot(prob.astype(BF16), _pick(mk, v2), "nn")
            o_ref[:, p * LANES:(p + 1) * LANES] = acc.astype(o_ref.dtype)

    return pl.pallas_call(
        body, name=name, grid=(s // tq,),
        in_specs=[pl.BlockSpec((tq, MEM_WIDTH), lambda i: (i, qblk)), pl.BlockSpec(mkv.shape, lambda i: (0, 0))],
        out_specs=pl.BlockSpec((tq, MEM_WIDTH), lambda i: (i, 0)),
        out_shape=jax.ShapeDtypeStruct((s, MEM_WIDTH), BF16), compiler_params=_params("arbitrary"),
    )(proj, mkv)


def _mem_bwd(name, proj, mkv, dcat, tq=512):
    s = proj.shape[0]
    tq = min(tq, s)
    qblk = SB_WIDTH // MEM_WIDTH

    def body(q_ref, kv_ref, do_ref, dq_ref, dkv_ref):
        @pl.when(pl.program_id(0) == 0)
        def _():
            dkv_ref[...] = jnp.zeros_like(dkv_ref)

        m_a, m_b = _head_masks((tq, LANES))
        for p in range(MEM_WIDTH // LANES):
            ksl = slice(p * LANES, (p + 1) * LANES)
            vsl = slice(MEM_WIDTH + p * LANES, MEM_WIDTH + (p + 1) * LANES)
            q2, do2 = q_ref[:, ksl], do_ref[:, ksl]
            k2, v2 = kv_ref[:, ksl], kv_ref[:, vsl]
            mk_a, mk_b = _head_masks(k2.shape)
            dq = jnp.zeros((tq, LANES), F32)
            dk = jnp.zeros(k2.shape, F32)
            dv = jnp.zeros(k2.shape, F32)
            for mq, mk in ((m_a, mk_a), (m_b, mk_b)):
                qh, doh = _pick(mq, q2), _pick(mq, do2)
                logits = _dot(qh, k2, "nt") * ATT_SCALE
                e = jnp.exp(logits - jnp.max(logits, axis=-1, keepdims=True))
                prob = e / jnp.sum(e, axis=-1, keepdims=True)
                dp = _dot(doh, v2, "nt")
                ds = prob * (dp - jnp.sum(dp * prob, axis=-1, keepdims=True)) * ATT_SCALE
                ds = ds.astype(BF16)
                dq = dq + _dot(ds, _pick(mk, k2), "nn")
                dk = dk + _dot(ds, qh, "tn")
                dv = dv + _dot(prob.astype(BF16), doh, "tn")
            dq_ref[:, ksl] = dq.astype(dq_ref.dtype)
            dkv_ref[:, ksl] += dk
            dkv_ref[:, vsl] += dv

    return pl.pallas_call(
        body, name=name, grid=(s // tq,),
        in_specs=[pl.BlockSpec((tq, MEM_WIDTH), lambda i: (i, qblk)), pl.BlockSpec(mkv.shape, lambda i: (0, 0)),
                  pl.BlockSpec((tq, MEM_WIDTH), lambda i: (i, qblk))],
        out_specs=[pl.BlockSpec((tq, MEM_WIDTH), lambda i: (i, 0)), pl.BlockSpec(mkv.shape, lambda i: (0, 0))],
        out_shape=[jax.ShapeDtypeStruct((s, MEM_WIDTH), BF16), jax.ShapeDtypeStruct(mkv.shape, F32)],
        compiler_params=_params("arbitrary"),
    )(proj, mkv, dcat)


def _swiglu_fwd(name, gu):
    f = gu.shape[1] // 2

    def fn(g, u):
        g, u = g.astype(F32), u.astype(F32)
        return g * jax.nn.sigmoid(g) * u
    return _rowwise(name, fn, [(gu, (f, 0)), (gu, (f, 1))], [], [(f, BF16)], tb=256)[0]


def _swiglu_bwd(name, gu, dact):
    f = gu.shape[1] // 2

    def fn(g, u, d):
        g, u, d = g.astype(F32), u.astype(F32), d.astype(F32)
        sg = jax.nn.sigmoid(g)
        silu = g * sg
        return jnp.concatenate([d * u * (sg + silu * (1.0 - sg)), d * silu], axis=1)
    return _rowwise(name, fn, [(gu, (f, 0)), (gu, (f, 1)), dact], [], [(2 * f, BF16)], tb=256)[0]


def _ffn_fwd(tag, x, norm, w_gu, w_down):
    h = _rms_fwd(tag + "_ffn_norm", x, norm)
    gu = _mm(tag + "_gu", h, w_gu, "nn", BF16)
    act = _swiglu_fwd(tag + "_swiglu", gu)
    out = _mm(tag + "_down", act, w_down, "nn", F32, res=x, tn=1024)
    return out, (h, gu, act)


def _ffn_bwd(tag, x, norm, w_gu, w_down, saved, dout, dout_bf):
    h, gu, act = saved
    dact = _mm(tag + "_down_dx", dout_bf, w_down, "nt", BF16, tn=1408)
    g_down = _mm(tag + "_down_dw", act, dout_bf, "tn", BF16, tm=256, tn=1024)
    dgu = _swiglu_bwd(tag + "_swiglu_bwd", gu, dact)
    g_gu = _mm(tag + "_gu_dw", h, dgu, "tn", BF16, tm=1024)
    dh = _mm(tag + "_gu_dx", dgu, w_gu, "nt", F32)
    dx, dx_bf, g_norm = _rms_bwd(tag + "_ffn_norm_bwd", x, norm, dh, dres=dout)
    return dx, dx_bf, g_gu, g_down, g_norm


def _mem_kv(tag, mem_n, w_mem_kv):
    return _mm(tag + "_memkv", mem_n, w_mem_kv, "nn", BF16)


def _mem_kv_bwd(tag, mem, mem_norm, mem_n, w_mem_kv, dmkv):
    dmkv = dmkv.astype(BF16)
    g_w = _mm(tag + "_memkv_dw", mem_n, dmkv, "tn", BF16)
    dmem_n = _mm(tag + "_memkv_dx", dmkv, w_mem_kv, "nt", F32)
    (g_norm,) = _rms_bwd(tag + "_memnorm_bwd", mem, mem_norm, dmem_n, want_dx=False)
    return g_w, g_norm


def _block_diag(w_group):
    z = jnp.zeros((POOL_GROUP, POOL_GROUP), w_group.dtype)
    return jnp.concatenate(
        [jnp.concatenate([w_group[g] if h == g else z for h in range(4)], axis=1) for g in range(4)], axis=0)


def _local_step(x, mem, target, w):
    g = {}
    mem_n = _rms_fwd("mem_norm", mem, w["mem_norm"])
    h_a = _rms_fwd("a_mix_norm", x, w["a_norm_mix"])
    proj_a = _mm("a_in", h_a, w["a_w_in"], "nn", F32)
    pooled = _pool("a_pool", proj_a, reverse=False)
    w_bd = _block_diag(w["a_w_group"])
    g_pre = _mm("a_group", pooled, w_bd, "nn", BF16, tn=768)
    mkv_a = _mem_kv("a", mem_n, w["a_w_mem_kv"])
    proj_a_bf = proj_a.astype(BF16)
    mem_a = _mem_fwd("a_mem_attn", proj_a_bf, mkv_a)
    cat_a = _rowwise("a_cat", lambda gp, mo, sc: jnp.concatenate([gp.astype(F32) * sc, mo.astype(F32)], axis=1),
                     [g_pre, mem_a], [w["a_scale"]], [(1024, BF16)])[0]
    x1 = _mm("a_out", cat_a, w["a_w_out"], "nn", F32, res=x, tn=1024)
    x2, ffn_a = _ffn_fwd("a", x1, w["a_norm_ffn"], w["a_w_gu"], w["a_w_down"])
    h_k = _rms_fwd("kv_norm", x2, w["kv_norm"])
    kv = _mm("kv_proj", h_k, w["w_kv"], "nn", BF16)
    h_b = _rms_fwd("b_mix_norm", x2, w["b_norm_mix"])
    proj_b = _mm("b_q", h_b, w["b_w_q"], "nn", BF16)
    sb_out, tot = _sb_fwd(proj_b, kv)
    mkv_b = _mem_kv("b", mem_n, w["b_w_mem_kv"])
    mem_b = _mem_fwd("b_mem_attn", proj_b, mkv_b)
    cat_b = jnp.concatenate([sb_out, mem_b], axis=1)
    x3 = _mm("b_out", cat_b, w["b_w_out"], "nn", F32, res=x2, tn=1024)
    x4, ffn_b = _ffn_fwd("b", x3, w["b_norm_ffn"], w["b_w_gu"], w["b_w_down"])

    d = x.shape[1]

    def head(xt, tt, gt):
        rstd = lax.rsqrt(jnp.mean(xt * xt, axis=-1, keepdims=True) + EPS)
        xhat = xt * rstd
        err = xhat * gt - tt
        loss = 0.5 * jnp.sum(jnp.sum(err * err, axis=1, keepdims=True), axis=0, keepdims=True) / d
        dy = err / d
        dxhat = dy * gt
        dx = rstd * (dxhat - xhat * jnp.mean(dxhat * xhat, axis=-1, keepdims=True))
        return dx, dx, jnp.sum(dy * xhat, axis=0, keepdims=True), jnp.broadcast_to(loss, (1, LANES))

    dx4, dx4_bf, g["final_norm"], loss = _rowwise(
        "loss_head", head, [x4, target], [w["final_norm"]], [(d, F32), (d, BF16)], [d, LANES])

    dx3, dx3_bf, g["b_w_gu"], g["b_w_down"], g["b_norm_ffn"] = _ffn_bwd(
        "b", x3, w["b_norm_ffn"], w["b_w_gu"], w["b_w_down"], ffn_b, dx4, dx4_bf)
    dcat_b = _mm("b_out_dx", dx3_bf, w["b_w_out"], "nt", BF16, tn=1024)
    g["b_w_out"] = _mm("b_out_dw", cat_b, dx3_bf, "tn", BF16, tm=1024, tn=1024)
    dq_sb, dkv = _sb_bwd(proj_b, kv, dcat_b, tot)
    dq_mem_b, dmkv_b = _mem_bwd("b_mem_attn_bwd", proj_b, mkv_b, dcat_b)
    dproj_b = jnp.concatenate([dq_sb, dq_mem_b], axis=1)
    g["b_w_q"] = _mm("b_q_dw", h_b, dproj_b, "tn", BF16, tm=1024, tn=1024)
    dh_b = _mm("b_q_dx", dproj_b, w["b_w_q"], "nt", F32, tn=1024)
    dx2, _, g["b_norm_mix"] = _rms_bwd("b_mix_norm_bwd", x2, w["b_norm_mix"], dh_b, dres=dx3)
    g["b_w_mem_kv"], g_memnorm_b = _mem_kv_bwd("b", mem, w["mem_norm"], mem_n, w["b_w_mem_kv"], dmkv_b)
    g["w_kv"] = _mm("kv_proj_dw", h_k, dkv, "tn", BF16, tm=1024)
    dh_k = _mm("kv_proj_dx", dkv, w["w_kv"], "nt", F32, tn=1024)
    dx2, dx2_bf, g["kv_norm"] = _rms_bwd("kv_norm_bwd", x2, w["kv_norm"], dh_k, dres=dx2)

    dx1, dx1_bf, g["a_w_gu"], g["a_w_down"], g["a_norm_ffn"] = _ffn_bwd(
        "a", x1, w["a_norm_ffn"], w["a_w_gu"], w["a_w_down"], ffn_a, dx2, dx2_bf)
    dcat_a = _mm("a_out_dx", dx1_bf, w["a_w_out"], "nt", BF16, tn=1024)
    g["a_w_out"] = _mm("a_out_dw", cat_a, dx1_bf, "tn", BF16, tm=1024, tn=1024)

    def scale_bwd(dc, gp, sc):
        dc, gp = dc.astype(F32), gp.astype(F32)
        return dc * sc, jnp.sum(dc * gp, axis=0, keepdims=True)

    dg_pre, g["a_scale"] = _rowwise("a_scale_bwd", scale_bwd, [(dcat_a, (SB_WIDTH, 0)), g_pre], [w["a_scale"]],
                                    [(SB_WIDTH, BF16)], [SB_WIDTH])
    g_bd = _mm("a_group_dw", pooled, dg_pre, "tn", F32, tm=768, tn=768)
    g["a_w_group"] = jnp.stack([g_bd[i * POOL_GROUP:(i + 1) * POOL_GROUP, i * POOL_GROUP:(i + 1) * POOL_GROUP]
                                for i in range(4)])
    dpooled = _mm("a_group_dx", dg_pre, w_bd, "nt", F32, tn=768)
    du_pool = _pool("a_pool_bwd", dpooled, reverse=True)
    dq_mem_a, dmkv_a = _mem_bwd("a_mem_attn_bwd", proj_a_bf, mkv_a, dcat_a)
    dproj_a = jnp.concatenate([du_pool, dq_mem_a], axis=1)
    g["a_w_in"] = _mm("a_in_dw", h_a, dproj_a, "tn", BF16, tm=1024, tn=1024)
    dh_a = _mm("a_in_dx", dproj_a, w["a_w_in"], "nt", F32, tn=1024)
    grad_x, _, g["a_norm_mix"] = _rms_bwd("a_mix_norm_bwd", x, w["a_norm_mix"], dh_a, dres=dx1)
    g["a_w_mem_kv"], g_memnorm_a = _mem_kv_bwd("a", mem, w["mem_norm"], mem_n, w["a_w_mem_kv"], dmkv_a)
    g["mem_norm"] = g_memnorm_a + g_memnorm_b
    return loss, grad_x, g


ROW_SHARDED = ("a_w_in", "a_w_mem_kv", "a_w_out", "a_w_down", "b_w_q", "b_w_mem_kv", "b_w_out", "b_w_down")
COL_SHARDED = ("a_w_gu", "w_kv", "b_w_gu")
BIG = ("a_w_in", "a_w_mem_kv", "a_w_out", "a_w_gu", "a_w_down", "w_kv", "b_w_q", "b_w_mem_kv", "b_w_out", "b_w_gu",
       "b_w_down")
N_CHIPS = 4
N_DEV = 8


def _position():
    x, y, c = lax.axis_index("x"), lax.axis_index("y"), lax.axis_index("c")
    other_chips = [(1 - x, y), (x, 1 - y), (1 - x, 1 - y)]
    return x, y, c, other_chips


def _remote(src, dst, send_sem, recv_sem, device):
    return pltpu.make_async_remote_copy(src_ref=src, dst_ref=dst, send_sem=send_sem, recv_sem=recv_sem,
                                        device_id=device, device_id_type=MESH)


def _comm_call(name, body, args, out_shape, n_remote, n_local):
    return pl.pallas_call(
        body, name=name, in_specs=[ANY] * len(args), out_specs=[ANY] * len(out_shape), out_shape=out_shape,
        scratch_shapes=[pltpu.SemaphoreType.DMA((n_remote,)), pltpu.SemaphoreType.DMA((n_remote,)),
                        pltpu.SemaphoreType.DMA((n_local,))],
    )(*args)


def _row_chunks(nrows, row_bytes):
    per = max(16, (COPY_BYTES // row_bytes) // 16 * 16)
    return [(r0, min(per, nrows - r0)) for r0 in range(0, nrows, per)]


def _rows(ref, start, size, lead=()):
    if isinstance(start, int):
        return ref.at[(*lead, pl.ds(start, size))]
    return ref.at[(*lead, pl.ds(pl.multiple_of(start, 16), size))]


class _Copies:
    def __init__(self, send, recv, loc):
        self.send, self.recv, self.loc = send, recv, loc
        self.n_remote = 0
        self.locals, self.remotes = [], []

    def local(self, src, dst):
        cp = pltpu.make_async_copy(src, dst, self.loc.at[len(self.locals)])
        cp.start()
        self.locals.append(cp)

    def slot(self):
        self.n_remote += 1
        return self.n_remote - 1

    def remote(self, k, src, dst, device, start=True):
        cp = _remote(src, dst, self.send.at[k], self.recv.at[k], device)
        if start:
            cp.start()
            self.remotes.append(cp)
        return cp

    def finish(self):
        for cp in self.remotes:
            cp.wait_send()
        for cp in self.locals:
            cp.wait()


def _shard_cols(ref, row_sharded, cdim, chip):
    if row_sharded:
        return ref
    return ref.at[:, pl.ds(pl.multiple_of(chip * cdim, LANES), cdim)]


def _gather_weights(shards):
    names = list(shards)
    n = len(names)
    args = [shards[name] for name in names]
    row_sharded = [name in ROW_SHARDED for name in names]
    out_shape, plan = [], []
    for name in names:
        r, cdim = shards[name].shape
        full = (N_CHIPS * r, cdim) if name in ROW_SHARDED else (r, N_CHIPS * cdim)
        out_shape.append(jax.ShapeDtypeStruct(full, BF16))
        plan.append(_row_chunks(r // 2, cdim * 2))
    n_chunks = sum(len(p) for p in plan)

    def body(*refs):
        src, dst = refs[:n], refs[n:2 * n]
        cps = _Copies(*refs[2 * n:])
        x, y, c, chips = _position()
        me = 2 * x + y

        def window(i, chip, half, r0, size):
            r, cdim = src[i].shape
            base = (chip * r if row_sharded[i] else 0) + half * (r // 2) + r0
            return _rows(_shard_cols(dst[i], row_sharded[i], cdim, chip), base, size)

        def mine(i, half, r0, size):
            return _rows(src[i], half * (src[i].shape[0] // 2) + r0, size)

        ici = {}
        for i in range(n):
            for r0, size in plan[i]:
                for half in range(2):
                    cps.local(mine(i, half, r0, size), window(i, me, half, r0, size))
                for k, (px, py) in enumerate(chips):
                    ici[i, r0, k] = cps.slot()
                    cps.remote(ici[i, r0, k], mine(i, c, r0, size), window(i, me, c, r0, size), (px, py, c))
        fwd = {}
        for i in range(n):
            for r0, size in plan[i]:
                for k, (px, py) in enumerate(chips):
                    chip = 2 * px + py
                    cps.remote(ici[i, r0, k], mine(i, c, r0, size), window(i, chip, c, r0, size), (px, py, c),
                               start=False).wait_recv()
                    fwd[i, r0, k] = cps.slot()
                    cps.remote(fwd[i, r0, k], window(i, chip, c, r0, size), window(i, chip, c, r0, size), (x, y, 1 - c))
        for i in range(n):
            for r0, size in plan[i]:
                for k, (px, py) in enumerate(chips):
                    cps.remote(fwd[i, r0, k], mine(i, c, r0, size), window(i, 2 * px + py, 1 - c, r0, size),
                               (x, y, 1 - c), start=False).wait_recv()
        cps.finish()

    outs = _comm_call("gather_weights", body, args, out_shape, 6 * n_chunks, 2 * n_chunks)
    return dict(zip(names, outs))


def _add2(name, a, b):
    return _rowwise(name, lambda p, q: p.astype(F32) + q.astype(F32), [a, b], [], [(a.shape[1], BF16)], tb=256)[0]


def _sum_slots(name, b, tb=256):
    _, r, ncol = b.shape
    tb = _row_tile(r, tb)

    def body(b_ref, o_ref):
        acc = b_ref[0].astype(F32)
        for k in range(1, N_CHIPS):
            acc = acc + b_ref[k].astype(F32)
        o_ref[...] = acc

    return pl.pallas_call(
        body, name=name, grid=(r // tb,), in_specs=[pl.BlockSpec((N_CHIPS, tb, ncol), lambda i: (0, i, 0))],
        out_specs=pl.BlockSpec((tb, ncol), lambda i: (i, 0)), out_shape=jax.ShapeDtypeStruct((r, ncol), F32),
        compiler_params=_params("arbitrary"),
    )(b)


def _reduce_scatter(grads):
    names = list(grads)
    n = len(names)
    row_sharded = [name in ROW_SHARDED for name in names]
    shard_shape = []
    for name in names:
        rows, cols = grads[name].shape
        shard_shape.append((rows // N_CHIPS, cols) if name in ROW_SHARDED else (rows, cols // N_CHIPS))

    half_shape = [(r // 2, cdim) for r, cdim in shard_shape]

    args, shape_a, plan_a = [], [], []
    for i, name in enumerate(names):
        r, cdim = shard_shape[i]
        if row_sharded[i]:
            args.append(grads[name].reshape(N_CHIPS, 2, r // 2, cdim))
            shape_a.append((N_CHIPS, r // 2, cdim))
            plan_a.append([(j, r0, size) for j in range(N_CHIPS) for r0, size in _row_chunks(r // 2, cdim * 2)])
        else:
            args.append(grads[name])
            shape_a.append((r // 2, N_CHIPS * cdim))
            plan_a.append([(None, r0, size) for r0, size in _row_chunks(r // 2, N_CHIPS * cdim * 2)])
    out_shape = [jax.ShapeDtypeStruct(s, BF16) for s in shape_a] * 2
    n_a = sum(len(p) for p in plan_a)

    def body_a(*refs):
        src, own, sib = refs[:n], refs[n:2 * n], refs[2 * n:3 * n]
        cps = _Copies(*refs[3 * n:])
        x, y, c, _ = _position()
        waits = []
        for i in range(n):
            for j, r0, size in plan_a[i]:
                if row_sharded[i]:
                    piece = lambda half: _rows(src[i], r0, size, lead=(j, half))
                    out = lambda ref: _rows(ref, r0, size, lead=(j,))
                else:
                    piece = lambda half: _rows(src[i], half * half_shape[i][0] + r0, size)
                    out = lambda ref: _rows(ref, r0, size)
                cps.local(piece(c), out(own[i]))
                waits.append(cps.remote(cps.slot(), piece(1 - c), out(sib[i]), (x, y, 1 - c)))
        for cp in waits:
            cp.wait_recv()
        cps.finish()

    outs = _comm_call("grads_pair_exchange", body_a, args, out_shape, n_a, n_a)
    partial = []
    for i, name in enumerate(names):
        s = shape_a[i]
        flat = (s[0] * s[1], s[2]) if row_sharded[i] else s
        partial.append(_add2(name + "_pair_sum", outs[i].reshape(flat), outs[n + i].reshape(flat)).reshape(s))

    out_shape = [jax.ShapeDtypeStruct((N_CHIPS,) + s, BF16) for s in half_shape]
    plan_b = [_row_chunks(r, cdim * 2) for r, cdim in half_shape]
    n_b = sum(len(p) for p in plan_b)

    def body_b(*refs):
        src, dst = refs[:n], refs[n:2 * n]
        cps = _Copies(*refs[2 * n:])
        x, y, c, chips = _position()
        me = 2 * x + y

        def shard(i, chip, r0, size):
            if row_sharded[i]:
                return _rows(src[i], r0, size, lead=(chip,))
            return _rows(_shard_cols(src[i], False, half_shape[i][1], chip), r0, size)

        waits = []
        for i in range(n):
            for r0, size in plan_b[i]:
                cps.local(shard(i, me, r0, size), _rows(dst[i], r0, size, lead=(3,)))
                for k, (px, py) in enumerate(chips):
                    waits.append(cps.remote(cps.slot(), shard(i, 2 * px + py, r0, size), _rows(dst[i], r0, size, lead=(k,)),
                                            (px, py, c)))
        for cp in waits:
            cp.wait_recv()
        cps.finish()

    outs = _comm_call("grads_chip_exchange", body_b, partial, out_shape, 3 * n_b, n_b)
    halves = [_sum_slots(name + "_chip_sum", o) for name, o in zip(names, outs)]

    out_shape = [jax.ShapeDtypeStruct((2,) + s, F32) for s in half_shape]
    plan_c = [_row_chunks(r, cdim * 4) for r, cdim in half_shape]
    n_c = sum(len(p) for p in plan_c)

    def body_c(*refs):
        src, dst = refs[:n], refs[n:2 * n]
        cps = _Copies(*refs[2 * n:])
        x, y, c, _ = _position()
        waits = []
        for i in range(n):
            for r0, size in plan_c[i]:
                cps.local(_rows(src[i], r0, size), _rows(dst[i], r0, size, lead=(c,)))
                waits.append(cps.remote(cps.slot(), _rows(src[i], r0, size), _rows(dst[i], r0, size, lead=(c,)),
                                        (x, y, 1 - c)))
        for cp in waits:
            cp.wait_recv()
        cps.finish()

    outs = _comm_call("grads_pair_share", body_c, halves, out_shape, n_c, n_c)
    return {name: o.reshape(2 * o.shape[1], o.shape[2]) for name, o in zip(names, outs)}


def _all_reduce_small(name, v):
    rows, cols = v.shape

    def body(v_ref, o_ref, buf, send, recv):
        x, y, c, _ = _position()
        me = 4 * x + 2 * y + c
        buf[me] = v_ref[...]
        peers = []
        for k in range(1, N_DEV):
            fx, fy, fc = (k >> 2) & 1, (k >> 1) & 1, k & 1
            peers.append((x + fx - 2 * fx * x, y + fy - 2 * fy * y, c + fc - 2 * fc * c))
        sends = []
        for k, peer in enumerate(peers):
            cp = _remote(v_ref, buf.at[me], send.at[k], recv.at[k], peer)
            cp.start()
            sends.append(cp)
        for k, (px, py, pc) in enumerate(peers):
            _remote(v_ref, buf.at[4 * px + 2 * py + pc], send.at[k], recv.at[k], (px, py, pc)).wait_recv()
        for cp in sends:
            cp.wait_send()
        acc = buf[0]
        for d in range(1, N_DEV):
            acc = acc + buf[d]
        o_ref[...] = acc

    vm = pl.BlockSpec(memory_space=pltpu.VMEM)
    return pl.pallas_call(
        body, name=name, in_specs=[vm], out_specs=vm, out_shape=jax.ShapeDtypeStruct(v.shape, F32),
        scratch_shapes=[pltpu.VMEM((N_DEV, rows, cols), F32), pltpu.SemaphoreType.DMA((N_DEV - 1,)),
                        pltpu.SemaphoreType.DMA((N_DEV - 1,))],
        compiler_params=pltpu.CompilerParams(vmem_limit_bytes=VMEM_LIMIT),
    )(v)


def _adamw(name, w, g, m, v):
    def fn(wt, gt, mt, vt):
        mt = ADAM_B1 * mt + (1.0 - ADAM_B1) * gt
        vt = ADAM_B2 * vt + (1.0 - ADAM_B2) * (gt * gt)
        m_hat = mt / (1.0 - ADAM_B1 ** ADAM_STEP)
        v_hat = vt / (1.0 - ADAM_B2 ** ADAM_STEP)
        delta = -ADAM_LR * (m_hat / (jnp.sqrt(v_hat) + ADAM_EPS) + ADAM_WD * wt)
        return delta, mt, vt
    n = w.shape[1]
    return _rowwise(name, fn, [w, g, m, v], [], [(n, F32)] * 3, tb=256)


WEIGHTS = ("mem_norm", "a_norm_mix", "a_w_in", "a_w_group", "a_scale", "a_w_mem_kv", "a_w_out", "a_norm_ffn", "a_w_gu",
           "a_w_down", "kv_norm", "w_kv", "b_norm_mix", "b_w_q", "b_w_mem_kv", "b_w_out", "b_norm_ffn", "b_w_gu",
           "b_w_down", "final_norm")
REPLICATED_VECS = ("mem_norm", "kv_norm", "b_norm_mix", "b_norm_ffn", "final_norm")
SHARDED_VECS = ("a_norm_mix", "a_norm_ffn", "a_scale")
D_MODEL = 1024
GROUP_ROWS = 4 * POOL_GROUP * POOL_GROUP // D_MODEL


def _row(v):
    v = v.reshape(1, -1).astype(F32)
    return jnp.pad(v, ((0, 0), (0, D_MODEL - v.shape[1])))


def _pack_small(t):
    rows = [_row(t[k]) for k in REPLICATED_VECS]
    rows.append(_row(jnp.concatenate([t[k].reshape(-1) for k in SHARDED_VECS])))
    rows.append(jnp.zeros((2, D_MODEL), F32))
    rows.append(t["a_w_group"].reshape(GROUP_ROWS, D_MODEL).astype(F32))
    return jnp.concatenate(rows, axis=0)


def _unpack_small(p, like):
    out = {k: p[i, :].reshape(like[k].shape) for i, k in enumerate(REPLICATED_VECS)}
    off = 0
    for k in SHARDED_VECS:
        size = like[k].size
        out[k] = p[len(REPLICATED_VECS), off:off + size].reshape(like[k].shape)
        off += size
    out["a_w_group"] = p[len(REPLICATED_VECS) + 3:, :].reshape(like["a_w_group"].shape)
    return out


def kernel(x, mem, mem_norm, a_norm_mix, a_w_in, a_w_group, a_scale, a_w_mem_kv, a_w_out, a_norm_ffn, a_w_gu, a_w_down, kv_norm, w_kv, b_norm_mix, b_w_q, b_w_mem_kv, b_w_out, b_norm_ffn, b_w_gu, b_w_down, final_norm, loss_target, m_mem_norm, m_a_norm_mix, m_a_w_in, m_a_w_group, m_a_scale, m_a_w_mem_kv, m_a_w_out, m_a_norm_ffn, m_a_w_gu, m_a_w_down, m_kv_norm, m_w_kv, m_b_norm_mix, m_b_w_q, m_b_w_mem_kv, m_b_w_out, m_b_norm_ffn, m_b_w_gu, m_b_w_down, m_final_norm, v_mem_norm, v_a_norm_mix, v_a_w_in, v_a_w_group, v_a_scale, v_a_w_mem_kv, v_a_w_out, v_a_norm_ffn, v_a_w_gu, v_a_w_down, v_kv_norm, v_w_kv, v_b_norm_mix, v_b_w_q, v_b_w_mem_kv, v_b_w_out, v_b_norm_ffn, v_b_w_gu, v_b_w_down, v_final_norm):
    given = dict(locals())
    wl = {k: given[k] for k in WEIGHTS}
    ml = {k: given["m_" + k] for k in WEIGHTS}
    vl = {k: given["v_" + k] for k in WEIGHTS}
    chip = 2 * lax.axis_index("x") + lax.axis_index("y")

    def mat(a):
        return a.reshape(a.shape[-2], a.shape[-1])

    full = _gather_weights({k: mat(wl[k]).astype(BF16) for k in BIG})
    gains = jnp.zeros((8, D_MODEL), F32)
    for i, k in enumerate(SHARDED_VECS):
        part = wl[k].reshape(1, -1)
        width = part.shape[1]
        gains = lax.dynamic_update_slice(gains, part, (i, chip * width))
    gains = _all_reduce_small("gains_all_gather", gains) * 0.5
    w = dict(full)
    for k in REPLICATED_VECS:
        w[k] = wl[k].reshape(1, D_MODEL)
    w["a_norm_mix"], w["a_norm_ffn"] = gains[0:1], gains[1:2]
    w["a_scale"] = gains[2:3, :SB_WIDTH]
    w["a_w_group"] = wl["a_w_group"][0].astype(BF16)

    loss, grad_x, g = _local_step(x[0], mem[0], loss_target[0], w)

    red = _reduce_scatter({k: g[k] for k in BIG})
    small = jnp.concatenate(
        [_row(g[k]) for k in REPLICATED_VECS] + [_row(g[k]) for k in SHARDED_VECS] + [_row(loss)]
        + [jnp.zeros((7, D_MODEL), F32), g["a_w_group"].reshape(GROUP_ROWS, D_MODEL)], axis=0)
    small = _all_reduce_small("small_grads_all_reduce", small)
    gs = {k: small[i] for i, k in enumerate(REPLICATED_VECS)}
    for i, k in enumerate(SHARDED_VECS):
        width = wl[k].shape[-1]
        gs[k] = lax.dynamic_slice(small[len(REPLICATED_VECS) + i], (chip * width,), (width,))
    gs["a_w_group"] = small[16:]
    total_loss = small[8, 0]

    out_g, out_d, out_m, out_v = {}, {}, {}, {}
    for k in BIG:
        shape = wl[k].shape
        out_g[k] = red[k].reshape(shape)
        d, nm, nv = _adamw(k + "_adamw", mat(wl[k]), red[k], mat(ml[k]), mat(vl[k]))
        out_d[k], out_m[k], out_v[k] = d.reshape(shape), nm.reshape(shape), nv.reshape(shape)
    small_names = REPLICATED_VECS + SHARDED_VECS + ("a_w_group",)
    d, nm, nv = _adamw("small_adamw", _pack_small(wl), _pack_small(gs), _pack_small(ml), _pack_small(vl))
    like = {k: wl[k] for k in small_names}
    for dst, p in ((out_d, d), (out_m, nm), (out_v, nv)):
        dst.update(_unpack_small(p, like))
    for k in small_names:
        out_g[k] = gs[k].reshape(wl[k].shape)

    return (total_loss, grad_x[None], *[out_g[k] for k in WEIGHTS], *[out_d[k] for k in WEIGHTS],
            *[out_m[k] for k in WEIGHTS], *[out_v[k] for k in WEIGHTS])
```

```python
import functools

import jax
import jax.numpy as jnp
from jax import lax
from jax.experimental import pallas as pl
from jax.experimental.pallas import tpu as pltpu

F32 = jnp.float32
BF16 = jnp.bfloat16

HEAD_DIM = 64
SB_WIDTH = 768
MEM_WIDTH = 256
POOL_WINDOWS = (2, 4, 8, 16)
POOL_GROUP = 192
POOL_HALO = 16
EPS = 1e-6
ATT_SCALE = HEAD_DIM ** -0.5
ADAM_LR, ADAM_B1, ADAM_B2, ADAM_EPS, ADAM_WD, ADAM_STEP = 0.001, 0.9, 0.999, 1e-08, 0.01, 10

LANES = 128
SB_TQ, SB_TK = 512, 512
VMEM_LIMIT = 56 * 1024 * 1024
MESH = pl.DeviceIdType.MESH
COPY_BYTES = 512 * 1024
ANY = pl.BlockSpec(memory_space=pl.ANY)


def _params(*sem):
    return pltpu.CompilerParams(dimension_semantics=sem, vmem_limit_bytes=VMEM_LIMIT)


def _tile(n, pref):
    if n <= pref:
        return n
    best = None
    for t in range(LANES, pref + 1, LANES):
        if n % t == 0:
            best = t
    assert best is not None, (n, pref)
    return best


def _row_tile(t, pref):
    if t <= pref:
        return t
    for tb in range(pref - pref % 16, 0, -16):
        if t % tb == 0:
            return tb
    raise ValueError((t, pref))


def _rowwise(name, fn, rows, vecs, row_outs, sum_outs=(), tb=512):
    norm_rows = []
    for r in rows:
        if isinstance(r, tuple):
            arr, (bc, cb) = r
        else:
            arr, (bc, cb) = r, (r.shape[1], 0)
        norm_rows.append((arr, bc, cb))
    t = norm_rows[0][0].shape[0]
    tb = _row_tile(t, tb)
    n_in, n_ro = len(norm_rows) + len(vecs), len(row_outs)

    def body(*refs):
        ins = [r[...] for r in refs[:n_in]]
        outs = fn(*ins)
        if not isinstance(outs, tuple):
            outs = (outs,)
        for o_ref, o in zip(refs[n_in:n_in + n_ro], outs[:n_ro]):
            o_ref[...] = o.astype(o_ref.dtype)
        for s_ref, s in zip(refs[n_in + n_ro:], outs[n_ro:]):
            @pl.when(pl.program_id(0) == 0)
            def _():
                s_ref[...] = jnp.zeros_like(s_ref)
            s_ref[...] += s

    in_specs = [pl.BlockSpec((tb, bc), functools.partial(lambda i, cb: (i, cb), cb=cb)) for _, bc, cb in norm_rows]
    in_specs += [pl.BlockSpec(v.shape, lambda i: (0, 0)) for v in vecs]
    out_specs = [pl.BlockSpec((tb, c), lambda i: (i, 0)) for c, _ in row_outs]
    out_specs += [pl.BlockSpec((1, c), lambda i: (0, 0)) for c in sum_outs]
    out_shape = [jax.ShapeDtypeStruct((t, c), d) for c, d in row_outs]
    out_shape += [jax.ShapeDtypeStruct((1, c), F32) for c in sum_outs]
    res = pl.pallas_call(
        body, name=name, grid=(t // tb,), in_specs=in_specs, out_specs=out_specs, out_shape=out_shape,
        compiler_params=_params("arbitrary"),
    )(*[a for a, _, _ in norm_rows], *vecs)
    return res


def _rms_fwd(name, x, g):
    def fn(xt, gt):
        rstd = lax.rsqrt(jnp.mean(xt * xt, axis=-1, keepdims=True) + EPS)
        return xt * rstd * gt
    return _rowwise(name, fn, [x], [g], [(x.shape[1], BF16)])[0]


def _rms_bwd(name, x, g, dh, dres=None, want_dx=True):
    has_res = dres is not None

    def fn(*a):
        if has_res:
            xt, dht, drt, gt = a
        else:
            xt, dht, gt = a
        rstd = lax.rsqrt(jnp.mean(xt * xt, axis=-1, keepdims=True) + EPS)
        xhat = xt * rstd
        dht = dht.astype(F32)
        dg = jnp.sum(dht * xhat, axis=0, keepdims=True)
        if not want_dx:
            return (dg,)
        dxhat = dht * gt
        dx = rstd * (dxhat - xhat * jnp.mean(dxhat * xhat, axis=-1, keepdims=True))
        if has_res:
            dx = dx + drt
        return dx, dx, dg

    d = x.shape[1]
    rows = [x, dh] + ([dres] if has_res else [])
    outs = [(d, F32), (d, BF16)] if want_dx else []
    return _rowwise(name, fn, rows, [g], outs, [d])


_DOT_DIMS = {"nn": ((1,), (0,)), "nt": ((1,), (1,)), "tn": ((0,), (0,))}


def _mm(name, a, b, mode, out_dtype, res=None, tm=512, tn=512):
    if mode == "nn":
        (m, k), (k2, n) = a.shape, b.shape
    elif mode == "nt":
        (m, k), (n, k2) = a.shape, b.shape
    else:
        (k, m), (k2, n) = a.shape, b.shape
    assert k == k2, (name, a.shape, b.shape)
    tm, tn = _tile(m, tm), _tile(n, tn)
    dims = (_DOT_DIMS[mode], ((), ()))
    has_res = res is not None

    def body(a_ref, b_ref, *rest):
        acc = lax.dot_general(a_ref[...], b_ref[...], dims, preferred_element_type=F32)
        if has_res:
            acc = acc + rest[0][...]
        rest[-1][...] = acc.astype(out_dtype)

    a_spec = pl.BlockSpec((k, tm), lambda i, j: (0, i)) if mode == "tn" else pl.BlockSpec((tm, k), lambda i, j: (i, 0))
    b_spec = pl.BlockSpec((tn, k), lambda i, j: (j, 0)) if mode == "nt" else pl.BlockSpec((k, tn), lambda i, j: (0, j))
    o_spec = pl.BlockSpec((tm, tn), lambda i, j: (i, j))
    in_specs, args = [a_spec, b_spec], [a, b]
    if has_res:
        in_specs.append(o_spec)
        args.append(res)
    return pl.pallas_call(
        body, name=name, grid=(m // tm, n // tn), in_specs=in_specs, out_specs=o_spec,
        out_shape=jax.ShapeDtypeStruct((m, n), out_dtype), compiler_params=_params("parallel", "arbitrary"),
    )(*args)


def _pool(name, u, reverse, tb=512):
    t = u.shape[0]
    tb = min(tb, t)
    nt = t // tb
    c = SB_WIDTH
    hpb = tb // POOL_HALO

    def body(cur_ref, halo_ref, o_ref):
        i = pl.program_id(0)
        cur = cur_ref[...].astype(F32)
        edge = (i == nt - 1) if reverse else (i == 0)
        halo = jnp.where(edge, 0.0, halo_ref[...].astype(F32))
        col = lax.broadcasted_iota(jnp.int32, (tb + POOL_HALO, c), 1)
        row = lax.broadcasted_iota(jnp.int32, (tb + POOL_HALO, c), 0)
        wcol = jnp.where(col < POOL_GROUP, 2, jnp.where(col < 2 * POOL_GROUP, 4, jnp.where(col < 3 * POOL_GROUP, 8, 16)))
        n = tb + POOL_HALO
        if reverse:
            ext = jnp.concatenate([cur, halo], axis=0)
            tpos = i * tb + row
            ext = ext / jnp.minimum(tpos + 1, wcol).astype(F32)
            shift = lambda a, k: pltpu.roll(a, n - k, 0)
        else:
            ext = jnp.concatenate([halo, cur], axis=0)
            shift = lambda a, k: pltpu.roll(a, k, 0)
        s2 = ext + shift(ext, 1)
        s4 = s2 + shift(s2, 2)
        s8 = s4 + shift(s4, 4)
        s16 = s8 + shift(s8, 8)
        win = jnp.where(wcol == 2, s2, jnp.where(wcol == 4, s4, jnp.where(wcol == 8, s8, s16)))
        if reverse:
            out = win[:tb] - cur
        else:
            tpos = i * tb + row[POOL_HALO:] - POOL_HALO
            out = win[POOL_HALO:] / jnp.minimum(tpos + 1, wcol[POOL_HALO:]).astype(F32) - cur
        o_ref[...] = out.astype(o_ref.dtype)

    if reverse:
        halo_map = lambda i: (jnp.minimum((i + 1) * hpb, t // POOL_HALO - 1), 0)
    else:
        halo_map = lambda i: (jnp.maximum(i * hpb - 1, 0), 0)
    return pl.pallas_call(
        body, name=name, grid=(nt,),
        in_specs=[pl.BlockSpec((tb, c), lambda i: (i, 0)), pl.BlockSpec((POOL_HALO, c), halo_map)],
        out_specs=pl.BlockSpec((tb, c), lambda i: (i, 0)),
        out_shape=jax.ShapeDtypeStruct((t, c), BF16), compiler_params=_params("arbitrary"),
    )(u, u)


def _head_masks(shape):
    lane = lax.broadcasted_iota(jnp.int32, shape, 1)
    return lane < HEAD_DIM, lane >= HEAD_DIM


def _pick(mask, a):
    return jnp.where(mask, a, jnp.zeros_like(a))


def _dot(a, b, mode):
    return lax.dot_general(a, b, (_DOT_DIMS[mode], ((), ())), preferred_element_type=F32)


def _dot_tri(a, tri):
    return _dot(a.astype(BF16), tri, "nn")


def _log_gates(z):
    nz = -z
    l = jnp.log(1.0 + jnp.exp(jnp.minimum(z, nz)))
    ln = jnp.minimum(nz, 0.0) - l
    return ln, z + ln


def _sb_blocks(s, tq, tk):
    tq, tk = min(tq, s), min(tk, s)
    assert tk % tq == 0 and s % tk == 0, (s, tq, tk)
    return tq, tk


def _sb_fwd(proj, kv, tq=SB_TQ, tk=SB_TK):
    s = proj.shape[0]
    tq, tk = _sb_blocks(s, tq, tk)
    npair = SB_WIDTH // LANES

    def body(q_ref, k_ref, v_ref, o_ref, tot_ref):
        r = lax.broadcasted_iota(jnp.int32, (tq, tk), 0)
        cidx = lax.broadcasted_iota(jnp.int32, (tq, tk), 1)
        ahead = cidx - r
        rk = lax.broadcasted_iota(jnp.int32, (tk, tk), 0)
        ck = lax.broadcasted_iota(jnp.int32, (tk, tk), 1)
        tri_gt = (rk > ck).astype(BF16)
        m_a, m_b = _head_masks((tq, LANES))
        mk_a, mk_b = _head_masks((tk, LANES))

        def block(qh, k2, vh, carry, acc, mask):
            ln_full, lsz = _log_gates(_dot(qh, k2, "nt"))
            ln = ln_full if mask is None else jnp.where(mask, ln_full, 0.0)
            w = jnp.exp(lsz + (_dot_tri(ln, tri_gt) + carry))
            if mask is not None:
                w = jnp.where(mask, w, 0.0)
            acc = acc + _dot(w.astype(BF16), vh, "nn")
            return carry + jnp.sum(ln, axis=1, keepdims=True), acc

        def q_block(qi, _):
            q0 = pl.multiple_of(qi * tq, tq)
            q2 = q_ref[pl.ds(q0, tq), :] * ATT_SCALE
            qa, qb = _pick(m_a, q2), _pick(m_b, q2)
            n_full = q0 // tk

            def both(k0, ca, cb, acc, mask):
                k2 = k_ref[pl.ds(k0, tk), :]
                v2 = v_ref[pl.ds(k0, tk), :]
                ca, acc = block(qa, k2, _pick(mk_a, v2), ca, acc, mask)
                cb, acc = block(qb, k2, _pick(mk_b, v2), cb, acc, mask)
                return ca, cb, acc

            kd = pl.multiple_of(n_full * tk, tk)
            zero_c = jnp.zeros((tq, 1), F32)
            carry = both(kd, zero_c, zero_c, jnp.zeros((tq, LANES), F32), ahead < q0 - kd)

            def k_block(step, carry):
                return both(pl.multiple_of((n_full - 1 - step) * tk, tk), *carry, None)

            ca, cb, acc = lax.fori_loop(0, n_full, k_block, carry)
            o_ref[pl.ds(q0, tq), :] = acc.astype(o_ref.dtype)
            tot_ref[0, pl.ds(q0, tq), :] = jnp.broadcast_to(ca, (tq, LANES))
            tot_ref[1, pl.ds(q0, tq), :] = jnp.broadcast_to(cb, (tq, LANES))
            return 0

        lax.fori_loop(0, s // tq, q_block, 0)

    return pl.pallas_call(
        body, name="sb_fwd", grid=(npair,),
        in_specs=[pl.BlockSpec((s, LANES), lambda p: (0, p)), pl.BlockSpec((s, LANES), lambda p: (0, p)),
                  pl.BlockSpec((s, LANES), lambda p: (0, npair + p))],
        out_specs=[pl.BlockSpec((s, LANES), lambda p: (0, p)), pl.BlockSpec((None, 2, s, LANES), lambda p: (p, 0, 0, 0))],
        out_shape=[jax.ShapeDtypeStruct((s, SB_WIDTH), BF16), jax.ShapeDtypeStruct((npair, 2, s, LANES), F32)],
        compiler_params=_params("arbitrary"),
    )(proj, kv, kv)


def _sb_bwd(proj, kv, dcat, tot, tq=SB_TQ, tk=SB_TK):
    s = proj.shape[0]
    tq, tk = _sb_blocks(s, tq, tk)
    npair = SB_WIDTH // LANES

    def body(q_ref, k_ref, v_ref, do_ref, tot_ref, dq_ref, dk_ref, dv_ref, dk_acc, dv_acc):
        r = lax.broadcasted_iota(jnp.int32, (tq, tk), 0)
        cidx = lax.broadcasted_iota(jnp.int32, (tq, tk), 1)
        ahead = cidx - r
        rk = lax.broadcasted_iota(jnp.int32, (tk, tk), 0)
        ck = lax.broadcasted_iota(jnp.int32, (tk, tk), 1)
        tri_gt = (rk > ck).astype(BF16)
        tri_lt = (rk < ck).astype(BF16)
        m_a, m_b = _head_masks((tq, LANES))
        mk_a, mk_b = _head_masks((tk, LANES))
        dk_acc[...] = jnp.zeros_like(dk_acc)
        dv_acc[...] = jnp.zeros_like(dv_acc)

        def block(qh, doh, k2, v2, kh, tot_h, carry, mask):
            c_ln, c_d, dq = carry
            ln_full, lsz = _log_gates(_dot(qh, k2, "nt"))
            ln = ln_full if mask is None else jnp.where(mask, ln_full, 0.0)
            c_ln = c_ln + jnp.sum(ln, axis=1, keepdims=True)
            later = (tot_h - c_ln) + _dot_tri(ln, tri_gt)
            w = jnp.exp(lsz + later)
            if mask is not None:
                w = jnp.where(mask, w, 0.0)
            dlw = _dot(doh, v2, "nt") * w
            before = _dot_tri(dlw, tri_lt) + c_d
            dz = dlw * jnp.exp(ln_full) - before * jnp.exp(lsz)
            if mask is not None:
                dz = jnp.where(mask, dz, 0.0)
            dz = dz.astype(BF16)
            dq = dq + _dot(dz, kh, "nn")
            dk = _dot(dz, qh, "tn")
            dv = _dot(w.astype(BF16), doh, "tn")
            carry = (c_ln, c_d + jnp.sum(dlw, axis=1, keepdims=True), dq)
            return carry, dk, dv

        def q_block(qi, _):
            q0 = pl.multiple_of(qi * tq, tq)
            q2 = q_ref[pl.ds(q0, tq), :] * ATT_SCALE
            do2 = do_ref[pl.ds(q0, tq), :]
            qa, qb = _pick(m_a, q2), _pick(m_b, q2)
            doa, dob = _pick(m_a, do2), _pick(m_b, do2)
            tot_a = tot_ref[0, pl.ds(q0, tq), 0:1]
            tot_b = tot_ref[1, pl.ds(q0, tq), 0:1]
            zero_c = jnp.zeros((tq, 1), F32)
            zero_q = jnp.zeros((tq, LANES), F32)
            n_full = q0 // tk

            def both(k0, ca, cb, mask):
                k2 = k_ref[pl.ds(k0, tk), :]
                v2 = v_ref[pl.ds(k0, tk), :]
                ca, dka, dva = block(qa, doa, k2, v2, _pick(mk_a, k2), tot_a, ca, mask)
                cb, dkb, dvb = block(qb, dob, k2, v2, _pick(mk_b, k2), tot_b, cb, mask)
                dk_acc[pl.ds(k0, tk), :] += dka + dkb
                dv_acc[pl.ds(k0, tk), :] += dva + dvb
                return ca, cb

            def k_block(kj, carry):
                return both(pl.multiple_of(kj * tk, tk), carry[0], carry[1], None)

            init = ((zero_c, zero_c, zero_q), (zero_c, zero_c, zero_q))
            ca, cb = lax.fori_loop(0, n_full, k_block, init)
            kd = pl.multiple_of(n_full * tk, tk)
            ca, cb = both(kd, ca, cb, ahead < q0 - kd)
            dq_ref[pl.ds(q0, tq), :] = ((ca[2] + cb[2]) * ATT_SCALE).astype(dq_ref.dtype)
            return 0

        lax.fori_loop(0, s // tq, q_block, 0)
        dk_ref[...] = dk_acc[...].astype(dk_ref.dtype)
        dv_ref[...] = dv_acc[...].astype(dv_ref.dtype)

    col = lambda off: pl.BlockSpec((s, LANES), functools.partial(lambda p, off: (0, off + p), off=off))
    dq, dk, dv = pl.pallas_call(
        body, name="sb_bwd", grid=(npair,),
        in_specs=[col(0), col(0), col(npair), col(0), pl.BlockSpec((None, 2, s, LANES), lambda p: (p, 0, 0, 0))],
        out_specs=[col(0), col(0), col(0)],
        out_shape=[jax.ShapeDtypeStruct((s, SB_WIDTH), BF16)] * 3,
        scratch_shapes=[pltpu.VMEM((s, LANES), F32), pltpu.VMEM((s, LANES), F32)],
        compiler_params=_params("arbitrary"),
    )(proj, kv, kv, dcat, tot)
    return dq, jnp.concatenate([dk, dv], axis=1)


def _mem_fwd(name, proj, mkv, tq=512):
    s = proj.shape[0]
    tq = min(tq, s)
    qblk = SB_WIDTH // MEM_WIDTH

    def body(q_ref, kv_ref, o_ref):
        m_a, m_b = _head_masks((tq, LANES))
        mk_a, mk_b = _head_masks((kv_ref.shape[0], LANES))
        for p in range(MEM_WIDTH // LANES):
            q2 = q_ref[:, p * LANES:(p + 1) * LANES]
            k2 = kv_ref[:, p * LANES:(p + 1) * LANES]
            v2 = kv_ref[:, MEM_WIDTH + p * LANES:MEM_WIDTH + (p + 1) * LANES]
            acc = jnp.zeros((tq, LANES), F32)
            for mq, mk in ((m_a, mk_a), (m_b, mk_b)):
                logits = _dot(_pick(mq, q2), k2, "nt") * ATT_SCALE
                e = jnp.exp(logits - jnp.max(logits, axis=-1, keepdims=True))
                prob = e / jnp.sum(e, axis=-1, keepdims=True)
                acc = acc + _dot(prob.astype(BF16), _pick(mk, v2), "nn")
            o_ref[:, p * LANES:(p + 1) * LANES] = acc.astype(o_ref.dtype)

    return pl.pallas_call(
        body, name=name, grid=(s // tq,),
        in_specs=[pl.BlockSpec((tq, MEM_WIDTH), lambda i: (i, qblk)), pl.BlockSpec(mkv.shape, lambda i: (0, 0))],
        out_specs=pl.BlockSpec((tq, MEM_WIDTH), lambda i: (i, 0)),
        out_shape=jax.ShapeDtypeStruct((s, MEM_WIDTH), BF16), compiler_params=_params("arbitrary"),
    )(proj, mkv)


def _mem_bwd(name, proj, mkv, dcat, tq=512):
    s = proj.shape[0]
    tq = min(tq, s)
    qblk = SB_WIDTH // MEM_WIDTH

    def body(q_ref, kv_ref, do_ref, dq_ref, dkv_ref):
        @pl.when(pl.program_id(0) == 0)
        def _():
            dkv_ref[...] = jnp.zeros_like(dkv_ref)

        m_a, m_b = _head_masks((tq, LANES))
        for p in range(MEM_WIDTH // LANES):
            ksl = slice(p * LANES, (p + 1) * LANES)
            vsl = slice(MEM_WIDTH + p * LANES, MEM_WIDTH + (p + 1) * LANES)
            q2, do2 = q_ref[:, ksl], do_ref[:, ksl]
            k2, v2 = kv_ref[:, ksl], kv_ref[:, vsl]
            mk_a, mk_b = _head_masks(k2.shape)
            dq = jnp.zeros((tq, LANES), F32)
            dk = jnp.zeros(k2.shape, F32)
            dv = jnp.zeros(k2.shape, F32)
            for mq, mk in ((m_a, mk_a), (m_b, mk_b)):
                qh, doh = _pick(mq, q2), _pick(mq, do2)
                logits = _dot(qh, k2, "nt") * ATT_SCALE
                e = jnp.exp(logits - jnp.max(logits, axis=-1, keepdims=True))
                prob = e / jnp.sum(e, axis=-1, keepdims=True)
                dp = _dot(doh, v2, "nt")
                ds = prob * (dp - jnp.sum(dp * prob, axis=-1, keepdims=True)) * ATT_SCALE
                ds = ds.astype(BF16)
                dq = dq + _dot(ds, _pick(mk, k2), "nn")
                dk = dk + _dot(ds, qh, "tn")
                dv = dv + _dot(prob.astype(BF16), doh, "tn")
            dq_ref[:, ksl] = dq.astype(dq_ref.dtype)
            dkv_ref[:, ksl] += dk
            dkv_ref[:, vsl] += dv

    return pl.pallas_call(
        body, name=name, grid=(s // tq,),
        in_specs=[pl.BlockSpec((tq, MEM_WIDTH), lambda i: (i, qblk)), pl.BlockSpec(mkv.shape, lambda i: (0, 0)),
                  pl.BlockSpec((tq, MEM_WIDTH), lambda i: (i, qblk))],
        out_specs=[pl.BlockSpec((tq, MEM_WIDTH), lambda i: (i, 0)), pl.BlockSpec(mkv.shape, lambda i: (0, 0))],
        out_shape=[jax.ShapeDtypeStruct((s, MEM_WIDTH), BF16), jax.ShapeDtypeStruct(mkv.shape, F32)],
        compiler_params=_params("arbitrary"),
    )(proj, mkv, dcat)


def _swiglu_fwd(name, gu):
    f = gu.shape[1] // 2

    def fn(g, u):
        g, u = g.astype(F32), u.astype(F32)
        return g * jax.nn.sigmoid(g) * u
    return _rowwise(name, fn, [(gu, (f, 0)), (gu, (f, 1))], [], [(f, BF16)], tb=256)[0]


def _swiglu_bwd(name, gu, dact):
    f = gu.shape[1] // 2

    def fn(g, u, d):
        g, u, d = g.astype(F32), u.astype(F32), d.astype(F32)
        sg = jax.nn.sigmoid(g)
        silu = g * sg
        return jnp.concatenate([d * u * (sg + silu * (1.0 - sg)), d * silu], axis=1)
    return _rowwise(name, fn, [(gu, (f, 0)), (gu, (f, 1)), dact], [], [(2 * f, BF16)], tb=256)[0]


def _ffn_fwd(tag, x, norm, w_gu, w_down):
    h = _rms_fwd(tag + "_ffn_norm", x, norm)
    gu = _mm(tag + "_gu", h, w_gu, "nn", BF16, tm=4096)
    act = _swiglu_fwd(tag + "_swiglu", gu)
    out = _mm(tag + "_down", act, w_down, "nn", F32, res=x, tn=1024)
    return out, (h, gu, act)


def _ffn_bwd(tag, x, norm, w_gu, w_down, saved, dout, dout_bf):
    h, gu, act = saved
    dact = _mm(tag + "_down_dx", dout_bf, w_down, "nt", BF16, tn=2816)
    g_down = _mm(tag + "_down_dw", act, dout_bf, "tn", BF16, tm=256, tn=1024)
    dgu = _swiglu_bwd(tag + "_swiglu_bwd", gu, dact)
    g_gu = _mm(tag + "_gu_dw", h, dgu, "tn", BF16, tm=1024)
    dh = _mm(tag + "_gu_dx", dgu, w_gu, "nt", F32, tn=1024)
    dx, dx_bf, g_norm = _rms_bwd(tag + "_ffn_norm_bwd", x, norm, dh, dres=dout)
    return dx, dx_bf, g_gu, g_down, g_norm


def _mem_kv(tag, mem_n, w_mem_kv):
    return _mm(tag + "_memkv", mem_n, w_mem_kv, "nn", BF16)


def _mem_kv_bwd(tag, mem, mem_norm, mem_n, w_mem_kv, dmkv):
    dmkv = dmkv.astype(BF16)
    g_w = _mm(tag + "_memkv_dw", mem_n, dmkv, "tn", BF16)
    dmem_n = _mm(tag + "_memkv_dx", dmkv, w_mem_kv, "nt", F32)
    (g_norm,) = _rms_bwd(tag + "_memnorm_bwd", mem, mem_norm, dmem_n, want_dx=False)
    return g_w, g_norm


def _block_diag(w_group):
    z = jnp.zeros((POOL_GROUP, POOL_GROUP), w_group.dtype)
    return jnp.concatenate(
        [jnp.concatenate([w_group[g] if h == g else z for h in range(4)], axis=1) for g in range(4)], axis=0)


def _local_step(x, mem, target, w):
    g = {}
    mem_n = _rms_fwd("mem_norm", mem, w["mem_norm"])
    h_a = _rms_fwd("a_mix_norm", x, w["a_norm_mix"])
    proj_a = _mm("a_in", h_a, w["a_w_in"], "nn", F32, tn=1024)
    pooled = _pool("a_pool", proj_a, reverse=False)
    w_bd = _block_diag(w["a_w_group"])
    g_pre = _mm("a_group", pooled, w_bd, "nn", BF16, tn=768)
    mkv_a = _mem_kv("a", mem_n, w["a_w_mem_kv"])
    proj_a_bf = proj_a.astype(BF16)
    mem_a = _mem_fwd("a_mem_attn", proj_a_bf, mkv_a)
    cat_a = _rowwise("a_cat", lambda gp, mo, sc: jnp.concatenate([gp.astype(F32) * sc, mo.astype(F32)], axis=1),
                     [g_pre, mem_a], [w["a_scale"]], [(1024, BF16)])[0]
    x1 = _mm("a_out", cat_a, w["a_w_out"], "nn", F32, res=x, tn=1024)
    x2, ffn_a = _ffn_fwd("a", x1, w["a_norm_ffn"], w["a_w_gu"], w["a_w_down"])
    h_k = _rms_fwd("kv_norm", x2, w["kv_norm"])
    kv = _mm("kv_proj", h_k, w["w_kv"], "nn", BF16, tn=1536)
    h_b = _rms_fwd("b_mix_norm", x2, w["b_norm_mix"])
    proj_b = _mm("b_q", h_b, w["b_w_q"], "nn", BF16, tn=1024)
    sb_out, tot = _sb_fwd(proj_b, kv)
    mkv_b = _mem_kv("b", mem_n, w["b_w_mem_kv"])
    mem_b = _mem_fwd("b_mem_attn", proj_b, mkv_b)
    cat_b = jnp.concatenate([sb_out, mem_b], axis=1)
    x3 = _mm("b_out", cat_b, w["b_w_out"], "nn", F32, res=x2, tn=1024)
    x4, ffn_b = _ffn_fwd("b", x3, w["b_norm_ffn"], w["b_w_gu"], w["b_w_down"])

    d = x.shape[1]

    def head(xt, tt, gt):
        rstd = lax.rsqrt(jnp.mean(xt * xt, axis=-1, keepdims=True) + EPS)
        xhat = xt * rstd
        err = xhat * gt - tt
        loss = 0.5 * jnp.sum(jnp.sum(err * err, axis=1, keepdims=True), axis=0, keepdims=True) / d
        dy = err / d
        dxhat = dy * gt
        dx = rstd * (dxhat - xhat * jnp.mean(dxhat * xhat, axis=-1, keepdims=True))
        return dx, dx, jnp.sum(dy * xhat, axis=0, keepdims=True), jnp.broadcast_to(loss, (1, LANES))

    dx4, dx4_bf, g["final_norm"], loss = _rowwise(
        "loss_head", head, [x4, target], [w["final_norm"]], [(d, F32), (d, BF16)], [d, LANES])

    dx3, dx3_bf, g["b_w_gu"], g["b_w_down"], g["b_norm_ffn"] = _ffn_bwd(
        "b", x3, w["b_norm_ffn"], w["b_w_gu"], w["b_w_down"], ffn_b, dx4, dx4_bf)
    dcat_b = _mm("b_out_dx", dx3_bf, w["b_w_out"], "nt", BF16, tn=1024)
    g["b_w_out"] = _mm("b_out_dw", cat_b, dx3_bf, "tn", BF16, tm=1024, tn=1024)
    dq_sb, dkv = _sb_bwd(proj_b, kv, dcat_b, tot)
    dq_mem_b, dmkv_b = _mem_bwd("b_mem_attn_bwd", proj_b, mkv_b, dcat_b)
    dproj_b = jnp.concatenate([dq_sb, dq_mem_b], axis=1)
    g["b_w_q"] = _mm("b_q_dw", h_b, dproj_b, "tn", BF16, tm=1024, tn=1024)
    dh_b = _mm("b_q_dx", dproj_b, w["b_w_q"], "nt", F32, tn=1024)
    dx2, _, g["b_norm_mix"] = _rms_bwd("b_mix_norm_bwd", x2, w["b_norm_mix"], dh_b, dres=dx3)
    g["b_w_mem_kv"], g_memnorm_b = _mem_kv_bwd("b", mem, w["mem_norm"], mem_n, w["b_w_mem_kv"], dmkv_b)
    g["w_kv"] = _mm("kv_proj_dw", h_k, dkv, "tn", BF16, tm=1024)
    dh_k = _mm("kv_proj_dx", dkv, w["w_kv"], "nt", F32, tn=1024)
    dx2, dx2_bf, g["kv_norm"] = _rms_bwd("kv_norm_bwd", x2, w["kv_norm"], dh_k, dres=dx2)

    dx1, dx1_bf, g["a_w_gu"], g["a_w_down"], g["a_norm_ffn"] = _ffn_bwd(
        "a", x1, w["a_norm_ffn"], w["a_w_gu"], w["a_w_down"], ffn_a, dx2, dx2_bf)
    dcat_a = _mm("a_out_dx", dx1_bf, w["a_w_out"], "nt", BF16, tn=1024)
    g["a_w_out"] = _mm("a_out_dw", cat_a, dx1_bf, "tn", BF16, tm=1024, tn=1024)

    def scale_bwd(dc, gp, sc):
        dc, gp = dc.astype(F32), gp.astype(F32)
        return dc * sc, jnp.sum(dc * gp, axis=0, keepdims=True)

    dg_pre, g["a_scale"] = _rowwise("a_scale_bwd", scale_bwd, [(dcat_a, (SB_WIDTH, 0)), g_pre], [w["a_scale"]],
                                    [(SB_WIDTH, BF16)], [SB_WIDTH])
    g_bd = _mm("a_group_dw", pooled, dg_pre, "tn", F32, tm=768, tn=768)
    g["a_w_group"] = jnp.stack([g_bd[i * POOL_GROUP:(i + 1) * POOL_GROUP, i * POOL_GROUP:(i + 1) * POOL_GROUP]
                                for i in range(4)])
    dpooled = _mm("a_group_dx", dg_pre, w_bd, "nt", F32, tn=768)
    du_pool = _pool("a_pool_bwd", dpooled, reverse=True)
    dq_mem_a, dmkv_a = _mem_bwd("a_mem_attn_bwd", proj_a_bf, mkv_a, dcat_a)
    dproj_a = jnp.concatenate([du_pool, dq_mem_a], axis=1)
    g["a_w_in"] = _mm("a_in_dw", h_a, dproj_a, "tn", BF16, tm=1024, tn=1024)
    dh_a = _mm("a_in_dx", dproj_a, w["a_w_in"], "nt", F32, tn=1024)
    grad_x, _, g["a_norm_mix"] = _rms_bwd("a_mix_norm_bwd", x, w["a_norm_mix"], dh_a, dres=dx1)
    g["a_w_mem_kv"], g_memnorm_a = _mem_kv_bwd("a", mem, w["mem_norm"], mem_n, w["a_w_mem_kv"], dmkv_a)
    g["mem_norm"] = g_memnorm_a + g_memnorm_b
    return loss, grad_x, g


ROW_SHARDED = ("a_w_in", "a_w_mem_kv", "a_w_out", "a_w_down", "b_w_q", "b_w_mem_kv", "b_w_out", "b_w_down")
COL_SHARDED = ("a_w_gu", "w_kv", "b_w_gu")
BIG = ("a_w_in", "a_w_mem_kv", "a_w_out", "a_w_gu", "a_w_down", "w_kv", "b_w_q", "b_w_mem_kv", "b_w_out", "b_w_gu",
       "b_w_down")
N_CHIPS = 4
N_DEV = 8


def _position():
    x, y, c = lax.axis_index("x"), lax.axis_index("y"), lax.axis_index("c")
    other_chips = [(1 - x, y), (x, 1 - y), (1 - x, 1 - y)]
    return x, y, c, other_chips


def _remote(src, dst, send_sem, recv_sem, device):
    return pltpu.make_async_remote_copy(src_ref=src, dst_ref=dst, send_sem=send_sem, recv_sem=recv_sem,
                                        device_id=device, device_id_type=MESH)


def _comm_call(name, body, args, out_shape, n_remote, n_local, scratch=(), aliases=None):
    return pl.pallas_call(
        body, name=name, in_specs=[ANY] * len(args), out_specs=[ANY] * len(out_shape), out_shape=out_shape,
        scratch_shapes=[pltpu.SemaphoreType.DMA((n_remote,)), pltpu.SemaphoreType.DMA((n_remote,)),
                        pltpu.SemaphoreType.DMA((max(n_local, 1),)), *scratch],
        input_output_aliases=aliases or {},
        compiler_params=pltpu.CompilerParams(vmem_limit_bytes=VMEM_LIMIT),
    )(*args)


def _row_chunks(nrows, row_bytes):
    per = max(16, (COPY_BYTES // row_bytes) // 16 * 16)
    return [(r0, min(per, nrows - r0)) for r0 in range(0, nrows, per)]


def _rows(ref, start, size, lead=()):
    if isinstance(start, int):
        return ref.at[(*lead, pl.ds(start, size))]
    return ref.at[(*lead, pl.ds(pl.multiple_of(start, 16), size))]


class _Copies:
    def __init__(self, send, recv, loc):
        self.send, self.recv, self.loc = send, recv, loc
        self.n_remote = self.n_local_done = 0
        self.locals, self.remotes = [], []

    def local(self, src, dst):
        cp = pltpu.make_async_copy(src, dst, self.loc.at[len(self.locals)])
        cp.start()
        self.locals.append(cp)

    def slot(self):
        self.n_remote += 1
        return self.n_remote - 1

    def remote(self, k, src, dst, device, start=True):
        cp = _remote(src, dst, self.send.at[k], self.recv.at[k], device)
        if start:
            cp.start()
            self.remotes.append(cp)
        return cp

    def finish_local(self):
        for cp in self.locals[self.n_local_done:]:
            cp.wait()
        self.n_local_done = len(self.locals)

    def finish(self):
        for cp in self.remotes:
            cp.wait_send()
        self.finish_local()


def _shard_cols(ref, row_sharded, cdim, chip):
    if row_sharded:
        return ref
    return ref.at[:, pl.ds(pl.multiple_of(chip * cdim, LANES), cdim)]


def _gather_weights(shards):
    names = list(shards)
    n = len(names)
    args = [shards[name] for name in names]
    row_sharded = [name in ROW_SHARDED for name in names]
    out_shape, plan = [], []
    for name in names:
        r, cdim = shards[name].shape
        full = (N_CHIPS * r, cdim) if name in ROW_SHARDED else (r, N_CHIPS * cdim)
        out_shape.append(jax.ShapeDtypeStruct(full, BF16))
        plan.append(_row_chunks(r // 2, cdim * 2))
    n_chunks = sum(len(p) for p in plan)
    stage = [pltpu.VMEM(shards[name].shape, BF16) for name in names]

    def body(*refs):
        src, dst = refs[:n], refs[n:2 * n]
        cps = _Copies(*refs[2 * n:2 * n + 3])
        vm = refs[2 * n + 3:]
        x, y, c, chips = _position()
        me = 2 * x + y

        def window(i, chip, half, r0, size):
            r, cdim = src[i].shape
            base = (chip * r if row_sharded[i] else 0) + half * (r // 2) + r0
            return _rows(_shard_cols(dst[i], row_sharded[i], cdim, chip), base, size)

        def mine(ref, i, half, r0, size):
            return _rows(ref[i], half * (src[i].shape[0] // 2) + r0, size)

        ici = {}
        for i in range(n):
            for r0, size in plan[i]:
                for k, (px, py) in enumerate(chips):
                    ici[i, r0, k] = cps.slot()
                    cps.remote(ici[i, r0, k], mine(src, i, c, r0, size), window(i, me, c, r0, size), (px, py, c))
        for i in range(n):
            cps.local(src[i], vm[i])
        cps.finish_local()
        for i in range(n):
            for r0, size in plan[i]:
                for half in range(2):
                    cps.local(mine(vm, i, half, r0, size), window(i, me, half, r0, size))
        fwd = {}
        for i in range(n):
            for r0, size in plan[i]:
                for k, (px, py) in enumerate(chips):
                    chip = 2 * px + py
                    cps.remote(ici[i, r0, k], mine(src, i, c, r0, size), window(i, chip, c, r0, size), (px, py, c),
                               start=False).wait_recv()
                    fwd[i, r0, k] = cps.slot()
                    cps.remote(fwd[i, r0, k], window(i, chip, c, r0, size), window(i, chip, c, r0, size), (x, y, 1 - c))
        for i in range(n):
            for r0, size in plan[i]:
                for k, (px, py) in enumerate(chips):
                    cps.remote(fwd[i, r0, k], mine(src, i, c, r0, size), window(i, 2 * px + py, 1 - c, r0, size),
                               (x, y, 1 - c), start=False).wait_recv()
        cps.finish()

    outs = _comm_call("gather_weights", body, args, out_shape, 6 * n_chunks, 2 * n_chunks + n, scratch=stage)
    return dict(zip(names, outs))


def _scalar_grid_call(name, body, scalars, grid, in_specs, out_specs, out_shape, args):
    return pl.pallas_call(
        body, name=name, out_shape=out_shape,
        grid_spec=pltpu.PrefetchScalarGridSpec(num_scalar_prefetch=1, grid=grid, in_specs=in_specs, out_specs=out_specs),
        compiler_params=_params(*["arbitrary"] * len(grid)),
    )(scalars, *args)


def _pair_sum(name, g4, sib, where, tb=256):
    j, _, r, w = g4.shape
    tb = _row_tile(r, tb)

    def body(s_ref, g_ref, b_ref, o_ref):
        o_ref[...] = (g_ref[...].astype(F32) + b_ref[...].astype(F32)).astype(o_ref.dtype)

    blk = pl.BlockSpec((None, tb, w), lambda a, i, s: (a, i, 0))
    return _scalar_grid_call(
        name, body, where, (j, r // tb),
        [pl.BlockSpec((None, None, tb, w), lambda a, i, s: (a, s[0], i, 0)), blk], blk,
        jax.ShapeDtypeStruct((j, r, w), BF16), (g4, sib))


def _chip_sum(name, partial, got, where, row_sharded, tb=256):
    _, r, cdim = got.shape
    tb = _row_tile(r, tb)

    def body(s_ref, p_ref, g_ref, o_ref):
        acc = p_ref[...].astype(F32)
        for k in range(N_CHIPS - 1):
            acc = acc + g_ref[k].astype(F32)
        o_ref[...] = acc

    if row_sharded:
        own = pl.BlockSpec((None, tb, cdim), lambda i, s: (s[1], i, 0))
    else:
        own = pl.BlockSpec((None, tb, cdim), lambda i, s: (0, i, s[1]))
    return _scalar_grid_call(
        name, body, where, (r // tb,),
        [own, pl.BlockSpec((N_CHIPS - 1, tb, cdim), lambda i, s: (0, i, 0))],
        pl.BlockSpec((None, tb, cdim), lambda i, s: (s[0], i, 0)),
        jax.ShapeDtypeStruct((2, r, cdim), F32), (partial, got))


def _reduce_scatter(grads):
    names = list(grads)
    n = len(names)
    row_sharded = [name in ROW_SHARDED for name in names]
    where = jnp.stack([lax.axis_index("c"), 2 * lax.axis_index("x") + lax.axis_index("y")]).astype(jnp.int32)
    half_shape, g4 = [], []
    for name in names:
        rows, cols = grads[name].shape
        if name in ROW_SHARDED:
            r = rows // N_CHIPS
            half_shape.append((r // 2, cols))
            g4.append(grads[name].reshape(N_CHIPS, 2, r // 2, cols))
        else:
            half_shape.append((rows // 2, cols // N_CHIPS))
            g4.append(grads[name].reshape(1, 2, rows // 2, cols))

    plan_a = [[(j, r0, size) for j in range(g.shape[0]) for r0, size in _row_chunks(g.shape[2], g.shape[3] * 2)]
              for g in g4]
    n_a = sum(len(p) for p in plan_a)
    out_shape = [jax.ShapeDtypeStruct((g.shape[0],) + g.shape[2:], BF16) for g in g4]

    def body_a(*refs):
        src, sib = refs[:n], refs[n:2 * n]
        cps = _Copies(*refs[2 * n:])
        x, y, c, _ = _position()
        waits = []
        for i in range(n):
            for j, r0, size in plan_a[i]:
                waits.append(cps.remote(cps.slot(), _rows(src[i], r0, size, lead=(j, 1 - c)),
                                        _rows(sib[i], r0, size, lead=(j,)), (x, y, 1 - c)))
        for cp in waits:
            cp.wait_recv()
        cps.finish()

    sibs = _comm_call("grads_pair_exchange", body_a, g4, out_shape, n_a, 0)
    partial = [_pair_sum(name + "_pair_sum", g, s, where) for name, g, s in zip(names, g4, sibs)]

    out_shape = [jax.ShapeDtypeStruct((N_CHIPS - 1,) + s, BF16) for s in half_shape]
    plan_b = [_row_chunks(r, cdim * 2) for r, cdim in half_shape]
    n_b = sum(len(p) for p in plan_b)

    def body_b(*refs):
        src, dst = refs[:n], refs[n:2 * n]
        cps = _Copies(*refs[2 * n:])
        x, y, c, chips = _position()

        def shard(i, chip, r0, size):
            if row_sharded[i]:
                return _rows(src[i], r0, size, lead=(chip,))
            return _rows(_shard_cols(src[i].at[0], False, half_shape[i][1], chip), r0, size)

        waits = []
        for i in range(n):
            for r0, size in plan_b[i]:
                for k, (px, py) in enumerate(chips):
                    waits.append(cps.remote(cps.slot(), shard(i, 2 * px + py, r0, size), _rows(dst[i], r0, size, lead=(k,)),
                                            (px, py, c)))
        for cp in waits:
            cp.wait_recv()
        cps.finish()

    got = _comm_call("grads_chip_exchange", body_b, partial, out_shape, 3 * n_b, 0)
    halves = [_chip_sum(name + "_chip_sum", p, g, where, rs) for name, p, g, rs in zip(names, partial, got, row_sharded)]

    plan_c = [_row_chunks(r, cdim * 4) for r, cdim in half_shape]
    n_c = sum(len(p) for p in plan_c)
    out_shape = [jax.ShapeDtypeStruct(h.shape, F32) for h in halves]

    def body_c(*refs):
        src, dst = refs[:n], refs[n:2 * n]
        cps = _Copies(*refs[2 * n:])
        x, y, c, _ = _position()
        waits = []
        for i in range(n):
            for r0, size in plan_c[i]:
                waits.append(cps.remote(cps.slot(), _rows(src[i], r0, size, lead=(c,)), _rows(dst[i], r0, size, lead=(c,)),
                                        (x, y, 1 - c)))
        for cp in waits:
            cp.wait_recv()
        cps.finish()

    outs = _comm_call("grads_pair_share", body_c, halves, out_shape, n_c, 0, aliases={i: i for i in range(n)})
    return {name: o.reshape(2 * o.shape[1], o.shape[2]) for name, o in zip(names, outs)}


def _all_reduce_small(name, v):
    rows, cols = v.shape

    def body(v_ref, o_ref, buf, send, recv):
        x, y, c, _ = _position()
        me = 4 * x + 2 * y + c
        buf[me] = v_ref[...]
        peers = []
        for k in range(1, N_DEV):
            fx, fy, fc = (k >> 2) & 1, (k >> 1) & 1, k & 1
            peers.append((x + fx - 2 * fx * x, y + fy - 2 * fy * y, c + fc - 2 * fc * c))
        sends = []
        for k, peer in enumerate(peers):
            cp = _remote(v_ref, buf.at[me], send.at[k], recv.at[k], peer)
            cp.start()
            sends.append(cp)
        for k, (px, py, pc) in enumerate(peers):
            _remote(v_ref, buf.at[4 * px + 2 * py + pc], send.at[k], recv.at[k], (px, py, pc)).wait_recv()
        for cp in sends:
            cp.wait_send()
        acc = buf[0]
        for d in range(1, N_DEV):
            acc = acc + buf[d]
        o_ref[...] = acc

    vm = pl.BlockSpec(memory_space=pltpu.VMEM)
    return pl.pallas_call(
        body, name=name, in_specs=[vm], out_specs=vm, out_shape=jax.ShapeDtypeStruct(v.shape, F32),
        scratch_shapes=[pltpu.VMEM((N_DEV, rows, cols), F32), pltpu.SemaphoreType.DMA((N_DEV - 1,)),
                        pltpu.SemaphoreType.DMA((N_DEV - 1,))],
        compiler_params=pltpu.CompilerParams(vmem_limit_bytes=VMEM_LIMIT),
    )(v)


def _adamw(name, w, g, m, v):
    def fn(wt, gt, mt, vt):
        mt = ADAM_B1 * mt + (1.0 - ADAM_B1) * gt
        vt = ADAM_B2 * vt + (1.0 - ADAM_B2) * (gt * gt)
        m_hat = mt / (1.0 - ADAM_B1 ** ADAM_STEP)
        v_hat = vt / (1.0 - ADAM_B2 ** ADAM_STEP)
        delta = -ADAM_LR * (m_hat / (jnp.sqrt(v_hat) + ADAM_EPS) + ADAM_WD * wt)
        return delta, mt, vt
    n = w.shape[1]
    return _rowwise(name, fn, [w, g, m, v], [], [(n, F32)] * 3, tb=256)


WEIGHTS = ("mem_norm", "a_norm_mix", "a_w_in", "a_w_group", "a_scale", "a_w_mem_kv", "a_w_out", "a_norm_ffn", "a_w_gu",
           "a_w_down", "kv_norm", "w_kv", "b_norm_mix", "b_w_q", "b_w_mem_kv", "b_w_out", "b_norm_ffn", "b_w_gu",
           "b_w_down", "final_norm")
REPLICATED_VECS = ("mem_norm", "kv_norm", "b_norm_mix", "b_norm_ffn", "final_norm")
SHARDED_VECS = ("a_norm_mix", "a_norm_ffn", "a_scale")
D_MODEL = 1024
GROUP_ROWS = 4 * POOL_GROUP * POOL_GROUP // D_MODEL


def _row(v):
    v = v.reshape(1, -1).astype(F32)
    return jnp.pad(v, ((0, 0), (0, D_MODEL - v.shape[1])))


def _pack_small(t):
    rows = [_row(t[k]) for k in REPLICATED_VECS]
    rows.append(_row(jnp.concatenate([t[k].reshape(-1) for k in SHARDED_VECS])))
    rows.append(jnp.zeros((2, D_MODEL), F32))
    rows.append(t["a_w_group"].reshape(GROUP_ROWS, D_MODEL).astype(F32))
    return jnp.concatenate(rows, axis=0)


def _unpack_small(p, like):
    out = {k: p[i, :].reshape(like[k].shape) for i, k in enumerate(REPLICATED_VECS)}
    off = 0
    for k in SHARDED_VECS:
        size = like[k].size
        out[k] = p[len(REPLICATED_VECS), off:off + size].reshape(like[k].shape)
        off += size
    out["a_w_group"] = p[len(REPLICATED_VECS) + 3:, :].reshape(like["a_w_group"].shape)
    return out


def kernel(x, mem, mem_norm, a_norm_mix, a_w_in, a_w_group, a_scale, a_w_mem_kv, a_w_out, a_norm_ffn, a_w_gu, a_w_down, kv_norm, w_kv, b_norm_mix, b_w_q, b_w_mem_kv, b_w_out, b_norm_ffn, b_w_gu, b_w_down, final_norm, loss_target, m_mem_norm, m_a_norm_mix, m_a_w_in, m_a_w_group, m_a_scale, m_a_w_mem_kv, m_a_w_out, m_a_norm_ffn, m_a_w_gu, m_a_w_down, m_kv_norm, m_w_kv, m_b_norm_mix, m_b_w_q, m_b_w_mem_kv, m_b_w_out, m_b_norm_ffn, m_b_w_gu, m_b_w_down, m_final_norm, v_mem_norm, v_a_norm_mix, v_a_w_in, v_a_w_group, v_a_scale, v_a_w_mem_kv, v_a_w_out, v_a_norm_ffn, v_a_w_gu, v_a_w_down, v_kv_norm, v_w_kv, v_b_norm_mix, v_b_w_q, v_b_w_mem_kv, v_b_w_out, v_b_norm_ffn, v_b_w_gu, v_b_w_down, v_final_norm):
    given = dict(locals())
    wl = {k: given[k] for k in WEIGHTS}
    ml = {k: given["m_" + k] for k in WEIGHTS}
    vl = {k: given["v_" + k] for k in WEIGHTS}
    chip = 2 * lax.axis_index("x") + lax.axis_index("y")

    def mat(a):
        return a.reshape(a.shape[-2], a.shape[-1])

    full = _gather_weights({k: mat(wl[k]).astype(BF16) for k in BIG})
    gains = jnp.zeros((8, D_MODEL), F32)
    for i, k in enumerate(SHARDED_VECS):
        part = wl[k].reshape(1, -1)
        width = part.shape[1]
        gains = lax.dynamic_update_slice(gains, part, (i, chip * width))
    gains = _all_reduce_small("gains_all_gather", gains) * 0.5
    w = dict(full)
    for k in REPLICATED_VECS:
        w[k] = wl[k].reshape(1, D_MODEL)
    w["a_norm_mix"], w["a_norm_ffn"] = gains[0:1], gains[1:2]
    w["a_scale"] = gains[2:3, :SB_WIDTH]
    w["a_w_group"] = wl["a_w_group"][0].astype(BF16)

    loss, grad_x, g = _local_step(x[0], mem[0], loss_target[0], w)

    red = _reduce_scatter({k: g[k] for k in BIG})
    small = jnp.concatenate(
        [_row(g[k]) for k in REPLICATED_VECS] + [_row(g[k]) for k in SHARDED_VECS] + [_row(loss)]
        + [jnp.zeros((7, D_MODEL), F32), g["a_w_group"].reshape(GROUP_ROWS, D_MODEL)], axis=0)
    small = _all_reduce_small("small_grads_all_reduce", small)
    gs = {k: small[i] for i, k in enumerate(REPLICATED_VECS)}
    for i, k in enumerate(SHARDED_VECS):
        width = wl[k].shape[-1]
        gs[k] = lax.dynamic_slice(small[len(REPLICATED_VECS) + i], (chip * width,), (width,))
    gs["a_w_group"] = small[16:]
    total_loss = small[8, 0]

    out_g, out_d, out_m, out_v = {}, {}, {}, {}
    for k in BIG:
        shape = wl[k].shape
        out_g[k] = red[k].reshape(shape)
        d, nm, nv = _adamw(k + "_adamw", mat(wl[k]), red[k], mat(ml[k]), mat(vl[k]))
        out_d[k], out_m[k], out_v[k] = d.reshape(shape), nm.reshape(shape), nv.reshape(shape)
    small_names = REPLICATED_VECS + SHARDED_VECS + ("a_w_group",)
    d, nm, nv = _adamw("small_adamw", _pack_small(wl), _pack_small(gs), _pack_small(ml), _pack_small(vl))
    like = {k: wl[k] for k in small_names}
    for dst, p in ((out_d, d), (out_m, nm), (out_v, nv)):
        dst.update(_unpack_small(p, like))
    for k in small_names:
        out_g[k] = gs[k].reshape(wl[k].shape)

    return (total_loss, grad_x[None], *[out_g[k] for k in WEIGHTS], *[out_d[k] for k in WEIGHTS],
            *[out_m[k] for k in WEIGHTS], *[out_v[k] for k in WEIGHTS])
```

```python
import functools

import jax
import jax.numpy as jnp
from jax import lax
from jax.experimental import pallas as pl
from jax.experimental.pallas import tpu as pltpu

F32 = jnp.float32
BF16 = jnp.bfloat16

HEAD_DIM = 64
SB_WIDTH = 768
MEM_WIDTH = 256
POOL_WINDOWS = (2, 4, 8, 16)
POOL_GROUP = 192
POOL_HALO = 16
EPS = 1e-6
ATT_SCALE = HEAD_DIM ** -0.5
ADAM_LR, ADAM_B1, ADAM_B2, ADAM_EPS, ADAM_WD, ADAM_STEP = 0.001, 0.9, 0.999, 1e-08, 0.01, 10

LANES = 128
SB_TQ, SB_TK = 512, 512
VMEM_LIMIT = 56 * 1024 * 1024
MESH = pl.DeviceIdType.MESH
COPY_BYTES = 512 * 1024
ANY = pl.BlockSpec(memory_space=pl.ANY)


def _params(*sem):
    return pltpu.CompilerParams(dimension_semantics=sem, vmem_limit_bytes=VMEM_LIMIT)


def _tile(n, pref):
    if n <= pref:
        return n
    best = None
    for t in range(LANES, pref + 1, LANES):
        if n % t == 0:
            best = t
    assert best is not None, (n, pref)
    return best


def _row_tile(t, pref):
    if t <= pref:
        return t
    for tb in range(pref - pref % 16, 0, -16):
        if t % tb == 0:
            return tb
    raise ValueError((t, pref))


def _rowwise(name, fn, rows, vecs, row_outs, sum_outs=(), tb=512):
    norm_rows = []
    for r in rows:
        if isinstance(r, tuple):
            arr, (bc, cb) = r
        else:
            arr, (bc, cb) = r, (r.shape[1], 0)
        norm_rows.append((arr, bc, cb))
    t = norm_rows[0][0].shape[0]
    tb = _row_tile(t, tb)
    n_in, n_ro = len(norm_rows) + len(vecs), len(row_outs)

    def body(*refs):
        ins = [r[...] for r in refs[:n_in]]
        outs = fn(*ins)
        if not isinstance(outs, tuple):
            outs = (outs,)
        for o_ref, o in zip(refs[n_in:n_in + n_ro], outs[:n_ro]):
            o_ref[...] = o.astype(o_ref.dtype)
        for s_ref, s in zip(refs[n_in + n_ro:], outs[n_ro:]):
            @pl.when(pl.program_id(0) == 0)
            def _():
                s_ref[...] = jnp.zeros_like(s_ref)
            s_ref[...] += s

    in_specs = [pl.BlockSpec((tb, bc), functools.partial(lambda i, cb: (i, cb), cb=cb)) for _, bc, cb in norm_rows]
    in_specs += [pl.BlockSpec(v.shape, lambda i: (0, 0)) for v in vecs]
    out_specs = [pl.BlockSpec((tb, c), lambda i: (i, 0)) for c, _ in row_outs]
    out_specs += [pl.BlockSpec((1, c), lambda i: (0, 0)) for c in sum_outs]
    out_shape = [jax.ShapeDtypeStruct((t, c), d) for c, d in row_outs]
    out_shape += [jax.ShapeDtypeStruct((1, c), F32) for c in sum_outs]
    res = pl.pallas_call(
        body, name=name, grid=(t // tb,), in_specs=in_specs, out_specs=out_specs, out_shape=out_shape,
        compiler_params=_params("arbitrary"),
    )(*[a for a, _, _ in norm_rows], *vecs)
    return res


def _rms_fwd(name, x, g):
    def fn(xt, gt):
        rstd = lax.rsqrt(jnp.mean(xt * xt, axis=-1, keepdims=True) + EPS)
        return xt * rstd * gt
    return _rowwise(name, fn, [x], [g], [(x.shape[1], BF16)])[0]


def _rms_bwd(name, x, g, dh, dres=None, want_dx=True):
    has_res = dres is not None

    def fn(*a):
        if has_res:
            xt, dht, drt, gt = a
        else:
            xt, dht, gt = a
        rstd = lax.rsqrt(jnp.mean(xt * xt, axis=-1, keepdims=True) + EPS)
        xhat = xt * rstd
        dht = dht.astype(F32)
        dg = jnp.sum(dht * xhat, axis=0, keepdims=True)
        if not want_dx:
            return (dg,)
        dxhat = dht * gt
        dx = rstd * (dxhat - xhat * jnp.mean(dxhat * xhat, axis=-1, keepdims=True))
        if has_res:
            dx = dx + drt
        return dx, dx, dg

    d = x.shape[1]
    rows = [x, dh] + ([dres] if has_res else [])
    outs = [(d, F32), (d, BF16)] if want_dx else []
    return _rowwise(name, fn, rows, [g], outs, [d])


_DOT_DIMS = {"nn": ((1,), (0,)), "nt": ((1,), (1,)), "tn": ((0,), (0,))}


def _mm(name, a, b, mode, out_dtype, res=None, tm=512, tn=512):
    if mode == "nn":
        (m, k), (k2, n) = a.shape, b.shape
    elif mode == "nt":
        (m, k), (n, k2) = a.shape, b.shape
    else:
        (k, m), (k2, n) = a.shape, b.shape
    assert k == k2, (name, a.shape, b.shape)
    tm, tn = _tile(m, tm), _tile(n, tn)
    dims = (_DOT_DIMS[mode], ((), ()))
    has_res = res is not None

    def body(a_ref, b_ref, *rest):
        acc = lax.dot_general(a_ref[...], b_ref[...], dims, preferred_element_type=F32)
        if has_res:
            acc = acc + rest[0][...]
        rest[-1][...] = acc.astype(out_dtype)

    a_spec = pl.BlockSpec((k, tm), lambda i, j: (0, i)) if mode == "tn" else pl.BlockSpec((tm, k), lambda i, j: (i, 0))
    b_spec = pl.BlockSpec((tn, k), lambda i, j: (j, 0)) if mode == "nt" else pl.BlockSpec((k, tn), lambda i, j: (0, j))
    o_spec = pl.BlockSpec((tm, tn), lambda i, j: (i, j))
    in_specs, args = [a_spec, b_spec], [a, b]
    if has_res:
        in_specs.append(o_spec)
        args.append(res)
    return pl.pallas_call(
        body, name=name, grid=(m // tm, n // tn), in_specs=in_specs, out_specs=o_spec,
        out_shape=jax.ShapeDtypeStruct((m, n), out_dtype), compiler_params=_params("parallel", "arbitrary"),
    )(*args)


def _pool(name, u, reverse, tb=512):
    t = u.shape[0]
    tb = min(tb, t)
    nt = t // tb
    c = SB_WIDTH
    hpb = tb // POOL_HALO

    def body(cur_ref, halo_ref, o_ref):
        i = pl.program_id(0)
        cur = cur_ref[...].astype(F32)
        edge = (i == nt - 1) if reverse else (i == 0)
        halo = jnp.where(edge, 0.0, halo_ref[...].astype(F32))
        col = lax.broadcasted_iota(jnp.int32, (tb + POOL_HALO, c), 1)
        row = lax.broadcasted_iota(jnp.int32, (tb + POOL_HALO, c), 0)
        wcol = jnp.where(col < POOL_GROUP, 2, jnp.where(col < 2 * POOL_GROUP, 4, jnp.where(col < 3 * POOL_GROUP, 8, 16)))
        n = tb + POOL_HALO
        if reverse:
            ext = jnp.concatenate([cur, halo], axis=0)
            tpos = i * tb + row
            ext = ext / jnp.minimum(tpos + 1, wcol).astype(F32)
            shift = lambda a, k: pltpu.roll(a, n - k, 0)
        else:
            ext = jnp.concatenate([halo, cur], axis=0)
            shift = lambda a, k: pltpu.roll(a, k, 0)
        s2 = ext + shift(ext, 1)
        s4 = s2 + shift(s2, 2)
        s8 = s4 + shift(s4, 4)
        s16 = s8 + shift(s8, 8)
        win = jnp.where(wcol == 2, s2, jnp.where(wcol == 4, s4, jnp.where(wcol == 8, s8, s16)))
        if reverse:
            out = win[:tb] - cur
        else:
            tpos = i * tb + row[POOL_HALO:] - POOL_HALO
            out = win[POOL_HALO:] / jnp.minimum(tpos + 1, wcol[POOL_HALO:]).astype(F32) - cur
        o_ref[...] = out.astype(o_ref.dtype)

    if reverse:
        halo_map = lambda i: (jnp.minimum((i + 1) * hpb, t // POOL_HALO - 1), 0)
    else:
        halo_map = lambda i: (jnp.maximum(i * hpb - 1, 0), 0)
    return pl.pallas_call(
        body, name=name, grid=(nt,),
        in_specs=[pl.BlockSpec((tb, c), lambda i: (i, 0)), pl.BlockSpec((POOL_HALO, c), halo_map)],
        out_specs=pl.BlockSpec((tb, c), lambda i: (i, 0)),
        out_shape=jax.ShapeDtypeStruct((t, c), BF16), compiler_params=_params("arbitrary"),
    )(u, u)


def _head_masks(shape):
    lane = lax.broadcasted_iota(jnp.int32, shape, 1)
    return lane < HEAD_DIM, lane >= HEAD_DIM


def _pick(mask, a):
    return jnp.where(mask, a, jnp.zeros_like(a))


def _dot(a, b, mode):
    return lax.dot_general(a, b, (_DOT_DIMS[mode], ((), ())), preferred_element_type=F32)


def _dot_tri(a, tri):
    return _dot(a.astype(BF16), tri, "nn")


def _log_gates(z):
    nz = -z
    l = jnp.log(1.0 + jnp.exp(jnp.minimum(z, nz)))
    ln = jnp.minimum(nz, 0.0) - l
    return ln, z + ln


def _sb_blocks(s, tq, tk):
    tq, tk = min(tq, s), min(tk, s)
    assert tk % tq == 0 and s % tk == 0, (s, tq, tk)
    return tq, tk


def _sb_fwd(proj, kv, late_shards, tq=SB_TQ, tk=SB_TK):
    s = proj.shape[0]
    tq, tk = _sb_blocks(s, tq, tk)
    npair = SB_WIDTH // LANES
    gather = _Gather({k: v.shape for k, v in late_shards.items()})
    ng = gather.n

    def body(q_ref, k_ref, v_ref, *rest):
        o_ref, tot_ref = rest[ng:ng + 2]
        comm = rest[:ng], rest[ng + 2:2 * ng + 2], rest[2 * ng + 2:2 * ng + 5], rest[2 * ng + 5:]

        @pl.when(pl.program_id(0) == 0)
        def _():
            gather.start(*comm)

        r = lax.broadcasted_iota(jnp.int32, (tq, tk), 0)
        cidx = lax.broadcasted_iota(jnp.int32, (tq, tk), 1)
        ahead = cidx - r
        rk = lax.broadcasted_iota(jnp.int32, (tk, tk), 0)
        ck = lax.broadcasted_iota(jnp.int32, (tk, tk), 1)
        tri_gt = (rk > ck).astype(BF16)
        m_a, m_b = _head_masks((tq, LANES))
        mk_a, mk_b = _head_masks((tk, LANES))

        def block(qh, k2, vh, carry, acc, mask):
            ln_full, lsz = _log_gates(_dot(qh, k2, "nt"))
            ln = ln_full if mask is None else jnp.where(mask, ln_full, 0.0)
            w = jnp.exp(lsz + (_dot_tri(ln, tri_gt) + carry))
            if mask is not None:
                w = jnp.where(mask, w, 0.0)
            acc = acc + _dot(w.astype(BF16), vh, "nn")
            return carry + jnp.sum(ln, axis=1, keepdims=True), acc

        def q_block(qi, _):
            q0 = pl.multiple_of(qi * tq, tq)
            q2 = q_ref[pl.ds(q0, tq), :] * ATT_SCALE
            qa, qb = _pick(m_a, q2), _pick(m_b, q2)
            n_full = q0 // tk

            def both(k0, ca, cb, acc, mask):
                k2 = k_ref[pl.ds(k0, tk), :]
                v2 = v_ref[pl.ds(k0, tk), :]
                ca, acc = block(qa, k2, _pick(mk_a, v2), ca, acc, mask)
                cb, acc = block(qb, k2, _pick(mk_b, v2), cb, acc, mask)
                return ca, cb, acc

            kd = pl.multiple_of(n_full * tk, tk)
            zero_c = jnp.zeros((tq, 1), F32)
            carry = both(kd, zero_c, zero_c, jnp.zeros((tq, LANES), F32), ahead < q0 - kd)

            def k_block(step, carry):
                return both(pl.multiple_of((n_full - 1 - step) * tk, tk), *carry, None)

            ca, cb, acc = lax.fori_loop(0, n_full, k_block, carry)
            o_ref[pl.ds(q0, tq), :] = acc.astype(o_ref.dtype)
            tot_ref[0, pl.ds(q0, tq), :] = jnp.broadcast_to(ca, (tq, LANES))
            tot_ref[1, pl.ds(q0, tq), :] = jnp.broadcast_to(cb, (tq, LANES))
            return 0

        lax.fori_loop(0, s // tq, q_block, 0)

        @pl.when(pl.program_id(0) == npair - 1)
        def _():
            gather.finish(*comm)

    outs = pl.pallas_call(
        body, name="sb_fwd", grid=(npair,),
        in_specs=[pl.BlockSpec((s, LANES), lambda p: (0, p)), pl.BlockSpec((s, LANES), lambda p: (0, p)),
                  pl.BlockSpec((s, LANES), lambda p: (0, npair + p))] + [ANY] * ng,
        out_specs=[pl.BlockSpec((s, LANES), lambda p: (0, p)), pl.BlockSpec((None, 2, s, LANES), lambda p: (p, 0, 0, 0))]
        + [ANY] * ng,
        out_shape=[jax.ShapeDtypeStruct((s, SB_WIDTH), BF16), jax.ShapeDtypeStruct((npair, 2, s, LANES), F32)]
        + gather.out_shape(),
        scratch_shapes=gather.scratch(), compiler_params=_params("arbitrary"),
    )(proj, kv, kv, *[late_shards[k] for k in gather.names])
    return outs[0], outs[1], dict(zip(gather.names, outs[2:]))


def _sb_bwd(proj, kv, dcat, tot, early_partial, tq=SB_TQ, tk=SB_TK):
    s = proj.shape[0]
    tq, tk = _sb_blocks(s, tq, tk)
    npair = SB_WIDTH // LANES
    exchange = _ChipExchange(early_partial)
    ne = exchange.n

    def body(q_ref, k_ref, v_ref, do_ref, tot_ref, *rest):
        dq_ref, dk_ref, dv_ref = rest[ne:ne + 3]
        dk_acc, dv_acc = rest[2 * ne + 3:2 * ne + 5]
        comm = rest[:ne], rest[ne + 3:2 * ne + 3], rest[2 * ne + 5:]

        @pl.when(pl.program_id(0) == 0)
        def _():
            exchange.start(*comm)

        r = lax.broadcasted_iota(jnp.int32, (tq, tk), 0)
        cidx = lax.broadcasted_iota(jnp.int32, (tq, tk), 1)
        ahead = cidx - r
        rk = lax.broadcasted_iota(jnp.int32, (tk, tk), 0)
        ck = lax.broadcasted_iota(jnp.int32, (tk, tk), 1)
        tri_gt = (rk > ck).astype(BF16)
        tri_lt = (rk < ck).astype(BF16)
        m_a, m_b = _head_masks((tq, LANES))
        mk_a, mk_b = _head_masks((tk, LANES))
        dk_acc[...] = jnp.zeros_like(dk_acc)
        dv_acc[...] = jnp.zeros_like(dv_acc)

        def block(qh, doh, k2, v2, kh, tot_h, carry, mask):
            c_ln, c_d, dq = carry
            ln_full, lsz = _log_gates(_dot(qh, k2, "nt"))
            ln = ln_full if mask is None else jnp.where(mask, ln_full, 0.0)
            c_ln = c_ln + jnp.sum(ln, axis=1, keepdims=True)
            later = (tot_h - c_ln) + _dot_tri(ln, tri_gt)
            w = jnp.exp(lsz + later)
            if mask is not None:
                w = jnp.where(mask, w, 0.0)
            dlw = _dot(doh, v2, "nt") * w
            before = _dot_tri(dlw, tri_lt) + c_d
            dz = dlw * jnp.exp(ln_full) - before * jnp.exp(lsz)
            if mask is not None:
                dz = jnp.where(mask, dz, 0.0)
            dz = dz.astype(BF16)
            dq = dq + _dot(dz, kh, "nn")
            dk = _dot(dz, qh, "tn")
            dv = _dot(w.astype(BF16), doh, "tn")
            carry = (c_ln, c_d + jnp.sum(dlw, axis=1, keepdims=True), dq)
            return carry, dk, dv

        def q_block(qi, _):
            q0 = pl.multiple_of(qi * tq, tq)
            q2 = q_ref[pl.ds(q0, tq), :] * ATT_SCALE
            do2 = do_ref[pl.ds(q0, tq), :]
            qa, qb = _pick(m_a, q2), _pick(m_b, q2)
            doa, dob = _pick(m_a, do2), _pick(m_b, do2)
            tot_a = tot_ref[0, pl.ds(q0, tq), 0:1]
            tot_b = tot_ref[1, pl.ds(q0, tq), 0:1]
            zero_c = jnp.zeros((tq, 1), F32)
            zero_q = jnp.zeros((tq, LANES), F32)
            n_full = q0 // tk

            def both(k0, ca, cb, mask):
                k2 = k_ref[pl.ds(k0, tk), :]
                v2 = v_ref[pl.ds(k0, tk), :]
                ca, dka, dva = block(qa, doa, k2, v2, _pick(mk_a, k2), tot_a, ca, mask)
                cb, dkb, dvb = block(qb, dob, k2, v2, _pick(mk_b, k2), tot_b, cb, mask)
                dk_acc[pl.ds(k0, tk), :] += dka + dkb
                dv_acc[pl.ds(k0, tk), :] += dva + dvb
                return ca, cb

            def k_block(kj, carry):
                return both(pl.multiple_of(kj * tk, tk), carry[0], carry[1], None)

            init = ((zero_c, zero_c, zero_q), (zero_c, zero_c, zero_q))
            ca, cb = lax.fori_loop(0, n_full, k_block, init)
            kd = pl.multiple_of(n_full * tk, tk)
            ca, cb = both(kd, ca, cb, ahead < q0 - kd)
            dq_ref[pl.ds(q0, tq), :] = ((ca[2] + cb[2]) * ATT_SCALE).astype(dq_ref.dtype)
            return 0

        lax.fori_loop(0, s // tq, q_block, 0)
        dk_ref[...] = dk_acc[...].astype(dk_ref.dtype)
        dv_ref[...] = dv_acc[...].astype(dv_ref.dtype)

        @pl.when(pl.program_id(0) == npair - 1)
        def _():
            exchange.finish(*comm)

    col = lambda off: pl.BlockSpec((s, LANES), functools.partial(lambda p, off: (0, off + p), off=off))
    outs = pl.pallas_call(
        body, name="sb_bwd", grid=(npair,),
        in_specs=[col(0), col(0), col(npair), col(0), pl.BlockSpec((None, 2, s, LANES), lambda p: (p, 0, 0, 0))]
        + [ANY] * ne,
        out_specs=[col(0), col(0), col(0)] + [ANY] * ne,
        out_shape=[jax.ShapeDtypeStruct((s, SB_WIDTH), BF16)] * 3 + exchange.out_shape(),
        scratch_shapes=[pltpu.VMEM((s, LANES), F32), pltpu.VMEM((s, LANES), F32)] + exchange.scratch(),
        compiler_params=_params("arbitrary"),
    )(proj, kv, kv, dcat, tot, *[early_partial[k] for k in exchange.names])
    dq, dk, dv = outs[:3]
    return dq, jnp.concatenate([dk, dv], axis=1), dict(zip(exchange.names, outs[3:]))


def _mem_fwd(name, proj, mkv, tq=512):
    s = proj.shape[0]
    tq = min(tq, s)
    qblk = SB_WIDTH // MEM_WIDTH

    def body(q_ref, kv_ref, o_ref):
        m_a, m_b = _head_masks((tq, LANES))
        mk_a, mk_b = _head_masks((kv_ref.shape[0], LANES))
        for p in range(MEM_WIDTH // LANES):
            q2 = q_ref[:, p * LANES:(p + 1) * LANES]
            k2 = kv_ref[:, p * LANES:(p + 1) * LANES]
            v2 = kv_ref[:, MEM_WIDTH + p * LANES:MEM_WIDTH + (p + 1) * LANES]
            acc = jnp.zeros((tq, LANES), F32)
            for mq, mk in ((m_a, mk_a), (m_b, mk_b)):
                logits = _dot(_pick(mq, q2), k2, "nt") * ATT_SCALE
                e = jnp.exp(logits - jnp.max(logits, axis=-1, keepdims=True))
                prob = e / jnp.sum(e, axis=-1, keepdims=True)
                acc = acc + _dot(prob.astype(BF16), _pick(mk, v2), "nn")
            o_ref[:, p * LANES:(p + 1) * LANES] = acc.astype(o_ref.dtype)

    return pl.pallas_call(
        body, name=name, grid=(s // tq,),
        in_specs=[pl.BlockSpec((tq, MEM_WIDTH), lambda i: (i, qblk)), pl.BlockSpec(mkv.shape, lambda i: (0, 0))],
        out_specs=pl.BlockSpec((tq, MEM_WIDTH), lambda i: (i, 0)),
        out_shape=jax.ShapeDtypeStruct((s, MEM_WIDTH), BF16), compiler_params=_params("arbitrary"),
    )(proj, mkv)


def _mem_bwd(name, proj, mkv, dcat, tq=512):
    s = proj.shape[0]
    tq = min(tq, s)
    qblk = SB_WIDTH // MEM_WIDTH

    def body(q_ref, kv_ref, do_ref, dq_ref, dkv_ref):
        @pl.when(pl.program_id(0) == 0)
        def _():
            dkv_ref[...] = jnp.zeros_like(dkv_ref)

        m_a, m_b = _head_masks((tq, LANES))
        for p in range(MEM_WIDTH // LANES):
            ksl = slice(p * LANES, (p + 1) * LANES)
            vsl = slice(MEM_WIDTH + p * LANES, MEM_WIDTH + (p + 1) * LANES)
            q2, do2 = q_ref[:, ksl], do_ref[:, ksl]
            k2, v2 = kv_ref[:, ksl], kv_ref[:, vsl]
            mk_a, mk_b = _head_masks(k2.shape)
            dq = jnp.zeros((tq, LANES), F32)
            dk = jnp.zeros(k2.shape, F32)
            dv = jnp.zeros(k2.shape, F32)
            for mq, mk in ((m_a, mk_a), (m_b, mk_b)):
                qh, doh = _pick(mq, q2), _pick(mq, do2)
                logits = _dot(qh, k2, "nt") * ATT_SCALE
                e = jnp.exp(logits - jnp.max(logits, axis=-1, keepdims=True))
                prob = e / jnp.sum(e, axis=-1, keepdims=True)
                dp = _dot(doh, v2, "nt")
                ds = prob * (dp - jnp.sum(dp * prob, axis=-1, keepdims=True)) * ATT_SCALE
                ds = ds.astype(BF16)
                dq = dq + _dot(ds, _pick(mk, k2), "nn")
                dk = dk + _dot(ds, qh, "tn")
                dv = dv + _dot(prob.astype(BF16), doh, "tn")
            dq_ref[:, ksl] = dq.astype(dq_ref.dtype)
            dkv_ref[:, ksl] += dk
            dkv_ref[:, vsl] += dv

    return pl.pallas_call(
        body, name=name, grid=(s // tq,),
        in_specs=[pl.BlockSpec((tq, MEM_WIDTH), lambda i: (i, qblk)), pl.BlockSpec(mkv.shape, lambda i: (0, 0)),
                  pl.BlockSpec((tq, MEM_WIDTH), lambda i: (i, qblk))],
        out_specs=[pl.BlockSpec((tq, MEM_WIDTH), lambda i: (i, 0)), pl.BlockSpec(mkv.shape, lambda i: (0, 0))],
        out_shape=[jax.ShapeDtypeStruct((s, MEM_WIDTH), BF16), jax.ShapeDtypeStruct(mkv.shape, F32)],
        compiler_params=_params("arbitrary"),
    )(proj, mkv, dcat)


def _gu_swiglu(name, h, w_gu, tn=256):
    t, d = h.shape
    f = w_gu.shape[1] // 2
    nb = f // tn

    def body(h_ref, wg_ref, wu_ref, g_ref, u_ref, a_ref):
        hh = h_ref[...]
        g = _dot(hh, wg_ref[...], "nn")
        u = _dot(hh, wu_ref[...], "nn")
        g_ref[...] = g.astype(BF16)
        u_ref[...] = u.astype(BF16)
        a_ref[...] = (g * jax.nn.sigmoid(g) * u).astype(BF16)

    out = pl.BlockSpec((t, tn), lambda j: (0, j))
    return pl.pallas_call(
        body, name=name, grid=(nb,),
        in_specs=[pl.BlockSpec((t, d), lambda j: (0, 0)), pl.BlockSpec((d, tn), lambda j: (0, j)),
                  pl.BlockSpec((d, tn), lambda j: (0, nb + j))],
        out_specs=[out, out, out], out_shape=[jax.ShapeDtypeStruct((t, f), BF16)] * 3,
        compiler_params=_params("arbitrary"),
    )(h, w_gu, w_gu)


def _down_dx_swiglu(name, dout_bf, w_down, gate, up, tm=256):
    t, d = dout_bf.shape
    f = w_down.shape[0]

    def body(do_ref, w_ref, g_ref, u_ref, o_ref):
        dact = _dot(do_ref[...], w_ref[...], "nt")
        g, u = g_ref[...].astype(F32), u_ref[...].astype(F32)
        sg = jax.nn.sigmoid(g)
        silu = g * sg
        o_ref[:, :f] = (dact * u * (sg + silu * (1.0 - sg))).astype(BF16)
        o_ref[:, f:] = (dact * silu).astype(BF16)

    row = lambda c: pl.BlockSpec((tm, c), lambda i: (i, 0))
    return pl.pallas_call(
        body, name=name, grid=(t // tm,),
        in_specs=[row(d), pl.BlockSpec((f, d), lambda i: (0, 0)), row(f), row(f)],
        out_specs=row(2 * f), out_shape=jax.ShapeDtypeStruct((t, 2 * f), BF16), compiler_params=_params("arbitrary"),
    )(dout_bf, w_down, gate, up)


def _ffn_fwd(tag, x, norm, w_gu, w_down):
    h = _rms_fwd(tag + "_ffn_norm", x, norm)
    gate, up, act = _gu_swiglu(tag + "_gu", h, w_gu)
    out = _mm(tag + "_down", act, w_down, "nn", F32, res=x, tn=1024)
    return out, (h, gate, up, act)


def _ffn_bwd(tag, x, norm, w_gu, w_down, saved, dout, dout_bf):
    h, gate, up, act = saved
    g_down = _mm(tag + "_down_dw", act, dout_bf, "tn", BF16, tm=256, tn=1024)
    dgu = _down_dx_swiglu(tag + "_down_dx", dout_bf, w_down, gate, up)
    g_gu = _mm(tag + "_gu_dw", h, dgu, "tn", BF16, tm=1024)
    dh = _mm(tag + "_gu_dx", dgu, w_gu, "nt", F32, tn=1024)
    dx, dx_bf, g_norm = _rms_bwd(tag + "_ffn_norm_bwd", x, norm, dh, dres=dout)
    return dx, dx_bf, g_gu, g_down, g_norm


def _mem_kv(tag, mem_n, w_mem_kv):
    return _mm(tag + "_memkv", mem_n, w_mem_kv, "nn", BF16)


def _mem_kv_bwd(tag, mem, mem_norm, mem_n, w_mem_kv, dmkv):
    dmkv = dmkv.astype(BF16)
    g_w = _mm(tag + "_memkv_dw", mem_n, dmkv, "tn", BF16)
    dmem_n = _mm(tag + "_memkv_dx", dmkv, w_mem_kv, "nt", F32)
    (g_norm,) = _rms_bwd(tag + "_memnorm_bwd", mem, mem_norm, dmem_n, want_dx=False)
    return g_w, g_norm


def _block_diag(w_group):
    z = jnp.zeros((POOL_GROUP, POOL_GROUP), w_group.dtype)
    return jnp.concatenate(
        [jnp.concatenate([w_group[g] if h == g else z for h in range(4)], axis=1) for g in range(4)], axis=0)


def _local_step(x, mem, target, w, late_shards):
    g = {}
    w = dict(w)
    mem_n = _rms_fwd("mem_norm", mem, w["mem_norm"])
    h_a = _rms_fwd("a_mix_norm", x, w["a_norm_mix"])
    proj_a = _mm("a_in", h_a, w["a_w_in"], "nn", F32, tn=1024)
    pooled = _pool("a_pool", proj_a, reverse=False)
    w_bd = _block_diag(w["a_w_group"])
    g_pre = _mm("a_group", pooled, w_bd, "nn", BF16, tn=768)
    mkv_a = _mem_kv("a", mem_n, w["a_w_mem_kv"])
    proj_a_bf = proj_a.astype(BF16)
    mem_a = _mem_fwd("a_mem_attn", proj_a_bf, mkv_a)
    cat_a = _rowwise("a_cat", lambda gp, mo, sc: jnp.concatenate([gp.astype(F32) * sc, mo.astype(F32)], axis=1),
                     [g_pre, mem_a], [w["a_scale"]], [(1024, BF16)])[0]
    x1 = _mm("a_out", cat_a, w["a_w_out"], "nn", F32, res=x, tn=1024)
    x2, ffn_a = _ffn_fwd("a", x1, w["a_norm_ffn"], w["a_w_gu"], w["a_w_down"])
    h_k = _rms_fwd("kv_norm", x2, w["kv_norm"])
    kv = _mm("kv_proj", h_k, w["w_kv"], "nn", BF16, tn=1536)
    h_b = _rms_fwd("b_mix_norm", x2, w["b_norm_mix"])
    proj_b = _mm("b_q", h_b, w["b_w_q"], "nn", BF16, tn=1024)
    sb_out, tot, late = _sb_fwd(proj_b, kv, late_shards)
    w.update(late)
    mkv_b = _mem_kv("b", mem_n, w["b_w_mem_kv"])
    mem_b = _mem_fwd("b_mem_attn", proj_b, mkv_b)
    cat_b = jnp.concatenate([sb_out, mem_b], axis=1)
    x3 = _mm("b_out", cat_b, w["b_w_out"], "nn", F32, res=x2, tn=1024)
    x4, ffn_b = _ffn_fwd("b", x3, w["b_norm_ffn"], w["b_w_gu"], w["b_w_down"])

    d = x.shape[1]

    def head(xt, tt, gt):
        rstd = lax.rsqrt(jnp.mean(xt * xt, axis=-1, keepdims=True) + EPS)
        xhat = xt * rstd
        err = xhat * gt - tt
        loss = 0.5 * jnp.sum(jnp.sum(err * err, axis=1, keepdims=True), axis=0, keepdims=True) / d
        dy = err / d
        dxhat = dy * gt
        dx = rstd * (dxhat - xhat * jnp.mean(dxhat * xhat, axis=-1, keepdims=True))
        return dx, dx, jnp.sum(dy * xhat, axis=0, keepdims=True), jnp.broadcast_to(loss, (1, LANES))

    dx4, dx4_bf, g["final_norm"], loss = _rowwise(
        "loss_head", head, [x4, target], [w["final_norm"]], [(d, F32), (d, BF16)], [d, LANES])

    dx3, dx3_bf, g["b_w_gu"], g["b_w_down"], g["b_norm_ffn"] = _ffn_bwd(
        "b", x3, w["b_norm_ffn"], w["b_w_gu"], w["b_w_down"], ffn_b, dx4, dx4_bf)
    dcat_b = _mm("b_out_dx", dx3_bf, w["b_w_out"], "nt", BF16, tn=1024)
    g["b_w_out"] = _mm("b_out_dw", cat_b, dx3_bf, "tn", BF16, tm=1024, tn=1024)
    early_partial = _pair_sums("early", {k: g.pop(k) for k in EARLY})
    dq_sb, dkv, early_got = _sb_bwd(proj_b, kv, dcat_b, tot, early_partial)
    dq_mem_b, dmkv_b = _mem_bwd("b_mem_attn_bwd", proj_b, mkv_b, dcat_b)
    dproj_b = jnp.concatenate([dq_sb, dq_mem_b], axis=1)
    g["b_w_q"] = _mm("b_q_dw", h_b, dproj_b, "tn", BF16, tm=1024, tn=1024)
    dh_b = _mm("b_q_dx", dproj_b, w["b_w_q"], "nt", F32, tn=1024)
    dx2, _, g["b_norm_mix"] = _rms_bwd("b_mix_norm_bwd", x2, w["b_norm_mix"], dh_b, dres=dx3)
    g["b_w_mem_kv"], g_memnorm_b = _mem_kv_bwd("b", mem, w["mem_norm"], mem_n, w["b_w_mem_kv"], dmkv_b)
    g["w_kv"] = _mm("kv_proj_dw", h_k, dkv, "tn", BF16, tm=1024)
    dh_k = _mm("kv_proj_dx", dkv, w["w_kv"], "nt", F32, tn=1024)
    dx2, dx2_bf, g["kv_norm"] = _rms_bwd("kv_norm_bwd", x2, w["kv_norm"], dh_k, dres=dx2)

    dx1, dx1_bf, g["a_w_gu"], g["a_w_down"], g["a_norm_ffn"] = _ffn_bwd(
        "a", x1, w["a_norm_ffn"], w["a_w_gu"], w["a_w_down"], ffn_a, dx2, dx2_bf)
    dcat_a = _mm("a_out_dx", dx1_bf, w["a_w_out"], "nt", BF16, tn=1024)
    g["a_w_out"] = _mm("a_out_dw", cat_a, dx1_bf, "tn", BF16, tm=1024, tn=1024)

    def scale_bwd(dc, gp, sc):
        dc, gp = dc.astype(F32), gp.astype(F32)
        return dc * sc, jnp.sum(dc * gp, axis=0, keepdims=True)

    dg_pre, g["a_scale"] = _rowwise("a_scale_bwd", scale_bwd, [(dcat_a, (SB_WIDTH, 0)), g_pre], [w["a_scale"]],
                                    [(SB_WIDTH, BF16)], [SB_WIDTH])
    g_bd = _mm("a_group_dw", pooled, dg_pre, "tn", F32, tm=768, tn=768)
    g["a_w_group"] = jnp.stack([g_bd[i * POOL_GROUP:(i + 1) * POOL_GROUP, i * POOL_GROUP:(i + 1) * POOL_GROUP]
                                for i in range(4)])
    dpooled = _mm("a_group_dx", dg_pre, w_bd, "nt", F32, tn=768)
    du_pool = _pool("a_pool_bwd", dpooled, reverse=True)
    dq_mem_a, dmkv_a = _mem_bwd("a_mem_attn_bwd", proj_a_bf, mkv_a, dcat_a)
    dproj_a = jnp.concatenate([du_pool, dq_mem_a], axis=1)
    g["a_w_in"] = _mm("a_in_dw", h_a, dproj_a, "tn", BF16, tm=1024, tn=1024)
    dh_a = _mm("a_in_dx", dproj_a, w["a_w_in"], "nt", F32, tn=1024)
    grad_x, _, g["a_norm_mix"] = _rms_bwd("a_mix_norm_bwd", x, w["a_norm_mix"], dh_a, dres=dx1)
    g["a_w_mem_kv"], g_memnorm_a = _mem_kv_bwd("a", mem, w["mem_norm"], mem_n, w["a_w_mem_kv"], dmkv_a)
    g["mem_norm"] = g_memnorm_a + g_memnorm_b
    return loss, grad_x, g, early_partial, early_got


ROW_SHARDED = ("a_w_in", "a_w_mem_kv", "a_w_out", "a_w_down", "b_w_q", "b_w_mem_kv", "b_w_out", "b_w_down")
COL_SHARDED = ("a_w_gu", "w_kv", "b_w_gu")
BIG = ("a_w_in", "a_w_mem_kv", "a_w_out", "a_w_gu", "a_w_down", "w_kv", "b_w_q", "b_w_mem_kv", "b_w_out", "b_w_gu",
       "b_w_down")
LATE = ("b_w_mem_kv", "b_w_out", "b_w_gu", "b_w_down")
EARLY = ("b_w_gu", "b_w_down", "b_w_out")
N_CHIPS = 4
N_DEV = 8


def _position():
    x, y, c = lax.axis_index("x"), lax.axis_index("y"), lax.axis_index("c")
    other_chips = [(1 - x, y), (x, 1 - y), (1 - x, 1 - y)]
    return x, y, c, other_chips


def _remote(src, dst, send_sem, recv_sem, device):
    return pltpu.make_async_remote_copy(src_ref=src, dst_ref=dst, send_sem=send_sem, recv_sem=recv_sem,
                                        device_id=device, device_id_type=MESH)


def _comm_call(name, body, args, out_shape, n_remote, n_local, scratch=(), aliases=None):
    return pl.pallas_call(
        body, name=name, in_specs=[ANY] * len(args), out_specs=[ANY] * len(out_shape), out_shape=out_shape,
        scratch_shapes=[pltpu.SemaphoreType.DMA((n_remote,)), pltpu.SemaphoreType.DMA((n_remote,)),
                        pltpu.SemaphoreType.DMA((max(n_local, 1),)), *scratch],
        input_output_aliases=aliases or {},
        compiler_params=pltpu.CompilerParams(vmem_limit_bytes=VMEM_LIMIT),
    )(*args)


def _row_chunks(nrows, row_bytes):
    per = max(16, (COPY_BYTES // row_bytes) // 16 * 16)
    return [(r0, min(per, nrows - r0)) for r0 in range(0, nrows, per)]


def _rows(ref, start, size, lead=()):
    if isinstance(start, int):
        return ref.at[(*lead, pl.ds(start, size))]
    return ref.at[(*lead, pl.ds(pl.multiple_of(start, 16), size))]


class _Copies:
    def __init__(self, send, recv, loc):
        self.send, self.recv, self.loc = send, recv, loc
        self.n_remote = self.n_local_done = 0
        self.locals, self.remotes = [], []

    def local(self, src, dst):
        cp = pltpu.make_async_copy(src, dst, self.loc.at[len(self.locals)])
        cp.start()
        self.locals.append(cp)

    def slot(self):
        self.n_remote += 1
        return self.n_remote - 1

    def remote(self, k, src, dst, device, start=True):
        cp = _remote(src, dst, self.send.at[k], self.recv.at[k], device)
        if start:
            cp.start()
            self.remotes.append(cp)
        return cp

    def finish_local(self):
        for cp in self.locals[self.n_local_done:]:
            cp.wait()
        self.n_local_done = len(self.locals)

    def finish(self):
        for cp in self.remotes:
            cp.wait_send()
        self.finish_local()


def _shard_cols(ref, row_sharded, cdim, chip):
    if row_sharded:
        return ref
    return ref.at[:, pl.ds(pl.multiple_of(chip * cdim, LANES), cdim)]


class _Gather:
    def __init__(self, shapes):
        self.names = list(shapes)
        self.shapes = [tuple(shapes[k]) for k in self.names]
        self.row_sharded = [k in ROW_SHARDED for k in self.names]
        self.chunks = [(i, r0, size) for i, (r, cdim) in enumerate(self.shapes)
                       for r0, size in _row_chunks(r // 2, cdim * 2)]
        self.n = len(self.names)

    def out_shape(self):
        return [jax.ShapeDtypeStruct((N_CHIPS * r, cdim) if rs else (r, N_CHIPS * cdim), BF16)
                for (r, cdim), rs in zip(self.shapes, self.row_sharded)]

    def scratch(self):
        n_remote, n_local = 6 * len(self.chunks), self.n + 2 * len(self.chunks)
        return [pltpu.SemaphoreType.DMA((n_remote,)), pltpu.SemaphoreType.DMA((n_remote,)),
                pltpu.SemaphoreType.DMA((n_local,))] + [pltpu.VMEM(s, BF16) for s in self.shapes]

    def _window(self, dst, i, chip, half, r0, size):
        r, cdim = self.shapes[i]
        base = (chip * r if self.row_sharded[i] else 0) + half * (r // 2) + r0
        return _rows(_shard_cols(dst[i], self.row_sharded[i], cdim, chip), base, size)

    def _mine(self, refs, i, half, r0, size):
        return _rows(refs[i], half * (self.shapes[i][0] // 2) + r0, size)

    def _sent(self, src, dst, sems, q, k, px, py, c, me):
        i, r0, size = self.chunks[q]
        return _remote(self._mine(src, i, c, r0, size), self._window(dst, i, me, c, r0, size),
                       sems[0].at[6 * q + k], sems[1].at[6 * q + k], (px, py, c))

    def start(self, src, dst, sems, vm):
        x, y, c, chips = _position()
        for q in range(len(self.chunks)):
            for k, (px, py) in enumerate(chips):
                self._sent(src, dst, sems, q, k, px, py, c, 2 * x + y).start()
        for i in range(self.n):
            pltpu.make_async_copy(src[i], vm[i], sems[2].at[i]).start()

    def finish(self, src, dst, sems, vm):
        x, y, c, chips = _position()
        me = 2 * x + y
        send, recv, loc = sems
        for i in range(self.n):
            pltpu.make_async_copy(src[i], vm[i], loc.at[i]).wait()
        placed, forwarded = [], []
        for q, (i, r0, size) in enumerate(self.chunks):
            for half in range(2):
                cp = pltpu.make_async_copy(self._mine(vm, i, half, r0, size), self._window(dst, i, me, half, r0, size),
                                           loc.at[self.n + 2 * q + half])
                cp.start()
                placed.append(cp)
        for q, (i, r0, size) in enumerate(self.chunks):
            for k, (px, py) in enumerate(chips):
                landed = self._window(dst, i, 2 * px + py, c, r0, size)
                _remote(landed, landed, send.at[6 * q + k], recv.at[6 * q + k], (px, py, c)).wait_recv()
                cp = _remote(landed, landed, send.at[6 * q + 3 + k], recv.at[6 * q + 3 + k], (x, y, 1 - c))
                cp.start()
                forwarded.append(cp)
        for q, (i, r0, size) in enumerate(self.chunks):
            for k, (px, py) in enumerate(chips):
                landed = self._window(dst, i, 2 * px + py, 1 - c, r0, size)
                _remote(landed, landed, send.at[6 * q + 3 + k], recv.at[6 * q + 3 + k], (x, y, 1 - c)).wait_recv()
        for q in range(len(self.chunks)):
            for k, (px, py) in enumerate(chips):
                self._sent(src, dst, sems, q, k, px, py, c, me).wait_send()
        for cp in forwarded:
            cp.wait_send()
        for cp in placed:
            cp.wait()


def _gather_weights(shards):
    plan = _Gather({k: v.shape for k, v in shards.items()})
    n = plan.n

    def body(*refs):
        parts = refs[:n], refs[n:2 * n], refs[2 * n:2 * n + 3], refs[2 * n + 3:]
        plan.start(*parts)
        plan.finish(*parts)

    outs = pl.pallas_call(
        body, name="gather_weights", in_specs=[ANY] * n, out_specs=[ANY] * n, out_shape=plan.out_shape(),
        scratch_shapes=plan.scratch(), compiler_params=pltpu.CompilerParams(vmem_limit_bytes=VMEM_LIMIT),
    )(*[shards[k] for k in plan.names])
    return dict(zip(plan.names, outs))


def _scalar_grid_call(name, body, scalars, grid, in_specs, out_specs, out_shape, args):
    return pl.pallas_call(
        body, name=name, out_shape=out_shape,
        grid_spec=pltpu.PrefetchScalarGridSpec(num_scalar_prefetch=1, grid=grid, in_specs=in_specs, out_specs=out_specs),
        compiler_params=_params(*["arbitrary"] * len(grid)),
    )(scalars, *args)


def _pair_sum(name, g4, sib, where, tb=256):
    j, _, r, w = g4.shape
    tb = _row_tile(r, tb)

    def body(s_ref, g_ref, b_ref, o_ref):
        o_ref[...] = (g_ref[...].astype(F32) + b_ref[...].astype(F32)).astype(o_ref.dtype)

    blk = pl.BlockSpec((None, tb, w), lambda a, i, s: (a, i, 0))
    return _scalar_grid_call(
        name, body, where, (j, r // tb),
        [pl.BlockSpec((None, None, tb, w), lambda a, i, s: (a, s[0], i, 0)), blk], blk,
        jax.ShapeDtypeStruct((j, r, w), BF16), (g4, sib))


def _chip_sum(name, partial, got, where, row_sharded, tb=256):
    _, r, cdim = got.shape
    tb = _row_tile(r, tb)

    def body(s_ref, p_ref, g_ref, o_ref):
        acc = p_ref[...].astype(F32)
        for k in range(N_CHIPS - 1):
            acc = acc + g_ref[k].astype(F32)
        o_ref[...] = acc

    if row_sharded:
        own = pl.BlockSpec((None, tb, cdim), lambda i, s: (s[1], i, 0))
    else:
        own = pl.BlockSpec((None, tb, cdim), lambda i, s: (0, i, s[1]))
    return _scalar_grid_call(
        name, body, where, (r // tb,),
        [own, pl.BlockSpec((N_CHIPS - 1, tb, cdim), lambda i, s: (0, i, 0))],
        pl.BlockSpec((None, tb, cdim), lambda i, s: (s[0], i, 0)),
        jax.ShapeDtypeStruct((2, r, cdim), F32), (partial, got))


def _where():
    return jnp.stack([lax.axis_index("c"), 2 * lax.axis_index("x") + lax.axis_index("y")]).astype(jnp.int32)


def _grad_halves(name, g):
    rows, cols = g.shape
    if name in ROW_SHARDED:
        r = rows // N_CHIPS
        return g.reshape(N_CHIPS, 2, r // 2, cols), (r // 2, cols)
    return g.reshape(1, 2, rows // 2, cols), (rows // 2, cols // N_CHIPS)


def _pair_sums(tag, grads):
    names = list(grads)
    n = len(names)
    g4 = [_grad_halves(k, grads[k])[0] for k in names]
    plan = [[(j, r0, size) for j in range(g.shape[0]) for r0, size in _row_chunks(g.shape[2], g.shape[3] * 2)]
            for g in g4]
    out_shape = [jax.ShapeDtypeStruct((g.shape[0],) + g.shape[2:], BF16) for g in g4]

    def body(*refs):
        src, sib = refs[:n], refs[n:2 * n]
        cps = _Copies(*refs[2 * n:])
        x, y, c, _ = _position()
        waits = []
        for i in range(n):
            for j, r0, size in plan[i]:
                waits.append(cps.remote(cps.slot(), _rows(src[i], r0, size, lead=(j, 1 - c)),
                                        _rows(sib[i], r0, size, lead=(j,)), (x, y, 1 - c)))
        for cp in waits:
            cp.wait_recv()
        cps.finish()

    sibs = _comm_call("grads_pair_exchange_" + tag, body, g4, out_shape, sum(len(p) for p in plan), 0)
    where = _where()
    return {k: _pair_sum(k + "_pair_sum", g, s, where) for k, g, s in zip(names, g4, sibs)}


class _ChipExchange:
    def __init__(self, partial):
        self.names = list(partial)
        self.n = len(self.names)
        self.row_sharded = [k in ROW_SHARDED for k in self.names]
        self.half = []
        for k, rs in zip(self.names, self.row_sharded):
            _, r, w = partial[k].shape
            self.half.append((r, w) if rs else (r, w // N_CHIPS))
        self.chunks = [(i, r0, size) for i, (r, cdim) in enumerate(self.half) for r0, size in _row_chunks(r, cdim * 2)]

    def out_shape(self):
        return [jax.ShapeDtypeStruct((N_CHIPS - 1,) + s, BF16) for s in self.half]

    def scratch(self):
        n_remote = 3 * len(self.chunks)
        return [pltpu.SemaphoreType.DMA((n_remote,)), pltpu.SemaphoreType.DMA((n_remote,))]

    def _copies(self, src, dst, sems):
        x, y, c, chips = _position()
        for q, (i, r0, size) in enumerate(self.chunks):
            for k, (px, py) in enumerate(chips):
                chip = 2 * px + py
                if self.row_sharded[i]:
                    shard = _rows(src[i], r0, size, lead=(chip,))
                else:
                    shard = _rows(_shard_cols(src[i].at[0], False, self.half[i][1], chip), r0, size)
                yield _remote(shard, _rows(dst[i], r0, size, lead=(k,)), sems[0].at[3 * q + k], sems[1].at[3 * q + k],
                              (px, py, c))

    def start(self, src, dst, sems):
        for cp in self._copies(src, dst, sems):
            cp.start()

    def finish(self, src, dst, sems):
        for cp in self._copies(src, dst, sems):
            cp.wait_recv()
        for cp in self._copies(src, dst, sems):
            cp.wait_send()


def _chip_exchange(partial):
    plan = _ChipExchange(partial)
    n = plan.n

    def body(*refs):
        parts = refs[:n], refs[n:2 * n], refs[2 * n:]
        plan.start(*parts)
        plan.finish(*parts)

    got = pl.pallas_call(
        body, name="grads_chip_exchange", in_specs=[ANY] * n, out_specs=[ANY] * n, out_shape=plan.out_shape(),
        scratch_shapes=plan.scratch(),
    )(*[partial[k] for k in plan.names])
    return dict(zip(plan.names, got))


def _finish_reduce(partial, got):
    names = list(partial)
    n = len(names)
    where = _where()
    halves = [_chip_sum(k + "_chip_sum", partial[k], got[k], where, k in ROW_SHARDED) for k in names]
    plan = [_row_chunks(h.shape[1], h.shape[2] * 4) for h in halves]
    out_shape = [jax.ShapeDtypeStruct(h.shape, F32) for h in halves]

    def body(*refs):
        src, dst = refs[:n], refs[n:2 * n]
        cps = _Copies(*refs[2 * n:])
        x, y, c, _ = _position()
        waits = []
        for i in range(n):
            for r0, size in plan[i]:
                waits.append(cps.remote(cps.slot(), _rows(src[i], r0, size, lead=(c,)), _rows(dst[i], r0, size, lead=(c,)),
                                        (x, y, 1 - c)))
        for cp in waits:
            cp.wait_recv()
        cps.finish()

    outs = _comm_call("grads_pair_share", body, halves, out_shape, sum(len(p) for p in plan), 0,
                      aliases={i: i for i in range(n)})
    return {k: o.reshape(2 * o.shape[1], o.shape[2]) for k, o in zip(names, outs)}


def _all_reduce_small(name, v):
    rows, cols = v.shape

    def body(v_ref, o_ref, buf, send, recv):
        x, y, c, _ = _position()
        me = 4 * x + 2 * y + c
        buf[me] = v_ref[...]
        peers = []
        for k in range(1, N_DEV):
            fx, fy, fc = (k >> 2) & 1, (k >> 1) & 1, k & 1
            peers.append((x + fx - 2 * fx * x, y + fy - 2 * fy * y, c + fc - 2 * fc * c))
        sends = []
        for k, peer in enumerate(peers):
            cp = _remote(v_ref, buf.at[me], send.at[k], recv.at[k], peer)
            cp.start()
            sends.append(cp)
        for k, (px, py, pc) in enumerate(peers):
            _remote(v_ref, buf.at[4 * px + 2 * py + pc], send.at[k], recv.at[k], (px, py, pc)).wait_recv()
        for cp in sends:
            cp.wait_send()
        acc = buf[0]
        for d in range(1, N_DEV):
            acc = acc + buf[d]
        o_ref[...] = acc

    vm = pl.BlockSpec(memory_space=pltpu.VMEM)
    return pl.pallas_call(
        body, name=name, in_specs=[vm], out_specs=vm, out_shape=jax.ShapeDtypeStruct(v.shape, F32),
        scratch_shapes=[pltpu.VMEM((N_DEV, rows, cols), F32), pltpu.SemaphoreType.DMA((N_DEV - 1,)),
                        pltpu.SemaphoreType.DMA((N_DEV - 1,))],
        compiler_params=pltpu.CompilerParams(vmem_limit_bytes=VMEM_LIMIT),
    )(v)


def _adamw(name, w, g, m, v):
    def fn(wt, gt, mt, vt):
        mt = ADAM_B1 * mt + (1.0 - ADAM_B1) * gt
        vt = ADAM_B2 * vt + (1.0 - ADAM_B2) * (gt * gt)
        m_hat = mt / (1.0 - ADAM_B1 ** ADAM_STEP)
        v_hat = vt / (1.0 - ADAM_B2 ** ADAM_STEP)
        delta = -ADAM_LR * (m_hat / (jnp.sqrt(v_hat) + ADAM_EPS) + ADAM_WD * wt)
        return delta, mt, vt
    n = w.shape[1]
    return _rowwise(name, fn, [w, g, m, v], [], [(n, F32)] * 3, tb=256)


WEIGHTS = ("mem_norm", "a_norm_mix", "a_w_in", "a_w_group", "a_scale", "a_w_mem_kv", "a_w_out", "a_norm_ffn", "a_w_gu",
           "a_w_down", "kv_norm", "w_kv", "b_norm_mix", "b_w_q", "b_w_mem_kv", "b_w_out", "b_norm_ffn", "b_w_gu",
           "b_w_down", "final_norm")
REPLICATED_VECS = ("mem_norm", "kv_norm", "b_norm_mix", "b_norm_ffn", "final_norm")
SHARDED_VECS = ("a_norm_mix", "a_norm_ffn", "a_scale")
D_MODEL = 1024
GROUP_ROWS = 4 * POOL_GROUP * POOL_GROUP // D_MODEL


def _row(v):
    v = v.reshape(1, -1).astype(F32)
    return jnp.pad(v, ((0, 0), (0, D_MODEL - v.shape[1])))


def _pack_small(t):
    rows = [_row(t[k]) for k in REPLICATED_VECS]
    rows.append(_row(jnp.concatenate([t[k].reshape(-1) for k in SHARDED_VECS])))
    rows.append(jnp.zeros((2, D_MODEL), F32))
    rows.append(t["a_w_group"].reshape(GROUP_ROWS, D_MODEL).astype(F32))
    return jnp.concatenate(rows, axis=0)


def _unpack_small(p, like):
    out = {k: p[i, :].reshape(like[k].shape) for i, k in enumerate(REPLICATED_VECS)}
    off = 0
    for k in SHARDED_VECS:
        size = like[k].size
        out[k] = p[len(REPLICATED_VECS), off:off + size].reshape(like[k].shape)
        off += size
    out["a_w_group"] = p[len(REPLICATED_VECS) + 3:, :].reshape(like["a_w_group"].shape)
    return out


def kernel(x, mem, mem_norm, a_norm_mix, a_w_in, a_w_group, a_scale, a_w_mem_kv, a_w_out, a_norm_ffn, a_w_gu, a_w_down, kv_norm, w_kv, b_norm_mix, b_w_q, b_w_mem_kv, b_w_out, b_norm_ffn, b_w_gu, b_w_down, final_norm, loss_target, m_mem_norm, m_a_norm_mix, m_a_w_in, m_a_w_group, m_a_scale, m_a_w_mem_kv, m_a_w_out, m_a_norm_ffn, m_a_w_gu, m_a_w_down, m_kv_norm, m_w_kv, m_b_norm_mix, m_b_w_q, m_b_w_mem_kv, m_b_w_out, m_b_norm_ffn, m_b_w_gu, m_b_w_down, m_final_norm, v_mem_norm, v_a_norm_mix, v_a_w_in, v_a_w_group, v_a_scale, v_a_w_mem_kv, v_a_w_out, v_a_norm_ffn, v_a_w_gu, v_a_w_down, v_kv_norm, v_w_kv, v_b_norm_mix, v_b_w_q, v_b_w_mem_kv, v_b_w_out, v_b_norm_ffn, v_b_w_gu, v_b_w_down, v_final_norm):
    given = dict(locals())
    wl = {k: given[k] for k in WEIGHTS}
    ml = {k: given["m_" + k] for k in WEIGHTS}
    vl = {k: given["v_" + k] for k in WEIGHTS}
    chip = 2 * lax.axis_index("x") + lax.axis_index("y")

    def mat(a):
        return a.reshape(a.shape[-2], a.shape[-1])

    shards = {k: mat(wl[k]).astype(BF16) for k in BIG}
    full = _gather_weights({k: shards[k] for k in BIG if k not in LATE})
    gains = jnp.zeros((8, D_MODEL), F32)
    for i, k in enumerate(SHARDED_VECS):
        part = wl[k].reshape(1, -1)
        width = part.shape[1]
        gains = lax.dynamic_update_slice(gains, part, (i, chip * width))
    gains = _all_reduce_small("gains_all_gather", gains) * 0.5
    w = dict(full)
    for k in REPLICATED_VECS:
        w[k] = wl[k].reshape(1, D_MODEL)
    w["a_norm_mix"], w["a_norm_ffn"] = gains[0:1], gains[1:2]
    w["a_scale"] = gains[2:3, :SB_WIDTH]
    w["a_w_group"] = wl["a_w_group"][0].astype(BF16)

    loss, grad_x, g, partial, got = _local_step(x[0], mem[0], loss_target[0], w, {k: shards[k] for k in LATE})

    rest = _pair_sums("late", {k: g[k] for k in BIG if k not in EARLY})
    partial.update(rest)
    got.update(_chip_exchange(rest))
    red = _finish_reduce(partial, got)
    small = jnp.concatenate(
        [_row(g[k]) for k in REPLICATED_VECS] + [_row(g[k]) for k in SHARDED_VECS] + [_row(loss)]
        + [jnp.zeros((7, D_MODEL), F32), g["a_w_group"].reshape(GROUP_ROWS, D_MODEL)], axis=0)
    small = _all_reduce_small("small_grads_all_reduce", small)
    gs = {k: small[i] for i, k in enumerate(REPLICATED_VECS)}
    for i, k in enumerate(SHARDED_VECS):
        width = wl[k].shape[-1]
        gs[k] = lax.dynamic_slice(small[len(REPLICATED_VECS) + i], (chip * width,), (width,))
    gs["a_w_group"] = small[16:]
    total_loss = small[8, 0]

    out_g, out_d, out_m, out_v = {}, {}, {}, {}
    for k in BIG:
        shape = wl[k].shape
        out_g[k] = red[k].reshape(shape)
        d, nm, nv = _adamw(k + "_adamw", mat(wl[k]), red[k], mat(ml[k]), mat(vl[k]))
        out_d[k], out_m[k], out_v[k] = d.reshape(shape), nm.reshape(shape), nv.reshape(shape)
    small_names = REPLICATED_VECS + SHARDED_VECS + ("a_w_group",)
    d, nm, nv = _adamw("small_adamw", _pack_small(wl), _pack_small(gs), _pack_small(ml), _pack_small(vl))
    like = {k: wl[k] for k in small_names}
    for dst, p in ((out_d, d), (out_m, nm), (out_v, nv)):
        dst.update(_unpack_small(p, like))
    for k in small_names:
        out_g[k] = gs[k].reshape(wl[k].shape)

    return (total_loss, grad_x[None], *[out_g[k] for k in WEIGHTS], *[out_d[k] for k in WEIGHTS],
            *[out_m[k] for k in WEIGHTS], *[out_v[k] for k in WEIGHTS])
```

```python
import functools

import jax
import jax.numpy as jnp
from jax import lax
from jax.experimental import pallas as pl
from jax.experimental.pallas import tpu as pltpu

F32 = jnp.float32
BF16 = jnp.bfloat16

HEAD_DIM = 64
SB_WIDTH = 768
MEM_WIDTH = 256
POOL_WINDOWS = (2, 4, 8, 16)
POOL_GROUP = 192
POOL_HALO = 16
EPS = 1e-6
ATT_SCALE = HEAD_DIM ** -0.5
ADAM_LR, ADAM_B1, ADAM_B2, ADAM_EPS, ADAM_WD, ADAM_STEP = 0.001, 0.9, 0.999, 1e-08, 0.01, 10

LANES = 128
SB_TQ, SB_TK = 512, 512
VMEM_LIMIT = 56 * 1024 * 1024
MESH = pl.DeviceIdType.MESH
COPY_BYTES = 512 * 1024
ANY = pl.BlockSpec(memory_space=pl.ANY)


def _params(*sem):
    return pltpu.CompilerParams(dimension_semantics=sem, vmem_limit_bytes=VMEM_LIMIT)


def _tile(n, pref):
    if n <= pref:
        return n
    best = None
    for t in range(LANES, pref + 1, LANES):
        if n % t == 0:
            best = t
    assert best is not None, (n, pref)
    return best


def _row_tile(t, pref):
    if t <= pref:
        return t
    for tb in range(pref - pref % 16, 0, -16):
        if t % tb == 0:
            return tb
    raise ValueError((t, pref))


def _rowwise(name, fn, rows, vecs, row_outs, sum_outs=(), tb=512):
    norm_rows = []
    for r in rows:
        if isinstance(r, tuple):
            arr, (bc, cb) = r
        else:
            arr, (bc, cb) = r, (r.shape[1], 0)
        norm_rows.append((arr, bc, cb))
    t = norm_rows[0][0].shape[0]
    tb = _row_tile(t, tb)
    n_in, n_ro = len(norm_rows) + len(vecs), len(row_outs)

    def body(*refs):
        ins = [r[...] for r in refs[:n_in]]
        outs = fn(*ins)
        if not isinstance(outs, tuple):
            outs = (outs,)
        for o_ref, o in zip(refs[n_in:n_in + n_ro], outs[:n_ro]):
            o_ref[...] = o.astype(o_ref.dtype)
        for s_ref, s in zip(refs[n_in + n_ro:], outs[n_ro:]):
            @pl.when(pl.program_id(0) == 0)
            def _():
                s_ref[...] = jnp.zeros_like(s_ref)
            s_ref[...] += s

    in_specs = [pl.BlockSpec((tb, bc), functools.partial(lambda i, cb: (i, cb), cb=cb)) for _, bc, cb in norm_rows]
    in_specs += [pl.BlockSpec(v.shape, lambda i: (0, 0)) for v in vecs]
    out_specs = [pl.BlockSpec((tb, c), lambda i: (i, 0)) for c, _ in row_outs]
    out_specs += [pl.BlockSpec((1, c), lambda i: (0, 0)) for c in sum_outs]
    out_shape = [jax.ShapeDtypeStruct((t, c), d) for c, d in row_outs]
    out_shape += [jax.ShapeDtypeStruct((1, c), F32) for c in sum_outs]
    res = pl.pallas_call(
        body, name=name, grid=(t // tb,), in_specs=in_specs, out_specs=out_specs, out_shape=out_shape,
        compiler_params=_params("arbitrary"),
    )(*[a for a, _, _ in norm_rows], *vecs)
    return res


def _rms_fwd(name, x, g):
    def fn(xt, gt):
        rstd = lax.rsqrt(jnp.mean(xt * xt, axis=-1, keepdims=True) + EPS)
        return xt * rstd * gt
    return _rowwise(name, fn, [x], [g], [(x.shape[1], BF16)])[0]


def _rms_bwd(name, x, g, dh, dres=None, want_dx=True):
    has_res = dres is not None

    def fn(*a):
        if has_res:
            xt, dht, drt, gt = a
        else:
            xt, dht, gt = a
        rstd = lax.rsqrt(jnp.mean(xt * xt, axis=-1, keepdims=True) + EPS)
        xhat = xt * rstd
        dht = dht.astype(F32)
        dg = jnp.sum(dht * xhat, axis=0, keepdims=True)
        if not want_dx:
            return (dg,)
        dxhat = dht * gt
        dx = rstd * (dxhat - xhat * jnp.mean(dxhat * xhat, axis=-1, keepdims=True))
        if has_res:
            dx = dx + drt
        return dx, dx, dg

    d = x.shape[1]
    rows = [x, dh] + ([dres] if has_res else [])
    outs = [(d, F32), (d, BF16)] if want_dx else []
    return _rowwise(name, fn, rows, [g], outs, [d])


_DOT_DIMS = {"nn": ((1,), (0,)), "nt": ((1,), (1,)), "tn": ((0,), (0,))}


def _mm(name, a, b, mode, out_dtype, res=None, tm=512, tn=512):
    if mode == "nn":
        (m, k), (k2, n) = a.shape, b.shape
    elif mode == "nt":
        (m, k), (n, k2) = a.shape, b.shape
    else:
        (k, m), (k2, n) = a.shape, b.shape
    assert k == k2, (name, a.shape, b.shape)
    tm, tn = _tile(m, tm), _tile(n, tn)
    dims = (_DOT_DIMS[mode], ((), ()))
    has_res = res is not None

    def body(a_ref, b_ref, *rest):
        acc = lax.dot_general(a_ref[...], b_ref[...], dims, preferred_element_type=F32)
        if has_res:
            acc = acc + rest[0][...]
        rest[-1][...] = acc.astype(out_dtype)

    a_spec = pl.BlockSpec((k, tm), lambda i, j: (0, i)) if mode == "tn" else pl.BlockSpec((tm, k), lambda i, j: (i, 0))
    b_spec = pl.BlockSpec((tn, k), lambda i, j: (j, 0)) if mode == "nt" else pl.BlockSpec((k, tn), lambda i, j: (0, j))
    o_spec = pl.BlockSpec((tm, tn), lambda i, j: (i, j))
    in_specs, args = [a_spec, b_spec], [a, b]
    if has_res:
        in_specs.append(o_spec)
        args.append(res)
    return pl.pallas_call(
        body, name=name, grid=(m // tm, n // tn), in_specs=in_specs, out_specs=o_spec,
        out_shape=jax.ShapeDtypeStruct((m, n), out_dtype), compiler_params=_params("parallel", "arbitrary"),
    )(*args)


def _pool(name, u, reverse, tb=512):
    t = u.shape[0]
    tb = min(tb, t)
    nt = t // tb
    c = SB_WIDTH
    hpb = tb // POOL_HALO

    def body(cur_ref, halo_ref, o_ref):
        i = pl.program_id(0)
        cur = cur_ref[...].astype(F32)
        edge = (i == nt - 1) if reverse else (i == 0)
        halo = jnp.where(edge, 0.0, halo_ref[...].astype(F32))
        col = lax.broadcasted_iota(jnp.int32, (tb + POOL_HALO, c), 1)
        row = lax.broadcasted_iota(jnp.int32, (tb + POOL_HALO, c), 0)
        wcol = jnp.where(col < POOL_GROUP, 2, jnp.where(col < 2 * POOL_GROUP, 4, jnp.where(col < 3 * POOL_GROUP, 8, 16)))
        n = tb + POOL_HALO
        if reverse:
            ext = jnp.concatenate([cur, halo], axis=0)
            tpos = i * tb + row
            ext = ext / jnp.minimum(tpos + 1, wcol).astype(F32)
            shift = lambda a, k: pltpu.roll(a, n - k, 0)
        else:
            ext = jnp.concatenate([halo, cur], axis=0)
            shift = lambda a, k: pltpu.roll(a, k, 0)
        s2 = ext + shift(ext, 1)
        s4 = s2 + shift(s2, 2)
        s8 = s4 + shift(s4, 4)
        s16 = s8 + shift(s8, 8)
        win = jnp.where(wcol == 2, s2, jnp.where(wcol == 4, s4, jnp.where(wcol == 8, s8, s16)))
        if reverse:
            out = win[:tb] - cur
        else:
            tpos = i * tb + row[POOL_HALO:] - POOL_HALO
            out = win[POOL_HALO:] / jnp.minimum(tpos + 1, wcol[POOL_HALO:]).astype(F32) - cur
        o_ref[...] = out.astype(o_ref.dtype)

    if reverse:
        halo_map = lambda i: (jnp.minimum((i + 1) * hpb, t // POOL_HALO - 1), 0)
    else:
        halo_map = lambda i: (jnp.maximum(i * hpb - 1, 0), 0)
    return pl.pallas_call(
        body, name=name, grid=(nt,),
        in_specs=[pl.BlockSpec((tb, c), lambda i: (i, 0)), pl.BlockSpec((POOL_HALO, c), halo_map)],
        out_specs=pl.BlockSpec((tb, c), lambda i: (i, 0)),
        out_shape=jax.ShapeDtypeStruct((t, c), BF16), compiler_params=_params("arbitrary"),
    )(u, u)


def _head_masks(shape):
    lane = lax.broadcasted_iota(jnp.int32, shape, 1)
    return lane < HEAD_DIM, lane >= HEAD_DIM


def _pick(mask, a):
    return jnp.where(mask, a, jnp.zeros_like(a))


def _dot(a, b, mode):
    return lax.dot_general(a, b, (_DOT_DIMS[mode], ((), ())), preferred_element_type=F32)


def _dot_tri(a, tri, suffix):
    h = a.shape[1] // 2
    lo, hi = a[:, :h], a[:, h:]
    s_lo, s_hi = jnp.sum(lo, axis=1, keepdims=True), jnp.sum(hi, axis=1, keepdims=True)
    p_lo, p_hi = _dot(lo.astype(BF16), tri, "nn"), _dot(hi.astype(BF16), tri, "nn")
    if suffix:
        p_lo = p_lo + s_hi
    else:
        p_hi = p_hi + s_lo
    return jnp.concatenate([p_lo, p_hi], axis=1), s_lo + s_hi


def _log_gates(z):
    nz = -z
    l = jnp.log(1.0 + jnp.exp(jnp.minimum(z, nz)))
    ln = jnp.minimum(nz, 0.0) - l
    return ln, z + ln


def _sb_blocks(s, tq, tk):
    tq, tk = min(tq, s), min(tk, s)
    assert tq == tk and s % tk == 0 and tk % 64 == 0, (s, tq, tk)
    return tq, tk, tk // 2


def _strict_triangle(n, pred):
    return pred(lax.broadcasted_iota(jnp.int32, (n, n), 0), lax.broadcasted_iota(jnp.int32, (n, n), 1)).astype(BF16)


def _sb_fwd(proj, kv, late_shards, tq=SB_TQ, tk=SB_TK):
    s = proj.shape[0]
    tq, tk, th = _sb_blocks(s, tq, tk)
    npair = SB_WIDTH // LANES
    gather = _Gather({k: v.shape for k, v in late_shards.items()})
    ng = gather.n

    def body(q_ref, k_ref, v_ref, *rest):
        o_ref, tot_ref = rest[ng:ng + 2]
        comm = rest[:ng], rest[ng + 2:2 * ng + 2], rest[2 * ng + 2:2 * ng + 5], rest[2 * ng + 5:]

        @pl.when(pl.program_id(0) == 0)
        def _():
            gather.start(*comm)

        tri_gt = _strict_triangle(tk, lambda j, s_: j > s_)
        tri_gt_h = _strict_triangle(th, lambda j, s_: j > s_)
        seen = lax.broadcasted_iota(jnp.int32, (tq, th), 1) < lax.broadcasted_iota(jnp.int32, (tq, th), 0)
        m_a, m_b = _head_masks((tq, LANES))

        def block(qh, k2, v2, lane_mask, carry, acc, mask, tri):
            ln_full, lsz = _log_gates(_dot(qh, k2, "nt"))
            ln = ln_full if mask is None else jnp.where(mask, ln_full, 0.0)
            w = jnp.exp(lsz + _dot(ln.astype(BF16), tri, "nn"))
            if mask is not None:
                w = jnp.where(mask, w, 0.0)
            acc = acc + jnp.exp(carry) * _dot(w.astype(BF16), _pick(lane_mask[:v2.shape[0]], v2), "nn")
            return carry + jnp.sum(ln, axis=1, keepdims=True), acc

        def q_block(qi, _):
            q0 = pl.multiple_of(qi * tq, tq)
            q2 = q_ref[pl.ds(q0, tq), :] * ATT_SCALE
            qa, qb = _pick(m_a, q2), _pick(m_b, q2)

            def both(qa, qb, k0, size, ca, cb, acc, mask, tri):
                k2 = k_ref[pl.ds(k0, size), :]
                v2 = v_ref[pl.ds(k0, size), :]
                ca, acc = block(qa, k2, v2, m_a, ca, acc, mask, tri)
                cb, acc = block(qb, k2, v2, m_b, cb, acc, mask, tri)
                return ca, cb, acc

            zero_c = jnp.zeros((th, 1), F32)
            zero_o = jnp.zeros((th, LANES), F32)
            ca, cb, acc = both(qa[th:], qb[th:], pl.multiple_of(q0 + th, th), th, zero_c, zero_c, zero_o, seen[:th], tri_gt_h)
            ca, cb = jnp.concatenate([zero_c, ca], axis=0), jnp.concatenate([zero_c, cb], axis=0)
            carry = both(qa, qb, q0, th, ca, cb, jnp.concatenate([zero_o, acc], axis=0), seen, tri_gt_h)

            def k_block(step, carry):
                return both(qa, qb, pl.multiple_of(q0 - (step + 1) * tk, tk), tk, *carry, None, tri_gt)

            ca, cb, acc = lax.fori_loop(0, qi, k_block, carry)
            o_ref[pl.ds(q0, tq), :] = acc.astype(o_ref.dtype)
            tot_ref[0, pl.ds(q0, tq), :] = jnp.broadcast_to(ca, (tq, LANES))
            tot_ref[1, pl.ds(q0, tq), :] = jnp.broadcast_to(cb, (tq, LANES))
            return 0

        lax.fori_loop(0, s // tq, q_block, 0)

        @pl.when(pl.program_id(0) == npair - 1)
        def _():
            gather.finish(*comm)

    outs = pl.pallas_call(
        body, name="sb_fwd", grid=(npair,),
        in_specs=[pl.BlockSpec((s, LANES), lambda p: (0, p)), pl.BlockSpec((s, LANES), lambda p: (0, p)),
                  pl.BlockSpec((s, LANES), lambda p: (0, npair + p))] + [ANY] * ng,
        out_specs=[pl.BlockSpec((s, LANES), lambda p: (0, p)), pl.BlockSpec((None, 2, s, LANES), lambda p: (p, 0, 0, 0))]
        + [ANY] * ng,
        out_shape=[jax.ShapeDtypeStruct((s, SB_WIDTH), BF16), jax.ShapeDtypeStruct((npair, 2, s, LANES), F32)]
        + gather.out_shape(),
        scratch_shapes=gather.scratch(), compiler_params=_params("arbitrary"),
    )(proj, kv, kv, *[late_shards[k] for k in gather.names])
    return outs[0], outs[1], dict(zip(gather.names, outs[2:]))


def _sb_bwd(proj, kv, dcat, tot, early_partial, tq=SB_TQ, tk=SB_TK):
    s = proj.shape[0]
    tq, tk, th = _sb_blocks(s, tq, tk)
    npair = SB_WIDTH // LANES
    exchange = _ChipExchange(early_partial)
    ne = exchange.n

    def body(q_ref, k_ref, v_ref, do_ref, tot_ref, *rest):
        dq_ref, dk_ref, dv_ref = rest[ne:ne + 3]
        dk_acc, dv_acc = rest[2 * ne + 3:2 * ne + 5]
        comm = rest[:ne], rest[ne + 3:2 * ne + 3], rest[2 * ne + 5:]

        @pl.when(pl.program_id(0) == 0)
        def _():
            exchange.start(*comm)

        tris = (_strict_triangle(tk // 2, lambda j, s_: j > s_), _strict_triangle(tk // 2, lambda j, s_: j < s_))
        tris_h = (_strict_triangle(th // 2, lambda j, s_: j > s_), _strict_triangle(th // 2, lambda j, s_: j < s_))
        seen = lax.broadcasted_iota(jnp.int32, (tq, th), 1) < lax.broadcasted_iota(jnp.int32, (tq, th), 0)
        m_a, m_b = _head_masks((tq, LANES))
        dk_acc[...] = jnp.zeros_like(dk_acc)
        dv_acc[...] = jnp.zeros_like(dv_acc)

        def block(qh, doh, k2, v2, lane_mask, tot_h, carry, mask, tri):
            c_ln, c_d, dq = carry
            ln_full, lsz = _log_gates(_dot(qh, k2, "nt"))
            ln = ln_full if mask is None else jnp.where(mask, ln_full, 0.0)
            inside, total = _dot_tri(ln, tri[0], True)
            c_ln = c_ln + total
            w = jnp.exp(lsz + ((tot_h - c_ln) + inside))
            if mask is not None:
                w = jnp.where(mask, w, 0.0)
            dlw = _dot(doh, v2, "nt") * w
            before, d_total = _dot_tri(dlw, tri[1], False)
            dz = dlw * jnp.exp(ln_full) - (before + c_d) * jnp.exp(lsz)
            if mask is not None:
                dz = jnp.where(mask, dz, 0.0)
            dz = dz.astype(BF16)
            dq = dq + _dot(dz, _pick(lane_mask[:k2.shape[0]], k2), "nn")
            dk = _dot(dz, qh, "tn")
            dv = _dot(w.astype(BF16), doh, "tn")
            carry = (c_ln, c_d + d_total, dq)
            return carry, dk, dv

        def q_block(qi, _):
            q0 = pl.multiple_of(qi * tq, tq)
            q2 = q_ref[pl.ds(q0, tq), :] * ATT_SCALE
            do2 = do_ref[pl.ds(q0, tq), :]
            qa, qb = _pick(m_a, q2), _pick(m_b, q2)
            doa, dob = _pick(m_a, do2), _pick(m_b, do2)
            tot_a = tot_ref[0, pl.ds(q0, tq), 0:1]
            tot_b = tot_ref[1, pl.ds(q0, tq), 0:1]
            zero_c = jnp.zeros((tq, 1), F32)
            zero_q = jnp.zeros((tq, LANES), F32)

            def both(rows, k0, size, ca, cb, mask, tri):
                k2 = k_ref[pl.ds(k0, size), :]
                v2 = v_ref[pl.ds(k0, size), :]
                ca, dka, dva = block(qa[rows], doa[rows], k2, v2, m_a, tot_a[rows], ca, mask, tri)
                cb, dkb, dvb = block(qb[rows], dob[rows], k2, v2, m_b, tot_b[rows], cb, mask, tri)
                dk_acc[pl.ds(k0, size), :] += dka + dkb
                dv_acc[pl.ds(k0, size), :] += dva + dvb
                return ca, cb

            def k_block(kj, carry):
                return both(slice(None), pl.multiple_of(kj * tk, tk), tk, carry[0], carry[1], None, tris)

            init = ((zero_c, zero_c, zero_q), (zero_c, zero_c, zero_q))
            ca, cb = lax.fori_loop(0, qi, k_block, init)
            ca, cb = both(slice(None), q0, th, ca, cb, seen, tris_h)
            late = slice(th, tq)
            la, lb = both(late, pl.multiple_of(q0 + th, th), th, tuple(t[late] for t in ca), tuple(t[late] for t in cb),
                          seen[:th], tris_h)
            dq = jnp.concatenate([ca[2][:th] + cb[2][:th], la[2] + lb[2]], axis=0)
            dq_ref[pl.ds(q0, tq), :] = (dq * ATT_SCALE).astype(dq_ref.dtype)
            return 0

        lax.fori_loop(0, s // tq, q_block, 0)
        dk_ref[...] = dk_acc[...].astype(dk_ref.dtype)
        dv_ref[...] = dv_acc[...].astype(dv_ref.dtype)

        @pl.when(pl.program_id(0) == npair - 1)
        def _():
            exchange.finish(*comm)

    col = lambda off: pl.BlockSpec((s, LANES), functools.partial(lambda p, off: (0, off + p), off=off))
    outs = pl.pallas_call(
        body, name="sb_bwd", grid=(npair,),
        in_specs=[col(0), col(0), col(npair), col(0), pl.BlockSpec((None, 2, s, LANES), lambda p: (p, 0, 0, 0))]
        + [ANY] * ne,
        out_specs=[col(0), col(0), col(0)] + [ANY] * ne,
        out_shape=[jax.ShapeDtypeStruct((s, SB_WIDTH), BF16)] * 3 + exchange.out_shape(),
        scratch_shapes=[pltpu.VMEM((s, LANES), F32), pltpu.VMEM((s, LANES), F32)] + exchange.scratch(),
        compiler_params=_params("arbitrary"),
    )(proj, kv, kv, dcat, tot, *[early_partial[k] for k in exchange.names])
    dq, dk, dv = outs[:3]
    return dq, jnp.concatenate([dk, dv], axis=1), dict(zip(exchange.names, outs[3:]))


def _mem_fwd(name, proj, mkv, tq=512):
    s = proj.shape[0]
    tq = min(tq, s)
    qblk = SB_WIDTH // MEM_WIDTH

    def body(q_ref, kv_ref, o_ref):
        m_a, m_b = _head_masks((tq, LANES))
        mk_a, mk_b = _head_masks((kv_ref.shape[0], LANES))
        for p in range(MEM_WIDTH // LANES):
            q2 = q_ref[:, p * LANES:(p + 1) * LANES]
            k2 = kv_ref[:, p * LANES:(p + 1) * LANES]
            v2 = kv_ref[:, MEM_WIDTH + p * LANES:MEM_WIDTH + (p + 1) * LANES]
            acc = jnp.zeros((tq, LANES), F32)
            for mq, mk in ((m_a, mk_a), (m_b, mk_b)):
                logits = _dot(_pick(mq, q2), k2, "nt") * ATT_SCALE
                e = jnp.exp(logits - jnp.max(logits, axis=-1, keepdims=True))
                prob = e / jnp.sum(e, axis=-1, keepdims=True)
                acc = acc + _dot(prob.astype(BF16), _pick(mk, v2), "nn")
            o_ref[:, p * LANES:(p + 1) * LANES] = acc.astype(o_ref.dtype)

    return pl.pallas_call(
        body, name=name, grid=(s // tq,),
        in_specs=[pl.BlockSpec((tq, MEM_WIDTH), lambda i: (i, qblk)), pl.BlockSpec(mkv.shape, lambda i: (0, 0))],
        out_specs=pl.BlockSpec((tq, MEM_WIDTH), lambda i: (i, 0)),
        out_shape=jax.ShapeDtypeStruct((s, MEM_WIDTH), BF16), compiler_params=_params("arbitrary"),
    )(proj, mkv)


def _mem_bwd(name, proj, mkv, dcat, tq=512):
    s = proj.shape[0]
    tq = min(tq, s)
    qblk = SB_WIDTH // MEM_WIDTH

    def body(q_ref, kv_ref, do_ref, dq_ref, dkv_ref):
        @pl.when(pl.program_id(0) == 0)
        def _():
            dkv_ref[...] = jnp.zeros_like(dkv_ref)

        m_a, m_b = _head_masks((tq, LANES))
        for p in range(MEM_WIDTH // LANES):
            ksl = slice(p * LANES, (p + 1) * LANES)
            vsl = slice(MEM_WIDTH + p * LANES, MEM_WIDTH + (p + 1) * LANES)
            q2, do2 = q_ref[:, ksl], do_ref[:, ksl]
            k2, v2 = kv_ref[:, ksl], kv_ref[:, vsl]
            mk_a, mk_b = _head_masks(k2.shape)
            dq = jnp.zeros((tq, LANES), F32)
            dk = jnp.zeros(k2.shape, F32)
            dv = jnp.zeros(k2.shape, F32)
            for mq, mk in ((m_a, mk_a), (m_b, mk_b)):
                qh, doh = _pick(mq, q2), _pick(mq, do2)
                logits = _dot(qh, k2, "nt") * ATT_SCALE
                e = jnp.exp(logits - jnp.max(logits, axis=-1, keepdims=True))
                prob = e / jnp.sum(e, axis=-1, keepdims=True)
                dp = _dot(doh, v2, "nt")
                ds = prob * (dp - jnp.sum(dp * prob, axis=-1, keepdims=True)) * ATT_SCALE
                ds = ds.astype(BF16)
                dq = dq + _dot(ds, _pick(mk, k2), "nn")
                dk = dk + _dot(ds, qh, "tn")
                dv = dv + _dot(prob.astype(BF16), doh, "tn")
            dq_ref[:, ksl] = dq.astype(dq_ref.dtype)
            dkv_ref[:, ksl] += dk
            dkv_ref[:, vsl] += dv

    return pl.pallas_call(
        body, name=name, grid=(s // tq,),
        in_specs=[pl.BlockSpec((tq, MEM_WIDTH), lambda i: (i, qblk)), pl.BlockSpec(mkv.shape, lambda i: (0, 0)),
                  pl.BlockSpec((tq, MEM_WIDTH), lambda i: (i, qblk))],
        out_specs=[pl.BlockSpec((tq, MEM_WIDTH), lambda i: (i, 0)), pl.BlockSpec(mkv.shape, lambda i: (0, 0))],
        out_shape=[jax.ShapeDtypeStruct((s, MEM_WIDTH), BF16), jax.ShapeDtypeStruct(mkv.shape, F32)],
        compiler_params=_params("arbitrary"),
    )(proj, mkv, dcat)


def _gu_swiglu(name, h, w_gu, tn=256):
    t, d = h.shape
    f = w_gu.shape[1] // 2
    nb = f // tn

    def body(h_ref, wg_ref, wu_ref, g_ref, u_ref, a_ref):
        hh = h_ref[...]
        g = _dot(hh, wg_ref[...], "nn")
        u = _dot(hh, wu_ref[...], "nn")
        g_ref[...] = g.astype(BF16)
        u_ref[...] = u.astype(BF16)
        a_ref[...] = (g * jax.nn.sigmoid(g) * u).astype(BF16)

    out = pl.BlockSpec((t, tn), lambda j: (0, j))
    return pl.pallas_call(
        body, name=name, grid=(nb,),
        in_specs=[pl.BlockSpec((t, d), lambda j: (0, 0)), pl.BlockSpec((d, tn), lambda j: (0, j)),
                  pl.BlockSpec((d, tn), lambda j: (0, nb + j))],
        out_specs=[out, out, out], out_shape=[jax.ShapeDtypeStruct((t, f), BF16)] * 3,
        compiler_params=_params("arbitrary"),
    )(h, w_gu, w_gu)


def _down_dx_swiglu(name, dout_bf, w_down, gate, up, tm=256):
    t, d = dout_bf.shape
    f = w_down.shape[0]

    def body(do_ref, w_ref, g_ref, u_ref, o_ref):
        dact = _dot(do_ref[...], w_ref[...], "nt")
        g, u = g_ref[...].astype(F32), u_ref[...].astype(F32)
        sg = jax.nn.sigmoid(g)
        silu = g * sg
        o_ref[:, :f] = (dact * u * (sg + silu * (1.0 - sg))).astype(BF16)
        o_ref[:, f:] = (dact * silu).astype(BF16)

    row = lambda c: pl.BlockSpec((tm, c), lambda i: (i, 0))
    return pl.pallas_call(
        body, name=name, grid=(t // tm,),
        in_specs=[row(d), pl.BlockSpec((f, d), lambda i: (0, 0)), row(f), row(f)],
        out_specs=row(2 * f), out_shape=jax.ShapeDtypeStruct((t, 2 * f), BF16), compiler_params=_params("arbitrary"),
    )(dout_bf, w_down, gate, up)


def _ffn_fwd(tag, x, norm, w_gu, w_down):
    h = _rms_fwd(tag + "_ffn_norm", x, norm)
    gate, up, act = _gu_swiglu(tag + "_gu", h, w_gu)
    out = _mm(tag + "_down", act, w_down, "nn", F32, res=x, tn=1024)
    return out, (h, gate, up, act)


def _ffn_bwd(tag, x, norm, w_gu, w_down, saved, dout, dout_bf):
    h, gate, up, act = saved
    g_down = _mm(tag + "_down_dw", act, dout_bf, "tn", BF16, tm=256, tn=1024)
    dgu = _down_dx_swiglu(tag + "_down_dx", dout_bf, w_down, gate, up)
    g_gu = _mm(tag + "_gu_dw", h, dgu, "tn", BF16, tm=1024)
    dh = _mm(tag + "_gu_dx", dgu, w_gu, "nt", F32, tn=1024)
    dx, dx_bf, g_norm = _rms_bwd(tag + "_ffn_norm_bwd", x, norm, dh, dres=dout)
    return dx, dx_bf, g_gu, g_down, g_norm


def _mem_kv(tag, mem_n, w_mem_kv):
    return _mm(tag + "_memkv", mem_n, w_mem_kv, "nn", BF16)


def _mem_kv_bwd(tag, mem, mem_norm, mem_n, w_mem_kv, dmkv):
    dmkv = dmkv.astype(BF16)
    g_w = _mm(tag + "_memkv_dw", mem_n, dmkv, "tn", BF16)
    dmem_n = _mm(tag + "_memkv_dx", dmkv, w_mem_kv, "nt", F32)
    (g_norm,) = _rms_bwd(tag + "_memnorm_bwd", mem, mem_norm, dmem_n, want_dx=False)
    return g_w, g_norm


def _block_diag(w_group):
    z = jnp.zeros((POOL_GROUP, POOL_GROUP), w_group.dtype)
    return jnp.concatenate(
        [jnp.concatenate([w_group[g] if h == g else z for h in range(4)], axis=1) for g in range(4)], axis=0)


def _local_step(x, mem, target, w, late_shards):
    g = {}
    w = dict(w)
    mem_n = _rms_fwd("mem_norm", mem, w["mem_norm"])
    h_a = _rms_fwd("a_mix_norm", x, w["a_norm_mix"])
    proj_a = _mm("a_in", h_a, w["a_w_in"], "nn", F32, tn=1024)
    pooled = _pool("a_pool", proj_a, reverse=False)
    w_bd = _block_diag(w["a_w_group"])
    g_pre = _mm("a_group", pooled, w_bd, "nn", BF16, tn=768)
    mkv_a = _mem_kv("a", mem_n, w["a_w_mem_kv"])
    proj_a_bf = proj_a.astype(BF16)
    mem_a = _mem_fwd("a_mem_attn", proj_a_bf, mkv_a)
    cat_a = _rowwise("a_cat", lambda gp, mo, sc: jnp.concatenate([gp.astype(F32) * sc, mo.astype(F32)], axis=1),
                     [g_pre, mem_a], [w["a_scale"]], [(1024, BF16)])[0]
    x1 = _mm("a_out", cat_a, w["a_w_out"], "nn", F32, res=x, tn=1024)
    x2, ffn_a = _ffn_fwd("a", x1, w["a_norm_ffn"], w["a_w_gu"], w["a_w_down"])
    h_k = _rms_fwd("kv_norm", x2, w["kv_norm"])
    kv = _mm("kv_proj", h_k, w["w_kv"], "nn", BF16, tn=1536)
    h_b = _rms_fwd("b_mix_norm", x2, w["b_norm_mix"])
    proj_b = _mm("b_q", h_b, w["b_w_q"], "nn", BF16, tn=1024)
    sb_out, tot, late = _sb_fwd(proj_b, kv, late_shards)
    w.update(late)
    mkv_b = _mem_kv("b", mem_n, w["b_w_mem_kv"])
    mem_b = _mem_fwd("b_mem_attn", proj_b, mkv_b)
    cat_b = jnp.concatenate([sb_out, mem_b], axis=1)
    x3 = _mm("b_out", cat_b, w["b_w_out"], "nn", F32, res=x2, tn=1024)
    x4, ffn_b = _ffn_fwd("b", x3, w["b_norm_ffn"], w["b_w_gu"], w["b_w_down"])

    d = x.shape[1]

    def head(xt, tt, gt):
        rstd = lax.rsqrt(jnp.mean(xt * xt, axis=-1, keepdims=True) + EPS)
        xhat = xt * rstd
        err = xhat * gt - tt
        loss = 0.5 * jnp.sum(jnp.sum(err * err, axis=1, keepdims=True), axis=0, keepdims=True) / d
        dy = err / d
        dxhat = dy * gt
        dx = rstd * (dxhat - xhat * jnp.mean(dxhat * xhat, axis=-1, keepdims=True))
        return dx, dx, jnp.sum(dy * xhat, axis=0, keepdims=True), jnp.broadcast_to(loss, (1, LANES))

    dx4, dx4_bf, g["final_norm"], loss = _rowwise(
        "loss_head", head, [x4, target], [w["final_norm"]], [(d, F32), (d, BF16)], [d, LANES])

    dx3, dx3_bf, g["b_w_gu"], g["b_w_down"], g["b_norm_ffn"] = _ffn_bwd(
        "b", x3, w["b_norm_ffn"], w["b_w_gu"], w["b_w_down"], ffn_b, dx4, dx4_bf)
    dcat_b = _mm("b_out_dx", dx3_bf, w["b_w_out"], "nt", BF16, tn=1024)
    g["b_w_out"] = _mm("b_out_dw", cat_b, dx3_bf, "tn", BF16, tm=1024, tn=1024)
    early_partial = _pair_sums("early", {k: g.pop(k) for k in EARLY})
    dq_sb, dkv, early_got = _sb_bwd(proj_b, kv, dcat_b, tot, early_partial)
    dq_mem_b, dmkv_b = _mem_bwd("b_mem_attn_bwd", proj_b, mkv_b, dcat_b)
    dproj_b = jnp.concatenate([dq_sb, dq_mem_b], axis=1)
    g["b_w_q"] = _mm("b_q_dw", h_b, dproj_b, "tn", BF16, tm=1024, tn=1024)
    dh_b = _mm("b_q_dx", dproj_b, w["b_w_q"], "nt", F32, tn=1024)
    dx2, _, g["b_norm_mix"] = _rms_bwd("b_mix_norm_bwd", x2, w["b_norm_mix"], dh_b, dres=dx3)
    g["b_w_mem_kv"], g_memnorm_b = _mem_kv_bwd("b", mem, w["mem_norm"], mem_n, w["b_w_mem_kv"], dmkv_b)
    g["w_kv"] = _mm("kv_proj_dw", h_k, dkv, "tn", BF16, tm=1024)
    dh_k = _mm("kv_proj_dx", dkv, w["w_kv"], "nt", F32, tn=1024)
    dx2, dx2_bf, g["kv_norm"] = _rms_bwd("kv_norm_bwd", x2, w["kv_norm"], dh_k, dres=dx2)

    dx1, dx1_bf, g["a_w_gu"], g["a_w_down"], g["a_norm_ffn"] = _ffn_bwd(
        "a", x1, w["a_norm_ffn"], w["a_w_gu"], w["a_w_down"], ffn_a, dx2, dx2_bf)
    dcat_a = _mm("a_out_dx", dx1_bf, w["a_w_out"], "nt", BF16, tn=1024)
    g["a_w_out"] = _mm("a_out_dw", cat_a, dx1_bf, "tn", BF16, tm=1024, tn=1024)

    def scale_bwd(dc, gp, sc):
        dc, gp = dc.astype(F32), gp.astype(F32)
        return dc * sc, jnp.sum(dc * gp, axis=0, keepdims=True)

    dg_pre, g["a_scale"] = _rowwise("a_scale_bwd", scale_bwd, [(dcat_a, (SB_WIDTH, 0)), g_pre], [w["a_scale"]],
                                    [(SB_WIDTH, BF16)], [SB_WIDTH])
    g_bd = _mm("a_group_dw", pooled, dg_pre, "tn", F32, tm=768, tn=768)
    g["a_w_group"] = jnp.stack([g_bd[i * POOL_GROUP:(i + 1) * POOL_GROUP, i * POOL_GROUP:(i + 1) * POOL_GROUP]
                                for i in range(4)])
    dpooled = _mm("a_group_dx", dg_pre, w_bd, "nt", F32, tn=768)
    du_pool = _pool("a_pool_bwd", dpooled, reverse=True)
    dq_mem_a, dmkv_a = _mem_bwd("a_mem_attn_bwd", proj_a_bf, mkv_a, dcat_a)
    dproj_a = jnp.concatenate([du_pool, dq_mem_a], axis=1)
    g["a_w_in"] = _mm("a_in_dw", h_a, dproj_a, "tn", BF16, tm=1024, tn=1024)
    dh_a = _mm("a_in_dx", dproj_a, w["a_w_in"], "nt", F32, tn=1024)
    grad_x, _, g["a_norm_mix"] = _rms_bwd("a_mix_norm_bwd", x, w["a_norm_mix"], dh_a, dres=dx1)
    g["a_w_mem_kv"], g_memnorm_a = _mem_kv_bwd("a", mem, w["mem_norm"], mem_n, w["a_w_mem_kv"], dmkv_a)
    g["mem_norm"] = g_memnorm_a + g_memnorm_b
    return loss, grad_x, g, early_partial, early_got


ROW_SHARDED = ("a_w_in", "a_w_mem_kv", "a_w_out", "a_w_down", "b_w_q", "b_w_mem_kv", "b_w_out", "b_w_down")
COL_SHARDED = ("a_w_gu", "w_kv", "b_w_gu")
BIG = ("a_w_in", "a_w_mem_kv", "a_w_out", "a_w_gu", "a_w_down", "w_kv", "b_w_q", "b_w_mem_kv", "b_w_out", "b_w_gu",
       "b_w_down")
LATE = ("b_w_mem_kv", "b_w_out", "b_w_gu", "b_w_down")
EARLY = ("b_w_gu", "b_w_down", "b_w_out")
N_CHIPS = 4
N_DEV = 8


def _position():
    x, y, c = lax.axis_index("x"), lax.axis_index("y"), lax.axis_index("c")
    other_chips = [(1 - x, y), (x, 1 - y), (1 - x, 1 - y)]
    return x, y, c, other_chips


def _remote(src, dst, send_sem, recv_sem, device):
    return pltpu.make_async_remote_copy(src_ref=src, dst_ref=dst, send_sem=send_sem, recv_sem=recv_sem,
                                        device_id=device, device_id_type=MESH)


def _comm_call(name, body, args, out_shape, n_remote, n_local, scratch=(), aliases=None):
    return pl.pallas_call(
        body, name=name, in_specs=[ANY] * len(args), out_specs=[ANY] * len(out_shape), out_shape=out_shape,
        scratch_shapes=[pltpu.SemaphoreType.DMA((n_remote,)), pltpu.SemaphoreType.DMA((n_remote,)),
                        pltpu.SemaphoreType.DMA((max(n_local, 1),)), *scratch],
        input_output_aliases=aliases or {},
        compiler_params=pltpu.CompilerParams(vmem_limit_bytes=VMEM_LIMIT),
    )(*args)


def _row_chunks(nrows, row_bytes):
    per = max(16, (COPY_BYTES // row_bytes) // 16 * 16)
    return [(r0, min(per, nrows - r0)) for r0 in range(0, nrows, per)]


def _rows(ref, start, size, lead=()):
    if isinstance(start, int):
        return ref.at[(*lead, pl.ds(start, size))]
    return ref.at[(*lead, pl.ds(pl.multiple_of(start, 16), size))]


class _Copies:
    def __init__(self, send, recv, loc):
        self.send, self.recv, self.loc = send, recv, loc
        self.n_remote = self.n_local_done = 0
        self.locals, self.remotes = [], []

    def local(self, src, dst):
        cp = pltpu.make_async_copy(src, dst, self.loc.at[len(self.locals)])
        cp.start()
        self.locals.append(cp)

    def slot(self):
        self.n_remote += 1
        return self.n_remote - 1

    def remote(self, k, src, dst, device, start=True):
        cp = _remote(src, dst, self.send.at[k], self.recv.at[k], device)
        if start:
            cp.start()
            self.remotes.append(cp)
        return cp

    def finish_local(self):
        for cp in self.locals[self.n_local_done:]:
            cp.wait()
        self.n_local_done = len(self.locals)

    def finish(self):
        for cp in self.remotes:
            cp.wait_send()
        self.finish_local()


def _shard_cols(ref, row_sharded, cdim, chip):
    if row_sharded:
        return ref
    return ref.at[:, pl.ds(pl.multiple_of(chip * cdim, LANES), cdim)]


class _Gather:
    def __init__(self, shapes):
        self.names = list(shapes)
        self.shapes = [tuple(shapes[k]) for k in self.names]
        self.row_sharded = [k in ROW_SHARDED for k in self.names]
        self.chunks = [(i, r0, size) for i, (r, cdim) in enumerate(self.shapes)
                       for r0, size in _row_chunks(r // 2, cdim * 2)]
        self.n = len(self.names)

    def out_shape(self):
        return [jax.ShapeDtypeStruct((N_CHIPS * r, cdim) if rs else (r, N_CHIPS * cdim), BF16)
                for (r, cdim), rs in zip(self.shapes, self.row_sharded)]

    def scratch(self):
        n_remote, n_local = 6 * len(self.chunks), self.n + 2 * len(self.chunks)
        return [pltpu.SemaphoreType.DMA((n_remote,)), pltpu.SemaphoreType.DMA((n_remote,)),
                pltpu.SemaphoreType.DMA((n_local,))] + [pltpu.VMEM(s, BF16) for s in self.shapes]

    def _window(self, dst, i, chip, half, r0, size):
        r, cdim = self.shapes[i]
        base = (chip * r if self.row_sharded[i] else 0) + half * (r // 2) + r0
        return _rows(_shard_cols(dst[i], self.row_sharded[i], cdim, chip), base, size)

    def _mine(self, refs, i, half, r0, size):
        return _rows(refs[i], half * (self.shapes[i][0] // 2) + r0, size)

    def _sent(self, src, dst, sems, q, k, px, py, c, me):
        i, r0, size = self.chunks[q]
        return _remote(self._mine(src, i, c, r0, size), self._window(dst, i, me, c, r0, size),
                       sems[0].at[6 * q + k], sems[1].at[6 * q + k], (px, py, c))

    def start(self, src, dst, sems, vm):
        x, y, c, chips = _position()
        for q in range(len(self.chunks)):
            for k, (px, py) in enumerate(chips):
                self._sent(src, dst, sems, q, k, px, py, c, 2 * x + y).start()
        for i in range(self.n):
            pltpu.make_async_copy(src[i], vm[i], sems[2].at[i]).start()

    def finish(self, src, dst, sems, vm):
        x, y, c, chips = _position()
        me = 2 * x + y
        send, recv, loc = sems
        for i in range(self.n):
            pltpu.make_async_copy(src[i], vm[i], loc.at[i]).wait()
        placed, forwarded = [], []
        for q, (i, r0, size) in enumerate(self.chunks):
            for half in range(2):
                cp = pltpu.make_async_copy(self._mine(vm, i, half, r0, size), self._window(dst, i, me, half, r0, size),
                                           loc.at[self.n + 2 * q + half])
                cp.start()
                placed.append(cp)
        for q, (i, r0, size) in enumerate(self.chunks):
            for k, (px, py) in enumerate(chips):
                landed = self._window(dst, i, 2 * px + py, c, r0, size)
                _remote(landed, landed, send.at[6 * q + k], recv.at[6 * q + k], (px, py, c)).wait_recv()
                cp = _remote(landed, landed, send.at[6 * q + 3 + k], recv.at[6 * q + 3 + k], (x, y, 1 - c))
                cp.start()
                forwarded.append(cp)
        for q, (i, r0, size) in enumerate(self.chunks):
            for k, (px, py) in enumerate(chips):
                landed = self._window(dst, i, 2 * px + py, 1 - c, r0, size)
                _remote(landed, landed, send.at[6 * q + 3 + k], recv.at[6 * q + 3 + k], (x, y, 1 - c)).wait_recv()
        for q in range(len(self.chunks)):
            for k, (px, py) in enumerate(chips):
                self._sent(src, dst, sems, q, k, px, py, c, me).wait_send()
        for cp in forwarded:
            cp.wait_send()
        for cp in placed:
            cp.wait()


def _gather_weights(shards):
    plan = _Gather({k: v.shape for k, v in shards.items()})
    n = plan.n

    def body(*refs):
        parts = refs[:n], refs[n:2 * n], refs[2 * n:2 * n + 3], refs[2 * n + 3:]
        plan.start(*parts)
        plan.finish(*parts)

    outs = pl.pallas_call(
        body, name="gather_weights", in_specs=[ANY] * n, out_specs=[ANY] * n, out_shape=plan.out_shape(),
        scratch_shapes=plan.scratch(), compiler_params=pltpu.CompilerParams(vmem_limit_bytes=VMEM_LIMIT),
    )(*[shards[k] for k in plan.names])
    return dict(zip(plan.names, outs))


def _scalar_grid_call(name, body, scalars, grid, in_specs, out_specs, out_shape, args):
    return pl.pallas_call(
        body, name=name, out_shape=out_shape,
        grid_spec=pltpu.PrefetchScalarGridSpec(num_scalar_prefetch=1, grid=grid, in_specs=in_specs, out_specs=out_specs),
        compiler_params=_params(*["arbitrary"] * len(grid)),
    )(scalars, *args)


def _pair_sum(name, g4, sib, where, tb=256):
    j, _, r, w = g4.shape
    tb = _row_tile(r, tb)

    def body(s_ref, g_ref, b_ref, o_ref):
        o_ref[...] = (g_ref[...].astype(F32) + b_ref[...].astype(F32)).astype(o_ref.dtype)

    blk = pl.BlockSpec((None, tb, w), lambda a, i, s: (a, i, 0))
    return _scalar_grid_call(
        name, body, where, (j, r // tb),
        [pl.BlockSpec((None, None, tb, w), lambda a, i, s: (a, s[0], i, 0)), blk], blk,
        jax.ShapeDtypeStruct((j, r, w), BF16), (g4, sib))


def _chip_sum(name, partial, got, where, row_sharded, tb=256):
    _, r, cdim = got.shape
    tb = _row_tile(r, tb)

    def body(s_ref, p_ref, g_ref, o_ref):
        acc = p_ref[...].astype(F32)
        for k in range(N_CHIPS - 1):
            acc = acc + g_ref[k].astype(F32)
        o_ref[...] = acc

    if row_sharded:
        own = pl.BlockSpec((None, tb, cdim), lambda i, s: (s[1], i, 0))
    else:
        own = pl.BlockSpec((None, tb, cdim), lambda i, s: (0, i, s[1]))
    return _scalar_grid_call(
        name, body, where, (r // tb,),
        [own, pl.BlockSpec((N_CHIPS - 1, tb, cdim), lambda i, s: (0, i, 0))],
        pl.BlockSpec((None, tb, cdim), lambda i, s: (s[0], i, 0)),
        jax.ShapeDtypeStruct((2, r, cdim), F32), (partial, got))


def _where():
    return jnp.stack([lax.axis_index("c"), 2 * lax.axis_index("x") + lax.axis_index("y")]).astype(jnp.int32)


def _grad_halves(name, g):
    rows, cols = g.shape
    if name in ROW_SHARDED:
        r = rows // N_CHIPS
        return g.reshape(N_CHIPS, 2, r // 2, cols), (r // 2, cols)
    return g.reshape(1, 2, rows // 2, cols), (rows // 2, cols // N_CHIPS)


def _pair_sums(tag, grads):
    names = list(grads)
    n = len(names)
    g4 = [_grad_halves(k, grads[k])[0] for k in names]
    plan = [[(j, r0, size) for j in range(g.shape[0]) for r0, size in _row_chunks(g.shape[2], g.shape[3] * 2)]
            for g in g4]
    out_shape = [jax.ShapeDtypeStruct((g.shape[0],) + g.shape[2:], BF16) for g in g4]

    def body(*refs):
        src, sib = refs[:n], refs[n:2 * n]
        cps = _Copies(*refs[2 * n:])
        x, y, c, _ = _position()
        waits = []
        for i in range(n):
            for j, r0, size in plan[i]:
                waits.append(cps.remote(cps.slot(), _rows(src[i], r0, size, lead=(j, 1 - c)),
                                        _rows(sib[i], r0, size, lead=(j,)), (x, y, 1 - c)))
        for cp in waits:
            cp.wait_recv()
        cps.finish()

    sibs = _comm_call("grads_pair_exchange_" + tag, body, g4, out_shape, sum(len(p) for p in plan), 0)
    where = _where()
    return {k: _pair_sum(k + "_pair_sum", g, s, where) for k, g, s in zip(names, g4, sibs)}


class _ChipExchange:
    def __init__(self, partial):
        self.names = list(partial)
        self.n = len(self.names)
        self.row_sharded = [k in ROW_SHARDED for k in self.names]
        self.half = []
        for k, rs in zip(self.names, self.row_sharded):
            _, r, w = partial[k].shape
            self.half.append((r, w) if rs else (r, w // N_CHIPS))
        self.chunks = [(i, r0, size) for i, (r, cdim) in enumerate(self.half) for r0, size in _row_chunks(r, cdim * 2)]

    def out_shape(self):
        return [jax.ShapeDtypeStruct((N_CHIPS - 1,) + s, BF16) for s in self.half]

    def scratch(self):
        n_remote = 3 * len(self.chunks)
        return [pltpu.SemaphoreType.DMA((n_remote,)), pltpu.SemaphoreType.DMA((n_remote,))]

    def _copies(self, src, dst, sems):
        x, y, c, chips = _position()
        for q, (i, r0, size) in enumerate(self.chunks):
            for k, (px, py) in enumerate(chips):
                chip = 2 * px + py
                if self.row_sharded[i]:
                    shard = _rows(src[i], r0, size, lead=(chip,))
                else:
                    shard = _rows(_shard_cols(src[i].at[0], False, self.half[i][1], chip), r0, size)
                yield _remote(shard, _rows(dst[i], r0, size, lead=(k,)), sems[0].at[3 * q + k], sems[1].at[3 * q + k],
                              (px, py, c))

    def start(self, src, dst, sems):
        for cp in self._copies(src, dst, sems):
            cp.start()

    def finish(self, src, dst, sems):
        for cp in self._copies(src, dst, sems):
            cp.wait_recv()
        for cp in self._copies(src, dst, sems):
            cp.wait_send()


def _chip_exchange(partial):
    plan = _ChipExchange(partial)
    n = plan.n

    def body(*refs):
        parts = refs[:n], refs[n:2 * n], refs[2 * n:]
        plan.start(*parts)
        plan.finish(*parts)

    got = pl.pallas_call(
        body, name="grads_chip_exchange", in_specs=[ANY] * n, out_specs=[ANY] * n, out_shape=plan.out_shape(),
        scratch_shapes=plan.scratch(),
    )(*[partial[k] for k in plan.names])
    return dict(zip(plan.names, got))


def _finish_reduce(partial, got):
    names = list(partial)
    n = len(names)
    where = _where()
    halves = [_chip_sum(k + "_chip_sum", partial[k], got[k], where, k in ROW_SHARDED) for k in names]
    plan = [_row_chunks(h.shape[1], h.shape[2] * 4) for h in halves]
    out_shape = [jax.ShapeDtypeStruct(h.shape, F32) for h in halves]

    def body(*refs):
        src, dst = refs[:n], refs[n:2 * n]
        cps = _Copies(*refs[2 * n:])
        x, y, c, _ = _position()
        waits = []
        for i in range(n):
            for r0, size in plan[i]:
                waits.append(cps.remote(cps.slot(), _rows(src[i], r0, size, lead=(c,)), _rows(dst[i], r0, size, lead=(c,)),
                                        (x, y, 1 - c)))
        for cp in waits:
            cp.wait_recv()
        cps.finish()

    outs = _comm_call("grads_pair_share", body, halves, out_shape, sum(len(p) for p in plan), 0,
                      aliases={i: i for i in range(n)})
    return {k: o.reshape(2 * o.shape[1], o.shape[2]) for k, o in zip(names, outs)}


def _all_reduce_small(name, v):
    rows, cols = v.shape
    h = rows // 2

    def body(v_ref, o_ref, sib, pair, buf, send, recv):
        x, y, c, chips = _position()
        me = 2 * x + y
        sibling = (x, y, 1 - c)
        cp = _remote(v_ref, sib, send.at[0], recv.at[0], sibling)
        cp.start()
        cp.wait()
        pair[...] = v_ref[...] + sib[...]
        mine = pl.ds(pl.multiple_of(c * h, 8), h)
        buf[me] = pair[mine, :]
        sends = [_remote(pair.at[mine], buf.at[me], send.at[1 + k], recv.at[1 + k], (px, py, c))
                 for k, (px, py) in enumerate(chips)]
        for cp in sends:
            cp.start()
        for k, (px, py) in enumerate(chips):
            _remote(pair.at[mine], buf.at[2 * px + py], send.at[1 + k], recv.at[1 + k], (px, py, c)).wait_recv()
        for cp in sends:
            cp.wait_send()
        o_ref[mine, :] = (buf[0] + buf[1]) + (buf[2] + buf[3])
        cp = _remote(o_ref.at[mine], o_ref.at[mine], send.at[4], recv.at[4], sibling)
        cp.start()
        cp.wait()

    vm = pl.BlockSpec(memory_space=pltpu.VMEM)
    return pl.pallas_call(
        body, name=name, in_specs=[vm], out_specs=vm, out_shape=jax.ShapeDtypeStruct(v.shape, F32),
        scratch_shapes=[pltpu.VMEM((rows, cols), F32), pltpu.VMEM((rows, cols), F32), pltpu.VMEM((N_CHIPS, h, cols), F32),
                        pltpu.SemaphoreType.DMA((5,)), pltpu.SemaphoreType.DMA((5,))],
        compiler_params=pltpu.CompilerParams(vmem_limit_bytes=VMEM_LIMIT),
    )(v)


def _adamw(name, w, g, m, v):
    def fn(wt, gt, mt, vt):
        mt = ADAM_B1 * mt + (1.0 - ADAM_B1) * gt
        vt = ADAM_B2 * vt + (1.0 - ADAM_B2) * (gt * gt)
        m_hat = mt / (1.0 - ADAM_B1 ** ADAM_STEP)
        v_hat = vt / (1.0 - ADAM_B2 ** ADAM_STEP)
        delta = -ADAM_LR * (m_hat / (jnp.sqrt(v_hat) + ADAM_EPS) + ADAM_WD * wt)
        return delta, mt, vt
    n = w.shape[1]
    return _rowwise(name, fn, [w, g, m, v], [], [(n, F32)] * 3, tb=256)


WEIGHTS = ("mem_norm", "a_norm_mix", "a_w_in", "a_w_group", "a_scale", "a_w_mem_kv", "a_w_out", "a_norm_ffn", "a_w_gu",
           "a_w_down", "kv_norm", "w_kv", "b_norm_mix", "b_w_q", "b_w_mem_kv", "b_w_out", "b_norm_ffn", "b_w_gu",
           "b_w_down", "final_norm")
REPLICATED_VECS = ("mem_norm", "kv_norm", "b_norm_mix", "b_norm_ffn", "final_norm")
SHARDED_VECS = ("a_norm_mix", "a_norm_ffn", "a_scale")
D_MODEL = 1024
GROUP_ROWS = 4 * POOL_GROUP * POOL_GROUP // D_MODEL


def _row(v):
    v = v.reshape(1, -1).astype(F32)
    return jnp.pad(v, ((0, 0), (0, D_MODEL - v.shape[1])))


def _pack_small(t):
    rows = [_row(t[k]) for k in REPLICATED_VECS]
    rows.append(_row(jnp.concatenate([t[k].reshape(-1) for k in SHARDED_VECS])))
    rows.append(jnp.zeros((2, D_MODEL), F32))
    rows.append(t["a_w_group"].reshape(GROUP_ROWS, D_MODEL).astype(F32))
    return jnp.concatenate(rows, axis=0)


def _unpack_small(p, like):
    out = {k: p[i, :].reshape(like[k].shape) for i, k in enumerate(REPLICATED_VECS)}
    off = 0
    for k in SHARDED_VECS:
        size = like[k].size
        out[k] = p[len(REPLICATED_VECS), off:off + size].reshape(like[k].shape)
        off += size
    out["a_w_group"] = p[len(REPLICATED_VECS) + 3:, :].reshape(like["a_w_group"].shape)
    return out


def kernel(x, mem, mem_norm, a_norm_mix, a_w_in, a_w_group, a_scale, a_w_mem_kv, a_w_out, a_norm_ffn, a_w_gu, a_w_down, kv_norm, w_kv, b_norm_mix, b_w_q, b_w_mem_kv, b_w_out, b_norm_ffn, b_w_gu, b_w_down, final_norm, loss_target, m_mem_norm, m_a_norm_mix, m_a_w_in, m_a_w_group, m_a_scale, m_a_w_mem_kv, m_a_w_out, m_a_norm_ffn, m_a_w_gu, m_a_w_down, m_kv_norm, m_w_kv, m_b_norm_mix, m_b_w_q, m_b_w_mem_kv, m_b_w_out, m_b_norm_ffn, m_b_w_gu, m_b_w_down, m_final_norm, v_mem_norm, v_a_norm_mix, v_a_w_in, v_a_w_group, v_a_scale, v_a_w_mem_kv, v_a_w_out, v_a_norm_ffn, v_a_w_gu, v_a_w_down, v_kv_norm, v_w_kv, v_b_norm_mix, v_b_w_q, v_b_w_mem_kv, v_b_w_out, v_b_norm_ffn, v_b_w_gu, v_b_w_down, v_final_norm):
    given = dict(locals())
    wl = {k: given[k] for k in WEIGHTS}
    ml = {k: given["m_" + k] for k in WEIGHTS}
    vl = {k: given["v_" + k] for k in WEIGHTS}
    chip = 2 * lax.axis_index("x") + lax.axis_index("y")

    def mat(a):
        return a.reshape(a.shape[-2], a.shape[-1])

    shards = {k: mat(wl[k]).astype(BF16) for k in BIG}
    full = _gather_weights({k: shards[k] for k in BIG if k not in LATE})
    gains = jnp.zeros((16, D_MODEL), F32)
    for i, k in enumerate(SHARDED_VECS):
        part = wl[k].reshape(1, -1)
        width = part.shape[1]
        gains = lax.dynamic_update_slice(gains, part, (i, chip * width))
    gains = _all_reduce_small("gains_all_gather", gains) * 0.5
    w = dict(full)
    for k in REPLICATED_VECS:
        w[k] = wl[k].reshape(1, D_MODEL)
    w["a_norm_mix"], w["a_norm_ffn"] = gains[0:1], gains[1:2]
    w["a_scale"] = gains[2:3, :SB_WIDTH]
    w["a_w_group"] = wl["a_w_group"][0].astype(BF16)

    loss, grad_x, g, partial, got = _local_step(x[0], mem[0], loss_target[0], w, {k: shards[k] for k in LATE})

    rest = _pair_sums("late", {k: g[k] for k in BIG if k not in EARLY})
    partial.update(rest)
    got.update(_chip_exchange(rest))
    red = _finish_reduce(partial, got)
    small = jnp.concatenate(
        [_row(g[k]) for k in REPLICATED_VECS] + [_row(g[k]) for k in SHARDED_VECS] + [_row(loss)]
        + [jnp.zeros((7, D_MODEL), F32), g["a_w_group"].reshape(GROUP_ROWS, D_MODEL)], axis=0)
    small = _all_reduce_small("small_grads_all_reduce", small)
    gs = {k: small[i] for i, k in enumerate(REPLICATED_VECS)}
    for i, k in enumerate(SHARDED_VECS):
        width = wl[k].shape[-1]
        gs[k] = lax.dynamic_slice(small[len(REPLICATED_VECS) + i], (chip * width,), (width,))
    gs["a_w_group"] = small[16:]
    total_loss = small[8, 0]

    out_g, out_d, out_m, out_v = {}, {}, {}, {}
    for k in BIG:
        shape = wl[k].shape
        out_g[k] = red[k].reshape(shape)
        d, nm, nv = _adamw(k + "_adamw", mat(wl[k]), red[k], mat(ml[k]), mat(vl[k]))
        out_d[k], out_m[k], out_v[k] = d.reshape(shape), nm.reshape(shape), nv.reshape(shape)
    small_names = REPLICATED_VECS + SHARDED_VECS + ("a_w_group",)
    d, nm, nv = _adamw("small_adamw", _pack_small(wl), _pack_small(gs), _pack_small(ml), _pack_small(vl))
    like = {k: wl[k] for k in small_names}
    for dst, p in ((out_d, d), (out_m, nm), (out_v, nv)):
        dst.update(_unpack_small(p, like))
    for k in small_names:
        out_g[k] = gs[k].reshape(wl[k].shape)

    return (total_loss, grad_x[None], *[out_g[k] for k in WEIGHTS], *[out_d[k] for k in WEIGHTS],
            *[out_m[k] for k in WEIGHTS], *[out_v[k] for k in WEIGHTS])
```

```python
import functools

import jax
import jax.numpy as jnp
from jax import lax
from jax.experimental import pallas as pl
from jax.experimental.pallas import tpu as pltpu

F32 = jnp.float32
BF16 = jnp.bfloat16

HEAD_DIM = 64
SB_WIDTH = 768
MEM_WIDTH = 256
POOL_WINDOWS = (2, 4, 8, 16)
POOL_GROUP = 192
POOL_HALO = 16
EPS = 1e-6
ATT_SCALE = HEAD_DIM ** -0.5
ADAM_LR, ADAM_B1, ADAM_B2, ADAM_EPS, ADAM_WD, ADAM_STEP = 0.001, 0.9, 0.999, 1e-08, 0.01, 10

LANES = 128
SB_TQ, SB_TK = 512, 512
VMEM_LIMIT = 56 * 1024 * 1024
MESH = pl.DeviceIdType.MESH
COPY_BYTES = 512 * 1024
ANY = pl.BlockSpec(memory_space=pl.ANY)


def _params(*sem):
    return pltpu.CompilerParams(dimension_semantics=sem, vmem_limit_bytes=VMEM_LIMIT)


def _tile(n, pref):
    if n <= pref:
        return n
    best = None
    for t in range(LANES, pref + 1, LANES):
        if n % t == 0:
            best = t
    assert best is not None, (n, pref)
    return best


def _row_tile(t, pref):
    if t <= pref:
        return t
    for tb in range(pref - pref % 16, 0, -16):
        if t % tb == 0:
            return tb
    raise ValueError((t, pref))


def _rowwise(name, fn, rows, vecs, row_outs, sum_outs=(), tb=512):
    norm_rows = []
    for r in rows:
        if isinstance(r, tuple):
            arr, (bc, cb) = r
        else:
            arr, (bc, cb) = r, (r.shape[1], 0)
        norm_rows.append((arr, bc, cb))
    t = norm_rows[0][0].shape[0]
    tb = _row_tile(t, tb)
    n_in, n_ro = len(norm_rows) + len(vecs), len(row_outs)

    def body(*refs):
        ins = [r[...] for r in refs[:n_in]]
        outs = fn(*ins)
        if not isinstance(outs, tuple):
            outs = (outs,)
        for o_ref, o in zip(refs[n_in:n_in + n_ro], outs[:n_ro]):
            o_ref[...] = o.astype(o_ref.dtype)
        for s_ref, s in zip(refs[n_in + n_ro:], outs[n_ro:]):
            @pl.when(pl.program_id(0) == 0)
            def _():
                s_ref[...] = jnp.zeros_like(s_ref)
            s_ref[...] += s

    in_specs = [pl.BlockSpec((tb, bc), functools.partial(lambda i, cb: (i, cb), cb=cb)) for _, bc, cb in norm_rows]
    in_specs += [pl.BlockSpec(v.shape, lambda i: (0, 0)) for v in vecs]
    out_specs = [pl.BlockSpec((tb, c), lambda i: (i, 0)) for c, _ in row_outs]
    out_specs += [pl.BlockSpec((1, c), lambda i: (0, 0)) for c in sum_outs]
    out_shape = [jax.ShapeDtypeStruct((t, c), d) for c, d in row_outs]
    out_shape += [jax.ShapeDtypeStruct((1, c), F32) for c in sum_outs]
    res = pl.pallas_call(
        body, name=name, grid=(t // tb,), in_specs=in_specs, out_specs=out_specs, out_shape=out_shape,
        compiler_params=_params("arbitrary"),
    )(*[a for a, _, _ in norm_rows], *vecs)
    return res


def _rms_fwd(name, x, g):
    def fn(xt, gt):
        rstd = lax.rsqrt(jnp.mean(xt * xt, axis=-1, keepdims=True) + EPS)
        return xt * rstd * gt
    return _rowwise(name, fn, [x], [g], [(x.shape[1], BF16)])[0]


def _rms_bwd(name, x, g, dh, dres=None, want_dx=True):
    has_res = dres is not None

    def fn(*a):
        if has_res:
            xt, dht, drt, gt = a
        else:
            xt, dht, gt = a
        rstd = lax.rsqrt(jnp.mean(xt * xt, axis=-1, keepdims=True) + EPS)
        xhat = xt * rstd
        dht = dht.astype(F32)
        dg = jnp.sum(dht * xhat, axis=0, keepdims=True)
        if not want_dx:
            return (dg,)
        dxhat = dht * gt
        dx = rstd * (dxhat - xhat * jnp.mean(dxhat * xhat, axis=-1, keepdims=True))
        if has_res:
            dx = dx + drt
        return dx, dx, dg

    d = x.shape[1]
    rows = [x, dh] + ([dres] if has_res else [])
    outs = [(d, F32), (d, BF16)] if want_dx else []
    return _rowwise(name, fn, rows, [g], outs, [d])


_DOT_DIMS = {"nn": ((1,), (0,)), "nt": ((1,), (1,)), "tn": ((0,), (0,))}


def _mm(name, a, b, mode, out_dtype, res=None, tm=512, tn=512):
    if mode == "nn":
        (m, k), (k2, n) = a.shape, b.shape
    elif mode == "nt":
        (m, k), (n, k2) = a.shape, b.shape
    else:
        (k, m), (k2, n) = a.shape, b.shape
    assert k == k2, (name, a.shape, b.shape)
    tm, tn = _tile(m, tm), _tile(n, tn)
    dims = (_DOT_DIMS[mode], ((), ()))
    has_res = res is not None

    def body(a_ref, b_ref, *rest):
        acc = lax.dot_general(a_ref[...], b_ref[...], dims, preferred_element_type=F32)
        if has_res:
            acc = acc + rest[0][...]
        rest[-1][...] = acc.astype(out_dtype)

    a_spec = pl.BlockSpec((k, tm), lambda i, j: (0, i)) if mode == "tn" else pl.BlockSpec((tm, k), lambda i, j: (i, 0))
    b_spec = pl.BlockSpec((tn, k), lambda i, j: (j, 0)) if mode == "nt" else pl.BlockSpec((k, tn), lambda i, j: (0, j))
    o_spec = pl.BlockSpec((tm, tn), lambda i, j: (i, j))
    in_specs, args = [a_spec, b_spec], [a, b]
    if has_res:
        in_specs.append(o_spec)
        args.append(res)
    return pl.pallas_call(
        body, name=name, grid=(m // tm, n // tn), in_specs=in_specs, out_specs=o_spec,
        out_shape=jax.ShapeDtypeStruct((m, n), out_dtype), compiler_params=_params("parallel", "arbitrary"),
    )(*args)


def _pool(name, u, reverse, tb=512):
    t = u.shape[0]
    tb = min(tb, t)
    nt = t // tb
    c = SB_WIDTH
    hpb = tb // POOL_HALO

    def body(cur_ref, halo_ref, o_ref):
        i = pl.program_id(0)
        cur = cur_ref[...].astype(F32)
        edge = (i == nt - 1) if reverse else (i == 0)
        halo = jnp.where(edge, 0.0, halo_ref[...].astype(F32))
        col = lax.broadcasted_iota(jnp.int32, (tb + POOL_HALO, c), 1)
        row = lax.broadcasted_iota(jnp.int32, (tb + POOL_HALO, c), 0)
        wcol = jnp.where(col < POOL_GROUP, 2, jnp.where(col < 2 * POOL_GROUP, 4, jnp.where(col < 3 * POOL_GROUP, 8, 16)))
        n = tb + POOL_HALO
        if reverse:
            ext = jnp.concatenate([cur, halo], axis=0)
            tpos = i * tb + row
            ext = ext / jnp.minimum(tpos + 1, wcol).astype(F32)
            shift = lambda a, k: pltpu.roll(a, n - k, 0)
        else:
            ext = jnp.concatenate([halo, cur], axis=0)
            shift = lambda a, k: pltpu.roll(a, k, 0)
        s2 = ext + shift(ext, 1)
        s4 = s2 + shift(s2, 2)
        s8 = s4 + shift(s4, 4)
        s16 = s8 + shift(s8, 8)
        win = jnp.where(wcol == 2, s2, jnp.where(wcol == 4, s4, jnp.where(wcol == 8, s8, s16)))
        if reverse:
            out = win[:tb] - cur
        else:
            tpos = i * tb + row[POOL_HALO:] - POOL_HALO
            out = win[POOL_HALO:] / jnp.minimum(tpos + 1, wcol[POOL_HALO:]).astype(F32) - cur
        o_ref[...] = out.astype(o_ref.dtype)

    if reverse:
        halo_map = lambda i: (jnp.minimum((i + 1) * hpb, t // POOL_HALO - 1), 0)
    else:
        halo_map = lambda i: (jnp.maximum(i * hpb - 1, 0), 0)
    return pl.pallas_call(
        body, name=name, grid=(nt,),
        in_specs=[pl.BlockSpec((tb, c), lambda i: (i, 0)), pl.BlockSpec((POOL_HALO, c), halo_map)],
        out_specs=pl.BlockSpec((tb, c), lambda i: (i, 0)),
        out_shape=jax.ShapeDtypeStruct((t, c), BF16), compiler_params=_params("arbitrary"),
    )(u, u)


def _head_masks(shape):
    lane = lax.broadcasted_iota(jnp.int32, shape, 1)
    return lane < HEAD_DIM, lane >= HEAD_DIM


def _pick(mask, a):
    return jnp.where(mask, a, jnp.zeros_like(a))


def _dot(a, b, mode):
    return lax.dot_general(a, b, (_DOT_DIMS[mode], ((), ())), preferred_element_type=F32)


def _dot_tri(a, tri, suffix):
    h = a.shape[1] // 2
    lo, hi = a[:, :h], a[:, h:]
    s_lo, s_hi = jnp.sum(lo, axis=1, keepdims=True), jnp.sum(hi, axis=1, keepdims=True)
    p_lo, p_hi = _dot(lo.astype(BF16), tri, "nn"), _dot(hi.astype(BF16), tri, "nn")
    if suffix:
        p_lo = p_lo + s_hi
    else:
        p_hi = p_hi + s_lo
    return jnp.concatenate([p_lo, p_hi], axis=1), s_lo + s_hi


def _log_gates(z):
    nz = -z
    l = jnp.log(1.0 + jnp.exp(jnp.minimum(z, nz)))
    ln = jnp.minimum(nz, 0.0) - l
    return ln, z + ln


def _sb_blocks(s, tq, tk):
    tq, tk = min(tq, s), min(tk, s)
    assert tq == tk and s % tk == 0 and tk % 64 == 0, (s, tq, tk)
    return tq, tk, tk // 2


def _strict_triangle(n, pred):
    return pred(lax.broadcasted_iota(jnp.int32, (n, n), 0), lax.broadcasted_iota(jnp.int32, (n, n), 1)).astype(BF16)


def _sb_fwd(proj, kv, late_shards, tq=SB_TQ, tk=SB_TK):
    s = proj.shape[0]
    tq, tk, th = _sb_blocks(s, tq, tk)
    npair = SB_WIDTH // LANES
    gather = _Gather({k: v.shape for k, v in late_shards.items()})
    ng = gather.n

    def body(q_ref, k_ref, v_ref, *rest):
        o_ref, tot_ref = rest[ng:ng + 2]
        comm = rest[:ng], rest[ng + 2:2 * ng + 2], rest[2 * ng + 2:2 * ng + 5], rest[2 * ng + 5:]

        @pl.when(pl.program_id(0) == 0)
        def _():
            gather.start(*comm)

        tri_gt = _strict_triangle(tk, lambda j, s_: j > s_)
        tri_gt_h = _strict_triangle(th, lambda j, s_: j > s_)
        seen = lax.broadcasted_iota(jnp.int32, (tq, th), 1) < lax.broadcasted_iota(jnp.int32, (tq, th), 0)
        m_a, m_b = _head_masks((tq, LANES))

        def block(qh, k2, v2, lane_mask, carry, acc, mask, tri):
            ln_full, lsz = _log_gates(_dot(qh, k2, "nt"))
            ln = ln_full if mask is None else jnp.where(mask, ln_full, 0.0)
            w = jnp.exp(lsz + _dot(ln.astype(BF16), tri, "nn"))
            if mask is not None:
                w = jnp.where(mask, w, 0.0)
            acc = acc + jnp.exp(carry) * _dot(w.astype(BF16), _pick(lane_mask[:v2.shape[0]], v2), "nn")
            return carry + jnp.sum(ln, axis=1, keepdims=True), acc

        def q_block(qi, _):
            q0 = pl.multiple_of(qi * tq, tq)
            q2 = q_ref[pl.ds(q0, tq), :] * ATT_SCALE
            qa, qb = _pick(m_a, q2), _pick(m_b, q2)

            def both(qa, qb, k0, size, ca, cb, acc, mask, tri):
                k2 = k_ref[pl.ds(k0, size), :]
                v2 = v_ref[pl.ds(k0, size), :]
                ca, acc = block(qa, k2, v2, m_a, ca, acc, mask, tri)
                cb, acc = block(qb, k2, v2, m_b, cb, acc, mask, tri)
                return ca, cb, acc

            zero_c = jnp.zeros((th, 1), F32)
            zero_o = jnp.zeros((th, LANES), F32)
            ca, cb, acc = both(qa[th:], qb[th:], pl.multiple_of(q0 + th, th), th, zero_c, zero_c, zero_o, seen[:th], tri_gt_h)
            ca, cb = jnp.concatenate([zero_c, ca], axis=0), jnp.concatenate([zero_c, cb], axis=0)
            carry = both(qa, qb, q0, th, ca, cb, jnp.concatenate([zero_o, acc], axis=0), seen, tri_gt_h)

            def k_block(step, carry):
                return both(qa, qb, pl.multiple_of(q0 - (step + 1) * tk, tk), tk, *carry, None, tri_gt)

            ca, cb, acc = lax.fori_loop(0, qi, k_block, carry)
            o_ref[pl.ds(q0, tq), :] = acc.astype(o_ref.dtype)
            tot_ref[0, pl.ds(q0, tq), :] = jnp.broadcast_to(ca, (tq, LANES))
            tot_ref[1, pl.ds(q0, tq), :] = jnp.broadcast_to(cb, (tq, LANES))
            return 0

        lax.fori_loop(0, s // tq, q_block, 0)

        @pl.when(pl.program_id(0) == npair - 1)
        def _():
            gather.finish(*comm)

    outs = pl.pallas_call(
        body, name="sb_fwd", grid=(npair,),
        in_specs=[pl.BlockSpec((s, LANES), lambda p: (0, p)), pl.BlockSpec((s, LANES), lambda p: (0, p)),
                  pl.BlockSpec((s, LANES), lambda p: (0, npair + p))] + [ANY] * ng,
        out_specs=[pl.BlockSpec((s, LANES), lambda p: (0, p)), pl.BlockSpec((None, 2, s, LANES), lambda p: (p, 0, 0, 0))]
        + [ANY] * ng,
        out_shape=[jax.ShapeDtypeStruct((s, SB_WIDTH), BF16), jax.ShapeDtypeStruct((npair, 2, s, LANES), F32)]
        + gather.out_shape(),
        scratch_shapes=gather.scratch(), compiler_params=_params("arbitrary"),
    )(proj, kv, kv, *[late_shards[k] for k in gather.names])
    return outs[0], outs[1], dict(zip(gather.names, outs[2:]))


def _sb_bwd(proj, kv, dcat, tot, early_partial, tq=SB_TQ, tk=SB_TK):
    s = proj.shape[0]
    tq, tk, th = _sb_blocks(s, tq, tk)
    npair = SB_WIDTH // LANES
    exchange = _ChipExchange(early_partial)
    ne = exchange.n

    def body(q_ref, k_ref, v_ref, do_ref, tot_ref, *rest):
        dq_ref, dk_ref, dv_ref = rest[ne:ne + 3]
        dk_acc, dv_acc = rest[2 * ne + 3:2 * ne + 5]
        comm = rest[:ne], rest[ne + 3:2 * ne + 3], rest[2 * ne + 5:]

        @pl.when(pl.program_id(0) == 0)
        def _():
            exchange.start(*comm)

        tris = (_strict_triangle(tk // 2, lambda j, s_: j > s_), _strict_triangle(tk // 2, lambda j, s_: j < s_))
        tris_h = (_strict_triangle(th // 2, lambda j, s_: j > s_), _strict_triangle(th // 2, lambda j, s_: j < s_))
        seen = lax.broadcasted_iota(jnp.int32, (tq, th), 1) < lax.broadcasted_iota(jnp.int32, (tq, th), 0)
        m_a, m_b = _head_masks((tq, LANES))
        dk_acc[...] = jnp.zeros_like(dk_acc)
        dv_acc[...] = jnp.zeros_like(dv_acc)

        def block(qh, doh, k2, v2, lane_mask, tot_h, carry, mask, tri):
            c_ln, c_d, dq = carry
            ln_full, lsz = _log_gates(_dot(qh, k2, "nt"))
            ln = ln_full if mask is None else jnp.where(mask, ln_full, 0.0)
            inside, total = _dot_tri(ln, tri[0], True)
            c_ln = c_ln + total
            w = jnp.exp(lsz + ((tot_h - c_ln) + inside))
            if mask is not None:
                w = jnp.where(mask, w, 0.0)
            dlw = _dot(doh, v2, "nt") * w
            before, d_total = _dot_tri(dlw, tri[1], False)
            dz = dlw * jnp.exp(ln_full) - (before + c_d) * jnp.exp(lsz)
            if mask is not None:
                dz = jnp.where(mask, dz, 0.0)
            dz = dz.astype(BF16)
            dq = dq + _dot(dz, _pick(lane_mask[:k2.shape[0]], k2), "nn")
            dk = _dot(dz, qh, "tn")
            dv = _dot(w.astype(BF16), doh, "tn")
            carry = (c_ln, c_d + d_total, dq)
            return carry, dk, dv

        def q_block(qi, _):
            q0 = pl.multiple_of(qi * tq, tq)
            q2 = q_ref[pl.ds(q0, tq), :] * ATT_SCALE
            do2 = do_ref[pl.ds(q0, tq), :]
            qa, qb = _pick(m_a, q2), _pick(m_b, q2)
            doa, dob = _pick(m_a, do2), _pick(m_b, do2)
            tot_a = tot_ref[0, pl.ds(q0, tq), 0:1]
            tot_b = tot_ref[1, pl.ds(q0, tq), 0:1]
            zero_c = jnp.zeros((tq, 1), F32)
            zero_q = jnp.zeros((tq, LANES), F32)

            def both(rows, k0, size, ca, cb, mask, tri):
                k2 = k_ref[pl.ds(k0, size), :]
                v2 = v_ref[pl.ds(k0, size), :]
                ca, dka, dva = block(qa[rows], doa[rows], k2, v2, m_a, tot_a[rows], ca, mask, tri)
                cb, dkb, dvb = block(qb[rows], dob[rows], k2, v2, m_b, tot_b[rows], cb, mask, tri)
                dk_acc[pl.ds(k0, size), :] += dka + dkb
                dv_acc[pl.ds(k0, size), :] += dva + dvb
                return ca, cb

            def k_block(kj, carry):
                return both(slice(None), pl.multiple_of(kj * tk, tk), tk, carry[0], carry[1], None, tris)

            init = ((zero_c, zero_c, zero_q), (zero_c, zero_c, zero_q))
            ca, cb = lax.fori_loop(0, qi, k_block, init)
            ca, cb = both(slice(None), q0, th, ca, cb, seen, tris_h)
            late = slice(th, tq)
            la, lb = both(late, pl.multiple_of(q0 + th, th), th, tuple(t[late] for t in ca), tuple(t[late] for t in cb),
                          seen[:th], tris_h)
            dq = jnp.concatenate([ca[2][:th] + cb[2][:th], la[2] + lb[2]], axis=0)
            dq_ref[pl.ds(q0, tq), :] = (dq * ATT_SCALE).astype(dq_ref.dtype)
            return 0

        lax.fori_loop(0, s // tq, q_block, 0)
        dk_ref[...] = dk_acc[...].astype(dk_ref.dtype)
        dv_ref[...] = dv_acc[...].astype(dv_ref.dtype)

        @pl.when(pl.program_id(0) == npair - 1)
        def _():
            exchange.finish(*comm)

    col = lambda off: pl.BlockSpec((s, LANES), functools.partial(lambda p, off: (0, off + p), off=off))
    outs = pl.pallas_call(
        body, name="sb_bwd", grid=(npair,),
        in_specs=[col(0), col(0), col(npair), col(0), pl.BlockSpec((None, 2, s, LANES), lambda p: (p, 0, 0, 0))]
        + [ANY] * ne,
        out_specs=[col(0), col(0), col(0)] + [ANY] * ne,
        out_shape=[jax.ShapeDtypeStruct((s, SB_WIDTH), BF16)] * 3 + exchange.out_shape(),
        scratch_shapes=[pltpu.VMEM((s, LANES), F32), pltpu.VMEM((s, LANES), F32)] + exchange.scratch(),
        compiler_params=_params("arbitrary"),
    )(proj, kv, kv, dcat, tot, *[early_partial[k] for k in exchange.names])
    dq, dk, dv = outs[:3]
    return dq, jnp.concatenate([dk, dv], axis=1), dict(zip(exchange.names, outs[3:]))


def _mem_fwd(name, proj, mkv, tq=512):
    s = proj.shape[0]
    tq = min(tq, s)
    qblk = SB_WIDTH // MEM_WIDTH

    def body(q_ref, kv_ref, o_ref):
        m_a, m_b = _head_masks((tq, LANES))
        mk_a, mk_b = _head_masks((kv_ref.shape[0], LANES))
        for p in range(MEM_WIDTH // LANES):
            q2 = q_ref[:, p * LANES:(p + 1) * LANES]
            k2 = kv_ref[:, p * LANES:(p + 1) * LANES]
            v2 = kv_ref[:, MEM_WIDTH + p * LANES:MEM_WIDTH + (p + 1) * LANES]
            acc = jnp.zeros((tq, LANES), F32)
            for mq, mk in ((m_a, mk_a), (m_b, mk_b)):
                logits = _dot(_pick(mq, q2), k2, "nt") * ATT_SCALE
                e = jnp.exp(logits - jnp.max(logits, axis=-1, keepdims=True))
                prob = e / jnp.sum(e, axis=-1, keepdims=True)
                acc = acc + _dot(prob.astype(BF16), _pick(mk, v2), "nn")
            o_ref[:, p * LANES:(p + 1) * LANES] = acc.astype(o_ref.dtype)

    return pl.pallas_call(
        body, name=name, grid=(s // tq,),
        in_specs=[pl.BlockSpec((tq, MEM_WIDTH), lambda i: (i, qblk)), pl.BlockSpec(mkv.shape, lambda i: (0, 0))],
        out_specs=pl.BlockSpec((tq, MEM_WIDTH), lambda i: (i, 0)),
        out_shape=jax.ShapeDtypeStruct((s, MEM_WIDTH), BF16), compiler_params=_params("arbitrary"),
    )(proj, mkv)


def _mem_bwd(name, proj, mkv, dcat, tq=512):
    s = proj.shape[0]
    tq = min(tq, s)
    qblk = SB_WIDTH // MEM_WIDTH

    def body(q_ref, kv_ref, do_ref, dq_ref, dkv_ref):
        @pl.when(pl.program_id(0) == 0)
        def _():
            dkv_ref[...] = jnp.zeros_like(dkv_ref)

        m_a, m_b = _head_masks((tq, LANES))
        for p in range(MEM_WIDTH // LANES):
            ksl = slice(p * LANES, (p + 1) * LANES)
            vsl = slice(MEM_WIDTH + p * LANES, MEM_WIDTH + (p + 1) * LANES)
            q2, do2 = q_ref[:, ksl], do_ref[:, ksl]
            k2, v2 = kv_ref[:, ksl], kv_ref[:, vsl]
            mk_a, mk_b = _head_masks(k2.shape)
            dq = jnp.zeros((tq, LANES), F32)
            dk = jnp.zeros(k2.shape, F32)
            dv = jnp.zeros(k2.shape, F32)
            for mq, mk in ((m_a, mk_a), (m_b, mk_b)):
                qh, doh = _pick(mq, q2), _pick(mq, do2)
                logits = _dot(qh, k2, "nt") * ATT_SCALE
                e = jnp.exp(logits - jnp.max(logits, axis=-1, keepdims=True))
                prob = e / jnp.sum(e, axis=-1, keepdims=True)
                dp = _dot(doh, v2, "nt")
                ds = prob * (dp - jnp.sum(dp * prob, axis=-1, keepdims=True)) * ATT_SCALE
                ds = ds.astype(BF16)
                dq = dq + _dot(ds, _pick(mk, k2), "nn")
                dk = dk + _dot(ds, qh, "tn")
                dv = dv + _dot(prob.astype(BF16), doh, "tn")
            dq_ref[:, ksl] = dq.astype(dq_ref.dtype)
            dkv_ref[:, ksl] += dk
            dkv_ref[:, vsl] += dv

    return pl.pallas_call(
        body, name=name, grid=(s // tq,),
        in_specs=[pl.BlockSpec((tq, MEM_WIDTH), lambda i: (i, qblk)), pl.BlockSpec(mkv.shape, lambda i: (0, 0)),
                  pl.BlockSpec((tq, MEM_WIDTH), lambda i: (i, qblk))],
        out_specs=[pl.BlockSpec((tq, MEM_WIDTH), lambda i: (i, 0)), pl.BlockSpec(mkv.shape, lambda i: (0, 0))],
        out_shape=[jax.ShapeDtypeStruct((s, MEM_WIDTH), BF16), jax.ShapeDtypeStruct(mkv.shape, F32)],
        compiler_params=_params("arbitrary"),
    )(proj, mkv, dcat)


def _gu_swiglu(name, h, w_gu, tn=256):
    t, d = h.shape
    f = w_gu.shape[1] // 2
    nb = f // tn

    def body(h_ref, wg_ref, wu_ref, g_ref, u_ref, a_ref):
        hh = h_ref[...]
        g = _dot(hh, wg_ref[...], "nn")
        u = _dot(hh, wu_ref[...], "nn")
        g_ref[...] = g.astype(BF16)
        u_ref[...] = u.astype(BF16)
        a_ref[...] = (g * jax.nn.sigmoid(g) * u).astype(BF16)

    out = pl.BlockSpec((t, tn), lambda j: (0, j))
    return pl.pallas_call(
        body, name=name, grid=(nb,),
        in_specs=[pl.BlockSpec((t, d), lambda j: (0, 0)), pl.BlockSpec((d, tn), lambda j: (0, j)),
                  pl.BlockSpec((d, tn), lambda j: (0, nb + j))],
        out_specs=[out, out, out], out_shape=[jax.ShapeDtypeStruct((t, f), BF16)] * 3,
        compiler_params=_params("arbitrary"),
    )(h, w_gu, w_gu)


def _down_dx_swiglu(name, dout_bf, w_down, gate, up, tm=256):
    t, d = dout_bf.shape
    f = w_down.shape[0]

    def body(do_ref, w_ref, g_ref, u_ref, o_ref):
        dact = _dot(do_ref[...], w_ref[...], "nt")
        g, u = g_ref[...].astype(F32), u_ref[...].astype(F32)
        sg = jax.nn.sigmoid(g)
        silu = g * sg
        o_ref[:, :f] = (dact * u * (sg + silu * (1.0 - sg))).astype(BF16)
        o_ref[:, f:] = (dact * silu).astype(BF16)

    row = lambda c: pl.BlockSpec((tm, c), lambda i: (i, 0))
    return pl.pallas_call(
        body, name=name, grid=(t // tm,),
        in_specs=[row(d), pl.BlockSpec((f, d), lambda i: (0, 0)), row(f), row(f)],
        out_specs=row(2 * f), out_shape=jax.ShapeDtypeStruct((t, 2 * f), BF16), compiler_params=_params("arbitrary"),
    )(dout_bf, w_down, gate, up)


def _ffn_fwd(tag, x, norm, w_gu, w_down):
    h = _rms_fwd(tag + "_ffn_norm", x, norm)
    gate, up, act = _gu_swiglu(tag + "_gu", h, w_gu)
    out = _mm(tag + "_down", act, w_down, "nn", F32, res=x, tn=1024)
    return out, (h, gate, up, act)


def _ffn_bwd(tag, x, norm, w_gu, w_down, saved, dout, dout_bf):
    h, gate, up, act = saved
    g_down = _mm(tag + "_down_dw", act, dout_bf, "tn", BF16, tm=256, tn=1024)
    dgu = _down_dx_swiglu(tag + "_down_dx", dout_bf, w_down, gate, up)
    g_gu = _mm(tag + "_gu_dw", h, dgu, "tn", BF16, tm=1024)
    dh = _mm(tag + "_gu_dx", dgu, w_gu, "nt", F32, tn=1024)
    dx, dx_bf, g_norm = _rms_bwd(tag + "_ffn_norm_bwd", x, norm, dh, dres=dout)
    return dx, dx_bf, g_gu, g_down, g_norm


def _mem_kv(tag, mem_n, w_mem_kv):
    return _mm(tag + "_memkv", mem_n, w_mem_kv, "nn", BF16)


def _mem_kv_bwd(tag, mem, mem_norm, mem_n, w_mem_kv, dmkv):
    dmkv = dmkv.astype(BF16)
    g_w = _mm(tag + "_memkv_dw", mem_n, dmkv, "tn", BF16)
    dmem_n = _mm(tag + "_memkv_dx", dmkv, w_mem_kv, "nt", F32)
    (g_norm,) = _rms_bwd(tag + "_memnorm_bwd", mem, mem_norm, dmem_n, want_dx=False)
    return g_w, g_norm


def _block_diag(w_group):
    z = jnp.zeros((POOL_GROUP, POOL_GROUP), w_group.dtype)
    return jnp.concatenate(
        [jnp.concatenate([w_group[g] if h == g else z for h in range(4)], axis=1) for g in range(4)], axis=0)


def _local_step(x, mem, target, w, late_shards):
    g = {}
    w = dict(w)
    mem_n = _rms_fwd("mem_norm", mem, w["mem_norm"])
    h_a = _rms_fwd("a_mix_norm", x, w["a_norm_mix"])
    proj_a = _mm("a_in", h_a, w["a_w_in"], "nn", F32, tn=1024)
    pooled = _pool("a_pool", proj_a, reverse=False)
    w_bd = _block_diag(w["a_w_group"])
    g_pre = _mm("a_group", pooled, w_bd, "nn", BF16, tn=768)
    mkv_a = _mem_kv("a", mem_n, w["a_w_mem_kv"])
    proj_a_bf = proj_a.astype(BF16)
    mem_a = _mem_fwd("a_mem_attn", proj_a_bf, mkv_a)
    cat_a = _rowwise("a_cat", lambda gp, mo, sc: jnp.concatenate([gp.astype(F32) * sc, mo.astype(F32)], axis=1),
                     [g_pre, mem_a], [w["a_scale"]], [(1024, BF16)])[0]
    x1 = _mm("a_out", cat_a, w["a_w_out"], "nn", F32, res=x, tn=1024)
    x2, ffn_a = _ffn_fwd("a", x1, w["a_norm_ffn"], w["a_w_gu"], w["a_w_down"])
    h_k = _rms_fwd("kv_norm", x2, w["kv_norm"])
    kv = _mm("kv_proj", h_k, w["w_kv"], "nn", BF16, tn=1536)
    h_b = _rms_fwd("b_mix_norm", x2, w["b_norm_mix"])
    proj_b = _mm("b_q", h_b, w["b_w_q"], "nn", BF16, tn=1024)
    sb_out, tot, late = _sb_fwd(proj_b, kv, late_shards)
    w.update(late)
    mkv_b = _mem_kv("b", mem_n, w["b_w_mem_kv"])
    mem_b = _mem_fwd("b_mem_attn", proj_b, mkv_b)
    cat_b = jnp.concatenate([sb_out, mem_b], axis=1)
    x3 = _mm("b_out", cat_b, w["b_w_out"], "nn", F32, res=x2, tn=1024)
    x4, ffn_b = _ffn_fwd("b", x3, w["b_norm_ffn"], w["b_w_gu"], w["b_w_down"])

    d = x.shape[1]

    def head(xt, tt, gt):
        rstd = lax.rsqrt(jnp.mean(xt * xt, axis=-1, keepdims=True) + EPS)
        xhat = xt * rstd
        err = xhat * gt - tt
        loss = 0.5 * jnp.sum(jnp.sum(err * err, axis=1, keepdims=True), axis=0, keepdims=True) / d
        dy = err / d
        dxhat = dy * gt
        dx = rstd * (dxhat - xhat * jnp.mean(dxhat * xhat, axis=-1, keepdims=True))
        return dx, dx, jnp.sum(dy * xhat, axis=0, keepdims=True), jnp.broadcast_to(loss, (1, LANES))

    dx4, dx4_bf, g["final_norm"], loss = _rowwise(
        "loss_head", head, [x4, target], [w["final_norm"]], [(d, F32), (d, BF16)], [d, LANES])

    dx3, dx3_bf, g["b_w_gu"], g["b_w_down"], g["b_norm_ffn"] = _ffn_bwd(
        "b", x3, w["b_norm_ffn"], w["b_w_gu"], w["b_w_down"], ffn_b, dx4, dx4_bf)
    dcat_b = _mm("b_out_dx", dx3_bf, w["b_w_out"], "nt", BF16, tn=1024)
    g["b_w_out"] = _mm("b_out_dw", cat_b, dx3_bf, "tn", BF16, tm=1024, tn=1024)
    early_partial = _pair_sums("early", {k: g.pop(k) for k in EARLY})
    dq_sb, dkv, early_got = _sb_bwd(proj_b, kv, dcat_b, tot, early_partial)
    dq_mem_b, dmkv_b = _mem_bwd("b_mem_attn_bwd", proj_b, mkv_b, dcat_b)
    dproj_b = jnp.concatenate([dq_sb, dq_mem_b], axis=1)
    g["b_w_q"] = _mm("b_q_dw", h_b, dproj_b, "tn", BF16, tm=1024, tn=1024)
    dh_b = _mm("b_q_dx", dproj_b, w["b_w_q"], "nt", F32, tn=1024)
    dx2, _, g["b_norm_mix"] = _rms_bwd("b_mix_norm_bwd", x2, w["b_norm_mix"], dh_b, dres=dx3)
    g["b_w_mem_kv"], g_memnorm_b = _mem_kv_bwd("b", mem, w["mem_norm"], mem_n, w["b_w_mem_kv"], dmkv_b)
    g["w_kv"] = _mm("kv_proj_dw", h_k, dkv, "tn", BF16, tm=1024)
    dh_k = _mm("kv_proj_dx", dkv, w["w_kv"], "nt", F32, tn=1024)
    dx2, dx2_bf, g["kv_norm"] = _rms_bwd("kv_norm_bwd", x2, w["kv_norm"], dh_k, dres=dx2)

    dx1, dx1_bf, g["a_w_gu"], g["a_w_down"], g["a_norm_ffn"] = _ffn_bwd(
        "a", x1, w["a_norm_ffn"], w["a_w_gu"], w["a_w_down"], ffn_a, dx2, dx2_bf)
    dcat_a = _mm("a_out_dx", dx1_bf, w["a_w_out"], "nt", BF16, tn=1024)
    g["a_w_out"] = _mm("a_out_dw", cat_a, dx1_bf, "tn", BF16, tm=1024, tn=1024)

    def scale_bwd(dc, gp, sc):
        dc, gp = dc.astype(F32), gp.astype(F32)
        return dc * sc, jnp.sum(dc * gp, axis=0, keepdims=True)

    dg_pre, g["a_scale"] = _rowwise("a_scale_bwd", scale_bwd, [(dcat_a, (SB_WIDTH, 0)), g_pre], [w["a_scale"]],
                                    [(SB_WIDTH, BF16)], [SB_WIDTH])
    g_bd = _mm("a_group_dw", pooled, dg_pre, "tn", F32, tm=768, tn=768)
    g["a_w_group"] = jnp.stack([g_bd[i * POOL_GROUP:(i + 1) * POOL_GROUP, i * POOL_GROUP:(i + 1) * POOL_GROUP]
                                for i in range(4)])
    dpooled = _mm("a_group_dx", dg_pre, w_bd, "nt", F32, tn=768)
    du_pool = _pool("a_pool_bwd", dpooled, reverse=True)
    dq_mem_a, dmkv_a = _mem_bwd("a_mem_attn_bwd", proj_a_bf, mkv_a, dcat_a)
    dproj_a = jnp.concatenate([du_pool, dq_mem_a], axis=1)
    g["a_w_in"] = _mm("a_in_dw", h_a, dproj_a, "tn", BF16, tm=1024, tn=1024)
    dh_a = _mm("a_in_dx", dproj_a, w["a_w_in"], "nt", F32, tn=1024)
    grad_x, _, g["a_norm_mix"] = _rms_bwd("a_mix_norm_bwd", x, w["a_norm_mix"], dh_a, dres=dx1)
    g["a_w_mem_kv"], g_memnorm_a = _mem_kv_bwd("a", mem, w["mem_norm"], mem_n, w["a_w_mem_kv"], dmkv_a)
    g["mem_norm"] = g_memnorm_a + g_memnorm_b
    return loss, grad_x, g, early_partial, early_got


ROW_SHARDED = ("a_w_in", "a_w_mem_kv", "a_w_out", "a_w_down", "b_w_q", "b_w_mem_kv", "b_w_out", "b_w_down")
COL_SHARDED = ("a_w_gu", "w_kv", "b_w_gu")
BIG = ("a_w_in", "a_w_mem_kv", "a_w_out", "a_w_gu", "a_w_down", "w_kv", "b_w_q", "b_w_mem_kv", "b_w_out", "b_w_gu",
       "b_w_down")
LATE = ("b_w_mem_kv", "b_w_out", "b_w_gu", "b_w_down")
EARLY = ("b_w_gu", "b_w_down", "b_w_out")
N_CHIPS = 4
N_DEV = 8


def _position():
    x, y, c = lax.axis_index("x"), lax.axis_index("y"), lax.axis_index("c")
    other_chips = [(1 - x, y), (x, 1 - y), (1 - x, 1 - y)]
    return x, y, c, other_chips


def _remote(src, dst, send_sem, recv_sem, device):
    return pltpu.make_async_remote_copy(src_ref=src, dst_ref=dst, send_sem=send_sem, recv_sem=recv_sem,
                                        device_id=device, device_id_type=MESH)


def _comm_call(name, body, args, out_shape, n_remote, aliases=None):
    return pl.pallas_call(
        body, name=name, in_specs=[ANY] * len(args), out_specs=[ANY] * len(out_shape), out_shape=out_shape,
        scratch_shapes=[pltpu.SemaphoreType.DMA((n_remote,)), pltpu.SemaphoreType.DMA((n_remote,))],
        input_output_aliases=aliases or {},
    )(*args)


def _row_chunks(nrows, row_bytes, mult=16):
    assert nrows % mult == 0, (nrows, mult)
    per = max(mult, (COPY_BYTES // row_bytes) // mult * mult)
    return [(r0, min(per, nrows - r0)) for r0 in range(0, nrows, per)]


def _rows(ref, start, size, lead=()):
    if isinstance(start, int):
        return ref.at[(*lead, pl.ds(start, size))]
    return ref.at[(*lead, pl.ds(pl.multiple_of(start, 16), size))]


class _Copies:
    def __init__(self, send, recv):
        self.send, self.recv = send, recv
        self.n_remote = 0
        self.remotes = []

    def slot(self):
        self.n_remote += 1
        return self.n_remote - 1

    def remote(self, k, src, dst, device):
        cp = _remote(src, dst, self.send.at[k], self.recv.at[k], device)
        cp.start()
        self.remotes.append(cp)
        return cp

    def finish(self):
        for cp in self.remotes:
            cp.wait_send()


def _shard_cols(ref, row_sharded, cdim, chip):
    if row_sharded:
        return ref
    return ref.at[:, pl.ds(pl.multiple_of(chip * cdim, LANES), cdim)]


class _Gather:
    def __init__(self, shapes, mult=16):
        self.names = list(shapes)
        self.shapes = [tuple(shapes[k]) for k in self.names]
        self.row_sharded = [k in ROW_SHARDED for k in self.names]
        self.chunks = [(i, r0, size) for i, (r, cdim) in enumerate(self.shapes)
                       for r0, size in _row_chunks(r // 2, cdim * 2, mult)]
        self.n = len(self.names)

    def out_shape(self):
        return [jax.ShapeDtypeStruct((N_CHIPS * r, cdim) if rs else (r, N_CHIPS * cdim), BF16)
                for (r, cdim), rs in zip(self.shapes, self.row_sharded)]

    def scratch(self):
        n_remote, n_local = 6 * len(self.chunks), self.n + 2 * len(self.chunks)
        return [pltpu.SemaphoreType.DMA((n_remote,)), pltpu.SemaphoreType.DMA((n_remote,)),
                pltpu.SemaphoreType.DMA((n_local,))] + [pltpu.VMEM(s, BF16) for s in self.shapes]

    def _window(self, dst, i, chip, half, r0, size):
        r, cdim = self.shapes[i]
        base = (chip * r if self.row_sharded[i] else 0) + half * (r // 2) + r0
        return _rows(_shard_cols(dst[i], self.row_sharded[i], cdim, chip), base, size)

    def _mine(self, refs, i, half, r0, size):
        return _rows(refs[i], half * (self.shapes[i][0] // 2) + r0, size)

    def _sent(self, src, dst, sems, q, k, px, py, c, me):
        i, r0, size = self.chunks[q]
        return _remote(self._mine(src, i, c, r0, size), self._window(dst, i, me, c, r0, size),
                       sems[0].at[6 * q + k], sems[1].at[6 * q + k], (px, py, c))

    def start(self, src, dst, sems, vm):
        x, y, c, chips = _position()
        for q in range(len(self.chunks)):
            for k, (px, py) in enumerate(chips):
                self._sent(src, dst, sems, q, k, px, py, c, 2 * x + y).start()
        for i in range(self.n):
            pltpu.make_async_copy(src[i], vm[i], sems[2].at[i]).start()

    def finish(self, src, dst, sems, vm):
        x, y, c, chips = _position()
        me = 2 * x + y
        send, recv, loc = sems
        for i in range(self.n):
            pltpu.make_async_copy(src[i], vm[i], loc.at[i]).wait()
        placed, forwarded = [], []
        for q, (i, r0, size) in enumerate(self.chunks):
            for half in range(2):
                cp = pltpu.make_async_copy(self._mine(vm, i, half, r0, size), self._window(dst, i, me, half, r0, size),
                                           loc.at[self.n + 2 * q + half])
                cp.start()
                placed.append(cp)
        for q, (i, r0, size) in enumerate(self.chunks):
            for k, (px, py) in enumerate(chips):
                landed = self._window(dst, i, 2 * px + py, c, r0, size)
                _remote(landed, landed, send.at[6 * q + k], recv.at[6 * q + k], (px, py, c)).wait_recv()
                cp = _remote(landed, landed, send.at[6 * q + 3 + k], recv.at[6 * q + 3 + k], (x, y, 1 - c))
                cp.start()
                forwarded.append(cp)
        for q, (i, r0, size) in enumerate(self.chunks):
            for k, (px, py) in enumerate(chips):
                landed = self._window(dst, i, 2 * px + py, 1 - c, r0, size)
                _remote(landed, landed, send.at[6 * q + 3 + k], recv.at[6 * q + 3 + k], (x, y, 1 - c)).wait_recv()
        for q in range(len(self.chunks)):
            for k, (px, py) in enumerate(chips):
                self._sent(src, dst, sems, q, k, px, py, c, me).wait_send()
        for cp in forwarded:
            cp.wait_send()
        for cp in placed:
            cp.wait()


class _GatherRelay(_Gather):
    def __init__(self, shapes):
        super().__init__(shapes, mult=32)

    def scratch(self):
        n_remote, n_local = 8 * len(self.chunks), self.n + 2 * len(self.chunks)
        return [pltpu.SemaphoreType.DMA((n_remote,)), pltpu.SemaphoreType.DMA((n_remote,)),
                pltpu.SemaphoreType.DMA((n_local,))] + [pltpu.VMEM(s, BF16) for s in self.shapes]

    def run(self, src, dst, sems, vm):
        x, y, c, chips = _position()
        me, diag, sibling = 2 * x + y, 2 * (1 - x) + (1 - y), (x, y, 1 - c)
        nbrs = chips[:2]
        send, recv, loc = sems

        def copy(slot, ref, device):
            return _remote(ref, ref, send.at[slot], recv.at[slot], device)

        first = []
        for q, (i, r0, size) in enumerate(self.chunks):
            for k, (px, py) in enumerate(nbrs):
                cp = _remote(self._mine(src, i, c, r0, size), self._window(dst, i, me, c, r0, size),
                             send.at[8 * q + k], recv.at[8 * q + k], (px, py, c))
                cp.start()
                first.append(cp)
        for i in range(self.n):
            pltpu.make_async_copy(src[i], vm[i], loc.at[i]).start()
        for i in range(self.n):
            pltpu.make_async_copy(src[i], vm[i], loc.at[i]).wait()
        placed, passed = [], []
        for q, (i, r0, size) in enumerate(self.chunks):
            for half in range(2):
                cp = pltpu.make_async_copy(self._mine(vm, i, half, r0, size), self._window(dst, i, me, half, r0, size),
                                           loc.at[self.n + 2 * q + half])
                cp.start()
                placed.append(cp)
        for q, (i, r0, size) in enumerate(self.chunks):
            for k, (px, py) in enumerate(nbrs):
                landed = self._window(dst, i, 2 * px + py, c, r0, size)
                copy(8 * q + k, landed, (px, py, c)).wait_recv()
                piece = self._window(dst, i, 2 * px + py, c, r0 + k * (size // 2), size // 2)
                ox, oy = nbrs[1 - k]
                for cp in (copy(8 * q + 2 + k, piece, (ox, oy, c)), copy(8 * q + 4 + k, landed, sibling)):
                    cp.start()
                    passed.append(cp)
        for q, (i, r0, size) in enumerate(self.chunks):
            for k in range(2):
                piece = self._window(dst, i, diag, c, r0 + k * (size // 2), size // 2)
                ox, oy = nbrs[1 - k]
                copy(8 * q + 2 + k, piece, (ox, oy, c)).wait_recv()
                cp = copy(8 * q + 6 + k, piece, sibling)
                cp.start()
                passed.append(cp)
        for q, (i, r0, size) in enumerate(self.chunks):
            for k, (px, py) in enumerate(nbrs):
                copy(8 * q + 4 + k, self._window(dst, i, 2 * px + py, 1 - c, r0, size), sibling).wait_recv()
                copy(8 * q + 6 + k, self._window(dst, i, diag, 1 - c, r0 + k * (size // 2), size // 2), sibling).wait_recv()
        for cp in first + passed:
            cp.wait_send()
        for cp in placed:
            cp.wait()


def _gather_weights(shards):
    plan = _GatherRelay({k: v.shape for k, v in shards.items()})
    n = plan.n

    def body(*refs):
        plan.run(refs[:n], refs[n:2 * n], refs[2 * n:2 * n + 3], refs[2 * n + 3:])

    outs = pl.pallas_call(
        body, name="gather_weights", in_specs=[ANY] * n, out_specs=[ANY] * n, out_shape=plan.out_shape(),
        scratch_shapes=plan.scratch(), compiler_params=pltpu.CompilerParams(vmem_limit_bytes=VMEM_LIMIT),
    )(*[shards[k] for k in plan.names])
    return dict(zip(plan.names, outs))


def _scalar_grid_call(name, body, scalars, grid, in_specs, out_specs, out_shape, args):
    return pl.pallas_call(
        body, name=name, out_shape=out_shape,
        grid_spec=pltpu.PrefetchScalarGridSpec(num_scalar_prefetch=1, grid=grid, in_specs=in_specs, out_specs=out_specs),
        compiler_params=_params(*["arbitrary"] * len(grid)),
    )(scalars, *args)


def _pair_sum(name, g4, sib, where, tb=256):
    j, _, r, w = g4.shape
    tb = _row_tile(r, tb)

    def body(s_ref, g_ref, b_ref, o_ref):
        o_ref[...] = (g_ref[...].astype(F32) + b_ref[...].astype(F32)).astype(o_ref.dtype)

    blk = pl.BlockSpec((None, tb, w), lambda a, i, s: (a, i, 0))
    return _scalar_grid_call(
        name, body, where, (j, r // tb),
        [pl.BlockSpec((None, None, tb, w), lambda a, i, s: (a, s[0], i, 0)), blk], blk,
        jax.ShapeDtypeStruct((j, r, w), BF16), (g4, sib))


def _chip_sum(name, partial, got, where, row_sharded, tb=256):
    _, r, cdim = got.shape
    tb = _row_tile(r, tb)

    def body(s_ref, p_ref, g_ref, o_ref):
        acc = p_ref[...].astype(F32)
        for k in range(N_CHIPS - 1):
            acc = acc + g_ref[k].astype(F32)
        o_ref[...] = acc

    if row_sharded:
        own = pl.BlockSpec((None, tb, cdim), lambda i, s: (s[1], i, 0))
    else:
        own = pl.BlockSpec((None, tb, cdim), lambda i, s: (0, i, s[1]))
    return _scalar_grid_call(
        name, body, where, (r // tb,),
        [own, pl.BlockSpec((N_CHIPS - 1, tb, cdim), lambda i, s: (0, i, 0))],
        pl.BlockSpec((None, tb, cdim), lambda i, s: (s[0], i, 0)),
        jax.ShapeDtypeStruct((2, r, cdim), F32), (partial, got))


def _where():
    return jnp.stack([lax.axis_index("c"), 2 * lax.axis_index("x") + lax.axis_index("y")]).astype(jnp.int32)


def _grad_halves(name, g):
    rows, cols = g.shape
    if name in ROW_SHARDED:
        r = rows // N_CHIPS
        return g.reshape(N_CHIPS, 2, r // 2, cols), (r // 2, cols)
    return g.reshape(1, 2, rows // 2, cols), (rows // 2, cols // N_CHIPS)


def _pair_sums(tag, grads):
    names = list(grads)
    n = len(names)
    g4 = [_grad_halves(k, grads[k])[0] for k in names]
    plan = [[(j, r0, size) for j in range(g.shape[0]) for r0, size in _row_chunks(g.shape[2], g.shape[3] * 2)]
            for g in g4]
    out_shape = [jax.ShapeDtypeStruct((g.shape[0],) + g.shape[2:], BF16) for g in g4]

    def body(*refs):
        src, sib = refs[:n], refs[n:2 * n]
        cps = _Copies(*refs[2 * n:])
        x, y, c, _ = _position()
        waits = []
        for i in range(n):
            for j, r0, size in plan[i]:
                waits.append(cps.remote(cps.slot(), _rows(src[i], r0, size, lead=(j, 1 - c)),
                                        _rows(sib[i], r0, size, lead=(j,)), (x, y, 1 - c)))
        for cp in waits:
            cp.wait_recv()
        cps.finish()

    sibs = _comm_call("grads_pair_exchange_" + tag, body, g4, out_shape, sum(len(p) for p in plan))
    where = _where()
    return {k: _pair_sum(k + "_pair_sum", g, s, where) for k, g, s in zip(names, g4, sibs)}


class _ChipExchange:
    def __init__(self, partial, mult=16):
        self.names = list(partial)
        self.n = len(self.names)
        self.row_sharded = [k in ROW_SHARDED for k in self.names]
        self.half = []
        for k, rs in zip(self.names, self.row_sharded):
            _, r, w = partial[k].shape
            self.half.append((r, w) if rs else (r, w // N_CHIPS))
        self.chunks = [(i, r0, size) for i, (r, cdim) in enumerate(self.half)
                       for r0, size in _row_chunks(r, cdim * 2, mult)]

    def shard(self, src, i, chip, r0, size):
        if self.row_sharded[i]:
            return _rows(src[i], r0, size, lead=(chip,))
        return _rows(_shard_cols(src[i].at[0], False, self.half[i][1], chip), r0, size)

    def out_shape(self):
        return [jax.ShapeDtypeStruct((N_CHIPS - 1,) + s, BF16) for s in self.half]

    def scratch(self):
        n_remote = 3 * len(self.chunks)
        return [pltpu.SemaphoreType.DMA((n_remote,)), pltpu.SemaphoreType.DMA((n_remote,))]

    def _copies(self, src, dst, sems):
        x, y, c, chips = _position()
        for q, (i, r0, size) in enumerate(self.chunks):
            for k, (px, py) in enumerate(chips):
                yield _remote(self.shard(src, i, 2 * px + py, r0, size), _rows(dst[i], r0, size, lead=(k,)),
                              sems[0].at[3 * q + k], sems[1].at[3 * q + k], (px, py, c))

    def start(self, src, dst, sems):
        for cp in self._copies(src, dst, sems):
            cp.start()

    def finish(self, src, dst, sems):
        for cp in self._copies(src, dst, sems):
            cp.wait_recv()
        for cp in self._copies(src, dst, sems):
            cp.wait_send()


def _chip_exchange(partial):
    plan = _ChipExchange(partial, mult=32)
    n = plan.n

    def body(*refs):
        src, dst, via = refs[:n], refs[n:2 * n], refs[2 * n:3 * n]
        send, recv = refs[3 * n:]
        x, y, c, chips = _position()
        nbrs = chips[:2]
        diag = 2 * (1 - x) + (1 - y)
        started = []

        def go(slot, source, target, device):
            cp = _remote(source, target, send.at[slot], recv.at[slot], device)
            cp.start()
            started.append(cp)

        def arrived(slot, ref, device):
            _remote(ref, ref, send.at[slot], recv.at[slot], device).wait_recv()

        for q, (i, r0, size) in enumerate(plan.chunks):
            hs = size // 2
            for k, (px, py) in enumerate(nbrs):
                go(6 * q + k, plan.shard(src, i, 2 * px + py, r0, size), _rows(dst[i], r0, size, lead=(k,)), (px, py, c))
                go(6 * q + 2 + k, plan.shard(src, i, diag, r0 + k * hs, hs), _rows(via[i], r0 + k * hs, hs, lead=(k,)),
                   (px, py, c))
        for q, (i, r0, size) in enumerate(plan.chunks):
            hs = size // 2
            for k, (px, py) in enumerate(nbrs):
                piece = _rows(via[i], r0 + k * hs, hs, lead=(k,))
                arrived(6 * q + 2 + k, piece, (px, py, c))
                ox, oy = nbrs[1 - k]
                go(6 * q + 4 + k, piece, _rows(dst[i], r0 + k * hs, hs, lead=(2,)), (ox, oy, c))
        for q, (i, r0, size) in enumerate(plan.chunks):
            hs = size // 2
            for k, (px, py) in enumerate(nbrs):
                arrived(6 * q + k, _rows(dst[i], r0, size, lead=(k,)), (px, py, c))
                ox, oy = nbrs[1 - k]
                arrived(6 * q + 4 + k, _rows(dst[i], r0 + k * hs, hs, lead=(2,)), (ox, oy, c))
        for cp in started:
            cp.wait_send()

    n_remote = 6 * len(plan.chunks)
    outs = pl.pallas_call(
        body, name="grads_chip_exchange", in_specs=[ANY] * n, out_specs=[ANY] * (2 * n),
        out_shape=plan.out_shape() + [jax.ShapeDtypeStruct((2,) + s, BF16) for s in plan.half],
        scratch_shapes=[pltpu.SemaphoreType.DMA((n_remote,)), pltpu.SemaphoreType.DMA((n_remote,))],
    )(*[partial[k] for k in plan.names])
    return dict(zip(plan.names, outs[:n]))


def _finish_reduce(partial, got):
    names = list(partial)
    n = len(names)
    where = _where()
    halves = [_chip_sum(k + "_chip_sum", partial[k], got[k], where, k in ROW_SHARDED) for k in names]
    plan = [_row_chunks(h.shape[1], h.shape[2] * 4) for h in halves]
    out_shape = [jax.ShapeDtypeStruct(h.shape, F32) for h in halves]

    def body(*refs):
        src, dst = refs[:n], refs[n:2 * n]
        cps = _Copies(*refs[2 * n:])
        x, y, c, _ = _position()
        waits = []
        for i in range(n):
            for r0, size in plan[i]:
                waits.append(cps.remote(cps.slot(), _rows(src[i], r0, size, lead=(c,)), _rows(dst[i], r0, size, lead=(c,)),
                                        (x, y, 1 - c)))
        for cp in waits:
            cp.wait_recv()
        cps.finish()

    outs = _comm_call("grads_pair_share", body, halves, out_shape, sum(len(p) for p in plan),
                      aliases={i: i for i in range(n)})
    return {k: o.reshape(2 * o.shape[1], o.shape[2]) for k, o in zip(names, outs)}


def _all_reduce_small(name, v):
    rows, cols = v.shape
    h = rows // 2

    def body(v_ref, o_ref, sib, pair, buf, send, recv):
        x, y, c, chips = _position()
        me = 2 * x + y
        sibling = (x, y, 1 - c)
        cp = _remote(v_ref, sib, send.at[0], recv.at[0], sibling)
        cp.start()
        cp.wait()
        pair[...] = v_ref[...] + sib[...]
        mine = pl.ds(pl.multiple_of(c * h, 8), h)
        buf[me] = pair[mine, :]
        sends = [_remote(pair.at[mine], buf.at[me], send.at[1 + k], recv.at[1 + k], (px, py, c))
                 for k, (px, py) in enumerate(chips)]
        for cp in sends:
            cp.start()
        for k, (px, py) in enumerate(chips):
            _remote(pair.at[mine], buf.at[2 * px + py], send.at[1 + k], recv.at[1 + k], (px, py, c)).wait_recv()
        for cp in sends:
            cp.wait_send()
        o_ref[mine, :] = (buf[0] + buf[1]) + (buf[2] + buf[3])
        cp = _remote(o_ref.at[mine], o_ref.at[mine], send.at[4], recv.at[4], sibling)
        cp.start()
        cp.wait()

    vm = pl.BlockSpec(memory_space=pltpu.VMEM)
    return pl.pallas_call(
        body, name=name, in_specs=[vm], out_specs=vm, out_shape=jax.ShapeDtypeStruct(v.shape, F32),
        scratch_shapes=[pltpu.VMEM((rows, cols), F32), pltpu.VMEM((rows, cols), F32), pltpu.VMEM((N_CHIPS, h, cols), F32),
                        pltpu.SemaphoreType.DMA((5,)), pltpu.SemaphoreType.DMA((5,))],
        compiler_params=pltpu.CompilerParams(vmem_limit_bytes=VMEM_LIMIT),
    )(v)


def _adamw(name, w, g, m, v):
    def fn(wt, gt, mt, vt):
        mt = ADAM_B1 * mt + (1.0 - ADAM_B1) * gt
        vt = ADAM_B2 * vt + (1.0 - ADAM_B2) * (gt * gt)
        m_hat = mt / (1.0 - ADAM_B1 ** ADAM_STEP)
        v_hat = vt / (1.0 - ADAM_B2 ** ADAM_STEP)
        delta = -ADAM_LR * (m_hat / (jnp.sqrt(v_hat) + ADAM_EPS) + ADAM_WD * wt)
        return delta, mt, vt
    n = w.shape[1]
    return _rowwise(name, fn, [w, g, m, v], [], [(n, F32)] * 3, tb=256)


WEIGHTS = ("mem_norm", "a_norm_mix", "a_w_in", "a_w_group", "a_scale", "a_w_mem_kv", "a_w_out", "a_norm_ffn", "a_w_gu",
           "a_w_down", "kv_norm", "w_kv", "b_norm_mix", "b_w_q", "b_w_mem_kv", "b_w_out", "b_norm_ffn", "b_w_gu",
           "b_w_down", "final_norm")
REPLICATED_VECS = ("mem_norm", "kv_norm", "b_norm_mix", "b_norm_ffn", "final_norm")
SHARDED_VECS = ("a_norm_mix", "a_norm_ffn", "a_scale")
D_MODEL = 1024
GROUP_ROWS = 4 * POOL_GROUP * POOL_GROUP // D_MODEL


def _row(v):
    v = v.reshape(1, -1).astype(F32)
    return jnp.pad(v, ((0, 0), (0, D_MODEL - v.shape[1])))


def _pack_small(t):
    rows = [_row(t[k]) for k in REPLICATED_VECS]
    rows.append(_row(jnp.concatenate([t[k].reshape(-1) for k in SHARDED_VECS])))
    rows.append(jnp.zeros((2, D_MODEL), F32))
    rows.append(t["a_w_group"].reshape(GROUP_ROWS, D_MODEL).astype(F32))
    return jnp.concatenate(rows, axis=0)


def _unpack_small(p, like):
    out = {k: p[i, :].reshape(like[k].shape) for i, k in enumerate(REPLICATED_VECS)}
    off = 0
    for k in SHARDED_VECS:
        size = like[k].size
        out[k] = p[len(REPLICATED_VECS), off:off + size].reshape(like[k].shape)
        off += size
    out["a_w_group"] = p[len(REPLICATED_VECS) + 3:, :].reshape(like["a_w_group"].shape)
    return out


def kernel(x, mem, mem_norm, a_norm_mix, a_w_in, a_w_group, a_scale, a_w_mem_kv, a_w_out, a_norm_ffn, a_w_gu, a_w_down, kv_norm, w_kv, b_norm_mix, b_w_q, b_w_mem_kv, b_w_out, b_norm_ffn, b_w_gu, b_w_down, final_norm, loss_target, m_mem_norm, m_a_norm_mix, m_a_w_in, m_a_w_group, m_a_scale, m_a_w_mem_kv, m_a_w_out, m_a_norm_ffn, m_a_w_gu, m_a_w_down, m_kv_norm, m_w_kv, m_b_norm_mix, m_b_w_q, m_b_w_mem_kv, m_b_w_out, m_b_norm_ffn, m_b_w_gu, m_b_w_down, m_final_norm, v_mem_norm, v_a_norm_mix, v_a_w_in, v_a_w_group, v_a_scale, v_a_w_mem_kv, v_a_w_out, v_a_norm_ffn, v_a_w_gu, v_a_w_down, v_kv_norm, v_w_kv, v_b_norm_mix, v_b_w_q, v_b_w_mem_kv, v_b_w_out, v_b_norm_ffn, v_b_w_gu, v_b_w_down, v_final_norm):
    given = dict(locals())
    wl = {k: given[k] for k in WEIGHTS}
    ml = {k: given["m_" + k] for k in WEIGHTS}
    vl = {k: given["v_" + k] for k in WEIGHTS}
    chip = 2 * lax.axis_index("x") + lax.axis_index("y")

    def mat(a):
        return a.reshape(a.shape[-2], a.shape[-1])

    shards = {k: mat(wl[k]).astype(BF16) for k in BIG}
    full = _gather_weights({k: shards[k] for k in BIG if k not in LATE})
    gains = jnp.zeros((16, D_MODEL), F32)
    for i, k in enumerate(SHARDED_VECS):
        part = wl[k].reshape(1, -1)
        width = part.shape[1]
        gains = lax.dynamic_update_slice(gains, part, (i, chip * width))
    gains = _all_reduce_small("gains_all_gather", gains) * 0.5
    w = dict(full)
    for k in REPLICATED_VECS:
        w[k] = wl[k].reshape(1, D_MODEL)
    w["a_norm_mix"], w["a_norm_ffn"] = gains[0:1], gains[1:2]
    w["a_scale"] = gains[2:3, :SB_WIDTH]
    w["a_w_group"] = wl["a_w_group"][0].astype(BF16)

    loss, grad_x, g, partial, got = _local_step(x[0], mem[0], loss_target[0], w, {k: shards[k] for k in LATE})

    rest = _pair_sums("late", {k: g[k] for k in BIG if k not in EARLY})
    partial.update(rest)
    got.update(_chip_exchange(rest))
    red = _finish_reduce(partial, got)
    small = jnp.concatenate(
        [_row(g[k]) for k in REPLICATED_VECS] + [_row(g[k]) for k in SHARDED_VECS] + [_row(loss)]
        + [jnp.zeros((7, D_MODEL), F32), g["a_w_group"].reshape(GROUP_ROWS, D_MODEL)], axis=0)
    small = _all_reduce_small("small_grads_all_reduce", small)
    gs = {k: small[i] for i, k in enumerate(REPLICATED_VECS)}
    for i, k in enumerate(SHARDED_VECS):
        width = wl[k].shape[-1]
        gs[k] = lax.dynamic_slice(small[len(REPLICATED_VECS) + i], (chip * width,), (width,))
    gs["a_w_group"] = small[16:]
    total_loss = small[8, 0]

    out_g, out_d, out_m, out_v = {}, {}, {}, {}
    for k in BIG:
        shape = wl[k].shape
        out_g[k] = red[k].reshape(shape)
        d, nm, nv = _adamw(k + "_adamw", mat(wl[k]), red[k], mat(ml[k]), mat(vl[k]))
        out_d[k], out_m[k], out_v[k] = d.reshape(shape), nm.reshape(shape), nv.reshape(shape)
    small_names = REPLICATED_VECS + SHARDED_VECS + ("a_w_group",)
    d, nm, nv = _adamw("small_adamw", _pack_small(wl), _pack_small(gs), _pack_small(ml), _pack_small(vl))
    like = {k: wl[k] for k in small_names}
    for dst, p in ((out_d, d), (out_m, nm), (out_v, nv)):
        dst.update(_unpack_small(p, like))
    for k in small_names:
        out_g[k] = gs[k].reshape(wl[k].shape)

    return (total_loss, grad_x[None], *[out_g[k] for k in WEIGHTS], *[out_d[k] for k in WEIGHTS],
            *[out_m[k] for k in WEIGHTS], *[out_v[k] for k in WEIGHTS])
```

```python
import functools

import jax
import jax.numpy as jnp
from jax import lax
from jax.experimental import pallas as pl
from jax.experimental.pallas import tpu as pltpu

F32 = jnp.float32
BF16 = jnp.bfloat16

HEAD_DIM = 64
SB_WIDTH = 768
MEM_WIDTH = 256
POOL_WINDOWS = (2, 4, 8, 16)
POOL_GROUP = 192
POOL_HALO = 16
EPS = 1e-6
ATT_SCALE = HEAD_DIM ** -0.5
ADAM_LR, ADAM_B1, ADAM_B2, ADAM_EPS, ADAM_WD, ADAM_STEP = 0.001, 0.9, 0.999, 1e-08, 0.01, 10

LANES = 128
SB_TQ, SB_TK = 512, 512
VMEM_LIMIT = 56 * 1024 * 1024
MESH = pl.DeviceIdType.MESH
COPY_BYTES = 512 * 1024
ANY = pl.BlockSpec(memory_space=pl.ANY)


def _params(*sem):
    return pltpu.CompilerParams(dimension_semantics=sem, vmem_limit_bytes=VMEM_LIMIT)


def _tile(n, pref):
    if n <= pref:
        return n
    best = None
    for t in range(LANES, pref + 1, LANES):
        if n % t == 0:
            best = t
    assert best is not None, (n, pref)
    return best


def _row_tile(t, pref):
    if t <= pref:
        return t
    for tb in range(pref - pref % 16, 0, -16):
        if t % tb == 0:
            return tb
    raise ValueError((t, pref))


def _rowwise(name, fn, rows, vecs, row_outs, sum_outs=(), tb=512):
    norm_rows = []
    for r in rows:
        if isinstance(r, tuple):
            arr, (bc, cb) = r
        else:
            arr, (bc, cb) = r, (r.shape[1], 0)
        norm_rows.append((arr, bc, cb))
    t = norm_rows[0][0].shape[0]
    tb = _row_tile(t, tb)
    n_in, n_ro = len(norm_rows) + len(vecs), len(row_outs)

    def body(*refs):
        ins = [r[...] for r in refs[:n_in]]
        outs = fn(*ins)
        if not isinstance(outs, tuple):
            outs = (outs,)
        for o_ref, o in zip(refs[n_in:n_in + n_ro], outs[:n_ro]):
            o_ref[...] = o.astype(o_ref.dtype)
        for s_ref, s in zip(refs[n_in + n_ro:], outs[n_ro:]):
            @pl.when(pl.program_id(0) == 0)
            def _():
                s_ref[...] = jnp.zeros_like(s_ref)
            s_ref[...] += s

    in_specs = [pl.BlockSpec((tb, bc), functools.partial(lambda i, cb: (i, cb), cb=cb)) for _, bc, cb in norm_rows]
    in_specs += [pl.BlockSpec(v.shape, lambda i: (0, 0)) for v in vecs]
    out_specs = [pl.BlockSpec((tb, c), lambda i: (i, 0)) for c, _ in row_outs]
    out_specs += [pl.BlockSpec((1, c), lambda i: (0, 0)) for c in sum_outs]
    out_shape = [jax.ShapeDtypeStruct((t, c), d) for c, d in row_outs]
    out_shape += [jax.ShapeDtypeStruct((1, c), F32) for c in sum_outs]
    res = pl.pallas_call(
        body, name=name, grid=(t // tb,), in_specs=in_specs, out_specs=out_specs, out_shape=out_shape,
        compiler_params=_params("arbitrary"),
    )(*[a for a, _, _ in norm_rows], *vecs)
    return res


def _rms_fwd(name, x, g):
    def fn(xt, gt):
        rstd = lax.rsqrt(jnp.mean(xt * xt, axis=-1, keepdims=True) + EPS)
        return xt * rstd * gt
    return _rowwise(name, fn, [x], [g], [(x.shape[1], BF16)])[0]


def _rms_bwd(name, x, g, dh, dres=None, want_dx=True):
    has_res = dres is not None

    def fn(*a):
        if has_res:
            xt, dht, drt, gt = a
        else:
            xt, dht, gt = a
        rstd = lax.rsqrt(jnp.mean(xt * xt, axis=-1, keepdims=True) + EPS)
        xhat = xt * rstd
        dht = dht.astype(F32)
        dg = jnp.sum(dht * xhat, axis=0, keepdims=True)
        if not want_dx:
            return (dg,)
        dxhat = dht * gt
        dx = rstd * (dxhat - xhat * jnp.mean(dxhat * xhat, axis=-1, keepdims=True))
        if has_res:
            dx = dx + drt
        return dx, dx, dg

    d = x.shape[1]
    rows = [x, dh] + ([dres] if has_res else [])
    outs = [(d, F32), (d, BF16)] if want_dx else []
    return _rowwise(name, fn, rows, [g], outs, [d])


def _rms_fwd2(name, x, g1, g2):
    def fn(xt, g1t, g2t):
        xn = xt * lax.rsqrt(jnp.mean(xt * xt, axis=-1, keepdims=True) + EPS)
        return xn * g1t, xn * g2t
    return _rowwise(name, fn, [x], [g1, g2], [(x.shape[1], BF16)] * 2)


def _rms_bwd2(name, x, g1, dh1, g2, dh2, dres):
    def fn(xt, d1, d2, drt, g1t, g2t):
        rstd = lax.rsqrt(jnp.mean(xt * xt, axis=-1, keepdims=True) + EPS)
        xhat = xt * rstd
        d1, d2 = d1.astype(F32), d2.astype(F32)
        dxhat = d1 * g1t + d2 * g2t
        dx = rstd * (dxhat - xhat * jnp.mean(dxhat * xhat, axis=-1, keepdims=True)) + drt
        return dx, dx, jnp.sum(d1 * xhat, axis=0, keepdims=True), jnp.sum(d2 * xhat, axis=0, keepdims=True)

    d = x.shape[1]
    return _rowwise(name, fn, [x, dh1, dh2, dres], [g1, g2], [(d, F32), (d, BF16)], [d, d])


_DOT_DIMS = {"nn": ((1,), (0,)), "nt": ((1,), (1,)), "tn": ((0,), (0,))}


def _mm(name, a, b, mode, out_dtype, res=None, tm=512, tn=512):
    if mode == "nn":
        (m, k), (k2, n) = a.shape, b.shape
    elif mode == "nt":
        (m, k), (n, k2) = a.shape, b.shape
    else:
        (k, m), (k2, n) = a.shape, b.shape
    assert k == k2, (name, a.shape, b.shape)
    tm, tn = _tile(m, tm), _tile(n, tn)
    dims = (_DOT_DIMS[mode], ((), ()))
    has_res = res is not None

    def body(a_ref, b_ref, *rest):
        acc = lax.dot_general(a_ref[...], b_ref[...], dims, preferred_element_type=F32)
        if has_res:
            acc = acc + rest[0][...]
        rest[-1][...] = acc.astype(out_dtype)

    a_spec = pl.BlockSpec((k, tm), lambda i, j: (0, i)) if mode == "tn" else pl.BlockSpec((tm, k), lambda i, j: (i, 0))
    b_spec = pl.BlockSpec((tn, k), lambda i, j: (j, 0)) if mode == "nt" else pl.BlockSpec((k, tn), lambda i, j: (0, j))
    o_spec = pl.BlockSpec((tm, tn), lambda i, j: (i, j))
    in_specs, args = [a_spec, b_spec], [a, b]
    if has_res:
        in_specs.append(o_spec)
        args.append(res)
    return pl.pallas_call(
        body, name=name, grid=(m // tm, n // tn), in_specs=in_specs, out_specs=o_spec,
        out_shape=jax.ShapeDtypeStruct((m, n), out_dtype), compiler_params=_params("parallel", "arbitrary"),
    )(*args)


def _pool(name, u, reverse, tb=512):
    t = u.shape[0]
    tb = min(tb, t)
    nt = t // tb
    c = SB_WIDTH
    hpb = tb // POOL_HALO

    def body(cur_ref, halo_ref, o_ref):
        i = pl.program_id(0)
        cur = cur_ref[...].astype(F32)
        edge = (i == nt - 1) if reverse else (i == 0)
        halo = jnp.where(edge, 0.0, halo_ref[...].astype(F32))
        col = lax.broadcasted_iota(jnp.int32, (tb + POOL_HALO, c), 1)
        row = lax.broadcasted_iota(jnp.int32, (tb + POOL_HALO, c), 0)
        wcol = jnp.where(col < POOL_GROUP, 2, jnp.where(col < 2 * POOL_GROUP, 4, jnp.where(col < 3 * POOL_GROUP, 8, 16)))
        n = tb + POOL_HALO
        if reverse:
            ext = jnp.concatenate([cur, halo], axis=0)
            tpos = i * tb + row
            ext = ext / jnp.minimum(tpos + 1, wcol).astype(F32)
            shift = lambda a, k: pltpu.roll(a, n - k, 0)
        else:
            ext = jnp.concatenate([halo, cur], axis=0)
            shift = lambda a, k: pltpu.roll(a, k, 0)
        s2 = ext + shift(ext, 1)
        s4 = s2 + shift(s2, 2)
        s8 = s4 + shift(s4, 4)
        s16 = s8 + shift(s8, 8)
        win = jnp.where(wcol == 2, s2, jnp.where(wcol == 4, s4, jnp.where(wcol == 8, s8, s16)))
        if reverse:
            out = win[:tb] - cur
        else:
            tpos = i * tb + row[POOL_HALO:] - POOL_HALO
            out = win[POOL_HALO:] / jnp.minimum(tpos + 1, wcol[POOL_HALO:]).astype(F32) - cur
        o_ref[...] = out.astype(o_ref.dtype)

    if reverse:
        halo_map = lambda i: (jnp.minimum((i + 1) * hpb, t // POOL_HALO - 1), 0)
    else:
        halo_map = lambda i: (jnp.maximum(i * hpb - 1, 0), 0)
    return pl.pallas_call(
        body, name=name, grid=(nt,),
        in_specs=[pl.BlockSpec((tb, c), lambda i: (i, 0)), pl.BlockSpec((POOL_HALO, c), halo_map)],
        out_specs=pl.BlockSpec((tb, c), lambda i: (i, 0)),
        out_shape=jax.ShapeDtypeStruct((t, c), BF16), compiler_params=_params("arbitrary"),
    )(u, u)


def _head_masks(shape):
    lane = lax.broadcasted_iota(jnp.int32, shape, 1)
    return lane < HEAD_DIM, lane >= HEAD_DIM


def _pick(mask, a):
    return jnp.where(mask, a, jnp.zeros_like(a))


def _dot(a, b, mode):
    return lax.dot_general(a, b, (_DOT_DIMS[mode], ((), ())), preferred_element_type=F32)


def _dot_tri(a, tri, suffix):
    h = a.shape[1] // 2
    lo, hi = a[:, :h], a[:, h:]
    s_lo, s_hi = jnp.sum(lo, axis=1, keepdims=True), jnp.sum(hi, axis=1, keepdims=True)
    p_lo, p_hi = _dot(lo.astype(BF16), tri, "nn"), _dot(hi.astype(BF16), tri, "nn")
    if suffix:
        p_lo = p_lo + s_hi
    else:
        p_hi = p_hi + s_lo
    return jnp.concatenate([p_lo, p_hi], axis=1), s_lo + s_hi


def _log_gates(z):
    nz = -z
    l = jnp.log(1.0 + jnp.exp(jnp.minimum(z, nz)))
    ln = jnp.minimum(nz, 0.0) - l
    return ln, z + ln


def _sb_blocks(s, tq, tk):
    tq, tk = min(tq, s), min(tk, s)
    assert tq == tk and s % tk == 0 and tk % 64 == 0, (s, tq, tk)
    return tq, tk, tk // 2


def _strict_triangle(n, pred):
    return pred(lax.broadcasted_iota(jnp.int32, (n, n), 0), lax.broadcasted_iota(jnp.int32, (n, n), 1)).astype(BF16)


def _sb_fwd(proj, kv, late_shards, tq=SB_TQ, tk=SB_TK):
    s = proj.shape[0]
    tq, tk, th = _sb_blocks(s, tq, tk)
    npair = SB_WIDTH // LANES
    gather = _Gather({k: v.shape for k, v in late_shards.items()})
    ng = gather.n

    def body(q_ref, k_ref, v_ref, *rest):
        o_ref, tot_ref = rest[ng:ng + 2]
        comm = rest[:ng], rest[ng + 2:2 * ng + 2], rest[2 * ng + 2:2 * ng + 5], rest[2 * ng + 5:]

        @pl.when(pl.program_id(0) == 0)
        def _():
            gather.start(*comm)

        tri_gt = _strict_triangle(tk, lambda j, s_: j > s_)
        tri_gt_h = _strict_triangle(th, lambda j, s_: j > s_)
        seen = lax.broadcasted_iota(jnp.int32, (tq, th), 1) < lax.broadcasted_iota(jnp.int32, (tq, th), 0)
        m_a, m_b = _head_masks((tq, LANES))

        def block(qh, k2, v2, lane_mask, carry, acc, mask, tri):
            ln_full, lsz = _log_gates(_dot(qh, k2, "nt"))
            ln = ln_full if mask is None else jnp.where(mask, ln_full, 0.0)
            w = jnp.exp(lsz + _dot(ln.astype(BF16), tri, "nn"))
            if mask is not None:
                w = jnp.where(mask, w, 0.0)
            acc = acc + jnp.exp(carry) * _dot(w.astype(BF16), _pick(lane_mask[:v2.shape[0]], v2), "nn")
            return carry + jnp.sum(ln, axis=1, keepdims=True), acc

        def q_block(qi, _):
            q0 = pl.multiple_of(qi * tq, tq)
            q2 = q_ref[pl.ds(q0, tq), :] * ATT_SCALE
            qa, qb = _pick(m_a, q2), _pick(m_b, q2)

            def both(qa, qb, k0, size, ca, cb, acc, mask, tri):
                k2 = k_ref[pl.ds(k0, size), :]
                v2 = v_ref[pl.ds(k0, size), :]
                ca, acc = block(qa, k2, v2, m_a, ca, acc, mask, tri)
                cb, acc = block(qb, k2, v2, m_b, cb, acc, mask, tri)
                return ca, cb, acc

            zero_c = jnp.zeros((th, 1), F32)
            zero_o = jnp.zeros((th, LANES), F32)
            ca, cb, acc = both(qa[th:], qb[th:], pl.multiple_of(q0 + th, th), th, zero_c, zero_c, zero_o, seen[:th], tri_gt_h)
            ca, cb = jnp.concatenate([zero_c, ca], axis=0), jnp.concatenate([zero_c, cb], axis=0)
            carry = both(qa, qb, q0, th, ca, cb, jnp.concatenate([zero_o, acc], axis=0), seen, tri_gt_h)

            def k_block(step, carry):
                return both(qa, qb, pl.multiple_of(q0 - (step + 1) * tk, tk), tk, *carry, None, tri_gt)

            ca, cb, acc = lax.fori_loop(0, qi, k_block, carry)
            o_ref[pl.ds(q0, tq), :] = acc.astype(o_ref.dtype)
            tot_ref[0, pl.ds(q0, tq), :] = jnp.broadcast_to(ca, (tq, LANES))
            tot_ref[1, pl.ds(q0, tq), :] = jnp.broadcast_to(cb, (tq, LANES))
            return 0

        lax.fori_loop(0, s // tq, q_block, 0)

        @pl.when(pl.program_id(0) == npair - 1)
        def _():
            gather.finish(*comm)

    outs = pl.pallas_call(
        body, name="sb_fwd", grid=(npair,),
        in_specs=[pl.BlockSpec((s, LANES), lambda p: (0, p)), pl.BlockSpec((s, LANES), lambda p: (0, p)),
                  pl.BlockSpec((s, LANES), lambda p: (0, npair + p))] + [ANY] * ng,
        out_specs=[pl.BlockSpec((s, LANES), lambda p: (0, p)), pl.BlockSpec((None, 2, s, LANES), lambda p: (p, 0, 0, 0))]
        + [ANY] * ng,
        out_shape=[jax.ShapeDtypeStruct((s, SB_WIDTH), BF16), jax.ShapeDtypeStruct((npair, 2, s, LANES), F32)]
        + gather.out_shape(),
        scratch_shapes=gather.scratch(), compiler_params=_params("arbitrary"),
    )(proj, kv, kv, *[late_shards[k] for k in gather.names])
    return outs[0], outs[1], dict(zip(gather.names, outs[2:]))


def _sb_bwd(proj, kv, dcat, tot, early_partial, tq=SB_TQ, tk=SB_TK):
    s = proj.shape[0]
    tq, tk, th = _sb_blocks(s, tq, tk)
    npair = SB_WIDTH // LANES
    exchange = _ChipExchange(early_partial)
    ne = exchange.n

    def body(q_ref, k_ref, v_ref, do_ref, tot_ref, *rest):
        dq_ref, dk_ref, dv_ref = rest[ne:ne + 3]
        dk_acc, dv_acc = rest[2 * ne + 3:2 * ne + 5]
        comm = rest[:ne], rest[ne + 3:2 * ne + 3], rest[2 * ne + 5:]

        @pl.when(pl.program_id(0) == 0)
        def _():
            exchange.start(*comm)

        tris = (_strict_triangle(tk // 2, lambda j, s_: j > s_), _strict_triangle(tk // 2, lambda j, s_: j < s_))
        tris_h = (_strict_triangle(th // 2, lambda j, s_: j > s_), _strict_triangle(th // 2, lambda j, s_: j < s_))
        seen = lax.broadcasted_iota(jnp.int32, (tq, th), 1) < lax.broadcasted_iota(jnp.int32, (tq, th), 0)
        m_a, m_b = _head_masks((tq, LANES))
        dk_acc[...] = jnp.zeros_like(dk_acc)
        dv_acc[...] = jnp.zeros_like(dv_acc)

        def block(qh, doh, k2, v2, lane_mask, tot_h, carry, mask, tri):
            c_ln, c_d, dq = carry
            ln_full, lsz = _log_gates(_dot(qh, k2, "nt"))
            ln = ln_full if mask is None else jnp.where(mask, ln_full, 0.0)
            inside, total = _dot_tri(ln, tri[0], True)
            c_ln = c_ln + total
            w = jnp.exp(lsz + ((tot_h - c_ln) + inside))
            if mask is not None:
                w = jnp.where(mask, w, 0.0)
            dlw = _dot(doh, v2, "nt") * w
            before, d_total = _dot_tri(dlw, tri[1], False)
            dz = dlw * jnp.exp(ln_full) - (before + c_d) * jnp.exp(lsz)
            if mask is not None:
                dz = jnp.where(mask, dz, 0.0)
            dz = dz.astype(BF16)
            dq = dq + _dot(dz, _pick(lane_mask[:k2.shape[0]], k2), "nn")
            dk = _dot(dz, qh, "tn")
            dv = _dot(w.astype(BF16), doh, "tn")
            carry = (c_ln, c_d + d_total, dq)
            return carry, dk, dv

        def q_block(qi, _):
            q0 = pl.multiple_of(qi * tq, tq)
            q2 = q_ref[pl.ds(q0, tq), :] * ATT_SCALE
            do2 = do_ref[pl.ds(q0, tq), :]
            qa, qb = _pick(m_a, q2), _pick(m_b, q2)
            doa, dob = _pick(m_a, do2), _pick(m_b, do2)
            tot_a = tot_ref[0, pl.ds(q0, tq), 0:1]
            tot_b = tot_ref[1, pl.ds(q0, tq), 0:1]
            zero_c = jnp.zeros((tq, 1), F32)
            zero_q = jnp.zeros((tq, LANES), F32)

            def both(rows, k0, size, ca, cb, mask, tri):
                k2 = k_ref[pl.ds(k0, size), :]
                v2 = v_ref[pl.ds(k0, size), :]
                ca, dka, dva = block(qa[rows], doa[rows], k2, v2, m_a, tot_a[rows], ca, mask, tri)
                cb, dkb, dvb = block(qb[rows], dob[rows], k2, v2, m_b, tot_b[rows], cb, mask, tri)
                dk_acc[pl.ds(k0, size), :] += dka + dkb
                dv_acc[pl.ds(k0, size), :] += dva + dvb
                return ca, cb

            def k_block(kj, carry):
                return both(slice(None), pl.multiple_of(kj * tk, tk), tk, carry[0], carry[1], None, tris)

            init = ((zero_c, zero_c, zero_q), (zero_c, zero_c, zero_q))
            ca, cb = lax.fori_loop(0, qi, k_block, init)
            ca, cb = both(slice(None), q0, th, ca, cb, seen, tris_h)
            late = slice(th, tq)
            la, lb = both(late, pl.multiple_of(q0 + th, th), th, tuple(t[late] for t in ca), tuple(t[late] for t in cb),
                          seen[:th], tris_h)
            dq = jnp.concatenate([ca[2][:th] + cb[2][:th], la[2] + lb[2]], axis=0)
            dq_ref[pl.ds(q0, tq), :] = (dq * ATT_SCALE).astype(dq_ref.dtype)
            return 0

        lax.fori_loop(0, s // tq, q_block, 0)
        dk_ref[...] = dk_acc[...].astype(dk_ref.dtype)
        dv_ref[...] = dv_acc[...].astype(dv_ref.dtype)

        @pl.when(pl.program_id(0) == npair - 1)
        def _():
            exchange.finish(*comm)

    col = lambda off: pl.BlockSpec((s, LANES), functools.partial(lambda p, off: (0, off + p), off=off))
    outs = pl.pallas_call(
        body, name="sb_bwd", grid=(npair,),
        in_specs=[col(0), col(0), col(npair), col(0), pl.BlockSpec((None, 2, s, LANES), lambda p: (p, 0, 0, 0))]
        + [ANY] * ne,
        out_specs=[col(0), col(0), col(0)] + [ANY] * ne,
        out_shape=[jax.ShapeDtypeStruct((s, SB_WIDTH), BF16)] * 3 + exchange.out_shape(),
        scratch_shapes=[pltpu.VMEM((s, LANES), F32), pltpu.VMEM((s, LANES), F32)] + exchange.scratch(),
        compiler_params=_params("arbitrary"),
    )(proj, kv, kv, dcat, tot, *[early_partial[k] for k in exchange.names])
    dq, dk, dv = outs[:3]
    return dq, jnp.concatenate([dk, dv], axis=1), dict(zip(exchange.names, outs[3:]))


def _mem_fwd(name, proj, mkv, tq=512):
    s = proj.shape[0]
    tq = min(tq, s)
    qblk = SB_WIDTH // MEM_WIDTH

    def body(q_ref, kv_ref, o_ref):
        m_a, m_b = _head_masks((tq, LANES))
        mk_a, mk_b = _head_masks((kv_ref.shape[0], LANES))
        for p in range(MEM_WIDTH // LANES):
            q2 = q_ref[:, p * LANES:(p + 1) * LANES].astype(BF16)
            k2 = kv_ref[:, p * LANES:(p + 1) * LANES]
            v2 = kv_ref[:, MEM_WIDTH + p * LANES:MEM_WIDTH + (p + 1) * LANES]
            acc = jnp.zeros((tq, LANES), F32)
            for mq, mk in ((m_a, mk_a), (m_b, mk_b)):
                logits = _dot(_pick(mq, q2), k2, "nt") * ATT_SCALE
                e = jnp.exp(logits - jnp.max(logits, axis=-1, keepdims=True))
                prob = e / jnp.sum(e, axis=-1, keepdims=True)
                acc = acc + _dot(prob.astype(BF16), _pick(mk, v2), "nn")
            o_ref[:, p * LANES:(p + 1) * LANES] = acc.astype(o_ref.dtype)

    return pl.pallas_call(
        body, name=name, grid=(s // tq,),
        in_specs=[pl.BlockSpec((tq, MEM_WIDTH), lambda i: (i, qblk)), pl.BlockSpec(mkv.shape, lambda i: (0, 0))],
        out_specs=pl.BlockSpec((tq, MEM_WIDTH), lambda i: (i, 0)),
        out_shape=jax.ShapeDtypeStruct((s, MEM_WIDTH), BF16), compiler_params=_params("arbitrary"),
    )(proj, mkv)


def _mem_bwd(name, proj, mkv, dcat, tq=512):
    s = proj.shape[0]
    tq = min(tq, s)
    qblk = SB_WIDTH // MEM_WIDTH

    def body(q_ref, kv_ref, do_ref, dq_ref, dkv_ref):
        @pl.when(pl.program_id(0) == 0)
        def _():
            dkv_ref[...] = jnp.zeros_like(dkv_ref)

        m_a, m_b = _head_masks((tq, LANES))
        for p in range(MEM_WIDTH // LANES):
            ksl = slice(p * LANES, (p + 1) * LANES)
            vsl = slice(MEM_WIDTH + p * LANES, MEM_WIDTH + (p + 1) * LANES)
            q2, do2 = q_ref[:, ksl].astype(BF16), do_ref[:, ksl]
            k2, v2 = kv_ref[:, ksl], kv_ref[:, vsl]
            mk_a, mk_b = _head_masks(k2.shape)
            dq = jnp.zeros((tq, LANES), F32)
            dk = jnp.zeros(k2.shape, F32)
            dv = jnp.zeros(k2.shape, F32)
            for mq, mk in ((m_a, mk_a), (m_b, mk_b)):
                qh, doh = _pick(mq, q2), _pick(mq, do2)
                logits = _dot(qh, k2, "nt") * ATT_SCALE
                e = jnp.exp(logits - jnp.max(logits, axis=-1, keepdims=True))
                prob = e / jnp.sum(e, axis=-1, keepdims=True)
                dp = _dot(doh, v2, "nt")
                ds = prob * (dp - jnp.sum(dp * prob, axis=-1, keepdims=True)) * ATT_SCALE
                ds = ds.astype(BF16)
                dq = dq + _dot(ds, _pick(mk, k2), "nn")
                dk = dk + _dot(ds, qh, "tn")
                dv = dv + _dot(prob.astype(BF16), doh, "tn")
            dq_ref[:, ksl] = dq.astype(dq_ref.dtype)
            dkv_ref[:, ksl] += dk
            dkv_ref[:, vsl] += dv

    return pl.pallas_call(
        body, name=name, grid=(s // tq,),
        in_specs=[pl.BlockSpec((tq, MEM_WIDTH), lambda i: (i, qblk)), pl.BlockSpec(mkv.shape, lambda i: (0, 0)),
                  pl.BlockSpec((tq, MEM_WIDTH), lambda i: (i, qblk))],
        out_specs=[pl.BlockSpec((tq, MEM_WIDTH), lambda i: (i, 0)), pl.BlockSpec(mkv.shape, lambda i: (0, 0))],
        out_shape=[jax.ShapeDtypeStruct((s, MEM_WIDTH), BF16), jax.ShapeDtypeStruct(mkv.shape, F32)],
        compiler_params=_params("arbitrary"),
    )(proj, mkv, dcat)


def _gu_swiglu(name, h, w_gu, tn=256):
    t, d = h.shape
    f = w_gu.shape[1] // 2
    nb = f // tn

    def body(h_ref, wg_ref, wu_ref, g_ref, u_ref, a_ref):
        hh = h_ref[...]
        g = _dot(hh, wg_ref[...], "nn")
        u = _dot(hh, wu_ref[...], "nn")
        g_ref[...] = g.astype(BF16)
        u_ref[...] = u.astype(BF16)
        a_ref[...] = (g * jax.nn.sigmoid(g) * u).astype(BF16)

    out = pl.BlockSpec((t, tn), lambda j: (0, j))
    return pl.pallas_call(
        body, name=name, grid=(nb,),
        in_specs=[pl.BlockSpec((t, d), lambda j: (0, 0)), pl.BlockSpec((d, tn), lambda j: (0, j)),
                  pl.BlockSpec((d, tn), lambda j: (0, nb + j))],
        out_specs=[out, out, out], out_shape=[jax.ShapeDtypeStruct((t, f), BF16)] * 3,
        compiler_params=_params("arbitrary"),
    )(h, w_gu, w_gu)


def _down_dx_swiglu(name, dout_bf, w_down, gate, up, tm=256):
    t, d = dout_bf.shape
    f = w_down.shape[0]

    def body(do_ref, w_ref, g_ref, u_ref, o_ref):
        dact = _dot(do_ref[...], w_ref[...], "nt")
        g, u = g_ref[...].astype(F32), u_ref[...].astype(F32)
        sg = jax.nn.sigmoid(g)
        silu = g * sg
        o_ref[:, :f] = (dact * u * (sg + silu * (1.0 - sg))).astype(BF16)
        o_ref[:, f:] = (dact * silu).astype(BF16)

    row = lambda c: pl.BlockSpec((tm, c), lambda i: (i, 0))
    return pl.pallas_call(
        body, name=name, grid=(t // tm,),
        in_specs=[row(d), pl.BlockSpec((f, d), lambda i: (0, 0)), row(f), row(f)],
        out_specs=row(2 * f), out_shape=jax.ShapeDtypeStruct((t, 2 * f), BF16), compiler_params=_params("arbitrary"),
    )(dout_bf, w_down, gate, up)


def _ffn_fwd(tag, x, norm, w_gu, w_down):
    h = _rms_fwd(tag + "_ffn_norm", x, norm)
    gate, up, act = _gu_swiglu(tag + "_gu", h, w_gu)
    out = _mm(tag + "_down", act, w_down, "nn", F32, res=x, tn=1024)
    return out, (h, gate, up, act)


def _ffn_bwd(tag, x, norm, w_gu, w_down, saved, dout, dout_bf):
    h, gate, up, act = saved
    g_down = _mm(tag + "_down_dw", act, dout_bf, "tn", BF16, tm=256, tn=1024)
    dgu = _down_dx_swiglu(tag + "_down_dx", dout_bf, w_down, gate, up)
    g_gu = _mm(tag + "_gu_dw", h, dgu, "tn", BF16, tm=1024)
    dh = _mm(tag + "_gu_dx", dgu, w_gu, "nt", F32, tn=1024)
    dx, dx_bf, g_norm = _rms_bwd(tag + "_ffn_norm_bwd", x, norm, dh, dres=dout)
    return dx, dx_bf, g_gu, g_down, g_norm


def _mem_kv(tag, mem_n, w_mem_kv):
    return _mm(tag + "_memkv", mem_n, w_mem_kv, "nn", BF16)


def _mem_kv_bwd(tag, mem, mem_norm, mem_n, w_mem_kv, dmkv):
    dmkv = dmkv.astype(BF16)
    g_w = _mm(tag + "_memkv_dw", mem_n, dmkv, "tn", BF16)
    dmem_n = _mm(tag + "_memkv_dx", dmkv, w_mem_kv, "nt", F32)
    (g_norm,) = _rms_bwd(tag + "_memnorm_bwd", mem, mem_norm, dmem_n, want_dx=False)
    return g_w, g_norm


def _block_diag(w_group):
    z = jnp.zeros((POOL_GROUP, POOL_GROUP), w_group.dtype)
    return jnp.concatenate(
        [jnp.concatenate([w_group[g] if h == g else z for h in range(4)], axis=1) for g in range(4)], axis=0)


def _local_step(x, mem, target, w, late_shards):
    g = {}
    w = dict(w)
    mem_n = _rms_fwd("mem_norm", mem, w["mem_norm"])
    h_a = _rms_fwd("a_mix_norm", x, w["a_norm_mix"])
    proj_a = _mm("a_in", h_a, w["a_w_in"], "nn", F32, tn=1024)
    pooled = _pool("a_pool", proj_a, reverse=False)
    w_bd = _block_diag(w["a_w_group"])
    g_pre = _mm("a_group", pooled, w_bd, "nn", BF16, tn=768)
    mkv_a = _mem_kv("a", mem_n, w["a_w_mem_kv"])
    mem_a = _mem_fwd("a_mem_attn", proj_a, mkv_a)
    cat_a = _rowwise("a_cat", lambda gp, mo, sc: jnp.concatenate([gp.astype(F32) * sc, mo.astype(F32)], axis=1),
                     [g_pre, mem_a], [w["a_scale"]], [(1024, BF16)])[0]
    x1 = _mm("a_out", cat_a, w["a_w_out"], "nn", F32, res=x, tn=1024)
    x2, ffn_a = _ffn_fwd("a", x1, w["a_norm_ffn"], w["a_w_gu"], w["a_w_down"])
    h_k, h_b = _rms_fwd2("x2_norms", x2, w["kv_norm"], w["b_norm_mix"])
    kv = _mm("kv_proj", h_k, w["w_kv"], "nn", BF16, tn=1536)
    proj_b = _mm("b_q", h_b, w["b_w_q"], "nn", BF16, tn=1024)
    sb_out, tot, late = _sb_fwd(proj_b, kv, late_shards)
    w.update(late)
    mkv_b = _mem_kv("b", mem_n, w["b_w_mem_kv"])
    mem_b = _mem_fwd("b_mem_attn", proj_b, mkv_b)
    cat_b = jnp.concatenate([sb_out, mem_b], axis=1)
    x3 = _mm("b_out", cat_b, w["b_w_out"], "nn", F32, res=x2, tn=1024)
    x4, ffn_b = _ffn_fwd("b", x3, w["b_norm_ffn"], w["b_w_gu"], w["b_w_down"])

    d = x.shape[1]

    def head(xt, tt, gt):
        rstd = lax.rsqrt(jnp.mean(xt * xt, axis=-1, keepdims=True) + EPS)
        xhat = xt * rstd
        err = xhat * gt - tt
        loss = 0.5 * jnp.sum(jnp.sum(err * err, axis=1, keepdims=True), axis=0, keepdims=True) / d
        dy = err / d
        dxhat = dy * gt
        dx = rstd * (dxhat - xhat * jnp.mean(dxhat * xhat, axis=-1, keepdims=True))
        return dx, dx, jnp.sum(dy * xhat, axis=0, keepdims=True), jnp.broadcast_to(loss, (1, LANES))

    dx4, dx4_bf, g["final_norm"], loss = _rowwise(
        "loss_head", head, [x4, target], [w["final_norm"]], [(d, F32), (d, BF16)], [d, LANES])

    dx3, dx3_bf, g["b_w_gu"], g["b_w_down"], g["b_norm_ffn"] = _ffn_bwd(
        "b", x3, w["b_norm_ffn"], w["b_w_gu"], w["b_w_down"], ffn_b, dx4, dx4_bf)
    dcat_b = _mm("b_out_dx", dx3_bf, w["b_w_out"], "nt", BF16, tn=1024)
    g["b_w_out"] = _mm("b_out_dw", cat_b, dx3_bf, "tn", BF16, tm=1024, tn=1024)
    early_partial = _pair_sums("early", {k: g.pop(k) for k in EARLY})
    dq_sb, dkv, early_got = _sb_bwd(proj_b, kv, dcat_b, tot, early_partial)
    dq_mem_b, dmkv_b = _mem_bwd("b_mem_attn_bwd", proj_b, mkv_b, dcat_b)
    dproj_b = jnp.concatenate([dq_sb, dq_mem_b], axis=1)
    g["b_w_q"] = _mm("b_q_dw", h_b, dproj_b, "tn", BF16, tm=1024, tn=1024)
    dh_b = _mm("b_q_dx", dproj_b, w["b_w_q"], "nt", F32, tn=1024)
    g["b_w_mem_kv"], g_memnorm_b = _mem_kv_bwd("b", mem, w["mem_norm"], mem_n, w["b_w_mem_kv"], dmkv_b)
    g["w_kv"] = _mm("kv_proj_dw", h_k, dkv, "tn", BF16, tm=1024)
    dh_k = _mm("kv_proj_dx", dkv, w["w_kv"], "nt", F32, tn=1024)
    dx2, dx2_bf, g["kv_norm"], g["b_norm_mix"] = _rms_bwd2(
        "x2_norms_bwd", x2, w["kv_norm"], dh_k, w["b_norm_mix"], dh_b, dx3)

    dx1, dx1_bf, g["a_w_gu"], g["a_w_down"], g["a_norm_ffn"] = _ffn_bwd(
        "a", x1, w["a_norm_ffn"], w["a_w_gu"], w["a_w_down"], ffn_a, dx2, dx2_bf)
    mid_partial = _pair_sums("mid", {k: g.pop(k) for k in MID})
    mid_flight = _chip_exchange_start(mid_partial)
    scale_a = w["a_scale"] + mid_flight[-1][0:1, 0:1]
    dcat_a = _mm("a_out_dx", dx1_bf, w["a_w_out"], "nt", BF16, tn=1024)
    g["a_w_out"] = _mm("a_out_dw", cat_a, dx1_bf, "tn", BF16, tm=1024, tn=1024)

    def scale_bwd(dc, gp, sc):
        dc, gp = dc.astype(F32), gp.astype(F32)
        return dc * sc, jnp.sum(dc * gp, axis=0, keepdims=True)

    dg_pre, g["a_scale"] = _rowwise("a_scale_bwd", scale_bwd, [(dcat_a, (SB_WIDTH, 0)), g_pre], [scale_a],
                                    [(SB_WIDTH, BF16)], [SB_WIDTH])
    g_bd = _mm("a_group_dw", pooled, dg_pre, "tn", F32, tm=768, tn=768)
    g["a_w_group"] = jnp.stack([g_bd[i * POOL_GROUP:(i + 1) * POOL_GROUP, i * POOL_GROUP:(i + 1) * POOL_GROUP]
                                for i in range(4)])
    dpooled = _mm("a_group_dx", dg_pre, w_bd, "nt", F32, tn=768)
    du_pool = _pool("a_pool_bwd", dpooled, reverse=True)
    dq_mem_a, dmkv_a = _mem_bwd("a_mem_attn_bwd", proj_a, mkv_a, dcat_a)
    dproj_a = jnp.concatenate([du_pool, dq_mem_a], axis=1)
    g["a_w_in"] = _mm("a_in_dw", h_a, dproj_a, "tn", BF16, tm=1024, tn=1024)
    dh_a = _mm("a_in_dx", dproj_a, w["a_w_in"], "nt", F32, tn=1024)
    grad_x, _, g["a_norm_mix"] = _rms_bwd("a_mix_norm_bwd", x, w["a_norm_mix"], dh_a, dres=dx1)
    g["a_w_mem_kv"], g_memnorm_a = _mem_kv_bwd("a", mem, w["mem_norm"], mem_n, w["a_w_mem_kv"], dmkv_a)
    g["mem_norm"] = g_memnorm_a + g_memnorm_b
    mid_partial, mid_got = _chip_exchange_wait(mid_partial, mid_flight, g["a_norm_mix"])
    return loss, grad_x, g, {**early_partial, **mid_partial}, {**early_got, **mid_got}


ROW_SHARDED = ("a_w_in", "a_w_mem_kv", "a_w_out", "a_w_down", "b_w_q", "b_w_mem_kv", "b_w_out", "b_w_down")
COL_SHARDED = ("a_w_gu", "w_kv", "b_w_gu")
BIG = ("a_w_in", "a_w_mem_kv", "a_w_out", "a_w_gu", "a_w_down", "w_kv", "b_w_q", "b_w_mem_kv", "b_w_out", "b_w_gu",
       "b_w_down")
LATE = ("b_w_mem_kv", "b_w_out", "b_w_gu", "b_w_down")
EARLY = ("b_w_gu", "b_w_down", "b_w_out")
MID = ("a_w_gu", "a_w_down")
N_CHIPS = 4
N_DEV = 8


def _position():
    x, y, c = lax.axis_index("x"), lax.axis_index("y"), lax.axis_index("c")
    other_chips = [(1 - x, y), (x, 1 - y), (1 - x, 1 - y)]
    return x, y, c, other_chips


def _remote(src, dst, send_sem, recv_sem, device):
    return pltpu.make_async_remote_copy(src_ref=src, dst_ref=dst, send_sem=send_sem, recv_sem=recv_sem,
                                        device_id=device, device_id_type=MESH)


def _comm_call(name, body, args, out_shape, n_remote, aliases=None):
    return pl.pallas_call(
        body, name=name, in_specs=[ANY] * len(args), out_specs=[ANY] * len(out_shape), out_shape=out_shape,
        scratch_shapes=[pltpu.SemaphoreType.DMA((n_remote,)), pltpu.SemaphoreType.DMA((n_remote,))],
        input_output_aliases=aliases or {},
    )(*args)


def _row_chunks(nrows, row_bytes, mult=16):
    assert nrows % mult == 0, (nrows, mult)
    per = max(mult, (COPY_BYTES // row_bytes) // mult * mult)
    return [(r0, min(per, nrows - r0)) for r0 in range(0, nrows, per)]


def _rows(ref, start, size, lead=()):
    if isinstance(start, int):
        return ref.at[(*lead, pl.ds(start, size))]
    return ref.at[(*lead, pl.ds(pl.multiple_of(start, 16), size))]


class _Copies:
    def __init__(self, send, recv):
        self.send, self.recv = send, recv
        self.n_remote = 0
        self.remotes = []

    def slot(self):
        self.n_remote += 1
        return self.n_remote - 1

    def remote(self, k, src, dst, device):
        cp = _remote(src, dst, self.send.at[k], self.recv.at[k], device)
        cp.start()
        self.remotes.append(cp)
        return cp

    def finish(self):
        for cp in self.remotes:
            cp.wait_send()


def _shard_cols(ref, row_sharded, cdim, chip):
    if row_sharded:
        return ref
    return ref.at[:, pl.ds(pl.multiple_of(chip * cdim, LANES), cdim)]


class _Gather:
    def __init__(self, shapes, mult=16):
        self.names = list(shapes)
        self.shapes = [tuple(shapes[k]) for k in self.names]
        self.row_sharded = [k in ROW_SHARDED for k in self.names]
        self.chunks = [(i, r0, size) for i, (r, cdim) in enumerate(self.shapes)
                       for r0, size in _row_chunks(r // 2, cdim * 2, mult)]
        self.n = len(self.names)

    def out_shape(self):
        return [jax.ShapeDtypeStruct((N_CHIPS * r, cdim) if rs else (r, N_CHIPS * cdim), BF16)
                for (r, cdim), rs in zip(self.shapes, self.row_sharded)]

    def scratch(self):
        n_remote, n_local = 6 * len(self.chunks), self.n + 2 * len(self.chunks)
        return [pltpu.SemaphoreType.DMA((n_remote,)), pltpu.SemaphoreType.DMA((n_remote,)),
                pltpu.SemaphoreType.DMA((n_local,))] + [pltpu.VMEM(s, BF16) for s in self.shapes]

    def _window(self, dst, i, chip, half, r0, size):
        r, cdim = self.shapes[i]
        base = (chip * r if self.row_sharded[i] else 0) + half * (r // 2) + r0
        return _rows(_shard_cols(dst[i], self.row_sharded[i], cdim, chip), base, size)

    def _mine(self, refs, i, half, r0, size):
        return _rows(refs[i], half * (self.shapes[i][0] // 2) + r0, size)

    def _sent(self, src, dst, sems, q, k, px, py, c, me):
        i, r0, size = self.chunks[q]
        return _remote(self._mine(src, i, c, r0, size), self._window(dst, i, me, c, r0, size),
                       sems[0].at[6 * q + k], sems[1].at[6 * q + k], (px, py, c))

    def start(self, src, dst, sems, vm):
        x, y, c, chips = _position()
        for q in range(len(self.chunks)):
            for k, (px, py) in enumerate(chips):
                self._sent(src, dst, sems, q, k, px, py, c, 2 * x + y).start()
        for i in range(self.n):
            pltpu.make_async_copy(src[i], vm[i], sems[2].at[i]).start()

    def finish(self, src, dst, sems, vm):
        x, y, c, chips = _position()
        me = 2 * x + y
        send, recv, loc = sems
        for i in range(self.n):
            pltpu.make_async_copy(src[i], vm[i], loc.at[i]).wait()
        placed, forwarded = [], []
        for q, (i, r0, size) in enumerate(self.chunks):
            for half in range(2):
                cp = pltpu.make_async_copy(self._mine(vm, i, half, r0, size), self._window(dst, i, me, half, r0, size),
                                           loc.at[self.n + 2 * q + half])
                cp.start()
                placed.append(cp)
        for q, (i, r0, size) in enumerate(self.chunks):
            for k, (px, py) in enumerate(chips):
                landed = self._window(dst, i, 2 * px + py, c, r0, size)
                _remote(landed, landed, send.at[6 * q + k], recv.at[6 * q + k], (px, py, c)).wait_recv()
                cp = _remote(landed, landed, send.at[6 * q + 3 + k], recv.at[6 * q + 3 + k], (x, y, 1 - c))
                cp.start()
                forwarded.append(cp)
        for q, (i, r0, size) in enumerate(self.chunks):
            for k, (px, py) in enumerate(chips):
                landed = self._window(dst, i, 2 * px + py, 1 - c, r0, size)
                _remote(landed, landed, send.at[6 * q + 3 + k], recv.at[6 * q + 3 + k], (x, y, 1 - c)).wait_recv()
        for q in range(len(self.chunks)):
            for k, (px, py) in enumerate(chips):
                self._sent(src, dst, sems, q, k, px, py, c, me).wait_send()
        for cp in forwarded:
            cp.wait_send()
        for cp in placed:
            cp.wait()


class _GatherRelay(_Gather):
    def __init__(self, shapes):
        super().__init__(shapes, mult=32)

    def scratch(self):
        n_remote, n_local = 8 * len(self.chunks), self.n + 2 * len(self.chunks)
        return [pltpu.SemaphoreType.DMA((n_remote,)), pltpu.SemaphoreType.DMA((n_remote,)),
                pltpu.SemaphoreType.DMA((n_local,))] + [pltpu.VMEM(s, BF16) for s in self.shapes]

    def run(self, src, dst, sems, vm):
        x, y, c, chips = _position()
        me, diag, sibling = 2 * x + y, 2 * (1 - x) + (1 - y), (x, y, 1 - c)
        nbrs = chips[:2]
        send, recv, loc = sems

        def copy(slot, ref, device):
            return _remote(ref, ref, send.at[slot], recv.at[slot], device)

        first = []
        for q, (i, r0, size) in enumerate(self.chunks):
            for k, (px, py) in enumerate(nbrs):
                cp = _remote(self._mine(src, i, c, r0, size), self._window(dst, i, me, c, r0, size),
                             send.at[8 * q + k], recv.at[8 * q + k], (px, py, c))
                cp.start()
                first.append(cp)
        for i in range(self.n):
            pltpu.make_async_copy(src[i], vm[i], loc.at[i]).start()
        for i in range(self.n):
            pltpu.make_async_copy(src[i], vm[i], loc.at[i]).wait()
        placed, passed = [], []
        for q, (i, r0, size) in enumerate(self.chunks):
            for half in range(2):
                cp = pltpu.make_async_copy(self._mine(vm, i, half, r0, size), self._window(dst, i, me, half, r0, size),
                                           loc.at[self.n + 2 * q + half])
                cp.start()
                placed.append(cp)
        for q, (i, r0, size) in enumerate(self.chunks):
            for k, (px, py) in enumerate(nbrs):
                landed = self._window(dst, i, 2 * px + py, c, r0, size)
                copy(8 * q + k, landed, (px, py, c)).wait_recv()
                piece = self._window(dst, i, 2 * px + py, c, r0 + k * (size // 2), size // 2)
                ox, oy = nbrs[1 - k]
                for cp in (copy(8 * q + 2 + k, piece, (ox, oy, c)), copy(8 * q + 4 + k, landed, sibling)):
                    cp.start()
                    passed.append(cp)
        for q, (i, r0, size) in enumerate(self.chunks):
            for k in range(2):
                piece = self._window(dst, i, diag, c, r0 + k * (size // 2), size // 2)
                ox, oy = nbrs[1 - k]
                copy(8 * q + 2 + k, piece, (ox, oy, c)).wait_recv()
                cp = copy(8 * q + 6 + k, piece, sibling)
                cp.start()
                passed.append(cp)
        for q, (i, r0, size) in enumerate(self.chunks):
            for k, (px, py) in enumerate(nbrs):
                copy(8 * q + 4 + k, self._window(dst, i, 2 * px + py, 1 - c, r0, size), sibling).wait_recv()
                copy(8 * q + 6 + k, self._window(dst, i, diag, 1 - c, r0 + k * (size // 2), size // 2), sibling).wait_recv()
        for cp in first + passed:
            cp.wait_send()
        for cp in placed:
            cp.wait()


def _gather_weights(shards):
    plan = _GatherRelay({k: v.shape for k, v in shards.items()})
    n = plan.n

    def body(*refs):
        plan.run(refs[:n], refs[n:2 * n], refs[2 * n:2 * n + 3], refs[2 * n + 3:])

    outs = pl.pallas_call(
        body, name="gather_weights", in_specs=[ANY] * n, out_specs=[ANY] * n, out_shape=plan.out_shape(),
        scratch_shapes=plan.scratch(), compiler_params=pltpu.CompilerParams(vmem_limit_bytes=VMEM_LIMIT),
    )(*[shards[k] for k in plan.names])
    return dict(zip(plan.names, outs))


def _scalar_grid_call(name, body, scalars, grid, in_specs, out_specs, out_shape, args):
    return pl.pallas_call(
        body, name=name, out_shape=out_shape,
        grid_spec=pltpu.PrefetchScalarGridSpec(num_scalar_prefetch=1, grid=grid, in_specs=in_specs, out_specs=out_specs),
        compiler_params=_params(*["arbitrary"] * len(grid)),
    )(scalars, *args)


def _pair_sum(name, g4, sib, where, tb=256):
    j, _, r, w = g4.shape
    tb = _row_tile(r, tb)

    def body(s_ref, g_ref, b_ref, o_ref):
        o_ref[...] = (g_ref[...].astype(F32) + b_ref[...].astype(F32)).astype(o_ref.dtype)

    blk = pl.BlockSpec((None, tb, w), lambda a, i, s: (a, i, 0))
    return _scalar_grid_call(
        name, body, where, (j, r // tb),
        [pl.BlockSpec((None, None, tb, w), lambda a, i, s: (a, s[0], i, 0)), blk], blk,
        jax.ShapeDtypeStruct((j, r, w), BF16), (g4, sib))


def _chip_sum(name, partial, got, where, row_sharded, tb=256):
    _, r, cdim = got.shape
    tb = _row_tile(r, tb)

    def body(s_ref, p_ref, g_ref, o_ref):
        acc = p_ref[...].astype(F32)
        for k in range(N_CHIPS - 1):
            acc = acc + g_ref[k].astype(F32)
        o_ref[...] = acc

    if row_sharded:
        own = pl.BlockSpec((None, tb, cdim), lambda i, s: (s[1], i, 0))
    else:
        own = pl.BlockSpec((None, tb, cdim), lambda i, s: (0, i, s[1]))
    return _scalar_grid_call(
        name, body, where, (r // tb,),
        [own, pl.BlockSpec((N_CHIPS - 1, tb, cdim), lambda i, s: (0, i, 0))],
        pl.BlockSpec((None, tb, cdim), lambda i, s: (s[0], i, 0)),
        jax.ShapeDtypeStruct((2, r, cdim), F32), (partial, got))


def _where():
    return jnp.stack([lax.axis_index("c"), 2 * lax.axis_index("x") + lax.axis_index("y")]).astype(jnp.int32)


def _grad_halves(name, g):
    rows, cols = g.shape
    if name in ROW_SHARDED:
        r = rows // N_CHIPS
        return g.reshape(N_CHIPS, 2, r // 2, cols), (r // 2, cols)
    return g.reshape(1, 2, rows // 2, cols), (rows // 2, cols // N_CHIPS)


def _pair_sums(tag, grads):
    names = list(grads)
    n = len(names)
    g4 = [_grad_halves(k, grads[k])[0] for k in names]
    plan = [[(j, r0, size) for j in range(g.shape[0]) for r0, size in _row_chunks(g.shape[2], g.shape[3] * 2)]
            for g in g4]
    out_shape = [jax.ShapeDtypeStruct((g.shape[0],) + g.shape[2:], BF16) for g in g4]

    def body(*refs):
        src, sib = refs[:n], refs[n:2 * n]
        cps = _Copies(*refs[2 * n:])
        x, y, c, _ = _position()
        waits = []
        for i in range(n):
            for j, r0, size in plan[i]:
                waits.append(cps.remote(cps.slot(), _rows(src[i], r0, size, lead=(j, 1 - c)),
                                        _rows(sib[i], r0, size, lead=(j,)), (x, y, 1 - c)))
        for cp in waits:
            cp.wait_recv()
        cps.finish()

    sibs = _comm_call("grads_pair_exchange_" + tag, body, g4, out_shape, sum(len(p) for p in plan))
    where = _where()
    return {k: _pair_sum(k + "_pair_sum", g, s, where) for k, g, s in zip(names, g4, sibs)}


class _ChipExchange:
    def __init__(self, partial, mult=16):
        self.names = list(partial)
        self.n = len(self.names)
        self.row_sharded = [k in ROW_SHARDED for k in self.names]
        self.half = []
        for k, rs in zip(self.names, self.row_sharded):
            _, r, w = partial[k].shape
            self.half.append((r, w) if rs else (r, w // N_CHIPS))
        self.chunks = [(i, r0, size) for i, (r, cdim) in enumerate(self.half)
                       for r0, size in _row_chunks(r, cdim * 2, mult)]

    def shard(self, src, i, chip, r0, size):
        if self.row_sharded[i]:
            return _rows(src[i], r0, size, lead=(chip,))
        return _rows(_shard_cols(src[i].at[0], False, self.half[i][1], chip), r0, size)

    def out_shape(self):
        return [jax.ShapeDtypeStruct((N_CHIPS - 1,) + s, BF16) for s in self.half]

    def scratch(self):
        n_remote = 3 * len(self.chunks)
        return [pltpu.SemaphoreType.DMA((n_remote,)), pltpu.SemaphoreType.DMA((n_remote,))]

    def _copies(self, src, dst, sems):
        x, y, c, chips = _position()
        for q, (i, r0, size) in enumerate(self.chunks):
            for k, (px, py) in enumerate(chips):
                yield _remote(self.shard(src, i, 2 * px + py, r0, size), _rows(dst[i], r0, size, lead=(k,)),
                              sems[0].at[3 * q + k], sems[1].at[3 * q + k], (px, py, c))

    def start(self, src, dst, sems):
        for cp in self._copies(src, dst, sems):
            cp.start()

    def finish(self, src, dst, sems):
        for cp in self._copies(src, dst, sems):
            cp.wait_recv()
        for cp in self._copies(src, dst, sems):
            cp.wait_send()


def _chip_exchange(partial):
    plan = _ChipExchange(partial)
    n = plan.n

    def body(*refs):
        parts = refs[:n], refs[n:2 * n], refs[2 * n:]
        plan.start(*parts)
        plan.finish(*parts)

    got = pl.pallas_call(
        body, name="grads_chip_exchange", in_specs=[ANY] * n, out_specs=[ANY] * n, out_shape=plan.out_shape(),
        scratch_shapes=plan.scratch(),
    )(*[partial[k] for k in plan.names])
    return dict(zip(plan.names, got))


HBM_SPEC = pl.BlockSpec(memory_space=pltpu.HBM)
SEM_SPEC = pl.BlockSpec(memory_space=pltpu.SEMAPHORE)
SIDE_EFFECT = pltpu.SideEffectType.DATAFLOW_SIDE_EFFECTING


def _chip_exchange_start(partial):
    plan = _ChipExchange(partial)
    n = plan.n
    n_remote = 3 * len(plan.chunks)
    srcs = [pltpu.with_memory_space_constraint(partial[k], pltpu.HBM) for k in plan.names]
    lands = [pltpu.with_memory_space_constraint(lax.empty(s.shape, s.dtype), pltpu.HBM) for s in plan.out_shape()]

    def body(*refs):
        src, land = refs[:n], refs[n:2 * n]
        plan.start(src, land, (refs[2 * n], refs[2 * n + 1]))
        refs[-1][...] = jnp.zeros_like(refs[-1])

    return pl.pallas_call(
        body, name="grads_chip_exchange_start",
        out_shape=(pltpu.SemaphoreType.DMA((n_remote,)), pltpu.SemaphoreType.DMA((n_remote,)),
                   *[pltpu.HBM(a.shape, a.dtype) for a in srcs + lands], jax.ShapeDtypeStruct((8, LANES), F32)),
        in_specs=[HBM_SPEC] * (2 * n),
        out_specs=(SEM_SPEC, SEM_SPEC, *[HBM_SPEC] * (2 * n), pl.BlockSpec(memory_space=pltpu.VMEM)),
        input_output_aliases={i: 2 + i for i in range(2 * n)},
        compiler_params=pltpu.CompilerParams(has_side_effects=SIDE_EFFECT),
    )(*srcs, *lands)


def _chip_exchange_wait(partial, flight, after):
    plan = _ChipExchange(partial)
    n = plan.n
    send, recv, thru = flight[0], flight[1], flight[2:2 + 2 * n]

    def body(*refs):
        plan.finish(refs[:n], refs[n:2 * n], (refs[2 * n], refs[2 * n + 1]))

    outs = pl.pallas_call(
        body, name="grads_chip_exchange_wait",
        out_shape=tuple(pltpu.HBM(t.shape, t.dtype) for t in thru),
        in_specs=[HBM_SPEC] * (2 * n) + [SEM_SPEC, SEM_SPEC, ANY], out_specs=[HBM_SPEC] * (2 * n),
        input_output_aliases={i: i for i in range(2 * n)},
        compiler_params=pltpu.CompilerParams(has_side_effects=SIDE_EFFECT),
    )(*thru, send, recv, after)
    return dict(zip(plan.names, outs[:n])), dict(zip(plan.names, outs[n:]))


def _finish_reduce(partial, got):
    names = list(partial)
    n = len(names)
    where = _where()
    halves = [_chip_sum(k + "_chip_sum", partial[k], got[k], where, k in ROW_SHARDED) for k in names]
    plan = [_row_chunks(h.shape[1], h.shape[2] * 4) for h in halves]
    out_shape = [jax.ShapeDtypeStruct(h.shape, F32) for h in halves]

    def body(*refs):
        src, dst = refs[:n], refs[n:2 * n]
        cps = _Copies(*refs[2 * n:])
        x, y, c, _ = _position()
        waits = []
        for i in range(n):
            for r0, size in plan[i]:
                waits.append(cps.remote(cps.slot(), _rows(src[i], r0, size, lead=(c,)), _rows(dst[i], r0, size, lead=(c,)),
                                        (x, y, 1 - c)))
        for cp in waits:
            cp.wait_recv()
        cps.finish()

    outs = _comm_call("grads_pair_share", body, halves, out_shape, sum(len(p) for p in plan),
                      aliases={i: i for i in range(n)})
    return {k: o.reshape(2 * o.shape[1], o.shape[2]) for k, o in zip(names, outs)}


def _all_reduce_small(name, v):
    rows, cols = v.shape
    h = rows // 2

    def body(v_ref, o_ref, sib, pair, buf, send, recv):
        x, y, c, chips = _position()
        me = 2 * x + y
        sibling = (x, y, 1 - c)
        cp = _remote(v_ref, sib, send.at[0], recv.at[0], sibling)
        cp.start()
        cp.wait()
        pair[...] = v_ref[...] + sib[...]
        mine = pl.ds(pl.multiple_of(c * h, 8), h)
        buf[me] = pair[mine, :]
        sends = [_remote(pair.at[mine], buf.at[me], send.at[1 + k], recv.at[1 + k], (px, py, c))
                 for k, (px, py) in enumerate(chips)]
        for cp in sends:
            cp.start()
        for k, (px, py) in enumerate(chips):
            _remote(pair.at[mine], buf.at[2 * px + py], send.at[1 + k], recv.at[1 + k], (px, py, c)).wait_recv()
        for cp in sends:
            cp.wait_send()
        o_ref[mine, :] = (buf[0] + buf[1]) + (buf[2] + buf[3])
        cp = _remote(o_ref.at[mine], o_ref.at[mine], send.at[4], recv.at[4], sibling)
        cp.start()
        cp.wait()

    vm = pl.BlockSpec(memory_space=pltpu.VMEM)
    return pl.pallas_call(
        body, name=name, in_specs=[vm], out_specs=vm, out_shape=jax.ShapeDtypeStruct(v.shape, F32),
        scratch_shapes=[pltpu.VMEM((rows, cols), F32), pltpu.VMEM((rows, cols), F32), pltpu.VMEM((N_CHIPS, h, cols), F32),
                        pltpu.SemaphoreType.DMA((5,)), pltpu.SemaphoreType.DMA((5,))],
        compiler_params=pltpu.CompilerParams(vmem_limit_bytes=VMEM_LIMIT),
    )(v)


def _adamw(name, w, g, m, v):
    def fn(wt, gt, mt, vt):
        mt = ADAM_B1 * mt + (1.0 - ADAM_B1) * gt
        vt = ADAM_B2 * vt + (1.0 - ADAM_B2) * (gt * gt)
        m_hat = mt / (1.0 - ADAM_B1 ** ADAM_STEP)
        v_hat = vt / (1.0 - ADAM_B2 ** ADAM_STEP)
        delta = -ADAM_LR * (m_hat / (jnp.sqrt(v_hat) + ADAM_EPS) + ADAM_WD * wt)
        return delta, mt, vt
    n = w.shape[1]
    return _rowwise(name, fn, [w, g, m, v], [], [(n, F32)] * 3, tb=256)


WEIGHTS = ("mem_norm", "a_norm_mix", "a_w_in", "a_w_group", "a_scale", "a_w_mem_kv", "a_w_out", "a_norm_ffn", "a_w_gu",
           "a_w_down", "kv_norm", "w_kv", "b_norm_mix", "b_w_q", "b_w_mem_kv", "b_w_out", "b_norm_ffn", "b_w_gu",
           "b_w_down", "final_norm")
REPLICATED_VECS = ("mem_norm", "kv_norm", "b_norm_mix", "b_norm_ffn", "final_norm")
SHARDED_VECS = ("a_norm_mix", "a_norm_ffn", "a_scale")
D_MODEL = 1024
GROUP_ROWS = 4 * POOL_GROUP * POOL_GROUP // D_MODEL


def _row(v):
    v = v.reshape(1, -1).astype(F32)
    return jnp.pad(v, ((0, 0), (0, D_MODEL - v.shape[1])))


def _pack_small(t):
    rows = [_row(t[k]) for k in REPLICATED_VECS]
    rows.append(_row(jnp.concatenate([t[k].reshape(-1) for k in SHARDED_VECS])))
    rows.append(jnp.zeros((2, D_MODEL), F32))
    rows.append(t["a_w_group"].reshape(GROUP_ROWS, D_MODEL).astype(F32))
    return jnp.concatenate(rows, axis=0)


def _unpack_small(p, like):
    out = {k: p[i, :].reshape(like[k].shape) for i, k in enumerate(REPLICATED_VECS)}
    off = 0
    for k in SHARDED_VECS:
        size = like[k].size
        out[k] = p[len(REPLICATED_VECS), off:off + size].reshape(like[k].shape)
        off += size
    out["a_w_group"] = p[len(REPLICATED_VECS) + 3:, :].reshape(like["a_w_group"].shape)
    return out


def kernel(x, mem, mem_norm, a_norm_mix, a_w_in, a_w_group, a_scale, a_w_mem_kv, a_w_out, a_norm_ffn, a_w_gu, a_w_down, kv_norm, w_kv, b_norm_mix, b_w_q, b_w_mem_kv, b_w_out, b_norm_ffn, b_w_gu, b_w_down, final_norm, loss_target, m_mem_norm, m_a_norm_mix, m_a_w_in, m_a_w_group, m_a_scale, m_a_w_mem_kv, m_a_w_out, m_a_norm_ffn, m_a_w_gu, m_a_w_down, m_kv_norm, m_w_kv, m_b_norm_mix, m_b_w_q, m_b_w_mem_kv, m_b_w_out, m_b_norm_ffn, m_b_w_gu, m_b_w_down, m_final_norm, v_mem_norm, v_a_norm_mix, v_a_w_in, v_a_w_group, v_a_scale, v_a_w_mem_kv, v_a_w_out, v_a_norm_ffn, v_a_w_gu, v_a_w_down, v_kv_norm, v_w_kv, v_b_norm_mix, v_b_w_q, v_b_w_mem_kv, v_b_w_out, v_b_norm_ffn, v_b_w_gu, v_b_w_down, v_final_norm):
    given = dict(locals())
    wl = {k: given[k] for k in WEIGHTS}
    ml = {k: given["m_" + k] for k in WEIGHTS}
    vl = {k: given["v_" + k] for k in WEIGHTS}
    chip = 2 * lax.axis_index("x") + lax.axis_index("y")

    def mat(a):
        return a.reshape(a.shape[-2], a.shape[-1])

    shards = {k: mat(wl[k]).astype(BF16) for k in BIG}
    full = _gather_weights({k: shards[k] for k in BIG if k not in LATE})
    gains = jnp.zeros((16, D_MODEL), F32)
    for i, k in enumerate(SHARDED_VECS):
        part = wl[k].reshape(1, -1)
        width = part.shape[1]
        gains = lax.dynamic_update_slice(gains, part, (i, chip * width))
    gains = _all_reduce_small("gains_all_gather", gains) * 0.5
    w = dict(full)
    for k in REPLICATED_VECS:
        w[k] = wl[k].reshape(1, D_MODEL)
    w["a_norm_mix"], w["a_norm_ffn"] = gains[0:1], gains[1:2]
    w["a_scale"] = gains[2:3, :SB_WIDTH]
    w["a_w_group"] = wl["a_w_group"][0].astype(BF16)

    loss, grad_x, g, partial, got = _local_step(x[0], mem[0], loss_target[0], w, {k: shards[k] for k in LATE})

    rest = _pair_sums("late", {k: g[k] for k in BIG if k not in EARLY + MID})
    partial.update(rest)
    got.update(_chip_exchange(rest))
    red = _finish_reduce(partial, got)
    small = jnp.concatenate(
        [_row(g[k]) for k in REPLICATED_VECS] + [_row(g[k]) for k in SHARDED_VECS] + [_row(loss)]
        + [jnp.zeros((7, D_MODEL), F32), g["a_w_group"].reshape(GROUP_ROWS, D_MODEL)], axis=0)
    small = _all_reduce_small("small_grads_all_reduce", small)
    gs = {k: small[i] for i, k in enumerate(REPLICATED_VECS)}
    for i, k in enumerate(SHARDED_VECS):
        width = wl[k].shape[-1]
        gs[k] = lax.dynamic_slice(small[len(REPLICATED_VECS) + i], (chip * width,), (width,))
    gs["a_w_group"] = small[16:]
    total_loss = small[8, 0]

    out_g, out_d, out_m, out_v = {}, {}, {}, {}
    for k in BIG:
        shape = wl[k].shape
        out_g[k] = red[k].reshape(shape)
        d, nm, nv = _adamw(k + "_adamw", mat(wl[k]), red[k], mat(ml[k]), mat(vl[k]))
        out_d[k], out_m[k], out_v[k] = d.reshape(shape), nm.reshape(shape), nv.reshape(shape)
    small_names = REPLICATED_VECS + SHARDED_VECS + ("a_w_group",)
    d, nm, nv = _adamw("small_adamw", _pack_small(wl), _pack_small(gs), _pack_small(ml), _pack_small(vl))
    like = {k: wl[k] for k in small_names}
    for dst, p in ((out_d, d), (out_m, nm), (out_v, nv)):
        dst.update(_unpack_small(p, like))
    for k in small_names:
        out_g[k] = gs[k].reshape(wl[k].shape)

    return (total_loss, grad_x[None], *[out_g[k] for k in WEIGHTS], *[out_d[k] for k in WEIGHTS],
            *[out_m[k] for k in WEIGHTS], *[out_v[k] for k in WEIGHTS])
```

```python
import functools

import jax
import jax.numpy as jnp
from jax import lax
from jax.experimental import pallas as pl
from jax.experimental.pallas import tpu as pltpu

F32 = jnp.float32
BF16 = jnp.bfloat16

HEAD_DIM = 64
SB_WIDTH = 768
MEM_WIDTH = 256
POOL_WINDOWS = (2, 4, 8, 16)
POOL_GROUP = 192
POOL_HALO = 16
EPS = 1e-6
ATT_SCALE = HEAD_DIM ** -0.5
ADAM_LR, ADAM_B1, ADAM_B2, ADAM_EPS, ADAM_WD, ADAM_STEP = 0.001, 0.9, 0.999, 1e-08, 0.01, 10

LANES = 128
SB_TQ, SB_TK = 512, 512
VMEM_LIMIT = 56 * 1024 * 1024
MESH = pl.DeviceIdType.MESH
COPY_BYTES = 512 * 1024
ANY = pl.BlockSpec(memory_space=pl.ANY)


def _params(*sem):
    return pltpu.CompilerParams(dimension_semantics=sem, vmem_limit_bytes=VMEM_LIMIT)


def _tile(n, pref):
    if n <= pref:
        return n
    best = None
    for t in range(LANES, pref + 1, LANES):
        if n % t == 0:
            best = t
    assert best is not None, (n, pref)
    return best


def _row_tile(t, pref):
    if t <= pref:
        return t
    for tb in range(pref - pref % 16, 0, -16):
        if t % tb == 0:
            return tb
    raise ValueError((t, pref))


def _rowwise(name, fn, rows, vecs, row_outs, sum_outs=(), tb=512):
    norm_rows = []
    for r in rows:
        if isinstance(r, tuple):
            arr, (bc, cb) = r
        else:
            arr, (bc, cb) = r, (r.shape[1], 0)
        norm_rows.append((arr, bc, cb))
    t = norm_rows[0][0].shape[0]
    tb = _row_tile(t, tb)
    n_in, n_ro = len(norm_rows) + len(vecs), len(row_outs)

    def body(*refs):
        ins = [r[...] for r in refs[:n_in]]
        outs = fn(*ins)
        if not isinstance(outs, tuple):
            outs = (outs,)
        for o_ref, o in zip(refs[n_in:n_in + n_ro], outs[:n_ro]):
            o_ref[...] = o.astype(o_ref.dtype)
        for s_ref, s in zip(refs[n_in + n_ro:], outs[n_ro:]):
            @pl.when(pl.program_id(0) == 0)
            def _():
                s_ref[...] = jnp.zeros_like(s_ref)
            s_ref[...] += s

    in_specs = [pl.BlockSpec((tb, bc), functools.partial(lambda i, cb: (i, cb), cb=cb)) for _, bc, cb in norm_rows]
    in_specs += [pl.BlockSpec(v.shape, lambda i: (0, 0)) for v in vecs]
    out_specs = [pl.BlockSpec((tb, c), lambda i: (i, 0)) for c, _ in row_outs]
    out_specs += [pl.BlockSpec((1, c), lambda i: (0, 0)) for c in sum_outs]
    out_shape = [jax.ShapeDtypeStruct((t, c), d) for c, d in row_outs]
    out_shape += [jax.ShapeDtypeStruct((1, c), F32) for c in sum_outs]
    res = pl.pallas_call(
        body, name=name, grid=(t // tb,), in_specs=in_specs, out_specs=out_specs, out_shape=out_shape,
        compiler_params=_params("arbitrary"),
    )(*[a for a, _, _ in norm_rows], *vecs)
    return res


def _rms_fwd(name, x, g):
    def fn(xt, gt):
        rstd = lax.rsqrt(jnp.mean(xt * xt, axis=-1, keepdims=True) + EPS)
        return xt * rstd * gt
    return _rowwise(name, fn, [x], [g], [(x.shape[1], BF16)])[0]


def _rms_bwd(name, x, g, dh, dres=None, want_dx=True):
    has_res = dres is not None

    def fn(*a):
        if has_res:
            xt, dht, drt, gt = a
        else:
            xt, dht, gt = a
        rstd = lax.rsqrt(jnp.mean(xt * xt, axis=-1, keepdims=True) + EPS)
        xhat = xt * rstd
        dht = dht.astype(F32)
        dg = jnp.sum(dht * xhat, axis=0, keepdims=True)
        if not want_dx:
            return (dg,)
        dxhat = dht * gt
        dx = rstd * (dxhat - xhat * jnp.mean(dxhat * xhat, axis=-1, keepdims=True))
        if has_res:
            dx = dx + drt
        return dx, dx, dg

    d = x.shape[1]
    rows = [x, dh] + ([dres] if has_res else [])
    outs = [(d, F32), (d, BF16)] if want_dx else []
    return _rowwise(name, fn, rows, [g], outs, [d])


def _rms_fwd2(name, x, g1, g2):
    def fn(xt, g1t, g2t):
        xn = xt * lax.rsqrt(jnp.mean(xt * xt, axis=-1, keepdims=True) + EPS)
        return xn * g1t, xn * g2t
    return _rowwise(name, fn, [x], [g1, g2], [(x.shape[1], BF16)] * 2)


def _rms_bwd2(name, x, g1, dh1, g2, dh2, dres):
    def fn(xt, d1, d2, drt, g1t, g2t):
        rstd = lax.rsqrt(jnp.mean(xt * xt, axis=-1, keepdims=True) + EPS)
        xhat = xt * rstd
        d1, d2 = d1.astype(F32), d2.astype(F32)
        dxhat = d1 * g1t + d2 * g2t
        dx = rstd * (dxhat - xhat * jnp.mean(dxhat * xhat, axis=-1, keepdims=True)) + drt
        return dx, dx, jnp.sum(d1 * xhat, axis=0, keepdims=True), jnp.sum(d2 * xhat, axis=0, keepdims=True)

    d = x.shape[1]
    return _rowwise(name, fn, [x, dh1, dh2, dres], [g1, g2], [(d, F32), (d, BF16)], [d, d])


_DOT_DIMS = {"nn": ((1,), (0,)), "nt": ((1,), (1,)), "tn": ((0,), (0,))}


def _mm(name, a, b, mode, out_dtype, res=None, tm=512, tn=512):
    if mode == "nn":
        (m, k), (k2, n) = a.shape, b.shape
    elif mode == "nt":
        (m, k), (n, k2) = a.shape, b.shape
    else:
        (k, m), (k2, n) = a.shape, b.shape
    assert k == k2, (name, a.shape, b.shape)
    tm, tn = _tile(m, tm), _tile(n, tn)
    dims = (_DOT_DIMS[mode], ((), ()))
    has_res = res is not None

    def body(a_ref, b_ref, *rest):
        acc = lax.dot_general(a_ref[...], b_ref[...], dims, preferred_element_type=F32)
        if has_res:
            acc = acc + rest[0][...]
        rest[-1][...] = acc.astype(out_dtype)

    a_spec = pl.BlockSpec((k, tm), lambda i, j: (0, i)) if mode == "tn" else pl.BlockSpec((tm, k), lambda i, j: (i, 0))
    b_spec = pl.BlockSpec((tn, k), lambda i, j: (j, 0)) if mode == "nt" else pl.BlockSpec((k, tn), lambda i, j: (0, j))
    o_spec = pl.BlockSpec((tm, tn), lambda i, j: (i, j))
    in_specs, args = [a_spec, b_spec], [a, b]
    if has_res:
        in_specs.append(o_spec)
        args.append(res)
    return pl.pallas_call(
        body, name=name, grid=(m // tm, n // tn), in_specs=in_specs, out_specs=o_spec,
        out_shape=jax.ShapeDtypeStruct((m, n), out_dtype), compiler_params=_params("parallel", "arbitrary"),
    )(*args)


def _pool(name, u, reverse, tb=512):
    t = u.shape[0]
    tb = min(tb, t)
    nt = t // tb
    c = SB_WIDTH
    hpb = tb // POOL_HALO

    def body(cur_ref, halo_ref, o_ref):
        i = pl.program_id(0)
        cur = cur_ref[...].astype(F32)
        edge = (i == nt - 1) if reverse else (i == 0)
        halo = jnp.where(edge, 0.0, halo_ref[...].astype(F32))
        col = lax.broadcasted_iota(jnp.int32, (tb + POOL_HALO, c), 1)
        row = lax.broadcasted_iota(jnp.int32, (tb + POOL_HALO, c), 0)
        wcol = jnp.where(col < POOL_GROUP, 2, jnp.where(col < 2 * POOL_GROUP, 4, jnp.where(col < 3 * POOL_GROUP, 8, 16)))
        n = tb + POOL_HALO
        if reverse:
            ext = jnp.concatenate([cur, halo], axis=0)
            tpos = i * tb + row
            ext = ext / jnp.minimum(tpos + 1, wcol).astype(F32)
            shift = lambda a, k: pltpu.roll(a, n - k, 0)
        else:
            ext = jnp.concatenate([halo, cur], axis=0)
            shift = lambda a, k: pltpu.roll(a, k, 0)
        s2 = ext + shift(ext, 1)
        s4 = s2 + shift(s2, 2)
        s8 = s4 + shift(s4, 4)
        s16 = s8 + shift(s8, 8)
        win = jnp.where(wcol == 2, s2, jnp.where(wcol == 4, s4, jnp.where(wcol == 8, s8, s16)))
        if reverse:
            out = win[:tb] - cur
        else:
            tpos = i * tb + row[POOL_HALO:] - POOL_HALO
            out = win[POOL_HALO:] / jnp.minimum(tpos + 1, wcol[POOL_HALO:]).astype(F32) - cur
        o_ref[...] = out.astype(o_ref.dtype)

    if reverse:
        halo_map = lambda i: (jnp.minimum((i + 1) * hpb, t // POOL_HALO - 1), 0)
    else:
        halo_map = lambda i: (jnp.maximum(i * hpb - 1, 0), 0)
    return pl.pallas_call(
        body, name=name, grid=(nt,),
        in_specs=[pl.BlockSpec((tb, c), lambda i: (i, 0)), pl.BlockSpec((POOL_HALO, c), halo_map)],
        out_specs=pl.BlockSpec((tb, c), lambda i: (i, 0)),
        out_shape=jax.ShapeDtypeStruct((t, c), BF16), compiler_params=_params("arbitrary"),
    )(u, u)


def _head_masks(shape):
    lane = lax.broadcasted_iota(jnp.int32, shape, 1)
    return lane < HEAD_DIM, lane >= HEAD_DIM


def _pick(mask, a):
    return jnp.where(mask, a, jnp.zeros_like(a))


def _dot(a, b, mode):
    return lax.dot_general(a, b, (_DOT_DIMS[mode], ((), ())), preferred_element_type=F32)


def _dot_tri(a, tri, suffix):
    h = a.shape[1] // 2
    lo, hi = a[:, :h], a[:, h:]
    s_lo, s_hi = jnp.sum(lo, axis=1, keepdims=True), jnp.sum(hi, axis=1, keepdims=True)
    p_lo, p_hi = _dot(lo.astype(BF16), tri, "nn"), _dot(hi.astype(BF16), tri, "nn")
    if suffix:
        p_lo = p_lo + s_hi
    else:
        p_hi = p_hi + s_lo
    return jnp.concatenate([p_lo, p_hi], axis=1), s_lo + s_hi


def _log_gates(z):
    nz = -z
    l = jnp.log(1.0 + jnp.exp(jnp.minimum(z, nz)))
    ln = jnp.minimum(nz, 0.0) - l
    return ln, z + ln


def _sb_blocks(s, tq, tk):
    tq, tk = min(tq, s), min(tk, s)
    assert tq == tk and s % tk == 0 and tk % 64 == 0, (s, tq, tk)
    return tq, tk, tk // 2


def _strict_triangle(n, pred):
    return pred(lax.broadcasted_iota(jnp.int32, (n, n), 0), lax.broadcasted_iota(jnp.int32, (n, n), 1)).astype(BF16)


def _sb_fwd(proj, kv, late_shards, tq=SB_TQ, tk=SB_TK):
    s = proj.shape[0]
    tq, tk, th = _sb_blocks(s, tq, tk)
    npair = SB_WIDTH // LANES
    gather = _Gather({k: v.shape for k, v in late_shards.items()})
    ng = gather.n

    def body(q_ref, k_ref, v_ref, *rest):
        o_ref, tot_ref = rest[ng:ng + 2]
        comm = rest[:ng], rest[ng + 2:2 * ng + 2], rest[2 * ng + 2:2 * ng + 5], rest[2 * ng + 5:]

        @pl.when(pl.program_id(0) == 0)
        def _():
            gather.start(*comm)

        tri_gt = _strict_triangle(tk, lambda j, s_: j > s_)
        tri_gt_h = _strict_triangle(th, lambda j, s_: j > s_)
        seen = lax.broadcasted_iota(jnp.int32, (tq, th), 1) < lax.broadcasted_iota(jnp.int32, (tq, th), 0)
        m_a, m_b = _head_masks((tq, LANES))

        def block(qh, k2, v2, lane_mask, carry, acc, mask, tri):
            ln_full, lsz = _log_gates(_dot(qh, k2, "nt"))
            ln = ln_full if mask is None else jnp.where(mask, ln_full, 0.0)
            w = jnp.exp(lsz + _dot(ln.astype(BF16), tri, "nn"))
            if mask is not None:
                w = jnp.where(mask, w, 0.0)
            acc = acc + jnp.exp(carry) * _dot(w.astype(BF16), _pick(lane_mask[:v2.shape[0]], v2), "nn")
            return carry + jnp.sum(ln, axis=1, keepdims=True), acc

        def q_block(qi, _):
            q0 = pl.multiple_of(qi * tq, tq)
            q2 = q_ref[pl.ds(q0, tq), :] * ATT_SCALE
            qa, qb = _pick(m_a, q2), _pick(m_b, q2)

            def both(qa, qb, k0, size, ca, cb, acc, mask, tri):
                k2 = k_ref[pl.ds(k0, size), :]
                v2 = v_ref[pl.ds(k0, size), :]
                ca, acc = block(qa, k2, v2, m_a, ca, acc, mask, tri)
                cb, acc = block(qb, k2, v2, m_b, cb, acc, mask, tri)
                return ca, cb, acc

            zero_c = jnp.zeros((th, 1), F32)
            zero_o = jnp.zeros((th, LANES), F32)
            ca, cb, acc = both(qa[th:], qb[th:], pl.multiple_of(q0 + th, th), th, zero_c, zero_c, zero_o, seen[:th], tri_gt_h)
            ca, cb = jnp.concatenate([zero_c, ca], axis=0), jnp.concatenate([zero_c, cb], axis=0)
            carry = both(qa, qb, q0, th, ca, cb, jnp.concatenate([zero_o, acc], axis=0), seen, tri_gt_h)

            def k_block(step, carry):
                return both(qa, qb, pl.multiple_of(q0 - (step + 1) * tk, tk), tk, *carry, None, tri_gt)

            ca, cb, acc = lax.fori_loop(0, qi, k_block, carry)
            o_ref[pl.ds(q0, tq), :] = acc.astype(o_ref.dtype)
            tot_ref[0, pl.ds(q0, tq), :] = jnp.broadcast_to(ca, (tq, LANES))
            tot_ref[1, pl.ds(q0, tq), :] = jnp.broadcast_to(cb, (tq, LANES))
            return 0

        lax.fori_loop(0, s // tq, q_block, 0)

        @pl.when(pl.program_id(0) == npair - 1)
        def _():
            gather.finish(*comm)

    outs = pl.pallas_call(
        body, name="sb_fwd", grid=(npair,),
        in_specs=[pl.BlockSpec((s, LANES), lambda p: (0, p)), pl.BlockSpec((s, LANES), lambda p: (0, p)),
                  pl.BlockSpec((s, LANES), lambda p: (0, npair + p))] + [ANY] * ng,
        out_specs=[pl.BlockSpec((s, LANES), lambda p: (0, p)), pl.BlockSpec((None, 2, s, LANES), lambda p: (p, 0, 0, 0))]
        + [ANY] * ng,
        out_shape=[jax.ShapeDtypeStruct((s, SB_WIDTH), BF16), jax.ShapeDtypeStruct((npair, 2, s, LANES), F32)]
        + gather.out_shape(),
        scratch_shapes=gather.scratch(), compiler_params=_params("arbitrary"),
    )(proj, kv, kv, *[late_shards[k] for k in gather.names])
    return outs[0], outs[1], dict(zip(gather.names, outs[2:]))


def _sb_bwd(proj, kv, dcat, tot, early_partial, tq=SB_TQ, tk=SB_TK):
    s = proj.shape[0]
    tq, tk, th = _sb_blocks(s, tq, tk)
    npair = SB_WIDTH // LANES
    exchange = _ChipExchange(early_partial)
    ne = exchange.n

    def body(q_ref, k_ref, v_ref, do_ref, tot_ref, *rest):
        dq_ref, dk_ref, dv_ref = rest[ne:ne + 3]
        dk_acc, dv_acc = rest[2 * ne + 3:2 * ne + 5]
        comm = rest[:ne], rest[ne + 3:2 * ne + 3], rest[2 * ne + 5:]

        @pl.when(pl.program_id(0) == 0)
        def _():
            exchange.start(*comm)

        tris = (_strict_triangle(tk // 2, lambda j, s_: j > s_), _strict_triangle(tk // 2, lambda j, s_: j < s_))
        tris_h = (_strict_triangle(th // 2, lambda j, s_: j > s_), _strict_triangle(th // 2, lambda j, s_: j < s_))
        seen = lax.broadcasted_iota(jnp.int32, (tq, th), 1) < lax.broadcasted_iota(jnp.int32, (tq, th), 0)
        m_a, m_b = _head_masks((tq, LANES))
        dk_acc[...] = jnp.zeros_like(dk_acc)
        dv_acc[...] = jnp.zeros_like(dv_acc)

        def block(qh, doh, k2, v2, lane_mask, tot_h, carry, mask, tri):
            c_ln, c_d, dq = carry
            ln_full, lsz = _log_gates(_dot(qh, k2, "nt"))
            ln = ln_full if mask is None else jnp.where(mask, ln_full, 0.0)
            inside, total = _dot_tri(ln, tri[0], True)
            c_ln = c_ln + total
            w = jnp.exp(lsz + ((tot_h - c_ln) + inside))
            if mask is not None:
                w = jnp.where(mask, w, 0.0)
            dlw = _dot(doh, v2, "nt") * w
            before, d_total = _dot_tri(dlw, tri[1], False)
            dz = dlw * jnp.exp(ln_full) - (before + c_d) * jnp.exp(lsz)
            if mask is not None:
                dz = jnp.where(mask, dz, 0.0)
            dz = dz.astype(BF16)
            dq = dq + _dot(dz, _pick(lane_mask[:k2.shape[0]], k2), "nn")
            dk = _dot(dz, qh, "tn")
            dv = _dot(w.astype(BF16), doh, "tn")
            carry = (c_ln, c_d + d_total, dq)
            return carry, dk, dv

        def q_block(qi, _):
            q0 = pl.multiple_of(qi * tq, tq)
            q2 = q_ref[pl.ds(q0, tq), :] * ATT_SCALE
            do2 = do_ref[pl.ds(q0, tq), :]
            qa, qb = _pick(m_a, q2), _pick(m_b, q2)
            doa, dob = _pick(m_a, do2), _pick(m_b, do2)
            tot_a = tot_ref[0, pl.ds(q0, tq), 0:1]
            tot_b = tot_ref[1, pl.ds(q0, tq), 0:1]
            zero_c = jnp.zeros((tq, 1), F32)
            zero_q = jnp.zeros((tq, LANES), F32)

            def both(rows, k0, size, ca, cb, mask, tri):
                k2 = k_ref[pl.ds(k0, size), :]
                v2 = v_ref[pl.ds(k0, size), :]
                ca, dka, dva = block(qa[rows], doa[rows], k2, v2, m_a, tot_a[rows], ca, mask, tri)
                cb, dkb, dvb = block(qb[rows], dob[rows], k2, v2, m_b, tot_b[rows], cb, mask, tri)
                dk_acc[pl.ds(k0, size), :] += dka + dkb
                dv_acc[pl.ds(k0, size), :] += dva + dvb
                return ca, cb

            def k_block(kj, carry):
                return both(slice(None), pl.multiple_of(kj * tk, tk), tk, carry[0], carry[1], None, tris)

            init = ((zero_c, zero_c, zero_q), (zero_c, zero_c, zero_q))
            ca, cb = lax.fori_loop(0, qi, k_block, init)
            ca, cb = both(slice(None), q0, th, ca, cb, seen, tris_h)
            late = slice(th, tq)
            la, lb = both(late, pl.multiple_of(q0 + th, th), th, tuple(t[late] for t in ca), tuple(t[late] for t in cb),
                          seen[:th], tris_h)
            dq = jnp.concatenate([ca[2][:th] + cb[2][:th], la[2] + lb[2]], axis=0)
            dq_ref[pl.ds(q0, tq), :] = (dq * ATT_SCALE).astype(dq_ref.dtype)
            return 0

        lax.fori_loop(0, s // tq, q_block, 0)
        dk_ref[...] = dk_acc[...].astype(dk_ref.dtype)
        dv_ref[...] = dv_acc[...].astype(dv_ref.dtype)

        @pl.when(pl.program_id(0) == npair - 1)
        def _():
            exchange.finish(*comm)

    col = lambda off: pl.BlockSpec((s, LANES), functools.partial(lambda p, off: (0, off + p), off=off))
    outs = pl.pallas_call(
        body, name="sb_bwd", grid=(npair,),
        in_specs=[col(0), col(0), col(npair), col(0), pl.BlockSpec((None, 2, s, LANES), lambda p: (p, 0, 0, 0))]
        + [ANY] * ne,
        out_specs=[col(0), col(0), col(0)] + [ANY] * ne,
        out_shape=[jax.ShapeDtypeStruct((s, SB_WIDTH), BF16)] * 3 + exchange.out_shape(),
        scratch_shapes=[pltpu.VMEM((s, LANES), F32), pltpu.VMEM((s, LANES), F32)] + exchange.scratch(),
        compiler_params=_params("arbitrary"),
    )(proj, kv, kv, dcat, tot, *[early_partial[k] for k in exchange.names])
    dq, dk, dv = outs[:3]
    return dq, jnp.concatenate([dk, dv], axis=1), dict(zip(exchange.names, outs[3:]))


def _mem_fwd(name, proj, mkv, tq=512):
    s = proj.shape[0]
    tq = min(tq, s)
    qblk = SB_WIDTH // MEM_WIDTH

    def body(q_ref, kv_ref, o_ref):
        m_a, m_b = _head_masks((tq, LANES))
        mk_a, mk_b = _head_masks((kv_ref.shape[0], LANES))
        for p in range(MEM_WIDTH // LANES):
            q2 = q_ref[:, p * LANES:(p + 1) * LANES].astype(BF16)
            k2 = kv_ref[:, p * LANES:(p + 1) * LANES]
            v2 = kv_ref[:, MEM_WIDTH + p * LANES:MEM_WIDTH + (p + 1) * LANES]
            acc = jnp.zeros((tq, LANES), F32)
            for mq, mk in ((m_a, mk_a), (m_b, mk_b)):
                logits = _dot(_pick(mq, q2), k2, "nt") * ATT_SCALE
                e = jnp.exp(logits - jnp.max(logits, axis=-1, keepdims=True))
                prob = e / jnp.sum(e, axis=-1, keepdims=True)
                acc = acc + _dot(prob.astype(BF16), _pick(mk, v2), "nn")
            o_ref[:, p * LANES:(p + 1) * LANES] = acc.astype(o_ref.dtype)

    return pl.pallas_call(
        body, name=name, grid=(s // tq,),
        in_specs=[pl.BlockSpec((tq, MEM_WIDTH), lambda i: (i, qblk)), pl.BlockSpec(mkv.shape, lambda i: (0, 0))],
        out_specs=pl.BlockSpec((tq, MEM_WIDTH), lambda i: (i, 0)),
        out_shape=jax.ShapeDtypeStruct((s, MEM_WIDTH), BF16), compiler_params=_params("arbitrary"),
    )(proj, mkv)


def _mem_bwd(name, proj, mkv, dcat, tq=512):
    s = proj.shape[0]
    tq = min(tq, s)
    qblk = SB_WIDTH // MEM_WIDTH

    def body(q_ref, kv_ref, do_ref, dq_ref, dkv_ref):
        @pl.when(pl.program_id(0) == 0)
        def _():
            dkv_ref[...] = jnp.zeros_like(dkv_ref)

        m_a, m_b = _head_masks((tq, LANES))
        for p in range(MEM_WIDTH // LANES):
            ksl = slice(p * LANES, (p + 1) * LANES)
            vsl = slice(MEM_WIDTH + p * LANES, MEM_WIDTH + (p + 1) * LANES)
            q2, do2 = q_ref[:, ksl].astype(BF16), do_ref[:, ksl]
            k2, v2 = kv_ref[:, ksl], kv_ref[:, vsl]
            mk_a, mk_b = _head_masks(k2.shape)
            dq = jnp.zeros((tq, LANES), F32)
            dk = jnp.zeros(k2.shape, F32)
            dv = jnp.zeros(k2.shape, F32)
            for mq, mk in ((m_a, mk_a), (m_b, mk_b)):
                qh, doh = _pick(mq, q2), _pick(mq, do2)
                logits = _dot(qh, k2, "nt") * ATT_SCALE
                e = jnp.exp(logits - jnp.max(logits, axis=-1, keepdims=True))
                prob = e / jnp.sum(e, axis=-1, keepdims=True)
                dp = _dot(doh, v2, "nt")
                ds = prob * (dp - jnp.sum(dp * prob, axis=-1, keepdims=True)) * ATT_SCALE
                ds = ds.astype(BF16)
                dq = dq + _dot(ds, _pick(mk, k2), "nn")
                dk = dk + _dot(ds, qh, "tn")
                dv = dv + _dot(prob.astype(BF16), doh, "tn")
            dq_ref[:, ksl] = dq.astype(dq_ref.dtype)
            dkv_ref[:, ksl] += dk
            dkv_ref[:, vsl] += dv

    return pl.pallas_call(
        body, name=name, grid=(s // tq,),
        in_specs=[pl.BlockSpec((tq, MEM_WIDTH), lambda i: (i, qblk)), pl.BlockSpec(mkv.shape, lambda i: (0, 0)),
                  pl.BlockSpec((tq, MEM_WIDTH), lambda i: (i, qblk))],
        out_specs=[pl.BlockSpec((tq, MEM_WIDTH), lambda i: (i, 0)), pl.BlockSpec(mkv.shape, lambda i: (0, 0))],
        out_shape=[jax.ShapeDtypeStruct((s, MEM_WIDTH), BF16), jax.ShapeDtypeStruct(mkv.shape, F32)],
        compiler_params=_params("arbitrary"),
    )(proj, mkv, dcat)


def _gu_swiglu(name, h, w_gu, tn=256):
    t, d = h.shape
    f = w_gu.shape[1] // 2
    nb = f // tn

    def body(h_ref, wg_ref, wu_ref, g_ref, u_ref, a_ref):
        hh = h_ref[...]
        g = _dot(hh, wg_ref[...], "nn")
        u = _dot(hh, wu_ref[...], "nn")
        g_ref[...] = g.astype(BF16)
        u_ref[...] = u.astype(BF16)
        a_ref[...] = (g * jax.nn.sigmoid(g) * u).astype(BF16)

    out = pl.BlockSpec((t, tn), lambda j: (0, j))
    return pl.pallas_call(
        body, name=name, grid=(nb,),
        in_specs=[pl.BlockSpec((t, d), lambda j: (0, 0)), pl.BlockSpec((d, tn), lambda j: (0, j)),
                  pl.BlockSpec((d, tn), lambda j: (0, nb + j))],
        out_specs=[out, out, out], out_shape=[jax.ShapeDtypeStruct((t, f), BF16)] * 3,
        compiler_params=_params("arbitrary"),
    )(h, w_gu, w_gu)


def _down_dx_swiglu(name, dout_bf, w_down, gate, up, tm=256):
    t, d = dout_bf.shape
    f = w_down.shape[0]

    def body(do_ref, w_ref, g_ref, u_ref, o_ref):
        dact = _dot(do_ref[...], w_ref[...], "nt")
        g, u = g_ref[...].astype(F32), u_ref[...].astype(F32)
        sg = jax.nn.sigmoid(g)
        silu = g * sg
        o_ref[:, :f] = (dact * u * (sg + silu * (1.0 - sg))).astype(BF16)
        o_ref[:, f:] = (dact * silu).astype(BF16)

    row = lambda c: pl.BlockSpec((tm, c), lambda i: (i, 0))
    return pl.pallas_call(
        body, name=name, grid=(t // tm,),
        in_specs=[row(d), pl.BlockSpec((f, d), lambda i: (0, 0)), row(f), row(f)],
        out_specs=row(2 * f), out_shape=jax.ShapeDtypeStruct((t, 2 * f), BF16), compiler_params=_params("arbitrary"),
    )(dout_bf, w_down, gate, up)


def _ffn_fwd(tag, x, norm, w_gu, w_down):
    h = _rms_fwd(tag + "_ffn_norm", x, norm)
    gate, up, act = _gu_swiglu(tag + "_gu", h, w_gu)
    out = _mm(tag + "_down", act, w_down, "nn", F32, res=x, tn=1024)
    return out, (h, gate, up, act)


def _ffn_bwd(tag, x, norm, w_gu, w_down, saved, dout, dout_bf):
    h, gate, up, act = saved
    g_down = _mm(tag + "_down_dw", act, dout_bf, "tn", BF16, tm=256, tn=1024)
    dgu = _down_dx_swiglu(tag + "_down_dx", dout_bf, w_down, gate, up)
    g_gu = _mm(tag + "_gu_dw", h, dgu, "tn", BF16, tm=1024)
    dh = _mm(tag + "_gu_dx", dgu, w_gu, "nt", F32, tn=1024)
    dx, dx_bf, g_norm = _rms_bwd(tag + "_ffn_norm_bwd", x, norm, dh, dres=dout)
    return dx, dx_bf, g_gu, g_down, g_norm


def _mem_kv(tag, mem_n, w_mem_kv):
    return _mm(tag + "_memkv", mem_n, w_mem_kv, "nn", BF16)


def _mem_kv_bwd(tag, mem, mem_norm, mem_n, w_mem_kv, dmkv):
    dmkv = dmkv.astype(BF16)
    g_w = _mm(tag + "_memkv_dw", mem_n, dmkv, "tn", BF16)
    dmem_n = _mm(tag + "_memkv_dx", dmkv, w_mem_kv, "nt", F32)
    (g_norm,) = _rms_bwd(tag + "_memnorm_bwd", mem, mem_norm, dmem_n, want_dx=False)
    return g_w, g_norm


def _block_diag(w_group):
    z = jnp.zeros((POOL_GROUP, POOL_GROUP), w_group.dtype)
    return jnp.concatenate(
        [jnp.concatenate([w_group[g] if h == g else z for h in range(4)], axis=1) for g in range(4)], axis=0)


def _local_step(x, mem, target, w, late_shards, second):
    g = {}
    w = dict(w)
    mem_n = _rms_fwd("mem_norm", mem, w["mem_norm"])
    h_a = _rms_fwd("a_mix_norm", x, w["a_norm_mix"])
    proj_a = _mm("a_in", h_a, w["a_w_in"], "nn", F32, tn=1024)
    pooled = _pool("a_pool", proj_a, reverse=False)
    w_bd = _block_diag(w["a_w_group"])
    g_pre = _mm("a_group", pooled, w_bd, "nn", BF16, tn=768)
    mkv_a = _mem_kv("a", mem_n, w["a_w_mem_kv"])
    mem_a = _mem_fwd("a_mem_attn", proj_a, mkv_a)
    cat_a = _rowwise("a_cat", lambda gp, mo, sc: jnp.concatenate([gp.astype(F32) * sc, mo.astype(F32)], axis=1),
                     [g_pre, mem_a], [w["a_scale"]], [(1024, BF16)])[0]
    x1 = _mm("a_out", cat_a, w["a_w_out"], "nn", F32, res=x, tn=1024)
    w.update(_gather_finish(second[0], second[1], x1))
    x2, ffn_a = _ffn_fwd("a", x1, w["a_norm_ffn"], w["a_w_gu"], w["a_w_down"])
    h_k, h_b = _rms_fwd2("x2_norms", x2, w["kv_norm"], w["b_norm_mix"])
    kv = _mm("kv_proj", h_k, w["w_kv"], "nn", BF16, tn=1536)
    proj_b = _mm("b_q", h_b, w["b_w_q"], "nn", BF16, tn=1024)
    sb_out, tot, late = _sb_fwd(proj_b, kv, late_shards)
    w.update(late)
    mkv_b = _mem_kv("b", mem_n, w["b_w_mem_kv"])
    mem_b = _mem_fwd("b_mem_attn", proj_b, mkv_b)
    cat_b = jnp.concatenate([sb_out, mem_b], axis=1)
    x3 = _mm("b_out", cat_b, w["b_w_out"], "nn", F32, res=x2, tn=1024)
    x4, ffn_b = _ffn_fwd("b", x3, w["b_norm_ffn"], w["b_w_gu"], w["b_w_down"])

    d = x.shape[1]

    def head(xt, tt, gt):
        rstd = lax.rsqrt(jnp.mean(xt * xt, axis=-1, keepdims=True) + EPS)
        xhat = xt * rstd
        err = xhat * gt - tt
        loss = 0.5 * jnp.sum(jnp.sum(err * err, axis=1, keepdims=True), axis=0, keepdims=True) / d
        dy = err / d
        dxhat = dy * gt
        dx = rstd * (dxhat - xhat * jnp.mean(dxhat * xhat, axis=-1, keepdims=True))
        return dx, dx, jnp.sum(dy * xhat, axis=0, keepdims=True), jnp.broadcast_to(loss, (1, LANES))

    dx4, dx4_bf, g["final_norm"], loss = _rowwise(
        "loss_head", head, [x4, target], [w["final_norm"]], [(d, F32), (d, BF16)], [d, LANES])

    dx3, dx3_bf, g["b_w_gu"], g["b_w_down"], g["b_norm_ffn"] = _ffn_bwd(
        "b", x3, w["b_norm_ffn"], w["b_w_gu"], w["b_w_down"], ffn_b, dx4, dx4_bf)
    dcat_b = _mm("b_out_dx", dx3_bf, w["b_w_out"], "nt", BF16, tn=1024)
    g["b_w_out"] = _mm("b_out_dw", cat_b, dx3_bf, "tn", BF16, tm=1024, tn=1024)
    early_partial = _pair_sums("early", {k: g.pop(k) for k in EARLY})
    dq_sb, dkv, early_got = _sb_bwd(proj_b, kv, dcat_b, tot, early_partial)
    dq_mem_b, dmkv_b = _mem_bwd("b_mem_attn_bwd", proj_b, mkv_b, dcat_b)
    dproj_b = jnp.concatenate([dq_sb, dq_mem_b], axis=1)
    g["b_w_q"] = _mm("b_q_dw", h_b, dproj_b, "tn", BF16, tm=1024, tn=1024)
    dh_b = _mm("b_q_dx", dproj_b, w["b_w_q"], "nt", F32, tn=1024)
    g["b_w_mem_kv"], g_memnorm_b = _mem_kv_bwd("b", mem, w["mem_norm"], mem_n, w["b_w_mem_kv"], dmkv_b)
    g["w_kv"] = _mm("kv_proj_dw", h_k, dkv, "tn", BF16, tm=1024)
    dh_k = _mm("kv_proj_dx", dkv, w["w_kv"], "nt", F32, tn=1024)
    dx2, dx2_bf, g["kv_norm"], g["b_norm_mix"] = _rms_bwd2(
        "x2_norms_bwd", x2, w["kv_norm"], dh_k, w["b_norm_mix"], dh_b, dx3)
    post_partial = _pair_sums("post_sb", {k: g.pop(k) for k in POST_SB})
    post_flight = _chip_exchange_start("post_sb", post_partial)
    norm_ffn_a = w["a_norm_ffn"] + post_flight[-1][0:1, 0:1]

    dx1, dx1_bf, g["a_w_gu"], g["a_w_down"], g["a_norm_ffn"] = _ffn_bwd(
        "a", x1, norm_ffn_a, w["a_w_gu"], w["a_w_down"], ffn_a, dx2, dx2_bf)
    mid_partial = _pair_sums("mid", {k: g.pop(k) for k in MID})
    mid_flight = _chip_exchange_start("mid", mid_partial)
    scale_a = w["a_scale"] + mid_flight[-1][0:1, 0:1]
    dcat_a = _mm("a_out_dx", dx1_bf, w["a_w_out"], "nt", BF16, tn=1024)
    g["a_w_out"] = _mm("a_out_dw", cat_a, dx1_bf, "tn", BF16, tm=1024, tn=1024)

    def scale_bwd(dc, gp, sc):
        dc, gp = dc.astype(F32), gp.astype(F32)
        return dc * sc, jnp.sum(dc * gp, axis=0, keepdims=True)

    dg_pre, g["a_scale"] = _rowwise("a_scale_bwd", scale_bwd, [(dcat_a, (SB_WIDTH, 0)), g_pre], [scale_a],
                                    [(SB_WIDTH, BF16)], [SB_WIDTH])
    g_bd = _mm("a_group_dw", pooled, dg_pre, "tn", F32, tm=768, tn=768)
    g["a_w_group"] = jnp.stack([g_bd[i * POOL_GROUP:(i + 1) * POOL_GROUP, i * POOL_GROUP:(i + 1) * POOL_GROUP]
                                for i in range(4)])
    dpooled = _mm("a_group_dx", dg_pre, w_bd, "nt", F32, tn=768)
    du_pool = _pool("a_pool_bwd", dpooled, reverse=True)
    dq_mem_a, dmkv_a = _mem_bwd("a_mem_attn_bwd", proj_a, mkv_a, dcat_a)
    dproj_a = jnp.concatenate([du_pool, dq_mem_a], axis=1)
    g["a_w_in"] = _mm("a_in_dw", h_a, dproj_a, "tn", BF16, tm=1024, tn=1024)
    dh_a = _mm("a_in_dx", dproj_a, w["a_w_in"], "nt", F32, tn=1024)
    grad_x, _, g["a_norm_mix"] = _rms_bwd("a_mix_norm_bwd", x, w["a_norm_mix"], dh_a, dres=dx1)
    g["a_w_mem_kv"], g_memnorm_a = _mem_kv_bwd("a", mem, w["mem_norm"], mem_n, w["a_w_mem_kv"], dmkv_a)
    g["mem_norm"] = g_memnorm_a + g_memnorm_b
    post_partial, post_got = _chip_exchange_wait("post_sb", post_partial, post_flight, g["a_norm_mix"])
    mid_partial, mid_got = _chip_exchange_wait("mid", mid_partial, mid_flight, g["a_norm_mix"])
    return (loss, grad_x, g, {**early_partial, **post_partial, **mid_partial}, {**early_got, **post_got, **mid_got})


ROW_SHARDED = ("a_w_in", "a_w_mem_kv", "a_w_out", "a_w_down", "b_w_q", "b_w_mem_kv", "b_w_out", "b_w_down")
COL_SHARDED = ("a_w_gu", "w_kv", "b_w_gu")
BIG = ("a_w_in", "a_w_mem_kv", "a_w_out", "a_w_gu", "a_w_down", "w_kv", "b_w_q", "b_w_mem_kv", "b_w_out", "b_w_gu",
       "b_w_down")
LATE = ("b_w_mem_kv", "b_w_out", "b_w_gu", "b_w_down")
EARLY = ("b_w_gu", "b_w_down", "b_w_out")
POST_SB = ("w_kv", "b_w_q", "b_w_mem_kv")
MID = ("a_w_gu", "a_w_down")
SECOND = ("a_w_gu", "a_w_down", "w_kv", "b_w_q")
N_CHIPS = 4
N_DEV = 8


def _position():
    x, y, c = lax.axis_index("x"), lax.axis_index("y"), lax.axis_index("c")
    other_chips = [(1 - x, y), (x, 1 - y), (1 - x, 1 - y)]
    return x, y, c, other_chips


def _remote(src, dst, send_sem, recv_sem, device):
    return pltpu.make_async_remote_copy(src_ref=src, dst_ref=dst, send_sem=send_sem, recv_sem=recv_sem,
                                        device_id=device, device_id_type=MESH)


def _comm_call(name, body, args, out_shape, n_remote, aliases=None):
    return pl.pallas_call(
        body, name=name, in_specs=[ANY] * len(args), out_specs=[ANY] * len(out_shape), out_shape=out_shape,
        scratch_shapes=[pltpu.SemaphoreType.DMA((n_remote,)), pltpu.SemaphoreType.DMA((n_remote,))],
        input_output_aliases=aliases or {},
    )(*args)


def _row_chunks(nrows, row_bytes, mult=16):
    assert nrows % mult == 0, (nrows, mult)
    per = max(mult, (COPY_BYTES // row_bytes) // mult * mult)
    return [(r0, min(per, nrows - r0)) for r0 in range(0, nrows, per)]


def _rows(ref, start, size, lead=()):
    if isinstance(start, int):
        return ref.at[(*lead, pl.ds(start, size))]
    return ref.at[(*lead, pl.ds(pl.multiple_of(start, 16), size))]


class _Copies:
    def __init__(self, send, recv):
        self.send, self.recv = send, recv
        self.n_remote = 0
        self.remotes = []

    def slot(self):
        self.n_remote += 1
        return self.n_remote - 1

    def remote(self, k, src, dst, device):
        cp = _remote(src, dst, self.send.at[k], self.recv.at[k], device)
        cp.start()
        self.remotes.append(cp)
        return cp

    def finish(self):
        for cp in self.remotes:
            cp.wait_send()


def _shard_cols(ref, row_sharded, cdim, chip):
    if row_sharded:
        return ref
    return ref.at[:, pl.ds(pl.multiple_of(chip * cdim, LANES), cdim)]


class _Gather:
    def __init__(self, shapes, mult=16):
        self.names = list(shapes)
        self.shapes = [tuple(shapes[k]) for k in self.names]
        self.row_sharded = [k in ROW_SHARDED for k in self.names]
        self.chunks = [(i, r0, size) for i, (r, cdim) in enumerate(self.shapes)
                       for r0, size in _row_chunks(r // 2, cdim * 2, mult)]
        self.n = len(self.names)

    def out_shape(self):
        return [jax.ShapeDtypeStruct((N_CHIPS * r, cdim) if rs else (r, N_CHIPS * cdim), BF16)
                for (r, cdim), rs in zip(self.shapes, self.row_sharded)]

    def scratch(self):
        n_remote, n_local = 6 * len(self.chunks), self.n + 2 * len(self.chunks)
        return [pltpu.SemaphoreType.DMA((n_remote,)), pltpu.SemaphoreType.DMA((n_remote,)),
                pltpu.SemaphoreType.DMA((n_local,))] + [pltpu.VMEM(s, BF16) for s in self.shapes]

    def _window(self, dst, i, chip, half, r0, size):
        r, cdim = self.shapes[i]
        base = (chip * r if self.row_sharded[i] else 0) + half * (r // 2) + r0
        return _rows(_shard_cols(dst[i], self.row_sharded[i], cdim, chip), base, size)

    def _mine(self, refs, i, half, r0, size):
        return _rows(refs[i], half * (self.shapes[i][0] // 2) + r0, size)

    def _sent(self, src, dst, sems, q, k, px, py, c, me):
        i, r0, size = self.chunks[q]
        return _remote(self._mine(src, i, c, r0, size), self._window(dst, i, me, c, r0, size),
                       sems[0].at[6 * q + k], sems[1].at[6 * q + k], (px, py, c))

    def start(self, src, dst, sems, vm):
        x, y, c, chips = _position()
        for q in range(len(self.chunks)):
            for k, (px, py) in enumerate(chips):
                self._sent(src, dst, sems, q, k, px, py, c, 2 * x + y).start()
        for i in range(self.n):
            pltpu.make_async_copy(src[i], vm[i], sems[2].at[i]).start()

    def start_ici(self, src, dst, sems):
        x, y, c, chips = _position()
        for q in range(len(self.chunks)):
            for k, (px, py) in enumerate(chips):
                self._sent(src, dst, sems, q, k, px, py, c, 2 * x + y).start()

    def wait_ici(self, src, dst, sems):
        x, y, c, chips = _position()
        for q, (i, r0, size) in enumerate(self.chunks):
            for k, (px, py) in enumerate(chips):
                landed = self._window(dst, i, 2 * px + py, c, r0, size)
                _remote(landed, landed, sems[0].at[6 * q + k], sems[1].at[6 * q + k], (px, py, c)).wait_recv()
        for q in range(len(self.chunks)):
            for k, (px, py) in enumerate(chips):
                self._sent(src, dst, sems, q, k, px, py, c, 2 * x + y).wait_send()

    def finish(self, src, dst, sems, vm, landed=None):
        x, y, c, chips = _position()
        me = 2 * x + y
        send, recv, loc = sems
        arrived = dst if landed is None else landed
        if landed is not None:
            for i in range(self.n):
                pltpu.make_async_copy(src[i], vm[i], loc.at[i]).start()
        for i in range(self.n):
            pltpu.make_async_copy(src[i], vm[i], loc.at[i]).wait()
        placed, forwarded = [], []
        for q, (i, r0, size) in enumerate(self.chunks):
            for half in range(2):
                cp = pltpu.make_async_copy(self._mine(vm, i, half, r0, size), self._window(dst, i, me, half, r0, size),
                                           loc.at[self.n + 2 * q + half])
                cp.start()
                placed.append(cp)
        for q, (i, r0, size) in enumerate(self.chunks):
            for k, (px, py) in enumerate(chips):
                if landed is None:
                    there = self._window(dst, i, 2 * px + py, c, r0, size)
                    _remote(there, there, send.at[6 * q + k], recv.at[6 * q + k], (px, py, c)).wait_recv()
                cp = _remote(self._window(arrived, i, 2 * px + py, c, r0, size), self._window(dst, i, 2 * px + py, c, r0, size),
                             send.at[6 * q + 3 + k], recv.at[6 * q + 3 + k], (x, y, 1 - c))
                cp.start()
                forwarded.append(cp)
        for q, (i, r0, size) in enumerate(self.chunks):
            for k, (px, py) in enumerate(chips):
                there = self._window(dst, i, 2 * px + py, 1 - c, r0, size)
                _remote(there, there, send.at[6 * q + 3 + k], recv.at[6 * q + 3 + k], (x, y, 1 - c)).wait_recv()
        if landed is None:
            for q in range(len(self.chunks)):
                for k, (px, py) in enumerate(chips):
                    self._sent(src, dst, sems, q, k, px, py, c, me).wait_send()
        for cp in forwarded:
            cp.wait_send()
        for cp in placed:
            cp.wait()


class _GatherRelay(_Gather):
    def __init__(self, shapes):
        super().__init__(shapes, mult=32)

    def scratch(self):
        n_remote, n_local = 8 * len(self.chunks), self.n + 2 * len(self.chunks)
        return [pltpu.SemaphoreType.DMA((n_remote,)), pltpu.SemaphoreType.DMA((n_remote,)),
                pltpu.SemaphoreType.DMA((n_local,))] + [pltpu.VMEM(s, BF16) for s in self.shapes]

    def run(self, src, dst, sems, vm):
        x, y, c, chips = _position()
        me, diag, sibling = 2 * x + y, 2 * (1 - x) + (1 - y), (x, y, 1 - c)
        nbrs = chips[:2]
        send, recv, loc = sems

        def copy(slot, ref, device):
            return _remote(ref, ref, send.at[slot], recv.at[slot], device)

        first = []
        for q, (i, r0, size) in enumerate(self.chunks):
            for k, (px, py) in enumerate(nbrs):
                cp = _remote(self._mine(src, i, c, r0, size), self._window(dst, i, me, c, r0, size),
                             send.at[8 * q + k], recv.at[8 * q + k], (px, py, c))
                cp.start()
                first.append(cp)
        for i in range(self.n):
            pltpu.make_async_copy(src[i], vm[i], loc.at[i]).start()
        for i in range(self.n):
            pltpu.make_async_copy(src[i], vm[i], loc.at[i]).wait()
        placed, passed = [], []
        for q, (i, r0, size) in enumerate(self.chunks):
            for half in range(2):
                cp = pltpu.make_async_copy(self._mine(vm, i, half, r0, size), self._window(dst, i, me, half, r0, size),
                                           loc.at[self.n + 2 * q + half])
                cp.start()
                placed.append(cp)
        for q, (i, r0, size) in enumerate(self.chunks):
            for k, (px, py) in enumerate(nbrs):
                landed = self._window(dst, i, 2 * px + py, c, r0, size)
                copy(8 * q + k, landed, (px, py, c)).wait_recv()
                piece = self._window(dst, i, 2 * px + py, c, r0 + k * (size // 2), size // 2)
                ox, oy = nbrs[1 - k]
                for cp in (copy(8 * q + 2 + k, piece, (ox, oy, c)), copy(8 * q + 4 + k, landed, sibling)):
                    cp.start()
                    passed.append(cp)
        for q, (i, r0, size) in enumerate(self.chunks):
            for k in range(2):
                piece = self._window(dst, i, diag, c, r0 + k * (size // 2), size // 2)
                ox, oy = nbrs[1 - k]
                copy(8 * q + 2 + k, piece, (ox, oy, c)).wait_recv()
                cp = copy(8 * q + 6 + k, piece, sibling)
                cp.start()
                passed.append(cp)
        for q, (i, r0, size) in enumerate(self.chunks):
            for k, (px, py) in enumerate(nbrs):
                copy(8 * q + 4 + k, self._window(dst, i, 2 * px + py, 1 - c, r0, size), sibling).wait_recv()
                copy(8 * q + 6 + k, self._window(dst, i, diag, 1 - c, r0 + k * (size // 2), size // 2), sibling).wait_recv()
        for cp in first + passed:
            cp.wait_send()
        for cp in placed:
            cp.wait()


def _gather_weights(shards):
    plan = _GatherRelay({k: v.shape for k, v in shards.items()})
    n = plan.n

    def body(*refs):
        plan.run(refs[:n], refs[n:2 * n], refs[2 * n:2 * n + 3], refs[2 * n + 3:])

    outs = pl.pallas_call(
        body, name="gather_weights", in_specs=[ANY] * n, out_specs=[ANY] * n, out_shape=plan.out_shape(),
        scratch_shapes=plan.scratch(), compiler_params=pltpu.CompilerParams(vmem_limit_bytes=VMEM_LIMIT),
    )(*[shards[k] for k in plan.names])
    return dict(zip(plan.names, outs))


def _gather_start(shards):
    plan = _Gather({k: v.shape for k, v in shards.items()})
    n = plan.n
    n_remote = 6 * len(plan.chunks)
    srcs = [pltpu.with_memory_space_constraint(shards[k], pltpu.HBM) for k in plan.names]
    lands = [pltpu.with_memory_space_constraint(lax.empty(s.shape, s.dtype), pltpu.HBM) for s in plan.out_shape()]

    def body(*refs):
        plan.start_ici(refs[:n], refs[n:2 * n], (refs[2 * n], refs[2 * n + 1]))
        refs[-1][...] = jnp.zeros_like(refs[-1])

    return pl.pallas_call(
        body, name="gather_start",
        out_shape=(pltpu.SemaphoreType.DMA((n_remote,)), pltpu.SemaphoreType.DMA((n_remote,)),
                   *[pltpu.HBM(a.shape, a.dtype) for a in srcs + lands], jax.ShapeDtypeStruct((8, LANES), F32)),
        in_specs=[HBM_SPEC] * (2 * n),
        out_specs=(SEM_SPEC, SEM_SPEC, *[HBM_SPEC] * (2 * n), pl.BlockSpec(memory_space=pltpu.VMEM)),
        input_output_aliases={i: 2 + i for i in range(2 * n)},
        compiler_params=pltpu.CompilerParams(has_side_effects=SIDE_EFFECT),
    )(*srcs, *lands)


def _gather_finish(shapes, flight, after):
    plan = _Gather(shapes)
    n = plan.n
    send, recv, thru = flight[0], flight[1], flight[2:2 + 2 * n]

    def wait_body(*refs):
        plan.wait_ici(refs[:n], refs[n:2 * n], (refs[2 * n], refs[2 * n + 1]))

    thru = pl.pallas_call(
        wait_body, name="gather_wait", out_shape=tuple(pltpu.HBM(t.shape, t.dtype) for t in thru),
        in_specs=[HBM_SPEC] * (2 * n) + [SEM_SPEC, SEM_SPEC, ANY], out_specs=[HBM_SPEC] * (2 * n),
        input_output_aliases={i: i for i in range(2 * n)},
        compiler_params=pltpu.CompilerParams(has_side_effects=SIDE_EFFECT),
    )(*thru, send, recv, after)

    def body(*refs):
        src, landed, dst = refs[:n], refs[n:2 * n], refs[2 * n:3 * n]
        plan.finish(src, dst, refs[3 * n:3 * n + 3], refs[3 * n + 3:], landed=landed)

    outs = pl.pallas_call(
        body, name="gather_finish", in_specs=[ANY] * (2 * n), out_specs=[ANY] * n, out_shape=plan.out_shape(),
        scratch_shapes=plan.scratch(), input_output_aliases={n + i: i for i in range(n)},
        compiler_params=pltpu.CompilerParams(vmem_limit_bytes=VMEM_LIMIT),
    )(*thru)
    return dict(zip(plan.names, outs))


def _scalar_grid_call(name, body, scalars, grid, in_specs, out_specs, out_shape, args):
    return pl.pallas_call(
        body, name=name, out_shape=out_shape,
        grid_spec=pltpu.PrefetchScalarGridSpec(num_scalar_prefetch=1, grid=grid, in_specs=in_specs, out_specs=out_specs),
        compiler_params=_params(*["arbitrary"] * len(grid)),
    )(scalars, *args)


def _pair_sum(name, g4, sib, where, tb=256):
    j, _, r, w = g4.shape
    tb = _row_tile(r, tb)

    def body(s_ref, g_ref, b_ref, o_ref):
        o_ref[...] = (g_ref[...].astype(F32) + b_ref[...].astype(F32)).astype(o_ref.dtype)

    blk = pl.BlockSpec((None, tb, w), lambda a, i, s: (a, i, 0))
    return _scalar_grid_call(
        name, body, where, (j, r // tb),
        [pl.BlockSpec((None, None, tb, w), lambda a, i, s: (a, s[0], i, 0)), blk], blk,
        jax.ShapeDtypeStruct((j, r, w), BF16), (g4, sib))


def _chip_sum(name, partial, got, where, row_sharded, tb=256):
    _, r, cdim = got.shape
    tb = _row_tile(r, tb)

    def body(s_ref, p_ref, g_ref, o_ref):
        acc = p_ref[...].astype(F32)
        for k in range(N_CHIPS - 1):
            acc = acc + g_ref[k].astype(F32)
        o_ref[...] = acc

    if row_sharded:
        own = pl.BlockSpec((None, tb, cdim), lambda i, s: (s[1], i, 0))
    else:
        own = pl.BlockSpec((None, tb, cdim), lambda i, s: (0, i, s[1]))
    return _scalar_grid_call(
        name, body, where, (r // tb,),
        [own, pl.BlockSpec((N_CHIPS - 1, tb, cdim), lambda i, s: (0, i, 0))],
        pl.BlockSpec((None, tb, cdim), lambda i, s: (s[0], i, 0)),
        jax.ShapeDtypeStruct((2, r, cdim), F32), (partial, got))


def _where():
    return jnp.stack([lax.axis_index("c"), 2 * lax.axis_index("x") + lax.axis_index("y")]).astype(jnp.int32)


def _grad_halves(name, g):
    rows, cols = g.shape
    if name in ROW_SHARDED:
        r = rows // N_CHIPS
        return g.reshape(N_CHIPS, 2, r // 2, cols), (r // 2, cols)
    return g.reshape(1, 2, rows // 2, cols), (rows // 2, cols // N_CHIPS)


def _pair_sums(tag, grads):
    names = list(grads)
    n = len(names)
    g4 = [_grad_halves(k, grads[k])[0] for k in names]
    plan = [[(j, r0, size) for j in range(g.shape[0]) for r0, size in _row_chunks(g.shape[2], g.shape[3] * 2)]
            for g in g4]
    out_shape = [jax.ShapeDtypeStruct((g.shape[0],) + g.shape[2:], BF16) for g in g4]

    def body(*refs):
        src, sib = refs[:n], refs[n:2 * n]
        cps = _Copies(*refs[2 * n:])
        x, y, c, _ = _position()
        waits = []
        for i in range(n):
            for j, r0, size in plan[i]:
                waits.append(cps.remote(cps.slot(), _rows(src[i], r0, size, lead=(j, 1 - c)),
                                        _rows(sib[i], r0, size, lead=(j,)), (x, y, 1 - c)))
        for cp in waits:
            cp.wait_recv()
        cps.finish()

    sibs = _comm_call("grads_pair_exchange_" + tag, body, g4, out_shape, sum(len(p) for p in plan))
    where = _where()
    return {k: _pair_sum(k + "_pair_sum", g, s, where) for k, g, s in zip(names, g4, sibs)}


class _ChipExchange:
    def __init__(self, partial, mult=16):
        self.names = list(partial)
        self.n = len(self.names)
        self.row_sharded = [k in ROW_SHARDED for k in self.names]
        self.half = []
        for k, rs in zip(self.names, self.row_sharded):
            _, r, w = partial[k].shape
            self.half.append((r, w) if rs else (r, w // N_CHIPS))
        self.chunks = [(i, r0, size) for i, (r, cdim) in enumerate(self.half)
                       for r0, size in _row_chunks(r, cdim * 2, mult)]

    def shard(self, src, i, chip, r0, size):
        if self.row_sharded[i]:
            return _rows(src[i], r0, size, lead=(chip,))
        return _rows(_shard_cols(src[i].at[0], False, self.half[i][1], chip), r0, size)

    def out_shape(self):
        return [jax.ShapeDtypeStruct((N_CHIPS - 1,) + s, BF16) for s in self.half]

    def scratch(self):
        n_remote = 3 * len(self.chunks)
        return [pltpu.SemaphoreType.DMA((n_remote,)), pltpu.SemaphoreType.DMA((n_remote,))]

    def _copies(self, src, dst, sems):
        x, y, c, chips = _position()
        for q, (i, r0, size) in enumerate(self.chunks):
            for k, (px, py) in enumerate(chips):
                yield _remote(self.shard(src, i, 2 * px + py, r0, size), _rows(dst[i], r0, size, lead=(k,)),
                              sems[0].at[3 * q + k], sems[1].at[3 * q + k], (px, py, c))

    def start(self, src, dst, sems):
        for cp in self._copies(src, dst, sems):
            cp.start()

    def finish(self, src, dst, sems):
        for cp in self._copies(src, dst, sems):
            cp.wait_recv()
        for cp in self._copies(src, dst, sems):
            cp.wait_send()


def _chip_exchange(partial):
    plan = _ChipExchange(partial)
    n = plan.n

    def body(*refs):
        parts = refs[:n], refs[n:2 * n], refs[2 * n:]
        plan.start(*parts)
        plan.finish(*parts)

    got = pl.pallas_call(
        body, name="grads_chip_exchange", in_specs=[ANY] * n, out_specs=[ANY] * n, out_shape=plan.out_shape(),
        scratch_shapes=plan.scratch(),
    )(*[partial[k] for k in plan.names])
    return dict(zip(plan.names, got))


HBM_SPEC = pl.BlockSpec(memory_space=pltpu.HBM)
SEM_SPEC = pl.BlockSpec(memory_space=pltpu.SEMAPHORE)
SIDE_EFFECT = pltpu.SideEffectType.DATAFLOW_SIDE_EFFECTING


def _chip_exchange_start(tag, partial):
    plan = _ChipExchange(partial)
    n = plan.n
    n_remote = 3 * len(plan.chunks)
    srcs = [pltpu.with_memory_space_constraint(partial[k], pltpu.HBM) for k in plan.names]
    lands = [pltpu.with_memory_space_constraint(lax.empty(s.shape, s.dtype), pltpu.HBM) for s in plan.out_shape()]

    def body(*refs):
        src, land = refs[:n], refs[n:2 * n]
        plan.start(src, land, (refs[2 * n], refs[2 * n + 1]))
        refs[-1][...] = jnp.zeros_like(refs[-1])

    return pl.pallas_call(
        body, name="grads_chip_exchange_start_" + tag,
        out_shape=(pltpu.SemaphoreType.DMA((n_remote,)), pltpu.SemaphoreType.DMA((n_remote,)),
                   *[pltpu.HBM(a.shape, a.dtype) for a in srcs + lands], jax.ShapeDtypeStruct((8, LANES), F32)),
        in_specs=[HBM_SPEC] * (2 * n),
        out_specs=(SEM_SPEC, SEM_SPEC, *[HBM_SPEC] * (2 * n), pl.BlockSpec(memory_space=pltpu.VMEM)),
        input_output_aliases={i: 2 + i for i in range(2 * n)},
        compiler_params=pltpu.CompilerParams(has_side_effects=SIDE_EFFECT),
    )(*srcs, *lands)


def _chip_exchange_wait(tag, partial, flight, after):
    plan = _ChipExchange(partial)
    n = plan.n
    send, recv, thru = flight[0], flight[1], flight[2:2 + 2 * n]

    def body(*refs):
        plan.finish(refs[:n], refs[n:2 * n], (refs[2 * n], refs[2 * n + 1]))

    outs = pl.pallas_call(
        body, name="grads_chip_exchange_wait_" + tag,
        out_shape=tuple(pltpu.HBM(t.shape, t.dtype) for t in thru),
        in_specs=[HBM_SPEC] * (2 * n) + [SEM_SPEC, SEM_SPEC, ANY], out_specs=[HBM_SPEC] * (2 * n),
        input_output_aliases={i: i for i in range(2 * n)},
        compiler_params=pltpu.CompilerParams(has_side_effects=SIDE_EFFECT),
    )(*thru, send, recv, after)
    return dict(zip(plan.names, outs[:n])), dict(zip(plan.names, outs[n:]))


def _finish_reduce(partial, got):
    names = list(partial)
    n = len(names)
    where = _where()
    halves = [_chip_sum(k + "_chip_sum", partial[k], got[k], where, k in ROW_SHARDED) for k in names]
    plan = [_row_chunks(h.shape[1], h.shape[2] * 4) for h in halves]
    out_shape = [jax.ShapeDtypeStruct(h.shape, F32) for h in halves]

    def body(*refs):
        src, dst = refs[:n], refs[n:2 * n]
        cps = _Copies(*refs[2 * n:])
        x, y, c, _ = _position()
        waits = []
        for i in range(n):
            for r0, size in plan[i]:
                waits.append(cps.remote(cps.slot(), _rows(src[i], r0, size, lead=(c,)), _rows(dst[i], r0, size, lead=(c,)),
                                        (x, y, 1 - c)))
        for cp in waits:
            cp.wait_recv()
        cps.finish()

    outs = _comm_call("grads_pair_share", body, halves, out_shape, sum(len(p) for p in plan),
                      aliases={i: i for i in range(n)})
    return {k: o.reshape(2 * o.shape[1], o.shape[2]) for k, o in zip(names, outs)}


def _all_reduce_small(name, v):
    rows, cols = v.shape
    h = rows // 2

    def body(v_ref, o_ref, sib, pair, buf, send, recv):
        x, y, c, chips = _position()
        me = 2 * x + y
        sibling = (x, y, 1 - c)
        cp = _remote(v_ref, sib, send.at[0], recv.at[0], sibling)
        cp.start()
        cp.wait()
        pair[...] = v_ref[...] + sib[...]
        mine = pl.ds(pl.multiple_of(c * h, 8), h)
        buf[me] = pair[mine, :]
        sends = [_remote(pair.at[mine], buf.at[me], send.at[1 + k], recv.at[1 + k], (px, py, c))
                 for k, (px, py) in enumerate(chips)]
        for cp in sends:
            cp.start()
        for k, (px, py) in enumerate(chips):
            _remote(pair.at[mine], buf.at[2 * px + py], send.at[1 + k], recv.at[1 + k], (px, py, c)).wait_recv()
        for cp in sends:
            cp.wait_send()
        o_ref[mine, :] = (buf[0] + buf[1]) + (buf[2] + buf[3])
        cp = _remote(o_ref.at[mine], o_ref.at[mine], send.at[4], recv.at[4], sibling)
        cp.start()
        cp.wait()

    vm = pl.BlockSpec(memory_space=pltpu.VMEM)
    return pl.pallas_call(
        body, name=name, in_specs=[vm], out_specs=vm, out_shape=jax.ShapeDtypeStruct(v.shape, F32),
        scratch_shapes=[pltpu.VMEM((rows, cols), F32), pltpu.VMEM((rows, cols), F32), pltpu.VMEM((N_CHIPS, h, cols), F32),
                        pltpu.SemaphoreType.DMA((5,)), pltpu.SemaphoreType.DMA((5,))],
        compiler_params=pltpu.CompilerParams(vmem_limit_bytes=VMEM_LIMIT),
    )(v)


def _adamw(name, w, g, m, v):
    def fn(wt, gt, mt, vt):
        mt = ADAM_B1 * mt + (1.0 - ADAM_B1) * gt
        vt = ADAM_B2 * vt + (1.0 - ADAM_B2) * (gt * gt)
        m_hat = mt / (1.0 - ADAM_B1 ** ADAM_STEP)
        v_hat = vt / (1.0 - ADAM_B2 ** ADAM_STEP)
        delta = -ADAM_LR * (m_hat / (jnp.sqrt(v_hat) + ADAM_EPS) + ADAM_WD * wt)
        return delta, mt, vt
    n = w.shape[1]
    return _rowwise(name, fn, [w, g, m, v], [], [(n, F32)] * 3, tb=256)


WEIGHTS = ("mem_norm", "a_norm_mix", "a_w_in", "a_w_group", "a_scale", "a_w_mem_kv", "a_w_out", "a_norm_ffn", "a_w_gu",
           "a_w_down", "kv_norm", "w_kv", "b_norm_mix", "b_w_q", "b_w_mem_kv", "b_w_out", "b_norm_ffn", "b_w_gu",
           "b_w_down", "final_norm")
REPLICATED_VECS = ("mem_norm", "kv_norm", "b_norm_mix", "b_norm_ffn", "final_norm")
SHARDED_VECS = ("a_norm_mix", "a_norm_ffn", "a_scale")
D_MODEL = 1024
GROUP_ROWS = 4 * POOL_GROUP * POOL_GROUP // D_MODEL


def _row(v):
    v = v.reshape(1, -1).astype(F32)
    return jnp.pad(v, ((0, 0), (0, D_MODEL - v.shape[1])))


def _pack_small(t):
    rows = [_row(t[k]) for k in REPLICATED_VECS]
    rows.append(_row(jnp.concatenate([t[k].reshape(-1) for k in SHARDED_VECS])))
    rows.append(jnp.zeros((2, D_MODEL), F32))
    rows.append(t["a_w_group"].reshape(GROUP_ROWS, D_MODEL).astype(F32))
    return jnp.concatenate(rows, axis=0)


def _unpack_small(p, like):
    out = {k: p[i, :].reshape(like[k].shape) for i, k in enumerate(REPLICATED_VECS)}
    off = 0
    for k in SHARDED_VECS:
        size = like[k].size
        out[k] = p[len(REPLICATED_VECS), off:off + size].reshape(like[k].shape)
        off += size
    out["a_w_group"] = p[len(REPLICATED_VECS) + 3:, :].reshape(like["a_w_group"].shape)
    return out


def kernel(x, mem, mem_norm, a_norm_mix, a_w_in, a_w_group, a_scale, a_w_mem_kv, a_w_out, a_norm_ffn, a_w_gu, a_w_down, kv_norm, w_kv, b_norm_mix, b_w_q, b_w_mem_kv, b_w_out, b_norm_ffn, b_w_gu, b_w_down, final_norm, loss_target, m_mem_norm, m_a_norm_mix, m_a_w_in, m_a_w_group, m_a_scale, m_a_w_mem_kv, m_a_w_out, m_a_norm_ffn, m_a_w_gu, m_a_w_down, m_kv_norm, m_w_kv, m_b_norm_mix, m_b_w_q, m_b_w_mem_kv, m_b_w_out, m_b_norm_ffn, m_b_w_gu, m_b_w_down, m_final_norm, v_mem_norm, v_a_norm_mix, v_a_w_in, v_a_w_group, v_a_scale, v_a_w_mem_kv, v_a_w_out, v_a_norm_ffn, v_a_w_gu, v_a_w_down, v_kv_norm, v_w_kv, v_b_norm_mix, v_b_w_q, v_b_w_mem_kv, v_b_w_out, v_b_norm_ffn, v_b_w_gu, v_b_w_down, v_final_norm):
    given = dict(locals())
    wl = {k: given[k] for k in WEIGHTS}
    ml = {k: given["m_" + k] for k in WEIGHTS}
    vl = {k: given["v_" + k] for k in WEIGHTS}
    chip = 2 * lax.axis_index("x") + lax.axis_index("y")

    def mat(a):
        return a.reshape(a.shape[-2], a.shape[-1])

    shards = {k: mat(wl[k]).astype(BF16) for k in BIG}
    second = _gather_start({k: shards[k] for k in SECOND})
    full = _gather_weights({k: shards[k] for k in BIG if k not in LATE + SECOND})
    gains = jnp.zeros((16, D_MODEL), F32)
    for i, k in enumerate(SHARDED_VECS):
        part = wl[k].reshape(1, -1)
        width = part.shape[1]
        gains = lax.dynamic_update_slice(gains, part, (i, chip * width))
    gains = _all_reduce_small("gains_all_gather", gains) * 0.5
    w = dict(full)
    for k in REPLICATED_VECS:
        w[k] = wl[k].reshape(1, D_MODEL)
    w["a_norm_mix"], w["a_norm_ffn"] = gains[0:1] + second[-1][0:1, 0:1], gains[1:2]
    w["a_scale"] = gains[2:3, :SB_WIDTH]
    w["a_w_group"] = wl["a_w_group"][0].astype(BF16)

    loss, grad_x, g, partial, got = _local_step(x[0], mem[0], loss_target[0], w, {k: shards[k] for k in LATE},
                                                ({k: shards[k].shape for k in SECOND}, second))

    rest = _pair_sums("late", {k: g[k] for k in BIG if k not in EARLY + POST_SB + MID})
    partial.update(rest)
    got.update(_chip_exchange(rest))
    red = _finish_reduce(partial, got)
    small = jnp.concatenate(
        [_row(g[k]) for k in REPLICATED_VECS] + [_row(g[k]) for k in SHARDED_VECS] + [_row(loss)]
        + [jnp.zeros((7, D_MODEL), F32), g["a_w_group"].reshape(GROUP_ROWS, D_MODEL)], axis=0)
    small = _all_reduce_small("small_grads_all_reduce", small)
    gs = {k: small[i] for i, k in enumerate(REPLICATED_VECS)}
    for i, k in enumerate(SHARDED_VECS):
        width = wl[k].shape[-1]
        gs[k] = lax.dynamic_slice(small[len(REPLICATED_VECS) + i], (chip * width,), (width,))
    gs["a_w_group"] = small[16:]
    total_loss = small[8, 0]

    out_g, out_d, out_m, out_v = {}, {}, {}, {}
    for k in BIG:
        shape = wl[k].shape
        out_g[k] = red[k].reshape(shape)
        d, nm, nv = _adamw(k + "_adamw", mat(wl[k]), red[k], mat(ml[k]), mat(vl[k]))
        out_d[k], out_m[k], out_v[k] = d.reshape(shape), nm.reshape(shape), nv.reshape(shape)
    small_names = REPLICATED_VECS + SHARDED_VECS + ("a_w_group",)
    d, nm, nv = _adamw("small_adamw", _pack_small(wl), _pack_small(gs), _pack_small(ml), _pack_small(vl))
    like = {k: wl[k] for k in small_names}
    for dst, p in ((out_d, d), (out_m, nm), (out_v, nv)):
        dst.update(_unpack_small(p, like))
    for k in small_names:
        out_g[k] = gs[k].reshape(wl[k].shape)

    return (total_loss, grad_x[None], *[out_g[k] for k in WEIGHTS], *[out_d[k] for k in WEIGHTS],
            *[out_m[k] for k in WEIGHTS], *[out_v[k] for k in WEIGHTS])
```

```python
import functools

import jax
import jax.numpy as jnp
from jax import lax
from jax.experimental import pallas as pl
from jax.experimental.pallas import tpu as pltpu

F32 = jnp.float32
BF16 = jnp.bfloat16

HEAD_DIM = 64
SB_WIDTH = 768
MEM_WIDTH = 256
POOL_WINDOWS = (2, 4, 8, 16)
POOL_GROUP = 192
POOL_HALO = 16
EPS = 1e-6
ATT_SCALE = HEAD_DIM ** -0.5
ADAM_LR, ADAM_B1, ADAM_B2, ADAM_EPS, ADAM_WD, ADAM_STEP = 0.001, 0.9, 0.999, 1e-08, 0.01, 10

LANES = 128
SB_TQ, SB_TK = 512, 512
VMEM_LIMIT = 56 * 1024 * 1024
MESH = pl.DeviceIdType.MESH
COPY_BYTES = 512 * 1024
ANY = pl.BlockSpec(memory_space=pl.ANY)


def _params(*sem):
    return pltpu.CompilerParams(dimension_semantics=sem, vmem_limit_bytes=VMEM_LIMIT)


def _tile(n, pref):
    if n <= pref:
        return n
    best = None
    for t in range(LANES, pref + 1, LANES):
        if n % t == 0:
            best = t
    assert best is not None, (n, pref)
    return best


def _row_tile(t, pref):
    if t <= pref:
        return t
    for tb in range(pref - pref % 16, 0, -16):
        if t % tb == 0:
            return tb
    raise ValueError((t, pref))


def _rowwise(name, fn, rows, vecs, row_outs, sum_outs=(), tb=512):
    norm_rows = []
    for r in rows:
        if isinstance(r, tuple):
            arr, (bc, cb) = r
        else:
            arr, (bc, cb) = r, (r.shape[1], 0)
        norm_rows.append((arr, bc, cb))
    t = norm_rows[0][0].shape[0]
    tb = _row_tile(t, tb)
    n_in, n_ro = len(norm_rows) + len(vecs), len(row_outs)

    def body(*refs):
        ins = [r[...] for r in refs[:n_in]]
        outs = fn(*ins)
        if not isinstance(outs, tuple):
            outs = (outs,)
        for o_ref, o in zip(refs[n_in:n_in + n_ro], outs[:n_ro]):
            o_ref[...] = o.astype(o_ref.dtype)
        for s_ref, s in zip(refs[n_in + n_ro:], outs[n_ro:]):
            @pl.when(pl.program_id(0) == 0)
            def _():
                s_ref[...] = jnp.zeros_like(s_ref)
            s_ref[...] += s

    in_specs = [pl.BlockSpec((tb, bc), functools.partial(lambda i, cb: (i, cb), cb=cb)) for _, bc, cb in norm_rows]
    in_specs += [pl.BlockSpec(v.shape, lambda i: (0, 0)) for v in vecs]
    out_specs = [pl.BlockSpec((tb, c), lambda i: (i, 0)) for c, _ in row_outs]
    out_specs += [pl.BlockSpec((1, c), lambda i: (0, 0)) for c in sum_outs]
    out_shape = [jax.ShapeDtypeStruct((t, c), d) for c, d in row_outs]
    out_shape += [jax.ShapeDtypeStruct((1, c), F32) for c in sum_outs]
    res = pl.pallas_call(
        body, name=name, grid=(t // tb,), in_specs=in_specs, out_specs=out_specs, out_shape=out_shape,
        compiler_params=_params("arbitrary"),
    )(*[a for a, _, _ in norm_rows], *vecs)
    return res


def _rms_fwd(name, x, g):
    def fn(xt, gt):
        rstd = lax.rsqrt(jnp.mean(xt * xt, axis=-1, keepdims=True) + EPS)
        return xt * rstd * gt
    return _rowwise(name, fn, [x], [g], [(x.shape[1], BF16)])[0]


def _rms_bwd(name, x, g, dh, dres=None, want_dx=True):
    has_res = dres is not None

    def fn(*a):
        if has_res:
            xt, dht, drt, gt = a
        else:
            xt, dht, gt = a
        rstd = lax.rsqrt(jnp.mean(xt * xt, axis=-1, keepdims=True) + EPS)
        xhat = xt * rstd
        dht = dht.astype(F32)
        dg = jnp.sum(dht * xhat, axis=0, keepdims=True)
        if not want_dx:
            return (dg,)
        dxhat = dht * gt
        dx = rstd * (dxhat - xhat * jnp.mean(dxhat * xhat, axis=-1, keepdims=True))
        if has_res:
            dx = dx + drt
        return dx, dx, dg

    d = x.shape[1]
    rows = [x, dh] + ([dres] if has_res else [])
    outs = [(d, F32), (d, BF16)] if want_dx else []
    return _rowwise(name, fn, rows, [g], outs, [d])


def _rms_fwd2(name, x, g1, g2):
    def fn(xt, g1t, g2t):
        xn = xt * lax.rsqrt(jnp.mean(xt * xt, axis=-1, keepdims=True) + EPS)
        return xn * g1t, xn * g2t
    return _rowwise(name, fn, [x], [g1, g2], [(x.shape[1], BF16)] * 2)


def _rms_bwd2(name, x, g1, dh1, g2, dh2, dres):
    def fn(xt, d1, d2, drt, g1t, g2t):
        rstd = lax.rsqrt(jnp.mean(xt * xt, axis=-1, keepdims=True) + EPS)
        xhat = xt * rstd
        d1, d2 = d1.astype(F32), d2.astype(F32)
        dxhat = d1 * g1t + d2 * g2t
        dx = rstd * (dxhat - xhat * jnp.mean(dxhat * xhat, axis=-1, keepdims=True)) + drt
        return dx, dx, jnp.sum(d1 * xhat, axis=0, keepdims=True), jnp.sum(d2 * xhat, axis=0, keepdims=True)

    d = x.shape[1]
    return _rowwise(name, fn, [x, dh1, dh2, dres], [g1, g2], [(d, F32), (d, BF16)], [d, d])


_DOT_DIMS = {"nn": ((1,), (0,)), "nt": ((1,), (1,)), "tn": ((0,), (0,))}


def _mm(name, a, b, mode, out_dtype, res=None, tm=512, tn=512):
    if mode == "nn":
        (m, k), (k2, n) = a.shape, b.shape
    elif mode == "nt":
        (m, k), (n, k2) = a.shape, b.shape
    else:
        (k, m), (k2, n) = a.shape, b.shape
    assert k == k2, (name, a.shape, b.shape)
    tm, tn = _tile(m, tm), _tile(n, tn)
    dims = (_DOT_DIMS[mode], ((), ()))
    has_res = res is not None

    def body(a_ref, b_ref, *rest):
        acc = lax.dot_general(a_ref[...], b_ref[...], dims, preferred_element_type=F32)
        if has_res:
            acc = acc + rest[0][...]
        rest[-1][...] = acc.astype(out_dtype)

    a_spec = pl.BlockSpec((k, tm), lambda i, j: (0, i)) if mode == "tn" else pl.BlockSpec((tm, k), lambda i, j: (i, 0))
    b_spec = pl.BlockSpec((tn, k), lambda i, j: (j, 0)) if mode == "nt" else pl.BlockSpec((k, tn), lambda i, j: (0, j))
    o_spec = pl.BlockSpec((tm, tn), lambda i, j: (i, j))
    in_specs, args = [a_spec, b_spec], [a, b]
    if has_res:
        in_specs.append(o_spec)
        args.append(res)
    return pl.pallas_call(
        body, name=name, grid=(m // tm, n // tn), in_specs=in_specs, out_specs=o_spec,
        out_shape=jax.ShapeDtypeStruct((m, n), out_dtype), compiler_params=_params("parallel", "arbitrary"),
    )(*args)


def _pool(name, u, reverse, tb=512):
    t = u.shape[0]
    tb = min(tb, t)
    nt = t // tb
    c = SB_WIDTH
    hpb = tb // POOL_HALO

    def body(cur_ref, halo_ref, o_ref):
        i = pl.program_id(0)
        cur = cur_ref[...].astype(F32)
        edge = (i == nt - 1) if reverse else (i == 0)
        halo = jnp.where(edge, 0.0, halo_ref[...].astype(F32))
        col = lax.broadcasted_iota(jnp.int32, (tb + POOL_HALO, c), 1)
        row = lax.broadcasted_iota(jnp.int32, (tb + POOL_HALO, c), 0)
        wcol = jnp.where(col < POOL_GROUP, 2, jnp.where(col < 2 * POOL_GROUP, 4, jnp.where(col < 3 * POOL_GROUP, 8, 16)))
        n = tb + POOL_HALO
        if reverse:
            ext = jnp.concatenate([cur, halo], axis=0)
            tpos = i * tb + row
            ext = ext / jnp.minimum(tpos + 1, wcol).astype(F32)
            shift = lambda a, k: pltpu.roll(a, n - k, 0)
        else:
            ext = jnp.concatenate([halo, cur], axis=0)
            shift = lambda a, k: pltpu.roll(a, k, 0)
        s2 = ext + shift(ext, 1)
        s4 = s2 + shift(s2, 2)
        s8 = s4 + shift(s4, 4)
        s16 = s8 + shift(s8, 8)
        win = jnp.where(wcol == 2, s2, jnp.where(wcol == 4, s4, jnp.where(wcol == 8, s8, s16)))
        if reverse:
            out = win[:tb] - cur
        else:
            tpos = i * tb + row[POOL_HALO:] - POOL_HALO
            out = win[POOL_HALO:] / jnp.minimum(tpos + 1, wcol[POOL_HALO:]).astype(F32) - cur
        o_ref[...] = out.astype(o_ref.dtype)

    if reverse:
        halo_map = lambda i: (jnp.minimum((i + 1) * hpb, t // POOL_HALO - 1), 0)
    else:
        halo_map = lambda i: (jnp.maximum(i * hpb - 1, 0), 0)
    return pl.pallas_call(
        body, name=name, grid=(nt,),
        in_specs=[pl.BlockSpec((tb, c), lambda i: (i, 0)), pl.BlockSpec((POOL_HALO, c), halo_map)],
        out_specs=pl.BlockSpec((tb, c), lambda i: (i, 0)),
        out_shape=jax.ShapeDtypeStruct((t, c), BF16), compiler_params=_params("arbitrary"),
    )(u, u)


def _head_masks(shape):
    lane = lax.broadcasted_iota(jnp.int32, shape, 1)
    return lane < HEAD_DIM, lane >= HEAD_DIM


def _pick(mask, a):
    return jnp.where(mask, a, jnp.zeros_like(a))


def _dot(a, b, mode):
    return lax.dot_general(a, b, (_DOT_DIMS[mode], ((), ())), preferred_element_type=F32)


def _dot_tri(a, tri, suffix):
    h = a.shape[1] // 2
    lo, hi = a[:, :h], a[:, h:]
    s_lo, s_hi = jnp.sum(lo, axis=1, keepdims=True), jnp.sum(hi, axis=1, keepdims=True)
    p_lo, p_hi = _dot(lo.astype(BF16), tri, "nn"), _dot(hi.astype(BF16), tri, "nn")
    if suffix:
        p_lo = p_lo + s_hi
    else:
        p_hi = p_hi + s_lo
    return jnp.concatenate([p_lo, p_hi], axis=1), s_lo + s_hi


def _log_gates(z):
    nz = -z
    l = jnp.log(1.0 + jnp.exp(jnp.minimum(z, nz)))
    ln = jnp.minimum(nz, 0.0) - l
    return ln, z + ln


def _sb_blocks(s, tq, tk):
    tq, tk = min(tq, s), min(tk, s)
    assert tq == tk and s % tk == 0 and tk % 64 == 0, (s, tq, tk)
    return tq, tk, tk // 2


def _strict_triangle(n, pred):
    return pred(lax.broadcasted_iota(jnp.int32, (n, n), 0), lax.broadcasted_iota(jnp.int32, (n, n), 1)).astype(BF16)


def _sb_fwd(proj, kv, late_shards, tq=SB_TQ, tk=SB_TK):
    s = proj.shape[0]
    tq, tk, th = _sb_blocks(s, tq, tk)
    npair = SB_WIDTH // LANES
    gather = _Gather({k: v.shape for k, v in late_shards.items()})
    ng = gather.n

    def body(q_ref, k_ref, v_ref, *rest):
        o_ref, tot_ref = rest[ng:ng + 2]
        comm = rest[:ng], rest[ng + 2:2 * ng + 2], rest[2 * ng + 2:2 * ng + 5], rest[2 * ng + 5:]

        @pl.when(pl.program_id(0) == 0)
        def _():
            gather.start(*comm)

        tri_gt = _strict_triangle(tk, lambda j, s_: j > s_)
        tri_gt_h = _strict_triangle(th, lambda j, s_: j > s_)
        seen = lax.broadcasted_iota(jnp.int32, (tq, th), 1) < lax.broadcasted_iota(jnp.int32, (tq, th), 0)
        m_a, m_b = _head_masks((tq, LANES))

        def block(qh, k2, v2, lane_mask, carry, acc, mask, tri):
            ln_full, lsz = _log_gates(_dot(qh, k2, "nt"))
            ln = ln_full if mask is None else jnp.where(mask, ln_full, 0.0)
            w = jnp.exp(lsz + _dot(ln.astype(BF16), tri, "nn"))
            if mask is not None:
                w = jnp.where(mask, w, 0.0)
            acc = acc + jnp.exp(carry) * _dot(w.astype(BF16), _pick(lane_mask[:v2.shape[0]], v2), "nn")
            return carry + jnp.sum(ln, axis=1, keepdims=True), acc

        def q_block(qi, _):
            q0 = pl.multiple_of(qi * tq, tq)
            q2 = q_ref[pl.ds(q0, tq), :] * ATT_SCALE
            qa, qb = _pick(m_a, q2), _pick(m_b, q2)

            def both(qa, qb, k0, size, ca, cb, acc, mask, tri):
                k2 = k_ref[pl.ds(k0, size), :]
                v2 = v_ref[pl.ds(k0, size), :]
                ca, acc = block(qa, k2, v2, m_a, ca, acc, mask, tri)
                cb, acc = block(qb, k2, v2, m_b, cb, acc, mask, tri)
                return ca, cb, acc

            zero_c = jnp.zeros((th, 1), F32)
            zero_o = jnp.zeros((th, LANES), F32)
            ca, cb, acc = both(qa[th:], qb[th:], pl.multiple_of(q0 + th, th), th, zero_c, zero_c, zero_o, seen[:th], tri_gt_h)
            ca, cb = jnp.concatenate([zero_c, ca], axis=0), jnp.concatenate([zero_c, cb], axis=0)
            carry = both(qa, qb, q0, th, ca, cb, jnp.concatenate([zero_o, acc], axis=0), seen, tri_gt_h)

            def k_block(k0, carry):
                return both(qa, qb, pl.multiple_of(k0, tk), tk, *carry, None, tri_gt)

            def two_blocks(step, carry):
                k0 = q0 - (2 * step + 1) * tk
                return k_block(k0 - tk, k_block(k0, carry))

            carry = lax.fori_loop(0, qi // 2, two_blocks, carry)
            ca, cb, acc = lax.fori_loop(0, qi % 2, lambda _, c_: k_block(0, c_), carry)
            o_ref[pl.ds(q0, tq), :] = acc.astype(o_ref.dtype)
            tot_ref[0, pl.ds(q0, tq), :] = jnp.broadcast_to(ca, (tq, LANES))
            tot_ref[1, pl.ds(q0, tq), :] = jnp.broadcast_to(cb, (tq, LANES))
            return 0

        lax.fori_loop(0, s // tq, q_block, 0)

        @pl.when(pl.program_id(0) == npair - 1)
        def _():
            gather.finish(*comm)

    outs = pl.pallas_call(
        body, name="sb_fwd", grid=(npair,),
        in_specs=[pl.BlockSpec((s, LANES), lambda p: (0, p)), pl.BlockSpec((s, LANES), lambda p: (0, p)),
                  pl.BlockSpec((s, LANES), lambda p: (0, npair + p))] + [ANY] * ng,
        out_specs=[pl.BlockSpec((s, LANES), lambda p: (0, p)), pl.BlockSpec((None, 2, s, LANES), lambda p: (p, 0, 0, 0))]
        + [ANY] * ng,
        out_shape=[jax.ShapeDtypeStruct((s, SB_WIDTH), BF16), jax.ShapeDtypeStruct((npair, 2, s, LANES), F32)]
        + gather.out_shape(),
        scratch_shapes=gather.scratch(), compiler_params=_params("arbitrary"),
    )(proj, kv, kv, *[late_shards[k] for k in gather.names])
    return outs[0], outs[1], dict(zip(gather.names, outs[2:]))


def _sb_bwd(proj, kv, dcat, tot, early_partial, tq=SB_TQ, tk=SB_TK):
    s = proj.shape[0]
    tq, tk, th = _sb_blocks(s, tq, tk)
    npair = SB_WIDTH // LANES
    exchange = _ChipExchange(early_partial)
    ne = exchange.n

    def body(q_ref, k_ref, v_ref, do_ref, tot_ref, *rest):
        dq_ref, dk_ref, dv_ref = rest[ne:ne + 3]
        dk_acc, dv_acc = rest[2 * ne + 3:2 * ne + 5]
        comm = rest[:ne], rest[ne + 3:2 * ne + 3], rest[2 * ne + 5:]

        @pl.when(pl.program_id(0) == 0)
        def _():
            exchange.start(*comm)

        tris = (_strict_triangle(tk // 2, lambda j, s_: j > s_), _strict_triangle(tk // 2, lambda j, s_: j < s_))
        tris_h = (_strict_triangle(th // 2, lambda j, s_: j > s_), _strict_triangle(th // 2, lambda j, s_: j < s_))
        seen = lax.broadcasted_iota(jnp.int32, (tq, th), 1) < lax.broadcasted_iota(jnp.int32, (tq, th), 0)
        m_a, m_b = _head_masks((tq, LANES))
        dk_acc[...] = jnp.zeros_like(dk_acc)
        dv_acc[...] = jnp.zeros_like(dv_acc)

        def block(qh, doh, k2, v2, lane_mask, tot_h, carry, mask, tri):
            c_ln, c_d, dq = carry
            ln_full, lsz = _log_gates(_dot(qh, k2, "nt"))
            ln = ln_full if mask is None else jnp.where(mask, ln_full, 0.0)
            inside, total = _dot_tri(ln, tri[0], True)
            c_ln = c_ln + total
            w = jnp.exp(lsz + ((tot_h - c_ln) + inside))
            if mask is not None:
                w = jnp.where(mask, w, 0.0)
            dlw = _dot(doh, v2, "nt") * w
            before, d_total = _dot_tri(dlw, tri[1], False)
            dz = dlw * jnp.exp(ln_full) - (before + c_d) * jnp.exp(lsz)
            if mask is not None:
                dz = jnp.where(mask, dz, 0.0)
            dz = dz.astype(BF16)
            dq = dq + _dot(dz, _pick(lane_mask[:k2.shape[0]], k2), "nn")
            dk = _dot(dz, qh, "tn")
            dv = _dot(w.astype(BF16), doh, "tn")
            carry = (c_ln, c_d + d_total, dq)
            return carry, dk, dv

        def q_block(qi, _):
            q0 = pl.multiple_of(qi * tq, tq)
            q2 = q_ref[pl.ds(q0, tq), :] * ATT_SCALE
            do2 = do_ref[pl.ds(q0, tq), :]
            qa, qb = _pick(m_a, q2), _pick(m_b, q2)
            doa, dob = _pick(m_a, do2), _pick(m_b, do2)
            tot_a = tot_ref[0, pl.ds(q0, tq), 0:1]
            tot_b = tot_ref[1, pl.ds(q0, tq), 0:1]
            zero_c = jnp.zeros((tq, 1), F32)
            zero_q = jnp.zeros((tq, LANES), F32)

            def both(rows, k0, size, ca, cb, mask, tri):
                k2 = k_ref[pl.ds(k0, size), :]
                v2 = v_ref[pl.ds(k0, size), :]
                ca, dka, dva = block(qa[rows], doa[rows], k2, v2, m_a, tot_a[rows], ca, mask, tri)
                cb, dkb, dvb = block(qb[rows], dob[rows], k2, v2, m_b, tot_b[rows], cb, mask, tri)
                dk_acc[pl.ds(k0, size), :] += dka + dkb
                dv_acc[pl.ds(k0, size), :] += dva + dvb
                return ca, cb

            def k_block(k0, carry):
                return both(slice(None), pl.multiple_of(k0, tk), tk, carry[0], carry[1], None, tris)

            def two_blocks(step, carry):
                return k_block((2 * step + 1) * tk, k_block(2 * step * tk, carry))

            init = ((zero_c, zero_c, zero_q), (zero_c, zero_c, zero_q))
            carry = lax.fori_loop(0, qi // 2, two_blocks, init)
            ca, cb = lax.fori_loop(0, qi % 2, lambda _, c_: k_block(q0 - tk, c_), carry)
            ca, cb = both(slice(None), q0, th, ca, cb, seen, tris_h)
            late = slice(th, tq)
            la, lb = both(late, pl.multiple_of(q0 + th, th), th, tuple(t[late] for t in ca), tuple(t[late] for t in cb),
                          seen[:th], tris_h)
            dq = jnp.concatenate([ca[2][:th] + cb[2][:th], la[2] + lb[2]], axis=0)
            dq_ref[pl.ds(q0, tq), :] = (dq * ATT_SCALE).astype(dq_ref.dtype)
            return 0

        lax.fori_loop(0, s // tq, q_block, 0)
        dk_ref[...] = dk_acc[...].astype(dk_ref.dtype)
        dv_ref[...] = dv_acc[...].astype(dv_ref.dtype)

        @pl.when(pl.program_id(0) == npair - 1)
        def _():
            exchange.finish(*comm)

    col = lambda off: pl.BlockSpec((s, LANES), functools.partial(lambda p, off: (0, off + p), off=off))
    outs = pl.pallas_call(
        body, name="sb_bwd", grid=(npair,),
        in_specs=[col(0), col(0), col(npair), col(0), pl.BlockSpec((None, 2, s, LANES), lambda p: (p, 0, 0, 0))]
        + [ANY] * ne,
        out_specs=[col(0), col(0), col(0)] + [ANY] * ne,
        out_shape=[jax.ShapeDtypeStruct((s, SB_WIDTH), BF16)] * 3 + exchange.out_shape(),
        scratch_shapes=[pltpu.VMEM((s, LANES), F32), pltpu.VMEM((s, LANES), F32)] + exchange.scratch(),
        compiler_params=_params("arbitrary"),
    )(proj, kv, kv, dcat, tot, *[early_partial[k] for k in exchange.names])
    dq, dk, dv = outs[:3]
    return dq, jnp.concatenate([dk, dv], axis=1), dict(zip(exchange.names, outs[3:]))


def _mem_fwd(name, proj, mkv, tq=512):
    s = proj.shape[0]
    tq = min(tq, s)
    qblk = SB_WIDTH // MEM_WIDTH

    def body(q_ref, kv_ref, o_ref):
        m_a, m_b = _head_masks((tq, LANES))
        mk_a, mk_b = _head_masks((kv_ref.shape[0], LANES))
        for p in range(MEM_WIDTH // LANES):
            q2 = q_ref[:, p * LANES:(p + 1) * LANES].astype(BF16)
            k2 = kv_ref[:, p * LANES:(p + 1) * LANES]
            v2 = kv_ref[:, MEM_WIDTH + p * LANES:MEM_WIDTH + (p + 1) * LANES]
            acc = jnp.zeros((tq, LANES), F32)
            for mq, mk in ((m_a, mk_a), (m_b, mk_b)):
                logits = _dot(_pick(mq, q2), k2, "nt") * ATT_SCALE
                e = jnp.exp(logits - jnp.max(logits, axis=-1, keepdims=True))
                prob = e / jnp.sum(e, axis=-1, keepdims=True)
                acc = acc + _dot(prob.astype(BF16), _pick(mk, v2), "nn")
            o_ref[:, p * LANES:(p + 1) * LANES] = acc.astype(o_ref.dtype)

    return pl.pallas_call(
        body, name=name, grid=(s // tq,),
        in_specs=[pl.BlockSpec((tq, MEM_WIDTH), lambda i: (i, qblk)), pl.BlockSpec(mkv.shape, lambda i: (0, 0))],
        out_specs=pl.BlockSpec((tq, MEM_WIDTH), lambda i: (i, 0)),
        out_shape=jax.ShapeDtypeStruct((s, MEM_WIDTH), BF16), compiler_params=_params("arbitrary"),
    )(proj, mkv)


def _mem_bwd(name, proj, mkv, dcat, tq=512):
    s = proj.shape[0]
    tq = min(tq, s)
    qblk = SB_WIDTH // MEM_WIDTH

    def body(q_ref, kv_ref, do_ref, dq_ref, dkv_ref):
        @pl.when(pl.program_id(0) == 0)
        def _():
            dkv_ref[...] = jnp.zeros_like(dkv_ref)

        m_a, m_b = _head_masks((tq, LANES))
        for p in range(MEM_WIDTH // LANES):
            ksl = slice(p * LANES, (p + 1) * LANES)
            vsl = slice(MEM_WIDTH + p * LANES, MEM_WIDTH + (p + 1) * LANES)
            q2, do2 = q_ref[:, ksl].astype(BF16), do_ref[:, ksl]
            k2, v2 = kv_ref[:, ksl], kv_ref[:, vsl]
            mk_a, mk_b = _head_masks(k2.shape)
            dq = jnp.zeros((tq, LANES), F32)
            dk = jnp.zeros(k2.shape, F32)
            dv = jnp.zeros(k2.shape, F32)
            for mq, mk in ((m_a, mk_a), (m_b, mk_b)):
                qh, doh = _pick(mq, q2), _pick(mq, do2)
                logits = _dot(qh, k2, "nt") * ATT_SCALE
                e = jnp.exp(logits - jnp.max(logits, axis=-1, keepdims=True))
                prob = e / jnp.sum(e, axis=-1, keepdims=True)
                dp = _dot(doh, v2, "nt")
                ds = prob * (dp - jnp.sum(dp * prob, axis=-1, keepdims=True)) * ATT_SCALE
                ds = ds.astype(BF16)
                dq = dq + _dot(ds, _pick(mk, k2), "nn")
                dk = dk + _dot(ds, qh, "tn")
                dv = dv + _dot(prob.astype(BF16), doh, "tn")
            dq_ref[:, ksl] = dq.astype(dq_ref.dtype)
            dkv_ref[:, ksl] += dk
            dkv_ref[:, vsl] += dv

    return pl.pallas_call(
        body, name=name, grid=(s // tq,),
        in_specs=[pl.BlockSpec((tq, MEM_WIDTH), lambda i: (i, qblk)), pl.BlockSpec(mkv.shape, lambda i: (0, 0)),
                  pl.BlockSpec((tq, MEM_WIDTH), lambda i: (i, qblk))],
        out_specs=[pl.BlockSpec((tq, MEM_WIDTH), lambda i: (i, 0)), pl.BlockSpec(mkv.shape, lambda i: (0, 0))],
        out_shape=[jax.ShapeDtypeStruct((s, MEM_WIDTH), BF16), jax.ShapeDtypeStruct(mkv.shape, F32)],
        compiler_params=_params("arbitrary"),
    )(proj, mkv, dcat)


def _gu_swiglu(name, h, w_gu, tn=256):
    t, d = h.shape
    f = w_gu.shape[1] // 2
    nb = f // tn

    def body(h_ref, wg_ref, wu_ref, g_ref, u_ref, a_ref):
        hh = h_ref[...]
        g = _dot(hh, wg_ref[...], "nn")
        u = _dot(hh, wu_ref[...], "nn")
        g_ref[...] = g.astype(BF16)
        u_ref[...] = u.astype(BF16)
        a_ref[...] = (g * jax.nn.sigmoid(g) * u).astype(BF16)

    out = pl.BlockSpec((t, tn), lambda j: (0, j))
    return pl.pallas_call(
        body, name=name, grid=(nb,),
        in_specs=[pl.BlockSpec((t, d), lambda j: (0, 0)), pl.BlockSpec((d, tn), lambda j: (0, j)),
                  pl.BlockSpec((d, tn), lambda j: (0, nb + j))],
        out_specs=[out, out, out], out_shape=[jax.ShapeDtypeStruct((t, f), BF16)] * 3,
        compiler_params=_params("arbitrary"),
    )(h, w_gu, w_gu)


def _down_dx_swiglu(name, dout_bf, w_down, gate, up, tm=256):
    t, d = dout_bf.shape
    f = w_down.shape[0]

    def body(do_ref, w_ref, g_ref, u_ref, o_ref):
        dact = _dot(do_ref[...], w_ref[...], "nt")
        g, u = g_ref[...].astype(F32), u_ref[...].astype(F32)
        sg = jax.nn.sigmoid(g)
        silu = g * sg
        o_ref[:, :f] = (dact * u * (sg + silu * (1.0 - sg))).astype(BF16)
        o_ref[:, f:] = (dact * silu).astype(BF16)

    row = lambda c: pl.BlockSpec((tm, c), lambda i: (i, 0))
    return pl.pallas_call(
        body, name=name, grid=(t // tm,),
        in_specs=[row(d), pl.BlockSpec((f, d), lambda i: (0, 0)), row(f), row(f)],
        out_specs=row(2 * f), out_shape=jax.ShapeDtypeStruct((t, 2 * f), BF16), compiler_params=_params("arbitrary"),
    )(dout_bf, w_down, gate, up)


def _ffn_fwd(tag, x, norm, w_gu, w_down):
    h = _rms_fwd(tag + "_ffn_norm", x, norm)
    gate, up, act = _gu_swiglu(tag + "_gu", h, w_gu)
    out = _mm(tag + "_down", act, w_down, "nn", F32, res=x, tn=1024)
    return out, (h, gate, up, act)


def _ffn_bwd(tag, x, norm, w_gu, w_down, saved, dout, dout_bf):
    h, gate, up, act = saved
    g_down = _mm(tag + "_down_dw", act, dout_bf, "tn", BF16, tm=256, tn=1024)
    dgu = _down_dx_swiglu(tag + "_down_dx", dout_bf, w_down, gate, up)
    g_gu = _mm(tag + "_gu_dw", h, dgu, "tn", BF16, tm=1024)
    dh = _mm(tag + "_gu_dx", dgu, w_gu, "nt", F32, tn=1024)
    dx, dx_bf, g_norm = _rms_bwd(tag + "_ffn_norm_bwd", x, norm, dh, dres=dout)
    return dx, dx_bf, g_gu, g_down, g_norm


def _mem_kv(tag, mem_n, w_mem_kv):
    return _mm(tag + "_memkv", mem_n, w_mem_kv, "nn", BF16)


def _mem_kv_bwd(tag, mem, mem_norm, mem_n, w_mem_kv, dmkv):
    dmkv = dmkv.astype(BF16)
    g_w = _mm(tag + "_memkv_dw", mem_n, dmkv, "tn", BF16)
    dmem_n = _mm(tag + "_memkv_dx", dmkv, w_mem_kv, "nt", F32)
    (g_norm,) = _rms_bwd(tag + "_memnorm_bwd", mem, mem_norm, dmem_n, want_dx=False)
    return g_w, g_norm


def _block_diag(w_group):
    z = jnp.zeros((POOL_GROUP, POOL_GROUP), w_group.dtype)
    return jnp.concatenate(
        [jnp.concatenate([w_group[g] if h == g else z for h in range(4)], axis=1) for g in range(4)], axis=0)


def _local_step(x, mem, target, w, late_shards, second):
    g = {}
    w = dict(w)
    mem_n = _rms_fwd("mem_norm", mem, w["mem_norm"])
    h_a = _rms_fwd("a_mix_norm", x, w["a_norm_mix"])
    proj_a = _mm("a_in", h_a, w["a_w_in"], "nn", F32, tn=1024)
    pooled = _pool("a_pool", proj_a, reverse=False)
    w_bd = _block_diag(w["a_w_group"])
    g_pre = _mm("a_group", pooled, w_bd, "nn", BF16, tn=768)
    mkv_a = _mem_kv("a", mem_n, w["a_w_mem_kv"])
    mem_a = _mem_fwd("a_mem_attn", proj_a, mkv_a)
    cat_a = _rowwise("a_cat", lambda gp, mo, sc: jnp.concatenate([gp.astype(F32) * sc, mo.astype(F32)], axis=1),
                     [g_pre, mem_a], [w["a_scale"]], [(1024, BF16)])[0]
    x1 = _mm("a_out", cat_a, w["a_w_out"], "nn", F32, res=x, tn=1024)
    w.update(_gather_finish(second[0], second[1], x1))
    x2, ffn_a = _ffn_fwd("a", x1, w["a_norm_ffn"], w["a_w_gu"], w["a_w_down"])
    h_k, h_b = _rms_fwd2("x2_norms", x2, w["kv_norm"], w["b_norm_mix"])
    kv = _mm("kv_proj", h_k, w["w_kv"], "nn", BF16, tn=1536)
    proj_b = _mm("b_q", h_b, w["b_w_q"], "nn", BF16, tn=1024)
    sb_out, tot, late = _sb_fwd(proj_b, kv, late_shards)
    w.update(late)
    mkv_b = _mem_kv("b", mem_n, w["b_w_mem_kv"])
    mem_b = _mem_fwd("b_mem_attn", proj_b, mkv_b)
    cat_b = jnp.concatenate([sb_out, mem_b], axis=1)
    x3 = _mm("b_out", cat_b, w["b_w_out"], "nn", F32, res=x2, tn=1024)
    x4, ffn_b = _ffn_fwd("b", x3, w["b_norm_ffn"], w["b_w_gu"], w["b_w_down"])

    d = x.shape[1]

    def head(xt, tt, gt):
        rstd = lax.rsqrt(jnp.mean(xt * xt, axis=-1, keepdims=True) + EPS)
        xhat = xt * rstd
        err = xhat * gt - tt
        loss = 0.5 * jnp.sum(jnp.sum(err * err, axis=1, keepdims=True), axis=0, keepdims=True) / d
        dy = err / d
        dxhat = dy * gt
        dx = rstd * (dxhat - xhat * jnp.mean(dxhat * xhat, axis=-1, keepdims=True))
        return dx, dx, jnp.sum(dy * xhat, axis=0, keepdims=True), jnp.broadcast_to(loss, (1, LANES))

    dx4, dx4_bf, g["final_norm"], loss = _rowwise(
        "loss_head", head, [x4, target], [w["final_norm"]], [(d, F32), (d, BF16)], [d, LANES])

    dx3, dx3_bf, g["b_w_gu"], g["b_w_down"], g["b_norm_ffn"] = _ffn_bwd(
        "b", x3, w["b_norm_ffn"], w["b_w_gu"], w["b_w_down"], ffn_b, dx4, dx4_bf)
    dcat_b = _mm("b_out_dx", dx3_bf, w["b_w_out"], "nt", BF16, tn=1024)
    g["b_w_out"] = _mm("b_out_dw", cat_b, dx3_bf, "tn", BF16, tm=1024, tn=1024)
    early_partial = _pair_sums("early", {k: g.pop(k) for k in EARLY})
    dq_sb, dkv, early_got = _sb_bwd(proj_b, kv, dcat_b, tot, early_partial)
    dq_mem_b, dmkv_b = _mem_bwd("b_mem_attn_bwd", proj_b, mkv_b, dcat_b)
    dproj_b = jnp.concatenate([dq_sb, dq_mem_b], axis=1)
    g["b_w_q"] = _mm("b_q_dw", h_b, dproj_b, "tn", BF16, tm=1024, tn=1024)
    dh_b = _mm("b_q_dx", dproj_b, w["b_w_q"], "nt", F32, tn=1024)
    g["b_w_mem_kv"], g_memnorm_b = _mem_kv_bwd("b", mem, w["mem_norm"], mem_n, w["b_w_mem_kv"], dmkv_b)
    g["w_kv"] = _mm("kv_proj_dw", h_k, dkv, "tn", BF16, tm=1024)
    dh_k = _mm("kv_proj_dx", dkv, w["w_kv"], "nt", F32, tn=1024)
    dx2, dx2_bf, g["kv_norm"], g["b_norm_mix"] = _rms_bwd2(
        "x2_norms_bwd", x2, w["kv_norm"], dh_k, w["b_norm_mix"], dh_b, dx3)
    post_partial = _pair_sums("post_sb", {k: g.pop(k) for k in POST_SB})
    post_flight = _chip_exchange_start("post_sb", post_partial)
    norm_ffn_a = w["a_norm_ffn"] + post_flight[-1][0:1, 0:1]

    dx1, dx1_bf, g["a_w_gu"], g["a_w_down"], g["a_norm_ffn"] = _ffn_bwd(
        "a", x1, norm_ffn_a, w["a_w_gu"], w["a_w_down"], ffn_a, dx2, dx2_bf)
    mid_partial = _pair_sums("mid", {k: g.pop(k) for k in MID})
    mid_flight = _chip_exchange_start("mid", mid_partial)
    scale_a = w["a_scale"] + mid_flight[-1][0:1, 0:1]
    dcat_a = _mm("a_out_dx", dx1_bf, w["a_w_out"], "nt", BF16, tn=1024)
    g["a_w_out"] = _mm("a_out_dw", cat_a, dx1_bf, "tn", BF16, tm=1024, tn=1024)

    def scale_bwd(dc, gp, sc):
        dc, gp = dc.astype(F32), gp.astype(F32)
        return dc * sc, jnp.sum(dc * gp, axis=0, keepdims=True)

    dg_pre, g["a_scale"] = _rowwise("a_scale_bwd", scale_bwd, [(dcat_a, (SB_WIDTH, 0)), g_pre], [scale_a],
                                    [(SB_WIDTH, BF16)], [SB_WIDTH])
    g_bd = _mm("a_group_dw", pooled, dg_pre, "tn", F32, tm=768, tn=768)
    g["a_w_group"] = jnp.stack([g_bd[i * POOL_GROUP:(i + 1) * POOL_GROUP, i * POOL_GROUP:(i + 1) * POOL_GROUP]
                                for i in range(4)])
    dpooled = _mm("a_group_dx", dg_pre, w_bd, "nt", F32, tn=768)
    du_pool = _pool("a_pool_bwd", dpooled, reverse=True)
    dq_mem_a, dmkv_a = _mem_bwd("a_mem_attn_bwd", proj_a, mkv_a, dcat_a)
    dproj_a = jnp.concatenate([du_pool, dq_mem_a], axis=1)
    g["a_w_in"] = _mm("a_in_dw", h_a, dproj_a, "tn", BF16, tm=1024, tn=1024)
    dh_a = _mm("a_in_dx", dproj_a, w["a_w_in"], "nt", F32, tn=1024)
    grad_x, _, g["a_norm_mix"] = _rms_bwd("a_mix_norm_bwd", x, w["a_norm_mix"], dh_a, dres=dx1)
    g["a_w_mem_kv"], g_memnorm_a = _mem_kv_bwd("a", mem, w["mem_norm"], mem_n, w["a_w_mem_kv"], dmkv_a)
    g["mem_norm"] = g_memnorm_a + g_memnorm_b
    post_partial, post_got = _chip_exchange_wait("post_sb", post_partial, post_flight, g["a_norm_mix"])
    mid_partial, mid_got = _chip_exchange_wait("mid", mid_partial, mid_flight, g["a_norm_mix"])
    return (loss, grad_x, g, {**early_partial, **post_partial, **mid_partial}, {**early_got, **post_got, **mid_got})


ROW_SHARDED = ("a_w_in", "a_w_mem_kv", "a_w_out", "a_w_down", "b_w_q", "b_w_mem_kv", "b_w_out", "b_w_down")
COL_SHARDED = ("a_w_gu", "w_kv", "b_w_gu")
BIG = ("a_w_in", "a_w_mem_kv", "a_w_out", "a_w_gu", "a_w_down", "w_kv", "b_w_q", "b_w_mem_kv", "b_w_out", "b_w_gu",
       "b_w_down")
LATE = ("b_w_mem_kv", "b_w_out", "b_w_gu", "b_w_down")
EARLY = ("b_w_gu", "b_w_down", "b_w_out")
POST_SB = ("w_kv", "b_w_q", "b_w_mem_kv")
MID = ("a_w_gu", "a_w_down")
SECOND = ("a_w_gu", "a_w_down", "w_kv", "b_w_q")
N_CHIPS = 4
N_DEV = 8


def _position():
    x, y, c = lax.axis_index("x"), lax.axis_index("y"), lax.axis_index("c")
    other_chips = [(1 - x, y), (x, 1 - y), (1 - x, 1 - y)]
    return x, y, c, other_chips


def _remote(src, dst, send_sem, recv_sem, device):
    return pltpu.make_async_remote_copy(src_ref=src, dst_ref=dst, send_sem=send_sem, recv_sem=recv_sem,
                                        device_id=device, device_id_type=MESH)


def _comm_call(name, body, args, out_shape, n_remote, aliases=None):
    return pl.pallas_call(
        body, name=name, in_specs=[ANY] * len(args), out_specs=[ANY] * len(out_shape), out_shape=out_shape,
        scratch_shapes=[pltpu.SemaphoreType.DMA((n_remote,)), pltpu.SemaphoreType.DMA((n_remote,))],
        input_output_aliases=aliases or {},
    )(*args)


def _row_chunks(nrows, row_bytes, mult=16):
    assert nrows % mult == 0, (nrows, mult)
    per = max(mult, (COPY_BYTES // row_bytes) // mult * mult)
    return [(r0, min(per, nrows - r0)) for r0 in range(0, nrows, per)]


def _rows(ref, start, size, lead=()):
    if isinstance(start, int):
        return ref.at[(*lead, pl.ds(start, size))]
    return ref.at[(*lead, pl.ds(pl.multiple_of(start, 16), size))]


class _Copies:
    def __init__(self, send, recv):
        self.send, self.recv = send, recv
        self.n_remote = 0
        self.remotes = []

    def slot(self):
        self.n_remote += 1
        return self.n_remote - 1

    def remote(self, k, src, dst, device):
        cp = _remote(src, dst, self.send.at[k], self.recv.at[k], device)
        cp.start()
        self.remotes.append(cp)
        return cp

    def finish(self):
        for cp in self.remotes:
            cp.wait_send()


def _shard_cols(ref, row_sharded, cdim, chip):
    if row_sharded:
        return ref
    return ref.at[:, pl.ds(pl.multiple_of(chip * cdim, LANES), cdim)]


class _Gather:
    def __init__(self, shapes, mult=16):
        self.names = list(shapes)
        self.shapes = [tuple(shapes[k]) for k in self.names]
        self.row_sharded = [k in ROW_SHARDED for k in self.names]
        self.chunks = [(i, r0, size) for i, (r, cdim) in enumerate(self.shapes)
                       for r0, size in _row_chunks(r // 2, cdim * 2, mult)]
        self.n = len(self.names)

    def out_shape(self):
        return [jax.ShapeDtypeStruct((N_CHIPS * r, cdim) if rs else (r, N_CHIPS * cdim), BF16)
                for (r, cdim), rs in zip(self.shapes, self.row_sharded)]

    def scratch(self):
        n_remote, n_local = 6 * len(self.chunks), self.n + 2 * len(self.chunks)
        return [pltpu.SemaphoreType.DMA((n_remote,)), pltpu.SemaphoreType.DMA((n_remote,)),
                pltpu.SemaphoreType.DMA((n_local,))] + [pltpu.VMEM(s, BF16) for s in self.shapes]

    def _window(self, dst, i, chip, half, r0, size):
        r, cdim = self.shapes[i]
        base = (chip * r if self.row_sharded[i] else 0) + half * (r // 2) + r0
        return _rows(_shard_cols(dst[i], self.row_sharded[i], cdim, chip), base, size)

    def _mine(self, refs, i, half, r0, size):
        return _rows(refs[i], half * (self.shapes[i][0] // 2) + r0, size)

    def _sent(self, src, dst, sems, q, k, px, py, c, me):
        i, r0, size = self.chunks[q]
        return _remote(self._mine(src, i, c, r0, size), self._window(dst, i, me, c, r0, size),
                       sems[0].at[6 * q + k], sems[1].at[6 * q + k], (px, py, c))

    def start(self, src, dst, sems, vm):
        x, y, c, chips = _position()
        for q in range(len(self.chunks)):
            for k, (px, py) in enumerate(chips):
                self._sent(src, dst, sems, q, k, px, py, c, 2 * x + y).start()
        for i in range(self.n):
            pltpu.make_async_copy(src[i], vm[i], sems[2].at[i]).start()

    def start_ici(self, src, dst, sems):
        x, y, c, chips = _position()
        for q in range(len(self.chunks)):
            for k, (px, py) in enumerate(chips):
                self._sent(src, dst, sems, q, k, px, py, c, 2 * x + y).start()

    def wait_ici(self, src, dst, sems):
        x, y, c, chips = _position()
        for q, (i, r0, size) in enumerate(self.chunks):
            for k, (px, py) in enumerate(chips):
                landed = self._window(dst, i, 2 * px + py, c, r0, size)
                _remote(landed, landed, sems[0].at[6 * q + k], sems[1].at[6 * q + k], (px, py, c)).wait_recv()
        for q in range(len(self.chunks)):
            for k, (px, py) in enumerate(chips):
                self._sent(src, dst, sems, q, k, px, py, c, 2 * x + y).wait_send()

    def finish(self, src, dst, sems, vm, landed=None):
        x, y, c, chips = _position()
        me = 2 * x + y
        send, recv, loc = sems
        arrived = dst if landed is None else landed
        if landed is not None:
            for i in range(self.n):
                pltpu.make_async_copy(src[i], vm[i], loc.at[i]).start()
        for i in range(self.n):
            pltpu.make_async_copy(src[i], vm[i], loc.at[i]).wait()
        placed, forwarded = [], []
        for q, (i, r0, size) in enumerate(self.chunks):
            for half in range(2):
                cp = pltpu.make_async_copy(self._mine(vm, i, half, r0, size), self._window(dst, i, me, half, r0, size),
                                           loc.at[self.n + 2 * q + half])
                cp.start()
                placed.append(cp)
        for q, (i, r0, size) in enumerate(self.chunks):
            for k, (px, py) in enumerate(chips):
                if landed is None:
                    there = self._window(dst, i, 2 * px + py, c, r0, size)
                    _remote(there, there, send.at[6 * q + k], recv.at[6 * q + k], (px, py, c)).wait_recv()
                cp = _remote(self._window(arrived, i, 2 * px + py, c, r0, size), self._window(dst, i, 2 * px + py, c, r0, size),
                             send.at[6 * q + 3 + k], recv.at[6 * q + 3 + k], (x, y, 1 - c))
                cp.start()
                forwarded.append(cp)
        for q, (i, r0, size) in enumerate(self.chunks):
            for k, (px, py) in enumerate(chips):
                there = self._window(dst, i, 2 * px + py, 1 - c, r0, size)
                _remote(there, there, send.at[6 * q + 3 + k], recv.at[6 * q + 3 + k], (x, y, 1 - c)).wait_recv()
        if landed is None:
            for q in range(len(self.chunks)):
                for k, (px, py) in enumerate(chips):
                    self._sent(src, dst, sems, q, k, px, py, c, me).wait_send()
        for cp in forwarded:
            cp.wait_send()
        for cp in placed:
            cp.wait()


class _GatherRelay(_Gather):
    def __init__(self, shapes):
        super().__init__(shapes, mult=32)

    def scratch(self):
        n_remote, n_local = 8 * len(self.chunks), self.n + 2 * len(self.chunks)
        return [pltpu.SemaphoreType.DMA((n_remote,)), pltpu.SemaphoreType.DMA((n_remote,)),
                pltpu.SemaphoreType.DMA((n_local,))] + [pltpu.VMEM(s, BF16) for s in self.shapes]

    def run(self, src, dst, sems, vm):
        x, y, c, chips = _position()
        me, diag, sibling = 2 * x + y, 2 * (1 - x) + (1 - y), (x, y, 1 - c)
        nbrs = chips[:2]
        send, recv, loc = sems

        def copy(slot, ref, device):
            return _remote(ref, ref, send.at[slot], recv.at[slot], device)

        first = []
        for q, (i, r0, size) in enumerate(self.chunks):
            for k, (px, py) in enumerate(nbrs):
                cp = _remote(self._mine(src, i, c, r0, size), self._window(dst, i, me, c, r0, size),
                             send.at[8 * q + k], recv.at[8 * q + k], (px, py, c))
                cp.start()
                first.append(cp)
        for i in range(self.n):
            pltpu.make_async_copy(src[i], vm[i], loc.at[i]).start()
        for i in range(self.n):
            pltpu.make_async_copy(src[i], vm[i], loc.at[i]).wait()
        placed, passed = [], []
        for q, (i, r0, size) in enumerate(self.chunks):
            for half in range(2):
                cp = pltpu.make_async_copy(self._mine(vm, i, half, r0, size), self._window(dst, i, me, half, r0, size),
                                           loc.at[self.n + 2 * q + half])
                cp.start()
                placed.append(cp)
        for q, (i, r0, size) in enumerate(self.chunks):
            for k, (px, py) in enumerate(nbrs):
                landed = self._window(dst, i, 2 * px + py, c, r0, size)
                copy(8 * q + k, landed, (px, py, c)).wait_recv()
                piece = self._window(dst, i, 2 * px + py, c, r0 + k * (size // 2), size // 2)
                ox, oy = nbrs[1 - k]
                for cp in (copy(8 * q + 2 + k, piece, (ox, oy, c)), copy(8 * q + 4 + k, landed, sibling)):
                    cp.start()
                    passed.append(cp)
        for q, (i, r0, size) in enumerate(self.chunks):
            for k in range(2):
                piece = self._window(dst, i, diag, c, r0 + k * (size // 2), size // 2)
                ox, oy = nbrs[1 - k]
                copy(8 * q + 2 + k, piece, (ox, oy, c)).wait_recv()
                cp = copy(8 * q + 6 + k, piece, sibling)
                cp.start()
                passed.append(cp)
        for q, (i, r0, size) in enumerate(self.chunks):
            for k, (px, py) in enumerate(nbrs):
                copy(8 * q + 4 + k, self._window(dst, i, 2 * px + py, 1 - c, r0, size), sibling).wait_recv()
                copy(8 * q + 6 + k, self._window(dst, i, diag, 1 - c, r0 + k * (size // 2), size // 2), sibling).wait_recv()
        for cp in first + passed:
            cp.wait_send()
        for cp in placed:
            cp.wait()


def _gather_weights(shards):
    plan = _GatherRelay({k: v.shape for k, v in shards.items()})
    n = plan.n

    def body(*refs):
        plan.run(refs[:n], refs[n:2 * n], refs[2 * n:2 * n + 3], refs[2 * n + 3:])

    outs = pl.pallas_call(
        body, name="gather_weights", in_specs=[ANY] * n, out_specs=[ANY] * n, out_shape=plan.out_shape(),
        scratch_shapes=plan.scratch(), compiler_params=pltpu.CompilerParams(vmem_limit_bytes=VMEM_LIMIT),
    )(*[shards[k] for k in plan.names])
    return dict(zip(plan.names, outs))


def _gather_start(shards):
    plan = _Gather({k: v.shape for k, v in shards.items()})
    n = plan.n
    n_remote = 6 * len(plan.chunks)
    srcs = [pltpu.with_memory_space_constraint(shards[k], pltpu.HBM) for k in plan.names]
    lands = [pltpu.with_memory_space_constraint(lax.empty(s.shape, s.dtype), pltpu.HBM) for s in plan.out_shape()]

    def body(*refs):
        plan.start_ici(refs[:n], refs[n:2 * n], (refs[2 * n], refs[2 * n + 1]))
        refs[-1][...] = jnp.zeros_like(refs[-1])

    return pl.pallas_call(
        body, name="gather_start",
        out_shape=(pltpu.SemaphoreType.DMA((n_remote,)), pltpu.SemaphoreType.DMA((n_remote,)),
                   *[pltpu.HBM(a.shape, a.dtype) for a in srcs + lands], jax.ShapeDtypeStruct((8, LANES), F32)),
        in_specs=[HBM_SPEC] * (2 * n),
        out_specs=(SEM_SPEC, SEM_SPEC, *[HBM_SPEC] * (2 * n), pl.BlockSpec(memory_space=pltpu.VMEM)),
        input_output_aliases={i: 2 + i for i in range(2 * n)},
        compiler_params=pltpu.CompilerParams(has_side_effects=SIDE_EFFECT),
    )(*srcs, *lands)


def _gather_finish(shapes, flight, after):
    plan = _Gather(shapes)
    n = plan.n
    send, recv, thru = flight[0], flight[1], flight[2:2 + 2 * n]

    def wait_body(*refs):
        plan.wait_ici(refs[:n], refs[n:2 * n], (refs[2 * n], refs[2 * n + 1]))

    thru = pl.pallas_call(
        wait_body, name="gather_wait", out_shape=tuple(pltpu.HBM(t.shape, t.dtype) for t in thru),
        in_specs=[HBM_SPEC] * (2 * n) + [SEM_SPEC, SEM_SPEC, ANY], out_specs=[HBM_SPEC] * (2 * n),
        input_output_aliases={i: i for i in range(2 * n)},
        compiler_params=pltpu.CompilerParams(has_side_effects=SIDE_EFFECT),
    )(*thru, send, recv, after)

    def body(*refs):
        src, landed, dst = refs[:n], refs[n:2 * n], refs[2 * n:3 * n]
        plan.finish(src, dst, refs[3 * n:3 * n + 3], refs[3 * n + 3:], landed=landed)

    outs = pl.pallas_call(
        body, name="gather_finish", in_specs=[ANY] * (2 * n), out_specs=[ANY] * n, out_shape=plan.out_shape(),
        scratch_shapes=plan.scratch(), input_output_aliases={n + i: i for i in range(n)},
        compiler_params=pltpu.CompilerParams(vmem_limit_bytes=VMEM_LIMIT),
    )(*thru)
    return dict(zip(plan.names, outs))


def _scalar_grid_call(name, body, scalars, grid, in_specs, out_specs, out_shape, args):
    return pl.pallas_call(
        body, name=name, out_shape=out_shape,
        grid_spec=pltpu.PrefetchScalarGridSpec(num_scalar_prefetch=1, grid=grid, in_specs=in_specs, out_specs=out_specs),
        compiler_params=_params(*["arbitrary"] * len(grid)),
    )(scalars, *args)


def _pair_sum(name, g4, sib, where, tb=256):
    j, _, r, w = g4.shape
    tb = _row_tile(r, tb)

    def body(s_ref, g_ref, b_ref, o_ref):
        o_ref[...] = (g_ref[...].astype(F32) + b_ref[...].astype(F32)).astype(o_ref.dtype)

    blk = pl.BlockSpec((None, tb, w), lambda a, i, s: (a, i, 0))
    return _scalar_grid_call(
        name, body, where, (j, r // tb),
        [pl.BlockSpec((None, None, tb, w), lambda a, i, s: (a, s[0], i, 0)), blk], blk,
        jax.ShapeDtypeStruct((j, r, w), BF16), (g4, sib))


def _chip_sum(name, partial, got, where, row_sharded, tb=256):
    _, r, cdim = got.shape
    tb = _row_tile(r, tb)

    def body(s_ref, p_ref, g_ref, o_ref):
        acc = p_ref[...].astype(F32)
        for k in range(N_CHIPS - 1):
            acc = acc + g_ref[k].astype(F32)
        o_ref[...] = acc

    if row_sharded:
        own = pl.BlockSpec((None, tb, cdim), lambda i, s: (s[1], i, 0))
    else:
        own = pl.BlockSpec((None, tb, cdim), lambda i, s: (0, i, s[1]))
    return _scalar_grid_call(
        name, body, where, (r // tb,),
        [own, pl.BlockSpec((N_CHIPS - 1, tb, cdim), lambda i, s: (0, i, 0))],
        pl.BlockSpec((None, tb, cdim), lambda i, s: (s[0], i, 0)),
        jax.ShapeDtypeStruct((2, r, cdim), F32), (partial, got))


def _where():
    return jnp.stack([lax.axis_index("c"), 2 * lax.axis_index("x") + lax.axis_index("y")]).astype(jnp.int32)


def _grad_halves(name, g):
    rows, cols = g.shape
    if name in ROW_SHARDED:
        r = rows // N_CHIPS
        return g.reshape(N_CHIPS, 2, r // 2, cols), (r // 2, cols)
    return g.reshape(1, 2, rows // 2, cols), (rows // 2, cols // N_CHIPS)


def _pair_sums(tag, grads):
    names = list(grads)
    n = len(names)
    g4 = [_grad_halves(k, grads[k])[0] for k in names]
    plan = [[(j, r0, size) for j in range(g.shape[0]) for r0, size in _row_chunks(g.shape[2], g.shape[3] * 2)]
            for g in g4]
    out_shape = [jax.ShapeDtypeStruct((g.shape[0],) + g.shape[2:], BF16) for g in g4]

    def body(*refs):
        src, sib = refs[:n], refs[n:2 * n]
        cps = _Copies(*refs[2 * n:])
        x, y, c, _ = _position()
        waits = []
        for i in range(n):
            for j, r0, size in plan[i]:
                waits.append(cps.remote(cps.slot(), _rows(src[i], r0, size, lead=(j, 1 - c)),
                                        _rows(sib[i], r0, size, lead=(j,)), (x, y, 1 - c)))
        for cp in waits:
            cp.wait_recv()
        cps.finish()

    sibs = _comm_call("grads_pair_exchange_" + tag, body, g4, out_shape, sum(len(p) for p in plan))
    where = _where()
    return {k: _pair_sum(k + "_pair_sum", g, s, where) for k, g, s in zip(names, g4, sibs)}


class _ChipExchange:
    def __init__(self, partial, mult=16):
        self.names = list(partial)
        self.n = len(self.names)
        self.row_sharded = [k in ROW_SHARDED for k in self.names]
        self.half = []
        for k, rs in zip(self.names, self.row_sharded):
            _, r, w = partial[k].shape
            self.half.append((r, w) if rs else (r, w // N_CHIPS))
        self.chunks = [(i, r0, size) for i, (r, cdim) in enumerate(self.half)
                       for r0, size in _row_chunks(r, cdim * 2, mult)]

    def shard(self, src, i, chip, r0, size):
        if self.row_sharded[i]:
            return _rows(src[i], r0, size, lead=(chip,))
        return _rows(_shard_cols(src[i].at[0], False, self.half[i][1], chip), r0, size)

    def out_shape(self):
        return [jax.ShapeDtypeStruct((N_CHIPS - 1,) + s, BF16) for s in self.half]

    def scratch(self):
        n_remote = 3 * len(self.chunks)
        return [pltpu.SemaphoreType.DMA((n_remote,)), pltpu.SemaphoreType.DMA((n_remote,))]

    def _copies(self, src, dst, sems):
        x, y, c, chips = _position()
        for q, (i, r0, size) in enumerate(self.chunks):
            for k, (px, py) in enumerate(chips):
                yield _remote(self.shard(src, i, 2 * px + py, r0, size), _rows(dst[i], r0, size, lead=(k,)),
                              sems[0].at[3 * q + k], sems[1].at[3 * q + k], (px, py, c))

    def start(self, src, dst, sems):
        for cp in self._copies(src, dst, sems):
            cp.start()

    def finish(self, src, dst, sems):
        for cp in self._copies(src, dst, sems):
            cp.wait_recv()
        for cp in self._copies(src, dst, sems):
            cp.wait_send()


def _chip_exchange(partial):
    plan = _ChipExchange(partial)
    n = plan.n

    def body(*refs):
        parts = refs[:n], refs[n:2 * n], refs[2 * n:]
        plan.start(*parts)
        plan.finish(*parts)

    got = pl.pallas_call(
        body, name="grads_chip_exchange", in_specs=[ANY] * n, out_specs=[ANY] * n, out_shape=plan.out_shape(),
        scratch_shapes=plan.scratch(),
    )(*[partial[k] for k in plan.names])
    return dict(zip(plan.names, got))


HBM_SPEC = pl.BlockSpec(memory_space=pltpu.HBM)
SEM_SPEC = pl.BlockSpec(memory_space=pltpu.SEMAPHORE)
SIDE_EFFECT = pltpu.SideEffectType.DATAFLOW_SIDE_EFFECTING


def _chip_exchange_start(tag, partial):
    plan = _ChipExchange(partial)
    n = plan.n
    n_remote = 3 * len(plan.chunks)
    srcs = [pltpu.with_memory_space_constraint(partial[k], pltpu.HBM) for k in plan.names]
    lands = [pltpu.with_memory_space_constraint(lax.empty(s.shape, s.dtype), pltpu.HBM) for s in plan.out_shape()]

    def body(*refs):
        src, land = refs[:n], refs[n:2 * n]
        plan.start(src, land, (refs[2 * n], refs[2 * n + 1]))
        refs[-1][...] = jnp.zeros_like(refs[-1])

    return pl.pallas_call(
        body, name="grads_chip_exchange_start_" + tag,
        out_shape=(pltpu.SemaphoreType.DMA((n_remote,)), pltpu.SemaphoreType.DMA((n_remote,)),
                   *[pltpu.HBM(a.shape, a.dtype) for a in srcs + lands], jax.ShapeDtypeStruct((8, LANES), F32)),
        in_specs=[HBM_SPEC] * (2 * n),
        out_specs=(SEM_SPEC, SEM_SPEC, *[HBM_SPEC] * (2 * n), pl.BlockSpec(memory_space=pltpu.VMEM)),
        input_output_aliases={i: 2 + i for i in range(2 * n)},
        compiler_params=pltpu.CompilerParams(has_side_effects=SIDE_EFFECT),
    )(*srcs, *lands)


def _chip_exchange_wait(tag, partial, flight, after):
    plan = _ChipExchange(partial)
    n = plan.n
    send, recv, thru = flight[0], flight[1], flight[2:2 + 2 * n]

    def body(*refs):
        plan.finish(refs[:n], refs[n:2 * n], (refs[2 * n], refs[2 * n + 1]))

    outs = pl.pallas_call(
        body, name="grads_chip_exchange_wait_" + tag,
        out_shape=tuple(pltpu.HBM(t.shape, t.dtype) for t in thru),
        in_specs=[HBM_SPEC] * (2 * n) + [SEM_SPEC, SEM_SPEC, ANY], out_specs=[HBM_SPEC] * (2 * n),
        input_output_aliases={i: i for i in range(2 * n)},
        compiler_params=pltpu.CompilerParams(has_side_effects=SIDE_EFFECT),
    )(*thru, send, recv, after)
    return dict(zip(plan.names, outs[:n])), dict(zip(plan.names, outs[n:]))


def _finish_reduce(partial, got):
    names = list(partial)
    n = len(names)
    where = _where()
    halves = [_chip_sum(k + "_chip_sum", partial[k], got[k], where, k in ROW_SHARDED) for k in names]
    plan = [_row_chunks(h.shape[1], h.shape[2] * 4) for h in halves]
    out_shape = [jax.ShapeDtypeStruct(h.shape, F32) for h in halves]

    def body(*refs):
        src, dst = refs[:n], refs[n:2 * n]
        cps = _Copies(*refs[2 * n:])
        x, y, c, _ = _position()
        waits = []
        for i in range(n):
            for r0, size in plan[i]:
                waits.append(cps.remote(cps.slot(), _rows(src[i], r0, size, lead=(c,)), _rows(dst[i], r0, size, lead=(c,)),
                                        (x, y, 1 - c)))
        for cp in waits:
            cp.wait_recv()
        cps.finish()

    outs = _comm_call("grads_pair_share", body, halves, out_shape, sum(len(p) for p in plan),
                      aliases={i: i for i in range(n)})
    return {k: o.reshape(2 * o.shape[1], o.shape[2]) for k, o in zip(names, outs)}


def _all_reduce_small(name, v):
    rows, cols = v.shape
    h = rows // 2

    def body(v_ref, o_ref, sib, pair, buf, send, recv):
        x, y, c, chips = _position()
        me = 2 * x + y
        sibling = (x, y, 1 - c)
        cp = _remote(v_ref, sib, send.at[0], recv.at[0], sibling)
        cp.start()
        cp.wait()
        pair[...] = v_ref[...] + sib[...]
        mine = pl.ds(pl.multiple_of(c * h, 8), h)
        buf[me] = pair[mine, :]
        sends = [_remote(pair.at[mine], buf.at[me], send.at[1 + k], recv.at[1 + k], (px, py, c))
                 for k, (px, py) in enumerate(chips)]
        for cp in sends:
            cp.start()
        for k, (px, py) in enumerate(chips):
            _remote(pair.at[mine], buf.at[2 * px + py], send.at[1 + k], recv.at[1 + k], (px, py, c)).wait_recv()
        for cp in sends:
            cp.wait_send()
        o_ref[mine, :] = (buf[0] + buf[1]) + (buf[2] + buf[3])
        cp = _remote(o_ref.at[mine], o_ref.at[mine], send.at[4], recv.at[4], sibling)
        cp.start()
        cp.wait()

    vm = pl.BlockSpec(memory_space=pltpu.VMEM)
    return pl.pallas_call(
        body, name=name, in_specs=[vm], out_specs=vm, out_shape=jax.ShapeDtypeStruct(v.shape, F32),
        scratch_shapes=[pltpu.VMEM((rows, cols), F32), pltpu.VMEM((rows, cols), F32), pltpu.VMEM((N_CHIPS, h, cols), F32),
                        pltpu.SemaphoreType.DMA((5,)), pltpu.SemaphoreType.DMA((5,))],
        compiler_params=pltpu.CompilerParams(vmem_limit_bytes=VMEM_LIMIT),
    )(v)


def _adamw(name, w, g, m, v):
    def fn(wt, gt, mt, vt):
        mt = ADAM_B1 * mt + (1.0 - ADAM_B1) * gt
        vt = ADAM_B2 * vt + (1.0 - ADAM_B2) * (gt * gt)
        m_hat = mt / (1.0 - ADAM_B1 ** ADAM_STEP)
        v_hat = vt / (1.0 - ADAM_B2 ** ADAM_STEP)
        delta = -ADAM_LR * (m_hat / (jnp.sqrt(v_hat) + ADAM_EPS) + ADAM_WD * wt)
        return delta, mt, vt
    n = w.shape[1]
    return _rowwise(name, fn, [w, g, m, v], [], [(n, F32)] * 3, tb=256)


WEIGHTS = ("mem_norm", "a_norm_mix", "a_w_in", "a_w_group", "a_scale", "a_w_mem_kv", "a_w_out", "a_norm_ffn", "a_w_gu",
           "a_w_down", "kv_norm", "w_kv", "b_norm_mix", "b_w_q", "b_w_mem_kv", "b_w_out", "b_norm_ffn", "b_w_gu",
           "b_w_down", "final_norm")
REPLICATED_VECS = ("mem_norm", "kv_norm", "b_norm_mix", "b_norm_ffn", "final_norm")
SHARDED_VECS = ("a_norm_mix", "a_norm_ffn", "a_scale")
D_MODEL = 1024
GROUP_ROWS = 4 * POOL_GROUP * POOL_GROUP // D_MODEL


def _row(v):
    v = v.reshape(1, -1).astype(F32)
    return jnp.pad(v, ((0, 0), (0, D_MODEL - v.shape[1])))


def _pack_small(t):
    rows = [_row(t[k]) for k in REPLICATED_VECS]
    rows.append(_row(jnp.concatenate([t[k].reshape(-1) for k in SHARDED_VECS])))
    rows.append(jnp.zeros((2, D_MODEL), F32))
    rows.append(t["a_w_group"].reshape(GROUP_ROWS, D_MODEL).astype(F32))
    return jnp.concatenate(rows, axis=0)


def _unpack_small(p, like):
    out = {k: p[i, :].reshape(like[k].shape) for i, k in enumerate(REPLICATED_VECS)}
    off = 0
    for k in SHARDED_VECS:
        size = like[k].size
        out[k] = p[len(REPLICATED_VECS), off:off + size].reshape(like[k].shape)
        off += size
    out["a_w_group"] = p[len(REPLICATED_VECS) + 3:, :].reshape(like["a_w_group"].shape)
    return out


def kernel(x, mem, mem_norm, a_norm_mix, a_w_in, a_w_group, a_scale, a_w_mem_kv, a_w_out, a_norm_ffn, a_w_gu, a_w_down, kv_norm, w_kv, b_norm_mix, b_w_q, b_w_mem_kv, b_w_out, b_norm_ffn, b_w_gu, b_w_down, final_norm, loss_target, m_mem_norm, m_a_norm_mix, m_a_w_in, m_a_w_group, m_a_scale, m_a_w_mem_kv, m_a_w_out, m_a_norm_ffn, m_a_w_gu, m_a_w_down, m_kv_norm, m_w_kv, m_b_norm_mix, m_b_w_q, m_b_w_mem_kv, m_b_w_out, m_b_norm_ffn, m_b_w_gu, m_b_w_down, m_final_norm, v_mem_norm, v_a_norm_mix, v_a_w_in, v_a_w_group, v_a_scale, v_a_w_mem_kv, v_a_w_out, v_a_norm_ffn, v_a_w_gu, v_a_w_down, v_kv_norm, v_w_kv, v_b_norm_mix, v_b_w_q, v_b_w_mem_kv, v_b_w_out, v_b_norm_ffn, v_b_w_gu, v_b_w_down, v_final_norm):
    given = dict(locals())
    wl = {k: given[k] for k in WEIGHTS}
    ml = {k: given["m_" + k] for k in WEIGHTS}
    vl = {k: given["v_" + k] for k in WEIGHTS}
    chip = 2 * lax.axis_index("x") + lax.axis_index("y")

    def mat(a):
        return a.reshape(a.shape[-2], a.shape[-1])

    shards = {k: mat(wl[k]).astype(BF16) for k in BIG}
    second = _gather_start({k: shards[k] for k in SECOND})
    full = _gather_weights({k: shards[k] for k in BIG if k not in LATE + SECOND})
    gains = jnp.zeros((16, D_MODEL), F32)
    for i, k in enumerate(SHARDED_VECS):
        part = wl[k].reshape(1, -1)
        width = part.shape[1]
        gains = lax.dynamic_update_slice(gains, part, (i, chip * width))
    gains = _all_reduce_small("gains_all_gather", gains) * 0.5
    w = dict(full)
    for k in REPLICATED_VECS:
        w[k] = wl[k].reshape(1, D_MODEL)
    w["a_norm_mix"], w["a_norm_ffn"] = gains[0:1] + second[-1][0:1, 0:1], gains[1:2]
    w["a_scale"] = gains[2:3, :SB_WIDTH]
    w["a_w_group"] = wl["a_w_group"][0].astype(BF16)

    loss, grad_x, g, partial, got = _local_step(x[0], mem[0], loss_target[0], w, {k: shards[k] for k in LATE},
                                                ({k: shards[k].shape for k in SECOND}, second))

    rest = _pair_sums("late", {k: g[k] for k in BIG if k not in EARLY + POST_SB + MID})
    partial.update(rest)
    got.update(_chip_exchange(rest))
    red = _finish_reduce(partial, got)
    small = jnp.concatenate(
        [_row(g[k]) for k in REPLICATED_VECS] + [_row(g[k]) for k in SHARDED_VECS] + [_row(loss)]
        + [jnp.zeros((7, D_MODEL), F32), g["a_w_group"].reshape(GROUP_ROWS, D_MODEL)], axis=0)
    small = _all_reduce_small("small_grads_all_reduce", small)
    gs = {k: small[i] for i, k in enumerate(REPLICATED_VECS)}
    for i, k in enumerate(SHARDED_VECS):
        width = wl[k].shape[-1]
        gs[k] = lax.dynamic_slice(small[len(REPLICATED_VECS) + i], (chip * width,), (width,))
    gs["a_w_group"] = small[16:]
    total_loss = small[8, 0]

    out_g, out_d, out_m, out_v = {}, {}, {}, {}
    for k in BIG:
        shape = wl[k].shape
        out_g[k] = red[k].reshape(shape)
        d, nm, nv = _adamw(k + "_adamw", mat(wl[k]), red[k], mat(ml[k]), mat(vl[k]))
        out_d[k], out_m[k], out_v[k] = d.reshape(shape), nm.reshape(shape), nv.reshape(shape)
    small_names = REPLICATED_VECS + SHARDED_VECS + ("a_w_group",)
    d, nm, nv = _adamw("small_adamw", _pack_small(wl), _pack_small(gs), _pack_small(ml), _pack_small(vl))
    like = {k: wl[k] for k in small_names}
    for dst, p in ((out_d, d), (out_m, nm), (out_v, nv)):
        dst.update(_unpack_small(p, like))
    for k in small_names:
        out_g[k] = gs[k].reshape(wl[k].shape)

    return (total_loss, grad_x[None], *[out_g[k] for k in WEIGHTS], *[out_d[k] for k in WEIGHTS],
            *[out_m[k] for k in WEIGHTS], *[out_v[k] for k in WEIGHTS])
```

```python
import functools

import jax
import jax.numpy as jnp
from jax import lax
from jax.experimental import pallas as pl
from jax.experimental.pallas import tpu as pltpu

F32 = jnp.float32
BF16 = jnp.bfloat16

HEAD_DIM = 64
SB_WIDTH = 768
MEM_WIDTH = 256
POOL_WINDOWS = (2, 4, 8, 16)
POOL_GROUP = 192
POOL_HALO = 16
EPS = 1e-6
ATT_SCALE = HEAD_DIM ** -0.5
ADAM_LR, ADAM_B1, ADAM_B2, ADAM_EPS, ADAM_WD, ADAM_STEP = 0.001, 0.9, 0.999, 1e-08, 0.01, 10

LANES = 128
SB_TQ, SB_TK = 512, 512
VMEM_LIMIT = 56 * 1024 * 1024
MESH = pl.DeviceIdType.MESH
COPY_BYTES = 512 * 1024
ANY = pl.BlockSpec(memory_space=pl.ANY)


def _params(*sem):
    return pltpu.CompilerParams(dimension_semantics=sem, vmem_limit_bytes=VMEM_LIMIT)


def _tile(n, pref):
    if n <= pref:
        return n
    best = None
    for t in range(LANES, pref + 1, LANES):
        if n % t == 0:
            best = t
    assert best is not None, (n, pref)
    return best


def _row_tile(t, pref):
    if t <= pref:
        return t
    for tb in range(pref - pref % 16, 0, -16):
        if t % tb == 0:
            return tb
    raise ValueError((t, pref))


def _rowwise(name, fn, rows, vecs, row_outs, sum_outs=(), tb=512):
    norm_rows = []
    for r in rows:
        if isinstance(r, tuple):
            arr, (bc, cb) = r
        else:
            arr, (bc, cb) = r, (r.shape[1], 0)
        norm_rows.append((arr, bc, cb))
    t = norm_rows[0][0].shape[0]
    tb = _row_tile(t, tb)
    n_in, n_ro = len(norm_rows) + len(vecs), len(row_outs)

    def body(*refs):
        ins = [r[...] for r in refs[:n_in]]
        outs = fn(*ins)
        if not isinstance(outs, tuple):
            outs = (outs,)
        for o_ref, o in zip(refs[n_in:n_in + n_ro], outs[:n_ro]):
            o_ref[...] = o.astype(o_ref.dtype)
        for s_ref, s in zip(refs[n_in + n_ro:], outs[n_ro:]):
            @pl.when(pl.program_id(0) == 0)
            def _():
                s_ref[...] = jnp.zeros_like(s_ref)
            s_ref[...] += s

    in_specs = [pl.BlockSpec((tb, bc), functools.partial(lambda i, cb: (i, cb), cb=cb)) for _, bc, cb in norm_rows]
    in_specs += [pl.BlockSpec(v.shape, lambda i: (0, 0)) for v in vecs]
    out_specs = [pl.BlockSpec((tb, c), lambda i: (i, 0)) for c, _ in row_outs]
    out_specs += [pl.BlockSpec((1, c), lambda i: (0, 0)) for c in sum_outs]
    out_shape = [jax.ShapeDtypeStruct((t, c), d) for c, d in row_outs]
    out_shape += [jax.ShapeDtypeStruct((1, c), F32) for c in sum_outs]
    res = pl.pallas_call(
        body, name=name, grid=(t // tb,), in_specs=in_specs, out_specs=out_specs, out_shape=out_shape,
        compiler_params=_params("arbitrary"),
    )(*[a for a, _, _ in norm_rows], *vecs)
    return res


def _rms_fwd(name, x, g):
    def fn(xt, gt):
        rstd = lax.rsqrt(jnp.mean(xt * xt, axis=-1, keepdims=True) + EPS)
        return xt * rstd * gt
    return _rowwise(name, fn, [x], [g], [(x.shape[1], BF16)])[0]


def _rms_bwd(name, x, g, dh, dres=None, want_dx=True):
    has_res = dres is not None

    def fn(*a):
        if has_res:
            xt, dht, drt, gt = a
        else:
            xt, dht, gt = a
        rstd = lax.rsqrt(jnp.mean(xt * xt, axis=-1, keepdims=True) + EPS)
        xhat = xt * rstd
        dht = dht.astype(F32)
        dg = jnp.sum(dht * xhat, axis=0, keepdims=True)
        if not want_dx:
            return (dg,)
        dxhat = dht * gt
        dx = rstd * (dxhat - xhat * jnp.mean(dxhat * xhat, axis=-1, keepdims=True))
        if has_res:
            dx = dx + drt
        return dx, dx, dg

    d = x.shape[1]
    rows = [x, dh] + ([dres] if has_res else [])
    outs = [(d, F32), (d, BF16)] if want_dx else []
    return _rowwise(name, fn, rows, [g], outs, [d])


def _rms_fwd2(name, x, g1, g2):
    def fn(xt, g1t, g2t):
        xn = xt * lax.rsqrt(jnp.mean(xt * xt, axis=-1, keepdims=True) + EPS)
        return xn * g1t, xn * g2t
    return _rowwise(name, fn, [x], [g1, g2], [(x.shape[1], BF16)] * 2)


def _rms_bwd2(name, x, g1, dh1, g2, dh2, dres):
    def fn(xt, d1, d2, drt, g1t, g2t):
        rstd = lax.rsqrt(jnp.mean(xt * xt, axis=-1, keepdims=True) + EPS)
        xhat = xt * rstd
        d1, d2 = d1.astype(F32), d2.astype(F32)
        dxhat = d1 * g1t + d2 * g2t
        dx = rstd * (dxhat - xhat * jnp.mean(dxhat * xhat, axis=-1, keepdims=True)) + drt
        return dx, dx, jnp.sum(d1 * xhat, axis=0, keepdims=True), jnp.sum(d2 * xhat, axis=0, keepdims=True)

    d = x.shape[1]
    return _rowwise(name, fn, [x, dh1, dh2, dres], [g1, g2], [(d, F32), (d, BF16)], [d, d])


_DOT_DIMS = {"nn": ((1,), (0,)), "nt": ((1,), (1,)), "tn": ((0,), (0,))}


def _mm(name, a, b, mode, out_dtype, res=None, tm=512, tn=512):
    if mode == "nn":
        (m, k), (k2, n) = a.shape, b.shape
    elif mode == "nt":
        (m, k), (n, k2) = a.shape, b.shape
    else:
        (k, m), (k2, n) = a.shape, b.shape
    assert k == k2, (name, a.shape, b.shape)
    tm, tn = _tile(m, tm), _tile(n, tn)
    dims = (_DOT_DIMS[mode], ((), ()))
    has_res = res is not None

    def body(a_ref, b_ref, *rest):
        acc = lax.dot_general(a_ref[...], b_ref[...], dims, preferred_element_type=F32)
        if has_res:
            acc = acc + rest[0][...]
        rest[-1][...] = acc.astype(out_dtype)

    a_spec = pl.BlockSpec((k, tm), lambda i, j: (0, i)) if mode == "tn" else pl.BlockSpec((tm, k), lambda i, j: (i, 0))
    b_spec = pl.BlockSpec((tn, k), lambda i, j: (j, 0)) if mode == "nt" else pl.BlockSpec((k, tn), lambda i, j: (0, j))
    o_spec = pl.BlockSpec((tm, tn), lambda i, j: (i, j))
    in_specs, args = [a_spec, b_spec], [a, b]
    if has_res:
        in_specs.append(o_spec)
        args.append(res)
    return pl.pallas_call(
        body, name=name, grid=(m // tm, n // tn), in_specs=in_specs, out_specs=o_spec,
        out_shape=jax.ShapeDtypeStruct((m, n), out_dtype), compiler_params=_params("parallel", "arbitrary"),
    )(*args)


def _mm_nt_rms_bwd(name, a, b, x, g, dres, tm=256):
    m, k = a.shape
    d = b.shape[0]

    def body(a_ref, b_ref, x_ref, r_ref, g_ref, dx_ref, dxb_ref, dg_ref):
        dh = lax.dot_general(a_ref[...], b_ref[...], (_DOT_DIMS["nt"], ((), ())), preferred_element_type=F32)
        xt = x_ref[...]
        rstd = lax.rsqrt(jnp.mean(xt * xt, axis=-1, keepdims=True) + EPS)
        xhat = xt * rstd
        dxhat = dh * g_ref[...]
        dx = rstd * (dxhat - xhat * jnp.mean(dxhat * xhat, axis=-1, keepdims=True)) + r_ref[...]
        dx_ref[...] = dx
        dxb_ref[...] = dx.astype(BF16)

        @pl.when(pl.program_id(0) == 0)
        def _():
            dg_ref[...] = jnp.zeros_like(dg_ref)
        dg_ref[...] += jnp.sum(dh * xhat, axis=0, keepdims=True)

    row = lambda c: pl.BlockSpec((tm, c), lambda i: (i, 0))
    whole = lambda shape: pl.BlockSpec(shape, lambda i: (0, 0))
    return pl.pallas_call(
        body, name=name, grid=(m // tm,),
        in_specs=[row(k), whole((d, k)), row(d), row(d), whole((1, d))],
        out_specs=[row(d), row(d), whole((1, d))],
        out_shape=[jax.ShapeDtypeStruct((m, d), F32), jax.ShapeDtypeStruct((m, d), BF16), jax.ShapeDtypeStruct((1, d), F32)],
        compiler_params=_params("arbitrary"),
    )(a, b, x, dres, g)


def _pool(name, u, reverse, tb=512):
    t = u.shape[0]
    tb = min(tb, t)
    nt = t // tb
    c = SB_WIDTH
    hpb = tb // POOL_HALO

    def body(cur_ref, halo_ref, o_ref):
        i = pl.program_id(0)
        cur = cur_ref[...].astype(F32)
        edge = (i == nt - 1) if reverse else (i == 0)
        halo = jnp.where(edge, 0.0, halo_ref[...].astype(F32))
        col = lax.broadcasted_iota(jnp.int32, (tb + POOL_HALO, c), 1)
        row = lax.broadcasted_iota(jnp.int32, (tb + POOL_HALO, c), 0)
        wcol = jnp.where(col < POOL_GROUP, 2, jnp.where(col < 2 * POOL_GROUP, 4, jnp.where(col < 3 * POOL_GROUP, 8, 16)))
        n = tb + POOL_HALO
        if reverse:
            ext = jnp.concatenate([cur, halo], axis=0)
            tpos = i * tb + row
            ext = ext / jnp.minimum(tpos + 1, wcol).astype(F32)
            shift = lambda a, k: pltpu.roll(a, n - k, 0)
        else:
            ext = jnp.concatenate([halo, cur], axis=0)
            shift = lambda a, k: pltpu.roll(a, k, 0)
        s2 = ext + shift(ext, 1)
        s4 = s2 + shift(s2, 2)
        s8 = s4 + shift(s4, 4)
        s16 = s8 + shift(s8, 8)
        win = jnp.where(wcol == 2, s2, jnp.where(wcol == 4, s4, jnp.where(wcol == 8, s8, s16)))
        if reverse:
            out = win[:tb] - cur
        else:
            tpos = i * tb + row[POOL_HALO:] - POOL_HALO
            out = win[POOL_HALO:] / jnp.minimum(tpos + 1, wcol[POOL_HALO:]).astype(F32) - cur
        o_ref[...] = out.astype(o_ref.dtype)

    if reverse:
        halo_map = lambda i: (jnp.minimum((i + 1) * hpb, t // POOL_HALO - 1), 0)
    else:
        halo_map = lambda i: (jnp.maximum(i * hpb - 1, 0), 0)
    return pl.pallas_call(
        body, name=name, grid=(nt,),
        in_specs=[pl.BlockSpec((tb, c), lambda i: (i, 0)), pl.BlockSpec((POOL_HALO, c), halo_map)],
        out_specs=pl.BlockSpec((tb, c), lambda i: (i, 0)),
        out_shape=jax.ShapeDtypeStruct((t, c), BF16), compiler_params=_params("arbitrary"),
    )(u, u)


def _head_masks(shape):
    lane = lax.broadcasted_iota(jnp.int32, shape, 1)
    return lane < HEAD_DIM, lane >= HEAD_DIM


def _pick(mask, a):
    return jnp.where(mask, a, jnp.zeros_like(a))


def _dot(a, b, mode):
    return lax.dot_general(a, b, (_DOT_DIMS[mode], ((), ())), preferred_element_type=F32)


def _dot_tri(a, tri, suffix):
    h = a.shape[1] // 2
    lo, hi = a[:, :h], a[:, h:]
    s_lo, s_hi = jnp.sum(lo, axis=1, keepdims=True), jnp.sum(hi, axis=1, keepdims=True)
    p_lo, p_hi = _dot(lo.astype(BF16), tri, "nn"), _dot(hi.astype(BF16), tri, "nn")
    if suffix:
        p_lo = p_lo + s_hi
    else:
        p_hi = p_hi + s_lo
    return jnp.concatenate([p_lo, p_hi], axis=1), s_lo + s_hi


def _log_gates(z):
    nz = -z
    l = jnp.log(1.0 + jnp.exp(jnp.minimum(z, nz)))
    ln = jnp.minimum(nz, 0.0) - l
    return ln, z + ln


def _sb_blocks(s, tq, tk):
    tq, tk = min(tq, s), min(tk, s)
    assert tq == tk and s % tk == 0 and tk % 64 == 0, (s, tq, tk)
    return tq, tk, tk // 2


def _strict_triangle(n, pred):
    return pred(lax.broadcasted_iota(jnp.int32, (n, n), 0), lax.broadcasted_iota(jnp.int32, (n, n), 1)).astype(BF16)


def _sb_fwd(proj, kv, late_shards, tq=SB_TQ, tk=SB_TK):
    s = proj.shape[0]
    tq, tk, th = _sb_blocks(s, tq, tk)
    npair = SB_WIDTH // LANES
    gather = _Gather({k: v.shape for k, v in late_shards.items()})
    ng = gather.n

    def body(q_ref, k_ref, v_ref, *rest):
        o_ref, tot_ref = rest[ng:ng + 2]
        comm = rest[:ng], rest[ng + 2:2 * ng + 2], rest[2 * ng + 2:2 * ng + 5], rest[2 * ng + 5:]

        @pl.when(pl.program_id(0) == 0)
        def _():
            gather.start(*comm)

        tri_gt = _strict_triangle(tk, lambda j, s_: j > s_)
        tri_gt_h = _strict_triangle(th, lambda j, s_: j > s_)
        seen = lax.broadcasted_iota(jnp.int32, (tq, th), 1) < lax.broadcasted_iota(jnp.int32, (tq, th), 0)
        m_a, m_b = _head_masks((tq, LANES))

        def block(qh, k2, v2, lane_mask, carry, acc, mask, tri):
            ln_full, lsz = _log_gates(_dot(qh, k2, "nt"))
            ln = ln_full if mask is None else jnp.where(mask, ln_full, 0.0)
            w = jnp.exp(lsz + _dot(ln.astype(BF16), tri, "nn"))
            if mask is not None:
                w = jnp.where(mask, w, 0.0)
            acc = acc + jnp.exp(carry) * _dot(w.astype(BF16), _pick(lane_mask[:v2.shape[0]], v2), "nn")
            return carry + jnp.sum(ln, axis=1, keepdims=True), acc

        def q_block(qi, _):
            q0 = pl.multiple_of(qi * tq, tq)
            q2 = q_ref[pl.ds(q0, tq), :] * ATT_SCALE
            qa, qb = _pick(m_a, q2), _pick(m_b, q2)

            def both(qa, qb, k0, size, ca, cb, acc, mask, tri):
                k2 = k_ref[pl.ds(k0, size), :]
                v2 = v_ref[pl.ds(k0, size), :]
                ca, acc = block(qa, k2, v2, m_a, ca, acc, mask, tri)
                cb, acc = block(qb, k2, v2, m_b, cb, acc, mask, tri)
                return ca, cb, acc

            zero_c = jnp.zeros((th, 1), F32)
            zero_o = jnp.zeros((th, LANES), F32)
            ca, cb, acc = both(qa[th:], qb[th:], pl.multiple_of(q0 + th, th), th, zero_c, zero_c, zero_o, seen[:th], tri_gt_h)
            ca, cb = jnp.concatenate([zero_c, ca], axis=0), jnp.concatenate([zero_c, cb], axis=0)
            carry = both(qa, qb, q0, th, ca, cb, jnp.concatenate([zero_o, acc], axis=0), seen, tri_gt_h)

            def k_block(k0, carry):
                return both(qa, qb, pl.multiple_of(k0, tk), tk, *carry, None, tri_gt)

            def two_blocks(step, carry):
                k0 = q0 - (2 * step + 1) * tk
                return k_block(k0 - tk, k_block(k0, carry))

            carry = lax.fori_loop(0, qi // 2, two_blocks, carry)
            ca, cb, acc = lax.fori_loop(0, qi % 2, lambda _, c_: k_block(0, c_), carry)
            o_ref[pl.ds(q0, tq), :] = acc.astype(o_ref.dtype)
            tot_ref[0, pl.ds(q0, tq), :] = jnp.broadcast_to(ca, (tq, LANES))
            tot_ref[1, pl.ds(q0, tq), :] = jnp.broadcast_to(cb, (tq, LANES))
            return 0

        lax.fori_loop(0, s // tq, q_block, 0)

        @pl.when(pl.program_id(0) == npair - 1)
        def _():
            gather.finish(*comm)

    outs = pl.pallas_call(
        body, name="sb_fwd", grid=(npair,),
        in_specs=[pl.BlockSpec((s, LANES), lambda p: (0, p)), pl.BlockSpec((s, LANES), lambda p: (0, p)),
                  pl.BlockSpec((s, LANES), lambda p: (0, npair + p))] + [ANY] * ng,
        out_specs=[pl.BlockSpec((s, LANES), lambda p: (0, p)), pl.BlockSpec((None, 2, s, LANES), lambda p: (p, 0, 0, 0))]
        + [ANY] * ng,
        out_shape=[jax.ShapeDtypeStruct((s, SB_WIDTH), BF16), jax.ShapeDtypeStruct((npair, 2, s, LANES), F32)]
        + gather.out_shape(),
        scratch_shapes=gather.scratch(), compiler_params=_params("arbitrary"),
    )(proj, kv, kv, *[late_shards[k] for k in gather.names])
    return outs[0], outs[1], dict(zip(gather.names, outs[2:]))


def _sb_bwd(proj, kv, dcat, tot, early_partial, tq=SB_TQ, tk=SB_TK):
    s = proj.shape[0]
    tq, tk, th = _sb_blocks(s, tq, tk)
    npair = SB_WIDTH // LANES
    exchange = _ChipExchange(early_partial)
    ne = exchange.n

    def body(q_ref, k_ref, v_ref, do_ref, tot_ref, *rest):
        dq_ref, dk_ref, dv_ref = rest[ne:ne + 3]
        dk_acc, dv_acc = rest[2 * ne + 3:2 * ne + 5]
        comm = rest[:ne], rest[ne + 3:2 * ne + 3], rest[2 * ne + 5:]

        @pl.when(pl.program_id(0) == 0)
        def _():
            exchange.start(*comm)

        tris = (_strict_triangle(tk // 2, lambda j, s_: j > s_), _strict_triangle(tk // 2, lambda j, s_: j < s_))
        tris_h = (_strict_triangle(th // 2, lambda j, s_: j > s_), _strict_triangle(th // 2, lambda j, s_: j < s_))
        seen = lax.broadcasted_iota(jnp.int32, (tq, th), 1) < lax.broadcasted_iota(jnp.int32, (tq, th), 0)
        m_a, m_b = _head_masks((tq, LANES))
        dk_acc[...] = jnp.zeros_like(dk_acc)
        dv_acc[...] = jnp.zeros_like(dv_acc)

        def block(qh, doh, k2, v2, lane_mask, tot_h, carry, mask, tri):
            c_ln, c_d, dq = carry
            ln_full, lsz = _log_gates(_dot(qh, k2, "nt"))
            ln = ln_full if mask is None else jnp.where(mask, ln_full, 0.0)
            inside, total = _dot_tri(ln, tri[0], True)
            c_ln = c_ln + total
            w = jnp.exp(lsz + ((tot_h - c_ln) + inside))
            if mask is not None:
                w = jnp.where(mask, w, 0.0)
            dlw = _dot(doh, v2, "nt") * w
            before, d_total = _dot_tri(dlw, tri[1], False)
            dz = dlw * jnp.exp(ln_full) - (before + c_d) * jnp.exp(lsz)
            if mask is not None:
                dz = jnp.where(mask, dz, 0.0)
            dz = dz.astype(BF16)
            dq = dq + _dot(dz, _pick(lane_mask[:k2.shape[0]], k2), "nn")
            dk = _dot(dz, qh, "tn")
            dv = _dot(w.astype(BF16), doh, "tn")
            carry = (c_ln, c_d + d_total, dq)
            return carry, dk, dv

        def q_block(qi, _):
            q0 = pl.multiple_of(qi * tq, tq)
            q2 = q_ref[pl.ds(q0, tq), :] * ATT_SCALE
            do2 = do_ref[pl.ds(q0, tq), :]
            qa, qb = _pick(m_a, q2), _pick(m_b, q2)
            doa, dob = _pick(m_a, do2), _pick(m_b, do2)
            tot_a = tot_ref[0, pl.ds(q0, tq), 0:1]
            tot_b = tot_ref[1, pl.ds(q0, tq), 0:1]
            zero_c = jnp.zeros((tq, 1), F32)
            zero_q = jnp.zeros((tq, LANES), F32)

            def both(rows, k0, size, ca, cb, mask, tri):
                k2 = k_ref[pl.ds(k0, size), :]
                v2 = v_ref[pl.ds(k0, size), :]
                ca, dka, dva = block(qa[rows], doa[rows], k2, v2, m_a, tot_a[rows], ca, mask, tri)
                cb, dkb, dvb = block(qb[rows], dob[rows], k2, v2, m_b, tot_b[rows], cb, mask, tri)
                dk_acc[pl.ds(k0, size), :] += dka + dkb
                dv_acc[pl.ds(k0, size), :] += dva + dvb
                return ca, cb

            def k_block(k0, carry):
                return both(slice(None), pl.multiple_of(k0, tk), tk, carry[0], carry[1], None, tris)

            def two_blocks(step, carry):
                return k_block((2 * step + 1) * tk, k_block(2 * step * tk, carry))

            init = ((zero_c, zero_c, zero_q), (zero_c, zero_c, zero_q))
            carry = lax.fori_loop(0, qi // 2, two_blocks, init)
            ca, cb = lax.fori_loop(0, qi % 2, lambda _, c_: k_block(q0 - tk, c_), carry)
            ca, cb = both(slice(None), q0, th, ca, cb, seen, tris_h)
            late = slice(th, tq)
            la, lb = both(late, pl.multiple_of(q0 + th, th), th, tuple(t[late] for t in ca), tuple(t[late] for t in cb),
                          seen[:th], tris_h)
            dq = jnp.concatenate([ca[2][:th] + cb[2][:th], la[2] + lb[2]], axis=0)
            dq_ref[pl.ds(q0, tq), :] = (dq * ATT_SCALE).astype(dq_ref.dtype)
            return 0

        lax.fori_loop(0, s // tq, q_block, 0)
        dk_ref[...] = dk_acc[...].astype(dk_ref.dtype)
        dv_ref[...] = dv_acc[...].astype(dv_ref.dtype)

        @pl.when(pl.program_id(0) == npair - 1)
        def _():
            exchange.finish(*comm)

    col = lambda off: pl.BlockSpec((s, LANES), functools.partial(lambda p, off: (0, off + p), off=off))
    outs = pl.pallas_call(
        body, name="sb_bwd", grid=(npair,),
        in_specs=[col(0), col(0), col(npair), col(0), pl.BlockSpec((None, 2, s, LANES), lambda p: (p, 0, 0, 0))]
        + [ANY] * ne,
        out_specs=[col(0), col(0), col(0)] + [ANY] * ne,
        out_shape=[jax.ShapeDtypeStruct((s, SB_WIDTH), BF16)] * 3 + exchange.out_shape(),
        scratch_shapes=[pltpu.VMEM((s, LANES), F32), pltpu.VMEM((s, LANES), F32)] + exchange.scratch(),
        compiler_params=_params("arbitrary"),
    )(proj, kv, kv, dcat, tot, *[early_partial[k] for k in exchange.names])
    dq, dk, dv = outs[:3]
    return dq, jnp.concatenate([dk, dv], axis=1), dict(zip(exchange.names, outs[3:]))


def _mem_fwd(name, proj, mkv, tq=512):
    s = proj.shape[0]
    tq = min(tq, s)
    qblk = SB_WIDTH // MEM_WIDTH

    def body(q_ref, kv_ref, o_ref):
        m_a, m_b = _head_masks((tq, LANES))
        mk_a, mk_b = _head_masks((kv_ref.shape[0], LANES))
        for p in range(MEM_WIDTH // LANES):
            q2 = q_ref[:, p * LANES:(p + 1) * LANES].astype(BF16)
            k2 = kv_ref[:, p * LANES:(p + 1) * LANES]
            v2 = kv_ref[:, MEM_WIDTH + p * LANES:MEM_WIDTH + (p + 1) * LANES]
            acc = jnp.zeros((tq, LANES), F32)
            for mq, mk in ((m_a, mk_a), (m_b, mk_b)):
                logits = _dot(_pick(mq, q2), k2, "nt") * ATT_SCALE
                e = jnp.exp(logits - jnp.max(logits, axis=-1, keepdims=True))
                prob = e / jnp.sum(e, axis=-1, keepdims=True)
                acc = acc + _dot(prob.astype(BF16), _pick(mk, v2), "nn")
            o_ref[:, p * LANES:(p + 1) * LANES] = acc.astype(o_ref.dtype)

    return pl.pallas_call(
        body, name=name, grid=(s // tq,),
        in_specs=[pl.BlockSpec((tq, MEM_WIDTH), lambda i: (i, qblk)), pl.BlockSpec(mkv.shape, lambda i: (0, 0))],
        out_specs=pl.BlockSpec((tq, MEM_WIDTH), lambda i: (i, 0)),
        out_shape=jax.ShapeDtypeStruct((s, MEM_WIDTH), BF16), compiler_params=_params("arbitrary"),
    )(proj, mkv)


def _mem_bwd(name, proj, mkv, dcat, tq=512):
    s = proj.shape[0]
    tq = min(tq, s)
    qblk = SB_WIDTH // MEM_WIDTH

    def body(q_ref, kv_ref, do_ref, dq_ref, dkv_ref):
        @pl.when(pl.program_id(0) == 0)
        def _():
            dkv_ref[...] = jnp.zeros_like(dkv_ref)

        m_a, m_b = _head_masks((tq, LANES))
        for p in range(MEM_WIDTH // LANES):
            ksl = slice(p * LANES, (p + 1) * LANES)
            vsl = slice(MEM_WIDTH + p * LANES, MEM_WIDTH + (p + 1) * LANES)
            q2, do2 = q_ref[:, ksl].astype(BF16), do_ref[:, ksl]
            k2, v2 = kv_ref[:, ksl], kv_ref[:, vsl]
            mk_a, mk_b = _head_masks(k2.shape)
            dq = jnp.zeros((tq, LANES), F32)
            dk = jnp.zeros(k2.shape, F32)
            dv = jnp.zeros(k2.shape, F32)
            for mq, mk in ((m_a, mk_a), (m_b, mk_b)):
                qh, doh = _pick(mq, q2), _pick(mq, do2)
                logits = _dot(qh, k2, "nt") * ATT_SCALE
                e = jnp.exp(logits - jnp.max(logits, axis=-1, keepdims=True))
                prob = e / jnp.sum(e, axis=-1, keepdims=True)
                dp = _dot(doh, v2, "nt")
                ds = prob * (dp - jnp.sum(dp * prob, axis=-1, keepdims=True)) * ATT_SCALE
                ds = ds.astype(BF16)
                dq = dq + _dot(ds, _pick(mk, k2), "nn")
                dk = dk + _dot(ds, qh, "tn")
                dv = dv + _dot(prob.astype(BF16), doh, "tn")
            dq_ref[:, ksl] = dq.astype(dq_ref.dtype)
            dkv_ref[:, ksl] += dk
            dkv_ref[:, vsl] += dv

    return pl.pallas_call(
        body, name=name, grid=(s // tq,),
        in_specs=[pl.BlockSpec((tq, MEM_WIDTH), lambda i: (i, qblk)), pl.BlockSpec(mkv.shape, lambda i: (0, 0)),
                  pl.BlockSpec((tq, MEM_WIDTH), lambda i: (i, qblk))],
        out_specs=[pl.BlockSpec((tq, MEM_WIDTH), lambda i: (i, 0)), pl.BlockSpec(mkv.shape, lambda i: (0, 0))],
        out_shape=[jax.ShapeDtypeStruct((s, MEM_WIDTH), BF16), jax.ShapeDtypeStruct(mkv.shape, F32)],
        compiler_params=_params("arbitrary"),
    )(proj, mkv, dcat)


def _gu_swiglu(name, h, w_gu, tn=256):
    t, d = h.shape
    f = w_gu.shape[1] // 2
    nb = f // tn

    def body(h_ref, wg_ref, wu_ref, g_ref, u_ref, a_ref):
        hh = h_ref[...]
        g = _dot(hh, wg_ref[...], "nn")
        u = _dot(hh, wu_ref[...], "nn")
        g_ref[...] = g.astype(BF16)
        u_ref[...] = u.astype(BF16)
        a_ref[...] = (g * jax.nn.sigmoid(g) * u).astype(BF16)

    out = pl.BlockSpec((t, tn), lambda j: (0, j))
    return pl.pallas_call(
        body, name=name, grid=(nb,),
        in_specs=[pl.BlockSpec((t, d), lambda j: (0, 0)), pl.BlockSpec((d, tn), lambda j: (0, j)),
                  pl.BlockSpec((d, tn), lambda j: (0, nb + j))],
        out_specs=[out, out, out], out_shape=[jax.ShapeDtypeStruct((t, f), BF16)] * 3,
        compiler_params=_params("arbitrary"),
    )(h, w_gu, w_gu)


def _down_dx_swiglu(name, dout_bf, w_down, gate, up, tm=256):
    t, d = dout_bf.shape
    f = w_down.shape[0]

    def body(do_ref, w_ref, g_ref, u_ref, o_ref):
        dact = _dot(do_ref[...], w_ref[...], "nt")
        g, u = g_ref[...].astype(F32), u_ref[...].astype(F32)
        sg = jax.nn.sigmoid(g)
        silu = g * sg
        o_ref[:, :f] = (dact * u * (sg + silu * (1.0 - sg))).astype(BF16)
        o_ref[:, f:] = (dact * silu).astype(BF16)

    row = lambda c: pl.BlockSpec((tm, c), lambda i: (i, 0))
    return pl.pallas_call(
        body, name=name, grid=(t // tm,),
        in_specs=[row(d), pl.BlockSpec((f, d), lambda i: (0, 0)), row(f), row(f)],
        out_specs=row(2 * f), out_shape=jax.ShapeDtypeStruct((t, 2 * f), BF16), compiler_params=_params("arbitrary"),
    )(dout_bf, w_down, gate, up)


def _ffn_fwd(tag, x, norm, w_gu, w_down):
    h = _rms_fwd(tag + "_ffn_norm", x, norm)
    gate, up, act = _gu_swiglu(tag + "_gu", h, w_gu)
    out = _mm(tag + "_down", act, w_down, "nn", F32, res=x, tn=1024)
    return out, (h, gate, up, act)


def _ffn_bwd(tag, x, norm, w_gu, w_down, saved, dout, dout_bf):
    h, gate, up, act = saved
    g_down = _mm(tag + "_down_dw", act, dout_bf, "tn", BF16, tm=256, tn=1024)
    dgu = _down_dx_swiglu(tag + "_down_dx", dout_bf, w_down, gate, up)
    g_gu = _mm(tag + "_gu_dw", h, dgu, "tn", BF16, tm=1024)
    dx, dx_bf, g_norm = _mm_nt_rms_bwd(tag + "_gu_dx", dgu, w_gu, x, norm, dout)
    return dx, dx_bf, g_gu, g_down, g_norm


def _mem_kv(tag, mem_n, w_mem_kv):
    return _mm(tag + "_memkv", mem_n, w_mem_kv, "nn", BF16)


def _mem_kv_bwd(tag, mem, mem_norm, mem_n, w_mem_kv, dmkv):
    dmkv = dmkv.astype(BF16)
    g_w = _mm(tag + "_memkv_dw", mem_n, dmkv, "tn", BF16)
    dmem_n = _mm(tag + "_memkv_dx", dmkv, w_mem_kv, "nt", F32)
    (g_norm,) = _rms_bwd(tag + "_memnorm_bwd", mem, mem_norm, dmem_n, want_dx=False)
    return g_w, g_norm


def _block_diag(w_group):
    z = jnp.zeros((POOL_GROUP, POOL_GROUP), w_group.dtype)
    return jnp.concatenate(
        [jnp.concatenate([w_group[g] if h == g else z for h in range(4)], axis=1) for g in range(4)], axis=0)


def _local_step(x, mem, target, w, late_shards, second):
    g = {}
    w = dict(w)
    mem_n = _rms_fwd("mem_norm", mem, w["mem_norm"])
    h_a = _rms_fwd("a_mix_norm", x, w["a_norm_mix"])
    proj_a = _mm("a_in", h_a, w["a_w_in"], "nn", F32, tn=1024)
    pooled = _pool("a_pool", proj_a, reverse=False)
    w_bd = _block_diag(w["a_w_group"])
    g_pre = _mm("a_group", pooled, w_bd, "nn", BF16, tn=768)
    mkv_a = _mem_kv("a", mem_n, w["a_w_mem_kv"])
    mem_a = _mem_fwd("a_mem_attn", proj_a, mkv_a)
    cat_a = _rowwise("a_cat", lambda gp, mo, sc: jnp.concatenate([gp.astype(F32) * sc, mo.astype(F32)], axis=1),
                     [g_pre, mem_a], [w["a_scale"]], [(1024, BF16)])[0]
    x1 = _mm("a_out", cat_a, w["a_w_out"], "nn", F32, res=x, tn=1024)
    w.update(_gather_finish(second[0], second[1], x1))
    x2, ffn_a = _ffn_fwd("a", x1, w["a_norm_ffn"], w["a_w_gu"], w["a_w_down"])
    h_k, h_b = _rms_fwd2("x2_norms", x2, w["kv_norm"], w["b_norm_mix"])
    kv = _mm("kv_proj", h_k, w["w_kv"], "nn", BF16, tn=1536)
    proj_b = _mm("b_q", h_b, w["b_w_q"], "nn", BF16, tn=1024)
    sb_out, tot, late = _sb_fwd(proj_b, kv, late_shards)
    w.update(late)
    mkv_b = _mem_kv("b", mem_n, w["b_w_mem_kv"])
    mem_b = _mem_fwd("b_mem_attn", proj_b, mkv_b)
    cat_b = jnp.concatenate([sb_out, mem_b], axis=1)
    x3 = _mm("b_out", cat_b, w["b_w_out"], "nn", F32, res=x2, tn=1024)
    x4, ffn_b = _ffn_fwd("b", x3, w["b_norm_ffn"], w["b_w_gu"], w["b_w_down"])

    d = x.shape[1]

    def head(xt, tt, gt):
        rstd = lax.rsqrt(jnp.mean(xt * xt, axis=-1, keepdims=True) + EPS)
        xhat = xt * rstd
        err = xhat * gt - tt
        loss = 0.5 * jnp.sum(jnp.sum(err * err, axis=1, keepdims=True), axis=0, keepdims=True) / d
        dy = err / d
        dxhat = dy * gt
        dx = rstd * (dxhat - xhat * jnp.mean(dxhat * xhat, axis=-1, keepdims=True))
        return dx, dx, jnp.sum(dy * xhat, axis=0, keepdims=True), jnp.broadcast_to(loss, (1, LANES))

    dx4, dx4_bf, g["final_norm"], loss = _rowwise(
        "loss_head", head, [x4, target], [w["final_norm"]], [(d, F32), (d, BF16)], [d, LANES])

    dx3, dx3_bf, g["b_w_gu"], g["b_w_down"], g["b_norm_ffn"] = _ffn_bwd(
        "b", x3, w["b_norm_ffn"], w["b_w_gu"], w["b_w_down"], ffn_b, dx4, dx4_bf)
    dcat_b = _mm("b_out_dx", dx3_bf, w["b_w_out"], "nt", BF16, tn=1024)
    g["b_w_out"] = _mm("b_out_dw", cat_b, dx3_bf, "tn", BF16, tm=1024, tn=1024)
    early_partial = _pair_sums("early", {k: g.pop(k) for k in EARLY})
    dq_sb, dkv, early_got = _sb_bwd(proj_b, kv, dcat_b, tot, early_partial)
    dq_mem_b, dmkv_b = _mem_bwd("b_mem_attn_bwd", proj_b, mkv_b, dcat_b)
    dproj_b = jnp.concatenate([dq_sb, dq_mem_b], axis=1)
    g["b_w_q"] = _mm("b_q_dw", h_b, dproj_b, "tn", BF16, tm=1024, tn=1024)
    dh_b = _mm("b_q_dx", dproj_b, w["b_w_q"], "nt", F32, tn=1024)
    g["b_w_mem_kv"], g_memnorm_b = _mem_kv_bwd("b", mem, w["mem_norm"], mem_n, w["b_w_mem_kv"], dmkv_b)
    g["w_kv"] = _mm("kv_proj_dw", h_k, dkv, "tn", BF16, tm=1024)
    dh_k = _mm("kv_proj_dx", dkv, w["w_kv"], "nt", F32, tn=1024)
    dx2, dx2_bf, g["kv_norm"], g["b_norm_mix"] = _rms_bwd2(
        "x2_norms_bwd", x2, w["kv_norm"], dh_k, w["b_norm_mix"], dh_b, dx3)
    post_partial = _pair_sums("post_sb", {k: g.pop(k) for k in POST_SB})
    post_flight = _chip_exchange_start("post_sb", post_partial)
    norm_ffn_a = w["a_norm_ffn"] + post_flight[-1][0:1, 0:1]

    dx1, dx1_bf, g["a_w_gu"], g["a_w_down"], g["a_norm_ffn"] = _ffn_bwd(
        "a", x1, norm_ffn_a, w["a_w_gu"], w["a_w_down"], ffn_a, dx2, dx2_bf)
    mid_partial = _pair_sums("mid", {k: g.pop(k) for k in MID})
    mid_flight = _chip_exchange_start("mid", mid_partial)
    scale_a = w["a_scale"] + mid_flight[-1][0:1, 0:1]
    dcat_a = _mm("a_out_dx", dx1_bf, w["a_w_out"], "nt", BF16, tn=1024)
    g["a_w_out"] = _mm("a_out_dw", cat_a, dx1_bf, "tn", BF16, tm=1024, tn=1024)

    def scale_bwd(dc, gp, sc):
        dc, gp = dc.astype(F32), gp.astype(F32)
        return dc * sc, jnp.sum(dc * gp, axis=0, keepdims=True)

    dg_pre, g["a_scale"] = _rowwise("a_scale_bwd", scale_bwd, [(dcat_a, (SB_WIDTH, 0)), g_pre], [scale_a],
                                    [(SB_WIDTH, BF16)], [SB_WIDTH])
    g_bd = _mm("a_group_dw", pooled, dg_pre, "tn", F32, tm=768, tn=768)
    g["a_w_group"] = jnp.stack([g_bd[i * POOL_GROUP:(i + 1) * POOL_GROUP, i * POOL_GROUP:(i + 1) * POOL_GROUP]
                                for i in range(4)])
    dpooled = _mm("a_group_dx", dg_pre, w_bd, "nt", F32, tn=768)
    du_pool = _pool("a_pool_bwd", dpooled, reverse=True)
    dq_mem_a, dmkv_a = _mem_bwd("a_mem_attn_bwd", proj_a, mkv_a, dcat_a)
    dproj_a = jnp.concatenate([du_pool, dq_mem_a], axis=1)
    g["a_w_in"] = _mm("a_in_dw", h_a, dproj_a, "tn", BF16, tm=1024, tn=1024)
    grad_x, _, g["a_norm_mix"] = _mm_nt_rms_bwd("a_in_dx", dproj_a, w["a_w_in"], x, w["a_norm_mix"], dx1)
    g["a_w_mem_kv"], g_memnorm_a = _mem_kv_bwd("a", mem, w["mem_norm"], mem_n, w["a_w_mem_kv"], dmkv_a)
    g["mem_norm"] = g_memnorm_a + g_memnorm_b
    post_partial, post_got = _chip_exchange_wait("post_sb", post_partial, post_flight, g["a_norm_mix"])
    mid_partial, mid_got = _chip_exchange_wait("mid", mid_partial, mid_flight, g["a_norm_mix"])
    return (loss, grad_x, g, {**early_partial, **post_partial, **mid_partial}, {**early_got, **post_got, **mid_got})


ROW_SHARDED = ("a_w_in", "a_w_mem_kv", "a_w_out", "a_w_down", "b_w_q", "b_w_mem_kv", "b_w_out", "b_w_down")
COL_SHARDED = ("a_w_gu", "w_kv", "b_w_gu")
BIG = ("a_w_in", "a_w_mem_kv", "a_w_out", "a_w_gu", "a_w_down", "w_kv", "b_w_q", "b_w_mem_kv", "b_w_out", "b_w_gu",
       "b_w_down")
LATE = ("b_w_mem_kv", "b_w_out", "b_w_gu", "b_w_down")
EARLY = ("b_w_gu", "b_w_down", "b_w_out")
POST_SB = ("w_kv", "b_w_q", "b_w_mem_kv")
MID = ("a_w_gu", "a_w_down")
SECOND = ("a_w_gu", "a_w_down", "w_kv", "b_w_q")
N_CHIPS = 4
N_DEV = 8


def _position():
    x, y, c = lax.axis_index("x"), lax.axis_index("y"), lax.axis_index("c")
    other_chips = [(1 - x, y), (x, 1 - y), (1 - x, 1 - y)]
    return x, y, c, other_chips


def _remote(src, dst, send_sem, recv_sem, device):
    return pltpu.make_async_remote_copy(src_ref=src, dst_ref=dst, send_sem=send_sem, recv_sem=recv_sem,
                                        device_id=device, device_id_type=MESH)


def _comm_call(name, body, args, out_shape, n_remote, aliases=None):
    return pl.pallas_call(
        body, name=name, in_specs=[ANY] * len(args), out_specs=[ANY] * len(out_shape), out_shape=out_shape,
        scratch_shapes=[pltpu.SemaphoreType.DMA((n_remote,)), pltpu.SemaphoreType.DMA((n_remote,))],
        input_output_aliases=aliases or {},
    )(*args)


def _row_chunks(nrows, row_bytes, mult=16):
    assert nrows % mult == 0, (nrows, mult)
    per = max(mult, (COPY_BYTES // row_bytes) // mult * mult)
    return [(r0, min(per, nrows - r0)) for r0 in range(0, nrows, per)]


def _rows(ref, start, size, lead=()):
    if isinstance(start, int):
        return ref.at[(*lead, pl.ds(start, size))]
    return ref.at[(*lead, pl.ds(pl.multiple_of(start, 16), size))]


class _Copies:
    def __init__(self, send, recv):
        self.send, self.recv = send, recv
        self.n_remote = 0
        self.remotes = []

    def slot(self):
        self.n_remote += 1
        return self.n_remote - 1

    def remote(self, k, src, dst, device):
        cp = _remote(src, dst, self.send.at[k], self.recv.at[k], device)
        cp.start()
        self.remotes.append(cp)
        return cp

    def finish(self):
        for cp in self.remotes:
            cp.wait_send()


def _shard_cols(ref, row_sharded, cdim, chip):
    if row_sharded:
        return ref
    return ref.at[:, pl.ds(pl.multiple_of(chip * cdim, LANES), cdim)]


class _Gather:
    def __init__(self, shapes, mult=16):
        self.names = list(shapes)
        self.shapes = [tuple(shapes[k]) for k in self.names]
        self.row_sharded = [k in ROW_SHARDED for k in self.names]
        self.chunks = [(i, r0, size) for i, (r, cdim) in enumerate(self.shapes)
                       for r0, size in _row_chunks(r // 2, cdim * 2, mult)]
        self.n = len(self.names)

    def out_shape(self):
        return [jax.ShapeDtypeStruct((N_CHIPS * r, cdim) if rs else (r, N_CHIPS * cdim), BF16)
                for (r, cdim), rs in zip(self.shapes, self.row_sharded)]

    def scratch(self):
        n_remote, n_local = 6 * len(self.chunks), self.n + 2 * len(self.chunks)
        return [pltpu.SemaphoreType.DMA((n_remote,)), pltpu.SemaphoreType.DMA((n_remote,)),
                pltpu.SemaphoreType.DMA((n_local,))] + [pltpu.VMEM(s, BF16) for s in self.shapes]

    def _window(self, dst, i, chip, half, r0, size):
        r, cdim = self.shapes[i]
        base = (chip * r if self.row_sharded[i] else 0) + half * (r // 2) + r0
        return _rows(_shard_cols(dst[i], self.row_sharded[i], cdim, chip), base, size)

    def _mine(self, refs, i, half, r0, size):
        return _rows(refs[i], half * (self.shapes[i][0] // 2) + r0, size)

    def _sent(self, src, dst, sems, q, k, px, py, c, me):
        i, r0, size = self.chunks[q]
        return _remote(self._mine(src, i, c, r0, size), self._window(dst, i, me, c, r0, size),
                       sems[0].at[6 * q + k], sems[1].at[6 * q + k], (px, py, c))

    def start(self, src, dst, sems, vm):
        x, y, c, chips = _position()
        for q in range(len(self.chunks)):
            for k, (px, py) in enumerate(chips):
                self._sent(src, dst, sems, q, k, px, py, c, 2 * x + y).start()
        for i in range(self.n):
            pltpu.make_async_copy(src[i], vm[i], sems[2].at[i]).start()

    def start_ici(self, src, dst, sems):
        x, y, c, chips = _position()
        for q in range(len(self.chunks)):
            for k, (px, py) in enumerate(chips):
                self._sent(src, dst, sems, q, k, px, py, c, 2 * x + y).start()

    def wait_ici(self, src, dst, sems):
        x, y, c, chips = _position()
        for q, (i, r0, size) in enumerate(self.chunks):
            for k, (px, py) in enumerate(chips):
                landed = self._window(dst, i, 2 * px + py, c, r0, size)
                _remote(landed, landed, sems[0].at[6 * q + k], sems[1].at[6 * q + k], (px, py, c)).wait_recv()
        for q in range(len(self.chunks)):
            for k, (px, py) in enumerate(chips):
                self._sent(src, dst, sems, q, k, px, py, c, 2 * x + y).wait_send()

    def finish(self, src, dst, sems, vm, landed=None):
        x, y, c, chips = _position()
        me = 2 * x + y
        send, recv, loc = sems
        arrived = dst if landed is None else landed
        if landed is not None:
            for i in range(self.n):
                pltpu.make_async_copy(src[i], vm[i], loc.at[i]).start()
        for i in range(self.n):
            pltpu.make_async_copy(src[i], vm[i], loc.at[i]).wait()
        placed, forwarded = [], []
        for q, (i, r0, size) in enumerate(self.chunks):
            for half in range(2):
                cp = pltpu.make_async_copy(self._mine(vm, i, half, r0, size), self._window(dst, i, me, half, r0, size),
                                           loc.at[self.n + 2 * q + half])
                cp.start()
                placed.append(cp)
        for q, (i, r0, size) in enumerate(self.chunks):
            for k, (px, py) in enumerate(chips):
                if landed is None:
                    there = self._window(dst, i, 2 * px + py, c, r0, size)
                    _remote(there, there, send.at[6 * q + k], recv.at[6 * q + k], (px, py, c)).wait_recv()
                cp = _remote(self._window(arrived, i, 2 * px + py, c, r0, size), self._window(dst, i, 2 * px + py, c, r0, size),
                             send.at[6 * q + 3 + k], recv.at[6 * q + 3 + k], (x, y, 1 - c))
                cp.start()
                forwarded.append(cp)
        for q, (i, r0, size) in enumerate(self.chunks):
            for k, (px, py) in enumerate(chips):
                there = self._window(dst, i, 2 * px + py, 1 - c, r0, size)
                _remote(there, there, send.at[6 * q + 3 + k], recv.at[6 * q + 3 + k], (x, y, 1 - c)).wait_recv()
        if landed is None:
            for q in range(len(self.chunks)):
                for k, (px, py) in enumerate(chips):
                    self._sent(src, dst, sems, q, k, px, py, c, me).wait_send()
        for cp in forwarded:
            cp.wait_send()
        for cp in placed:
            cp.wait()


class _GatherRelay(_Gather):
    def __init__(self, shapes):
        super().__init__(shapes, mult=32)

    def scratch(self):
        n_remote, n_local = 8 * len(self.chunks), self.n + 2 * len(self.chunks)
        return [pltpu.SemaphoreType.DMA((n_remote,)), pltpu.SemaphoreType.DMA((n_remote,)),
                pltpu.SemaphoreType.DMA((n_local,))] + [pltpu.VMEM(s, BF16) for s in self.shapes]

    def run(self, src, dst, sems, vm):
        x, y, c, chips = _position()
        me, diag, sibling = 2 * x + y, 2 * (1 - x) + (1 - y), (x, y, 1 - c)
        nbrs = chips[:2]
        send, recv, loc = sems

        def copy(slot, ref, device):
            return _remote(ref, ref, send.at[slot], recv.at[slot], device)

        first = []
        for q, (i, r0, size) in enumerate(self.chunks):
            for k, (px, py) in enumerate(nbrs):
                cp = _remote(self._mine(src, i, c, r0, size), self._window(dst, i, me, c, r0, size),
                             send.at[8 * q + k], recv.at[8 * q + k], (px, py, c))
                cp.start()
                first.append(cp)
        for i in range(self.n):
            pltpu.make_async_copy(src[i], vm[i], loc.at[i]).start()
        for i in range(self.n):
            pltpu.make_async_copy(src[i], vm[i], loc.at[i]).wait()
        placed, passed = [], []
        for q, (i, r0, size) in enumerate(self.chunks):
            for half in range(2):
                cp = pltpu.make_async_copy(self._mine(vm, i, half, r0, size), self._window(dst, i, me, half, r0, size),
                                           loc.at[self.n + 2 * q + half])
                cp.start()
                placed.append(cp)
        for q, (i, r0, size) in enumerate(self.chunks):
            for k, (px, py) in enumerate(nbrs):
                landed = self._window(dst, i, 2 * px + py, c, r0, size)
                copy(8 * q + k, landed, (px, py, c)).wait_recv()
                piece = self._window(dst, i, 2 * px + py, c, r0 + k * (size // 2), size // 2)
                ox, oy = nbrs[1 - k]
                for cp in (copy(8 * q + 2 + k, piece, (ox, oy, c)), copy(8 * q + 4 + k, landed, sibling)):
                    cp.start()
                    passed.append(cp)
        for q, (i, r0, size) in enumerate(self.chunks):
            for k in range(2):
                piece = self._window(dst, i, diag, c, r0 + k * (size // 2), size // 2)
                ox, oy = nbrs[1 - k]
                copy(8 * q + 2 + k, piece, (ox, oy, c)).wait_recv()
                cp = copy(8 * q + 6 + k, piece, sibling)
                cp.start()
                passed.append(cp)
        for q, (i, r0, size) in enumerate(self.chunks):
            for k, (px, py) in enumerate(nbrs):
                copy(8 * q + 4 + k, self._window(dst, i, 2 * px + py, 1 - c, r0, size), sibling).wait_recv()
                copy(8 * q + 6 + k, self._window(dst, i, diag, 1 - c, r0 + k * (size // 2), size // 2), sibling).wait_recv()
        for cp in first + passed:
            cp.wait_send()
        for cp in placed:
            cp.wait()


def _gather_weights(shards):
    plan = _GatherRelay({k: v.shape for k, v in shards.items()})
    n = plan.n

    def body(*refs):
        plan.run(refs[:n], refs[n:2 * n], refs[2 * n:2 * n + 3], refs[2 * n + 3:])

    outs = pl.pallas_call(
        body, name="gather_weights", in_specs=[ANY] * n, out_specs=[ANY] * n, out_shape=plan.out_shape(),
        scratch_shapes=plan.scratch(), compiler_params=pltpu.CompilerParams(vmem_limit_bytes=VMEM_LIMIT),
    )(*[shards[k] for k in plan.names])
    return dict(zip(plan.names, outs))


def _gather_start(shards):
    plan = _Gather({k: v.shape for k, v in shards.items()})
    n = plan.n
    n_remote = 6 * len(plan.chunks)
    srcs = [pltpu.with_memory_space_constraint(shards[k], pltpu.HBM) for k in plan.names]
    lands = [pltpu.with_memory_space_constraint(lax.empty(s.shape, s.dtype), pltpu.HBM) for s in plan.out_shape()]

    def body(*refs):
        plan.start_ici(refs[:n], refs[n:2 * n], (refs[2 * n], refs[2 * n + 1]))
        refs[-1][...] = jnp.zeros_like(refs[-1])

    return pl.pallas_call(
        body, name="gather_start",
        out_shape=(pltpu.SemaphoreType.DMA((n_remote,)), pltpu.SemaphoreType.DMA((n_remote,)),
                   *[pltpu.HBM(a.shape, a.dtype) for a in srcs + lands], jax.ShapeDtypeStruct((8, LANES), F32)),
        in_specs=[HBM_SPEC] * (2 * n),
        out_specs=(SEM_SPEC, SEM_SPEC, *[HBM_SPEC] * (2 * n), pl.BlockSpec(memory_space=pltpu.VMEM)),
        input_output_aliases={i: 2 + i for i in range(2 * n)},
        compiler_params=pltpu.CompilerParams(has_side_effects=SIDE_EFFECT),
    )(*srcs, *lands)


def _gather_finish(shapes, flight, after):
    plan = _Gather(shapes)
    n = plan.n
    send, recv, thru = flight[0], flight[1], flight[2:2 + 2 * n]

    def wait_body(*refs):
        plan.wait_ici(refs[:n], refs[n:2 * n], (refs[2 * n], refs[2 * n + 1]))

    thru = pl.pallas_call(
        wait_body, name="gather_wait", out_shape=tuple(pltpu.HBM(t.shape, t.dtype) for t in thru),
        in_specs=[HBM_SPEC] * (2 * n) + [SEM_SPEC, SEM_SPEC, ANY], out_specs=[HBM_SPEC] * (2 * n),
        input_output_aliases={i: i for i in range(2 * n)},
        compiler_params=pltpu.CompilerParams(has_side_effects=SIDE_EFFECT),
    )(*thru, send, recv, after)

    def body(*refs):
        src, landed, dst = refs[:n], refs[n:2 * n], refs[2 * n:3 * n]
        plan.finish(src, dst, refs[3 * n:3 * n + 3], refs[3 * n + 3:], landed=landed)

    outs = pl.pallas_call(
        body, name="gather_finish", in_specs=[ANY] * (2 * n), out_specs=[ANY] * n, out_shape=plan.out_shape(),
        scratch_shapes=plan.scratch(), input_output_aliases={n + i: i for i in range(n)},
        compiler_params=pltpu.CompilerParams(vmem_limit_bytes=VMEM_LIMIT),
    )(*thru)
    return dict(zip(plan.names, outs))


def _scalar_grid_call(name, body, scalars, grid, in_specs, out_specs, out_shape, args):
    return pl.pallas_call(
        body, name=name, out_shape=out_shape,
        grid_spec=pltpu.PrefetchScalarGridSpec(num_scalar_prefetch=1, grid=grid, in_specs=in_specs, out_specs=out_specs),
        compiler_params=_params(*["arbitrary"] * len(grid)),
    )(scalars, *args)


def _pair_sum(name, g4, sib, where, tb=256):
    j, _, r, w = g4.shape
    tb = _row_tile(r, tb)

    def body(s_ref, g_ref, b_ref, o_ref):
        o_ref[...] = (g_ref[...].astype(F32) + b_ref[...].astype(F32)).astype(o_ref.dtype)

    blk = pl.BlockSpec((None, tb, w), lambda a, i, s: (a, i, 0))
    return _scalar_grid_call(
        name, body, where, (j, r // tb),
        [pl.BlockSpec((None, None, tb, w), lambda a, i, s: (a, s[0], i, 0)), blk], blk,
        jax.ShapeDtypeStruct((j, r, w), BF16), (g4, sib))


def _chip_sum(name, partial, got, where, row_sharded, tb=256):
    _, r, cdim = got.shape
    tb = _row_tile(r, tb)

    def body(s_ref, p_ref, g_ref, o_ref):
        acc = p_ref[...].astype(F32)
        for k in range(N_CHIPS - 1):
            acc = acc + g_ref[k].astype(F32)
        o_ref[...] = acc

    if row_sharded:
        own = pl.BlockSpec((None, tb, cdim), lambda i, s: (s[1], i, 0))
    else:
        own = pl.BlockSpec((None, tb, cdim), lambda i, s: (0, i, s[1]))
    return _scalar_grid_call(
        name, body, where, (r // tb,),
        [own, pl.BlockSpec((N_CHIPS - 1, tb, cdim), lambda i, s: (0, i, 0))],
        pl.BlockSpec((None, tb, cdim), lambda i, s: (s[0], i, 0)),
        jax.ShapeDtypeStruct((2, r, cdim), F32), (partial, got))


def _where():
    return jnp.stack([lax.axis_index("c"), 2 * lax.axis_index("x") + lax.axis_index("y")]).astype(jnp.int32)


def _grad_halves(name, g):
    rows, cols = g.shape
    if name in ROW_SHARDED:
        r = rows // N_CHIPS
        return g.reshape(N_CHIPS, 2, r // 2, cols), (r // 2, cols)
    return g.reshape(1, 2, rows // 2, cols), (rows // 2, cols // N_CHIPS)


def _pair_sums(tag, grads):
    names = list(grads)
    n = len(names)
    g4 = [_grad_halves(k, grads[k])[0] for k in names]
    plan = [[(j, r0, size) for j in range(g.shape[0]) for r0, size in _row_chunks(g.shape[2], g.shape[3] * 2)]
            for g in g4]
    out_shape = [jax.ShapeDtypeStruct((g.shape[0],) + g.shape[2:], BF16) for g in g4]

    def body(*refs):
        src, sib = refs[:n], refs[n:2 * n]
        cps = _Copies(*refs[2 * n:])
        x, y, c, _ = _position()
        waits = []
        for i in range(n):
            for j, r0, size in plan[i]:
                waits.append(cps.remote(cps.slot(), _rows(src[i], r0, size, lead=(j, 1 - c)),
                                        _rows(sib[i], r0, size, lead=(j,)), (x, y, 1 - c)))
        for cp in waits:
            cp.wait_recv()
        cps.finish()

    sibs = _comm_call("grads_pair_exchange_" + tag, body, g4, out_shape, sum(len(p) for p in plan))
    where = _where()
    return {k: _pair_sum(k + "_pair_sum", g, s, where) for k, g, s in zip(names, g4, sibs)}


class _ChipExchange:
    def __init__(self, partial, mult=16):
        self.names = list(partial)
        self.n = len(self.names)
        self.row_sharded = [k in ROW_SHARDED for k in self.names]
        self.half = []
        for k, rs in zip(self.names, self.row_sharded):
            _, r, w = partial[k].shape
            self.half.append((r, w) if rs else (r, w // N_CHIPS))
        self.chunks = [(i, r0, size) for i, (r, cdim) in enumerate(self.half)
                       for r0, size in _row_chunks(r, cdim * 2, mult)]

    def shard(self, src, i, chip, r0, size):
        if self.row_sharded[i]:
            return _rows(src[i], r0, size, lead=(chip,))
        return _rows(_shard_cols(src[i].at[0], False, self.half[i][1], chip), r0, size)

    def out_shape(self):
        return [jax.ShapeDtypeStruct((N_CHIPS - 1,) + s, BF16) for s in self.half]

    def scratch(self):
        n_remote = 3 * len(self.chunks)
        return [pltpu.SemaphoreType.DMA((n_remote,)), pltpu.SemaphoreType.DMA((n_remote,))]

    def _copies(self, src, dst, sems):
        x, y, c, chips = _position()
        for q, (i, r0, size) in enumerate(self.chunks):
            for k, (px, py) in enumerate(chips):
                yield _remote(self.shard(src, i, 2 * px + py, r0, size), _rows(dst[i], r0, size, lead=(k,)),
                              sems[0].at[3 * q + k], sems[1].at[3 * q + k], (px, py, c))

    def start(self, src, dst, sems):
        for cp in self._copies(src, dst, sems):
            cp.start()

    def finish(self, src, dst, sems):
        for cp in self._copies(src, dst, sems):
            cp.wait_recv()
        for cp in self._copies(src, dst, sems):
            cp.wait_send()


def _chip_exchange(partial):
    plan = _ChipExchange(partial)
    n = plan.n

    def body(*refs):
        parts = refs[:n], refs[n:2 * n], refs[2 * n:]
        plan.start(*parts)
        plan.finish(*parts)

    got = pl.pallas_call(
        body, name="grads_chip_exchange", in_specs=[ANY] * n, out_specs=[ANY] * n, out_shape=plan.out_shape(),
        scratch_shapes=plan.scratch(),
    )(*[partial[k] for k in plan.names])
    return dict(zip(plan.names, got))


HBM_SPEC = pl.BlockSpec(memory_space=pltpu.HBM)
SEM_SPEC = pl.BlockSpec(memory_space=pltpu.SEMAPHORE)
SIDE_EFFECT = pltpu.SideEffectType.DATAFLOW_SIDE_EFFECTING


def _chip_exchange_start(tag, partial):
    plan = _ChipExchange(partial)
    n = plan.n
    n_remote = 3 * len(plan.chunks)
    srcs = [pltpu.with_memory_space_constraint(partial[k], pltpu.HBM) for k in plan.names]
    lands = [pltpu.with_memory_space_constraint(lax.empty(s.shape, s.dtype), pltpu.HBM) for s in plan.out_shape()]

    def body(*refs):
        src, land = refs[:n], refs[n:2 * n]
        plan.start(src, land, (refs[2 * n], refs[2 * n + 1]))
        refs[-1][...] = jnp.zeros_like(refs[-1])

    return pl.pallas_call(
        body, name="grads_chip_exchange_start_" + tag,
        out_shape=(pltpu.SemaphoreType.DMA((n_remote,)), pltpu.SemaphoreType.DMA((n_remote,)),
                   *[pltpu.HBM(a.shape, a.dtype) for a in srcs + lands], jax.ShapeDtypeStruct((8, LANES), F32)),
        in_specs=[HBM_SPEC] * (2 * n),
        out_specs=(SEM_SPEC, SEM_SPEC, *[HBM_SPEC] * (2 * n), pl.BlockSpec(memory_space=pltpu.VMEM)),
        input_output_aliases={i: 2 + i for i in range(2 * n)},
        compiler_params=pltpu.CompilerParams(has_side_effects=SIDE_EFFECT),
    )(*srcs, *lands)


def _chip_exchange_wait(tag, partial, flight, after):
    plan = _ChipExchange(partial)
    n = plan.n
    send, recv, thru = flight[0], flight[1], flight[2:2 + 2 * n]

    def body(*refs):
        plan.finish(refs[:n], refs[n:2 * n], (refs[2 * n], refs[2 * n + 1]))

    outs = pl.pallas_call(
        body, name="grads_chip_exchange_wait_" + tag,
        out_shape=tuple(pltpu.HBM(t.shape, t.dtype) for t in thru),
        in_specs=[HBM_SPEC] * (2 * n) + [SEM_SPEC, SEM_SPEC, ANY], out_specs=[HBM_SPEC] * (2 * n),
        input_output_aliases={i: i for i in range(2 * n)},
        compiler_params=pltpu.CompilerParams(has_side_effects=SIDE_EFFECT),
    )(*thru, send, recv, after)
    return dict(zip(plan.names, outs[:n])), dict(zip(plan.names, outs[n:]))


def _finish_reduce(partial, got):
    names = list(partial)
    n = len(names)
    where = _where()
    halves = [_chip_sum(k + "_chip_sum", partial[k], got[k], where, k in ROW_SHARDED) for k in names]
    plan = [_row_chunks(h.shape[1], h.shape[2] * 4) for h in halves]
    out_shape = [jax.ShapeDtypeStruct(h.shape, F32) for h in halves]

    def body(*refs):
        src, dst = refs[:n], refs[n:2 * n]
        cps = _Copies(*refs[2 * n:])
        x, y, c, _ = _position()
        waits = []
        for i in range(n):
            for r0, size in plan[i]:
                waits.append(cps.remote(cps.slot(), _rows(src[i], r0, size, lead=(c,)), _rows(dst[i], r0, size, lead=(c,)),
                                        (x, y, 1 - c)))
        for cp in waits:
            cp.wait_recv()
        cps.finish()

    outs = _comm_call("grads_pair_share", body, halves, out_shape, sum(len(p) for p in plan),
                      aliases={i: i for i in range(n)})
    return {k: o.reshape(2 * o.shape[1], o.shape[2]) for k, o in zip(names, outs)}


def _all_reduce_small(name, v):
    rows, cols = v.shape
    h = rows // 2

    def body(v_ref, o_ref, sib, pair, buf, send, recv):
        x, y, c, chips = _position()
        me = 2 * x + y
        sibling = (x, y, 1 - c)
        cp = _remote(v_ref, sib, send.at[0], recv.at[0], sibling)
        cp.start()
        cp.wait()
        pair[...] = v_ref[...] + sib[...]
        mine = pl.ds(pl.multiple_of(c * h, 8), h)
        buf[me] = pair[mine, :]
        sends = [_remote(pair.at[mine], buf.at[me], send.at[1 + k], recv.at[1 + k], (px, py, c))
                 for k, (px, py) in enumerate(chips)]
        for cp in sends:
            cp.start()
        for k, (px, py) in enumerate(chips):
            _remote(pair.at[mine], buf.at[2 * px + py], send.at[1 + k], recv.at[1 + k], (px, py, c)).wait_recv()
        for cp in sends:
            cp.wait_send()
        o_ref[mine, :] = (buf[0] + buf[1]) + (buf[2] + buf[3])
        cp = _remote(o_ref.at[mine], o_ref.at[mine], send.at[4], recv.at[4], sibling)
        cp.start()
        cp.wait()

    vm = pl.BlockSpec(memory_space=pltpu.VMEM)
    return pl.pallas_call(
        body, name=name, in_specs=[vm], out_specs=vm, out_shape=jax.ShapeDtypeStruct(v.shape, F32),
        scratch_shapes=[pltpu.VMEM((rows, cols), F32), pltpu.VMEM((rows, cols), F32), pltpu.VMEM((N_CHIPS, h, cols), F32),
                        pltpu.SemaphoreType.DMA((5,)), pltpu.SemaphoreType.DMA((5,))],
        compiler_params=pltpu.CompilerParams(vmem_limit_bytes=VMEM_LIMIT),
    )(v)


def _adamw(name, w, g, m, v):
    def fn(wt, gt, mt, vt):
        mt = ADAM_B1 * mt + (1.0 - ADAM_B1) * gt
        vt = ADAM_B2 * vt + (1.0 - ADAM_B2) * (gt * gt)
        m_hat = mt / (1.0 - ADAM_B1 ** ADAM_STEP)
        v_hat = vt / (1.0 - ADAM_B2 ** ADAM_STEP)
        delta = -ADAM_LR * (m_hat / (jnp.sqrt(v_hat) + ADAM_EPS) + ADAM_WD * wt)
        return delta, mt, vt
    n = w.shape[1]
    return _rowwise(name, fn, [w, g, m, v], [], [(n, F32)] * 3, tb=256)


WEIGHTS = ("mem_norm", "a_norm_mix", "a_w_in", "a_w_group", "a_scale", "a_w_mem_kv", "a_w_out", "a_norm_ffn", "a_w_gu",
           "a_w_down", "kv_norm", "w_kv", "b_norm_mix", "b_w_q", "b_w_mem_kv", "b_w_out", "b_norm_ffn", "b_w_gu",
           "b_w_down", "final_norm")
REPLICATED_VECS = ("mem_norm", "kv_norm", "b_norm_mix", "b_norm_ffn", "final_norm")
SHARDED_VECS = ("a_norm_mix", "a_norm_ffn", "a_scale")
D_MODEL = 1024
GROUP_ROWS = 4 * POOL_GROUP * POOL_GROUP // D_MODEL


def _row(v):
    v = v.reshape(1, -1).astype(F32)
    return jnp.pad(v, ((0, 0), (0, D_MODEL - v.shape[1])))


def _pack_small(t):
    rows = [_row(t[k]) for k in REPLICATED_VECS]
    rows.append(_row(jnp.concatenate([t[k].reshape(-1) for k in SHARDED_VECS])))
    rows.append(jnp.zeros((2, D_MODEL), F32))
    rows.append(t["a_w_group"].reshape(GROUP_ROWS, D_MODEL).astype(F32))
    return jnp.concatenate(rows, axis=0)


def _unpack_small(p, like):
    out = {k: p[i, :].reshape(like[k].shape) for i, k in enumerate(REPLICATED_VECS)}
    off = 0
    for k in SHARDED_VECS:
        size = like[k].size
        out[k] = p[len(REPLICATED_VECS), off:off + size].reshape(like[k].shape)
        off += size
    out["a_w_group"] = p[len(REPLICATED_VECS) + 3:, :].reshape(like["a_w_group"].shape)
    return out


def kernel(x, mem, mem_norm, a_norm_mix, a_w_in, a_w_group, a_scale, a_w_mem_kv, a_w_out, a_norm_ffn, a_w_gu, a_w_down, kv_norm, w_kv, b_norm_mix, b_w_q, b_w_mem_kv, b_w_out, b_norm_ffn, b_w_gu, b_w_down, final_norm, loss_target, m_mem_norm, m_a_norm_mix, m_a_w_in, m_a_w_group, m_a_scale, m_a_w_mem_kv, m_a_w_out, m_a_norm_ffn, m_a_w_gu, m_a_w_down, m_kv_norm, m_w_kv, m_b_norm_mix, m_b_w_q, m_b_w_mem_kv, m_b_w_out, m_b_norm_ffn, m_b_w_gu, m_b_w_down, m_final_norm, v_mem_norm, v_a_norm_mix, v_a_w_in, v_a_w_group, v_a_scale, v_a_w_mem_kv, v_a_w_out, v_a_norm_ffn, v_a_w_gu, v_a_w_down, v_kv_norm, v_w_kv, v_b_norm_mix, v_b_w_q, v_b_w_mem_kv, v_b_w_out, v_b_norm_ffn, v_b_w_gu, v_b_w_down, v_final_norm):
    given = dict(locals())
    wl = {k: given[k] for k in WEIGHTS}
    ml = {k: given["m_" + k] for k in WEIGHTS}
    vl = {k: given["v_" + k] for k in WEIGHTS}
    chip = 2 * lax.axis_index("x") + lax.axis_index("y")

    def mat(a):
        return a.reshape(a.shape[-2], a.shape[-1])

    shards = {k: mat(wl[k]).astype(BF16) for k in BIG}
    second = _gather_start({k: shards[k] for k in SECOND})
    full = _gather_weights({k: shards[k] for k in BIG if k not in LATE + SECOND})
    gains = jnp.zeros((16, D_MODEL), F32)
    for i, k in enumerate(SHARDED_VECS):
        part = wl[k].reshape(1, -1)
        width = part.shape[1]
        gains = lax.dynamic_update_slice(gains, part, (i, chip * width))
    gains = _all_reduce_small("gains_all_gather", gains) * 0.5
    w = dict(full)
    for k in REPLICATED_VECS:
        w[k] = wl[k].reshape(1, D_MODEL)
    w["a_norm_mix"], w["a_norm_ffn"] = gains[0:1] + second[-1][0:1, 0:1], gains[1:2]
    w["a_scale"] = gains[2:3, :SB_WIDTH]
    w["a_w_group"] = wl["a_w_group"][0].astype(BF16)

    loss, grad_x, g, partial, got = _local_step(x[0], mem[0], loss_target[0], w, {k: shards[k] for k in LATE},
                                                ({k: shards[k].shape for k in SECOND}, second))

    rest = _pair_sums("late", {k: g[k] for k in BIG if k not in EARLY + POST_SB + MID})
    partial.update(rest)
    got.update(_chip_exchange(rest))
    red = _finish_reduce(partial, got)
    small = jnp.concatenate(
        [_row(g[k]) for k in REPLICATED_VECS] + [_row(g[k]) for k in SHARDED_VECS] + [_row(loss)]
        + [jnp.zeros((7, D_MODEL), F32), g["a_w_group"].reshape(GROUP_ROWS, D_MODEL)], axis=0)
    small = _all_reduce_small("small_grads_all_reduce", small)
    gs = {k: small[i] for i, k in enumerate(REPLICATED_VECS)}
    for i, k in enumerate(SHARDED_VECS):
        width = wl[k].shape[-1]
        gs[k] = lax.dynamic_slice(small[len(REPLICATED_VECS) + i], (chip * width,), (width,))
    gs["a_w_group"] = small[16:]
    total_loss = small[8, 0]

    out_g, out_d, out_m, out_v = {}, {}, {}, {}
    for k in BIG:
        shape = wl[k].shape
        out_g[k] = red[k].reshape(shape)
        d, nm, nv = _adamw(k + "_adamw", mat(wl[k]), red[k], mat(ml[k]), mat(vl[k]))
        out_d[k], out_m[k], out_v[k] = d.reshape(shape), nm.reshape(shape), nv.reshape(shape)
    small_names = REPLICATED_VECS + SHARDED_VECS + ("a_w_group",)
    d, nm, nv = _adamw("small_adamw", _pack_small(wl), _pack_small(gs), _pack_small(ml), _pack_small(vl))
    like = {k: wl[k] for k in small_names}
    for dst, p in ((out_d, d), (out_m, nm), (out_v, nv)):
        dst.update(_unpack_small(p, like))
    for k in small_names:
        out_g[k] = gs[k].reshape(wl[k].shape)

    return (total_loss, grad_x[None], *[out_g[k] for k in WEIGHTS], *[out_d[k] for k in WEIGHTS],
            *[out_m[k] for k in WEIGHTS], *[out_v[k] for k in WEIGHTS])
```

```python
import functools

import jax
import jax.numpy as jnp
from jax import lax
from jax.experimental import pallas as pl
from jax.experimental.pallas import tpu as pltpu

F32 = jnp.float32
BF16 = jnp.bfloat16

HEAD_DIM = 64
SB_WIDTH = 768
MEM_WIDTH = 256
POOL_WINDOWS = (2, 4, 8, 16)
POOL_GROUP = 192
POOL_HALO = 16
EPS = 1e-6
ATT_SCALE = HEAD_DIM ** -0.5
ADAM_LR, ADAM_B1, ADAM_B2, ADAM_EPS, ADAM_WD, ADAM_STEP = 0.001, 0.9, 0.999, 1e-08, 0.01, 10

LANES = 128
SB_TQ, SB_TK = 512, 512
VMEM_LIMIT = 56 * 1024 * 1024
MESH = pl.DeviceIdType.MESH
COPY_BYTES = 512 * 1024
ANY = pl.BlockSpec(memory_space=pl.ANY)


def _params(*sem):
    return pltpu.CompilerParams(dimension_semantics=sem, vmem_limit_bytes=VMEM_LIMIT)


def _tile(n, pref):
    if n <= pref:
        return n
    best = None
    for t in range(LANES, pref + 1, LANES):
        if n % t == 0:
            best = t
    assert best is not None, (n, pref)
    return best


def _row_tile(t, pref):
    if t <= pref:
        return t
    for tb in range(pref - pref % 16, 0, -16):
        if t % tb == 0:
            return tb
    raise ValueError((t, pref))


def _rowwise(name, fn, rows, vecs, row_outs, sum_outs=(), tb=512):
    norm_rows = []
    for r in rows:
        if isinstance(r, tuple):
            arr, (bc, cb) = r
        else:
            arr, (bc, cb) = r, (r.shape[1], 0)
        norm_rows.append((arr, bc, cb))
    t = norm_rows[0][0].shape[0]
    tb = _row_tile(t, tb)
    n_in, n_ro = len(norm_rows) + len(vecs), len(row_outs)

    def body(*refs):
        ins = [r[...] for r in refs[:n_in]]
        outs = fn(*ins)
        if not isinstance(outs, tuple):
            outs = (outs,)
        for o_ref, o in zip(refs[n_in:n_in + n_ro], outs[:n_ro]):
            o_ref[...] = o.astype(o_ref.dtype)
        for s_ref, s in zip(refs[n_in + n_ro:], outs[n_ro:]):
            @pl.when(pl.program_id(0) == 0)
            def _():
                s_ref[...] = jnp.zeros_like(s_ref)
            s_ref[...] += s

    in_specs = [pl.BlockSpec((tb, bc), functools.partial(lambda i, cb: (i, cb), cb=cb)) for _, bc, cb in norm_rows]
    in_specs += [pl.BlockSpec(v.shape, lambda i: (0, 0)) for v in vecs]
    out_specs = [pl.BlockSpec((tb, c), lambda i: (i, 0)) for c, _ in row_outs]
    out_specs += [pl.BlockSpec((1, c), lambda i: (0, 0)) for c in sum_outs]
    out_shape = [jax.ShapeDtypeStruct((t, c), d) for c, d in row_outs]
    out_shape += [jax.ShapeDtypeStruct((1, c), F32) for c in sum_outs]
    res = pl.pallas_call(
        body, name=name, grid=(t // tb,), in_specs=in_specs, out_specs=out_specs, out_shape=out_shape,
        compiler_params=_params("arbitrary"),
    )(*[a for a, _, _ in norm_rows], *vecs)
    return res


def _rms_fwd(name, x, g):
    def fn(xt, gt):
        rstd = lax.rsqrt(jnp.mean(xt * xt, axis=-1, keepdims=True) + EPS)
        return xt * rstd * gt
    return _rowwise(name, fn, [x], [g], [(x.shape[1], BF16)])[0]


def _rms_bwd(name, x, g, dh, dres=None, want_dx=True):
    has_res = dres is not None

    def fn(*a):
        if has_res:
            xt, dht, drt, gt = a
        else:
            xt, dht, gt = a
        rstd = lax.rsqrt(jnp.mean(xt * xt, axis=-1, keepdims=True) + EPS)
        xhat = xt * rstd
        dht = dht.astype(F32)
        dg = jnp.sum(dht * xhat, axis=0, keepdims=True)
        if not want_dx:
            return (dg,)
        dxhat = dht * gt
        dx = rstd * (dxhat - xhat * jnp.mean(dxhat * xhat, axis=-1, keepdims=True))
        if has_res:
            dx = dx + drt
        return dx, dx, dg

    d = x.shape[1]
    rows = [x, dh] + ([dres] if has_res else [])
    outs = [(d, F32), (d, BF16)] if want_dx else []
    return _rowwise(name, fn, rows, [g], outs, [d])


def _rms_fwd2(name, x, g1, g2):
    def fn(xt, g1t, g2t):
        xn = xt * lax.rsqrt(jnp.mean(xt * xt, axis=-1, keepdims=True) + EPS)
        return xn * g1t, xn * g2t
    return _rowwise(name, fn, [x], [g1, g2], [(x.shape[1], BF16)] * 2)


_DOT_DIMS = {"nn": ((1,), (0,)), "nt": ((1,), (1,)), "tn": ((0,), (0,))}


def _mm(name, a, b, mode, out_dtype, res=None, tm=512, tn=512):
    if mode == "nn":
        (m, k), (k2, n) = a.shape, b.shape
    elif mode == "nt":
        (m, k), (n, k2) = a.shape, b.shape
    else:
        (k, m), (k2, n) = a.shape, b.shape
    assert k == k2, (name, a.shape, b.shape)
    tm, tn = _tile(m, tm), _tile(n, tn)
    dims = (_DOT_DIMS[mode], ((), ()))
    has_res = res is not None

    def body(a_ref, b_ref, *rest):
        acc = lax.dot_general(a_ref[...], b_ref[...], dims, preferred_element_type=F32)
        if has_res:
            acc = acc + rest[0][...]
        rest[-1][...] = acc.astype(out_dtype)

    a_spec = pl.BlockSpec((k, tm), lambda i, j: (0, i)) if mode == "tn" else pl.BlockSpec((tm, k), lambda i, j: (i, 0))
    b_spec = pl.BlockSpec((tn, k), lambda i, j: (j, 0)) if mode == "nt" else pl.BlockSpec((k, tn), lambda i, j: (0, j))
    o_spec = pl.BlockSpec((tm, tn), lambda i, j: (i, j))
    in_specs, args = [a_spec, b_spec], [a, b]
    if has_res:
        in_specs.append(o_spec)
        args.append(res)
    return pl.pallas_call(
        body, name=name, grid=(m // tm, n // tn), in_specs=in_specs, out_specs=o_spec,
        out_shape=jax.ShapeDtypeStruct((m, n), out_dtype), compiler_params=_params("parallel", "arbitrary"),
    )(*args)


def _mm_nt_rms_bwd(name, a, b, x, g, dres, other=None, tm=256):
    m, k = a.shape
    d = b.shape[0]
    n_other = 0 if other is None else 2

    def body(a_ref, b_ref, x_ref, r_ref, g_ref, *rest):
        dx_ref, dxb_ref, dg_ref = rest[n_other:n_other + 3]
        dh = lax.dot_general(a_ref[...], b_ref[...], (_DOT_DIMS["nt"], ((), ())), preferred_element_type=F32)
        xt = x_ref[...]
        rstd = lax.rsqrt(jnp.mean(xt * xt, axis=-1, keepdims=True) + EPS)
        xhat = xt * rstd
        dxhat = dh * g_ref[...]
        if other is not None:
            dh2 = rest[1][...]
            dxhat = dxhat + dh2 * rest[0][...]
        dx = rstd * (dxhat - xhat * jnp.mean(dxhat * xhat, axis=-1, keepdims=True)) + r_ref[...]
        dx_ref[...] = dx
        dxb_ref[...] = dx.astype(BF16)

        @pl.when(pl.program_id(0) == 0)
        def _():
            for ref in rest[n_other + 2:]:
                ref[...] = jnp.zeros_like(ref)
        dg_ref[...] += jnp.sum(dh * xhat, axis=0, keepdims=True)
        if other is not None:
            rest[-1][...] += jnp.sum(dh2 * xhat, axis=0, keepdims=True)

    row = lambda c: pl.BlockSpec((tm, c), lambda i: (i, 0))
    whole = lambda shape: pl.BlockSpec(shape, lambda i: (0, 0))
    vec = jax.ShapeDtypeStruct((1, d), F32)
    extra_in = [] if other is None else [whole((1, d)), row(d)]
    extra_out = [] if other is None else [vec]
    return pl.pallas_call(
        body, name=name, grid=(m // tm,),
        in_specs=[row(k), whole((d, k)), row(d), row(d), whole((1, d))] + extra_in,
        out_specs=[row(d), row(d), whole((1, d))] + [whole((1, d))] * len(extra_out),
        out_shape=[jax.ShapeDtypeStruct((m, d), F32), jax.ShapeDtypeStruct((m, d), BF16), vec] + extra_out,
        compiler_params=_params("arbitrary"),
    )(a, b, x, dres, g, *(other or ()))


def _pool(name, u, reverse, tb=512):
    t = u.shape[0]
    tb = min(tb, t)
    nt = t // tb
    c = SB_WIDTH
    hpb = tb // POOL_HALO

    def body(cur_ref, halo_ref, o_ref):
        i = pl.program_id(0)
        cur = cur_ref[...].astype(F32)
        edge = (i == nt - 1) if reverse else (i == 0)
        halo = jnp.where(edge, 0.0, halo_ref[...].astype(F32))
        col = lax.broadcasted_iota(jnp.int32, (tb + POOL_HALO, c), 1)
        row = lax.broadcasted_iota(jnp.int32, (tb + POOL_HALO, c), 0)
        wcol = jnp.where(col < POOL_GROUP, 2, jnp.where(col < 2 * POOL_GROUP, 4, jnp.where(col < 3 * POOL_GROUP, 8, 16)))
        n = tb + POOL_HALO
        if reverse:
            ext = jnp.concatenate([cur, halo], axis=0)
            tpos = i * tb + row
            ext = ext / jnp.minimum(tpos + 1, wcol).astype(F32)
            shift = lambda a, k: pltpu.roll(a, n - k, 0)
        else:
            ext = jnp.concatenate([halo, cur], axis=0)
            shift = lambda a, k: pltpu.roll(a, k, 0)
        s2 = ext + shift(ext, 1)
        s4 = s2 + shift(s2, 2)
        s8 = s4 + shift(s4, 4)
        s16 = s8 + shift(s8, 8)
        win = jnp.where(wcol == 2, s2, jnp.where(wcol == 4, s4, jnp.where(wcol == 8, s8, s16)))
        if reverse:
            out = win[:tb] - cur
        else:
            tpos = i * tb + row[POOL_HALO:] - POOL_HALO
            out = win[POOL_HALO:] / jnp.minimum(tpos + 1, wcol[POOL_HALO:]).astype(F32) - cur
        o_ref[...] = out.astype(o_ref.dtype)

    if reverse:
        halo_map = lambda i: (jnp.minimum((i + 1) * hpb, t // POOL_HALO - 1), 0)
    else:
        halo_map = lambda i: (jnp.maximum(i * hpb - 1, 0), 0)
    return pl.pallas_call(
        body, name=name, grid=(nt,),
        in_specs=[pl.BlockSpec((tb, c), lambda i: (i, 0)), pl.BlockSpec((POOL_HALO, c), halo_map)],
        out_specs=pl.BlockSpec((tb, c), lambda i: (i, 0)),
        out_shape=jax.ShapeDtypeStruct((t, c), BF16), compiler_params=_params("arbitrary"),
    )(u, u)


def _head_masks(shape):
    lane = lax.broadcasted_iota(jnp.int32, shape, 1)
    return lane < HEAD_DIM, lane >= HEAD_DIM


def _pick(mask, a):
    return jnp.where(mask, a, jnp.zeros_like(a))


def _dot(a, b, mode):
    return lax.dot_general(a, b, (_DOT_DIMS[mode], ((), ())), preferred_element_type=F32)


def _dot_tri(a, tri, suffix):
    h = a.shape[1] // 2
    lo, hi = a[:, :h], a[:, h:]
    s_lo, s_hi = jnp.sum(lo, axis=1, keepdims=True), jnp.sum(hi, axis=1, keepdims=True)
    p_lo, p_hi = _dot(lo.astype(BF16), tri, "nn"), _dot(hi.astype(BF16), tri, "nn")
    if suffix:
        p_lo = p_lo + s_hi
    else:
        p_hi = p_hi + s_lo
    return jnp.concatenate([p_lo, p_hi], axis=1), s_lo + s_hi


def _log_gates(z):
    nz = -z
    l = jnp.log(1.0 + jnp.exp(jnp.minimum(z, nz)))
    ln = jnp.minimum(nz, 0.0) - l
    return ln, z + ln


def _sb_blocks(s, tq, tk):
    tq, tk = min(tq, s), min(tk, s)
    assert tq == tk and s % tk == 0 and tk % 64 == 0, (s, tq, tk)
    return tq, tk, tk // 2


def _strict_triangle(n, pred):
    return pred(lax.broadcasted_iota(jnp.int32, (n, n), 0), lax.broadcasted_iota(jnp.int32, (n, n), 1)).astype(BF16)


def _sb_fwd(proj, kv, late_shards, tq=SB_TQ, tk=SB_TK):
    s = proj.shape[0]
    tq, tk, th = _sb_blocks(s, tq, tk)
    npair = SB_WIDTH // LANES
    gather = _Gather({k: v.shape for k, v in late_shards.items()})
    ng = gather.n

    def body(q_ref, k_ref, v_ref, *rest):
        o_ref, tot_ref = rest[ng:ng + 2]
        comm = rest[:ng], rest[ng + 2:2 * ng + 2], rest[2 * ng + 2:2 * ng + 5], rest[2 * ng + 5:]

        @pl.when(pl.program_id(0) == 0)
        def _():
            gather.start(*comm)

        tri_gt = _strict_triangle(tk, lambda j, s_: j > s_)
        tri_gt_h = _strict_triangle(th, lambda j, s_: j > s_)
        seen = lax.broadcasted_iota(jnp.int32, (tq, th), 1) < lax.broadcasted_iota(jnp.int32, (tq, th), 0)
        m_a, m_b = _head_masks((tq, LANES))

        def block(qh, k2, v2, lane_mask, carry, acc, mask, tri):
            ln_full, lsz = _log_gates(_dot(qh, k2, "nt"))
            ln = ln_full if mask is None else jnp.where(mask, ln_full, 0.0)
            w = jnp.exp(lsz + _dot(ln.astype(BF16), tri, "nn"))
            if mask is not None:
                w = jnp.where(mask, w, 0.0)
            acc = acc + jnp.exp(carry) * _dot(w.astype(BF16), _pick(lane_mask[:v2.shape[0]], v2), "nn")
            return carry + jnp.sum(ln, axis=1, keepdims=True), acc

        def q_block(qi, _):
            q0 = pl.multiple_of(qi * tq, tq)
            q2 = q_ref[pl.ds(q0, tq), :] * ATT_SCALE
            qa, qb = _pick(m_a, q2), _pick(m_b, q2)

            def both(qa, qb, k0, size, ca, cb, acc, mask, tri):
                k2 = k_ref[pl.ds(k0, size), :]
                v2 = v_ref[pl.ds(k0, size), :]
                ca, acc = block(qa, k2, v2, m_a, ca, acc, mask, tri)
                cb, acc = block(qb, k2, v2, m_b, cb, acc, mask, tri)
                return ca, cb, acc

            zero_c = jnp.zeros((th, 1), F32)
            zero_o = jnp.zeros((th, LANES), F32)
            ca, cb, acc = both(qa[th:], qb[th:], pl.multiple_of(q0 + th, th), th, zero_c, zero_c, zero_o, seen[:th], tri_gt_h)
            ca, cb = jnp.concatenate([zero_c, ca], axis=0), jnp.concatenate([zero_c, cb], axis=0)
            carry = both(qa, qb, q0, th, ca, cb, jnp.concatenate([zero_o, acc], axis=0), seen, tri_gt_h)

            def k_block(k0, carry):
                return both(qa, qb, pl.multiple_of(k0, tk), tk, *carry, None, tri_gt)

            def two_blocks(step, carry):
                k0 = q0 - (2 * step + 1) * tk
                return k_block(k0 - tk, k_block(k0, carry))

            carry = lax.fori_loop(0, qi // 2, two_blocks, carry)
            ca, cb, acc = lax.fori_loop(0, qi % 2, lambda _, c_: k_block(0, c_), carry)
            o_ref[pl.ds(q0, tq), :] = acc.astype(o_ref.dtype)
            tot_ref[0, pl.ds(q0, tq), :] = jnp.broadcast_to(ca, (tq, LANES))
            tot_ref[1, pl.ds(q0, tq), :] = jnp.broadcast_to(cb, (tq, LANES))
            return 0

        lax.fori_loop(0, s // tq, q_block, 0)

        @pl.when(pl.program_id(0) == npair - 1)
        def _():
            gather.finish(*comm)

    outs = pl.pallas_call(
        body, name="sb_fwd", grid=(npair,),
        in_specs=[pl.BlockSpec((s, LANES), lambda p: (0, p)), pl.BlockSpec((s, LANES), lambda p: (0, p)),
                  pl.BlockSpec((s, LANES), lambda p: (0, npair + p))] + [ANY] * ng,
        out_specs=[pl.BlockSpec((s, LANES), lambda p: (0, p)), pl.BlockSpec((None, 2, s, LANES), lambda p: (p, 0, 0, 0))]
        + [ANY] * ng,
        out_shape=[jax.ShapeDtypeStruct((s, SB_WIDTH), BF16), jax.ShapeDtypeStruct((npair, 2, s, LANES), F32)]
        + gather.out_shape(),
        scratch_shapes=gather.scratch(), compiler_params=_params("arbitrary"),
    )(proj, kv, kv, *[late_shards[k] for k in gather.names])
    return outs[0], outs[1], dict(zip(gather.names, outs[2:]))


def _sb_bwd(proj, kv, dcat, tot, early_partial, tq=SB_TQ, tk=SB_TK):
    s = proj.shape[0]
    tq, tk, th = _sb_blocks(s, tq, tk)
    npair = SB_WIDTH // LANES
    exchange = _ChipExchange(early_partial)
    ne = exchange.n

    def body(q_ref, k_ref, v_ref, do_ref, tot_ref, *rest):
        dq_ref, dk_ref, dv_ref = rest[ne:ne + 3]
        dk_acc, dv_acc = rest[2 * ne + 3:2 * ne + 5]
        comm = rest[:ne], rest[ne + 3:2 * ne + 3], rest[2 * ne + 5:]

        @pl.when(pl.program_id(0) == 0)
        def _():
            exchange.start(*comm)

        tris = (_strict_triangle(tk // 2, lambda j, s_: j > s_), _strict_triangle(tk // 2, lambda j, s_: j < s_))
        tris_h = (_strict_triangle(th // 2, lambda j, s_: j > s_), _strict_triangle(th // 2, lambda j, s_: j < s_))
        seen = lax.broadcasted_iota(jnp.int32, (tq, th), 1) < lax.broadcasted_iota(jnp.int32, (tq, th), 0)
        m_a, m_b = _head_masks((tq, LANES))
        dk_acc[...] = jnp.zeros_like(dk_acc)
        dv_acc[...] = jnp.zeros_like(dv_acc)

        def block(qh, doh, k2, v2, lane_mask, tot_h, carry, mask, tri):
            c_ln, c_d, dq = carry
            ln_full, lsz = _log_gates(_dot(qh, k2, "nt"))
            ln = ln_full if mask is None else jnp.where(mask, ln_full, 0.0)
            inside, total = _dot_tri(ln, tri[0], True)
            c_ln = c_ln + total
            w = jnp.exp(lsz + ((tot_h - c_ln) + inside))
            if mask is not None:
                w = jnp.where(mask, w, 0.0)
            dlw = _dot(doh, v2, "nt") * w
            before, d_total = _dot_tri(dlw, tri[1], False)
            dz = dlw * jnp.exp(ln_full) - (before + c_d) * jnp.exp(lsz)
            if mask is not None:
                dz = jnp.where(mask, dz, 0.0)
            dz = dz.astype(BF16)
            dq = dq + _dot(dz, _pick(lane_mask[:k2.shape[0]], k2), "nn")
            dk = _dot(dz, qh, "tn")
            dv = _dot(w.astype(BF16), doh, "tn")
            carry = (c_ln, c_d + d_total, dq)
            return carry, dk, dv

        def q_block(qi, _):
            q0 = pl.multiple_of(qi * tq, tq)
            q2 = q_ref[pl.ds(q0, tq), :] * ATT_SCALE
            do2 = do_ref[pl.ds(q0, tq), :]
            qa, qb = _pick(m_a, q2), _pick(m_b, q2)
            doa, dob = _pick(m_a, do2), _pick(m_b, do2)
            tot_a = tot_ref[0, pl.ds(q0, tq), 0:1]
            tot_b = tot_ref[1, pl.ds(q0, tq), 0:1]
            zero_c = jnp.zeros((tq, 1), F32)
            zero_q = jnp.zeros((tq, LANES), F32)

            def both(rows, k0, size, ca, cb, mask, tri):
                k2 = k_ref[pl.ds(k0, size), :]
                v2 = v_ref[pl.ds(k0, size), :]
                ca, dka, dva = block(qa[rows], doa[rows], k2, v2, m_a, tot_a[rows], ca, mask, tri)
                cb, dkb, dvb = block(qb[rows], dob[rows], k2, v2, m_b, tot_b[rows], cb, mask, tri)
                dk_acc[pl.ds(k0, size), :] += dka + dkb
                dv_acc[pl.ds(k0, size), :] += dva + dvb
                return ca, cb

            def k_block(k0, carry):
                return both(slice(None), pl.multiple_of(k0, tk), tk, carry[0], carry[1], None, tris)

            def two_blocks(step, carry):
                return k_block((2 * step + 1) * tk, k_block(2 * step * tk, carry))

            init = ((zero_c, zero_c, zero_q), (zero_c, zero_c, zero_q))
            carry = lax.fori_loop(0, qi // 2, two_blocks, init)
            ca, cb = lax.fori_loop(0, qi % 2, lambda _, c_: k_block(q0 - tk, c_), carry)
            ca, cb = both(slice(None), q0, th, ca, cb, seen, tris_h)
            late = slice(th, tq)
            la, lb = both(late, pl.multiple_of(q0 + th, th), th, tuple(t[late] for t in ca), tuple(t[late] for t in cb),
                          seen[:th], tris_h)
            dq = jnp.concatenate([ca[2][:th] + cb[2][:th], la[2] + lb[2]], axis=0)
            dq_ref[pl.ds(q0, tq), :] = (dq * ATT_SCALE).astype(dq_ref.dtype)
            return 0

        lax.fori_loop(0, s // tq, q_block, 0)
        dk_ref[...] = dk_acc[...].astype(dk_ref.dtype)
        dv_ref[...] = dv_acc[...].astype(dv_ref.dtype)

        @pl.when(pl.program_id(0) == npair - 1)
        def _():
            exchange.finish(*comm)

    col = lambda off: pl.BlockSpec((s, LANES), functools.partial(lambda p, off: (0, off + p), off=off))
    outs = pl.pallas_call(
        body, name="sb_bwd", grid=(npair,),
        in_specs=[col(0), col(0), col(npair), col(0), pl.BlockSpec((None, 2, s, LANES), lambda p: (p, 0, 0, 0))]
        + [ANY] * ne,
        out_specs=[col(0), col(0), col(0)] + [ANY] * ne,
        out_shape=[jax.ShapeDtypeStruct((s, SB_WIDTH), BF16)] * 3 + exchange.out_shape(),
        scratch_shapes=[pltpu.VMEM((s, LANES), F32), pltpu.VMEM((s, LANES), F32)] + exchange.scratch(),
        compiler_params=_params("arbitrary"),
    )(proj, kv, kv, dcat, tot, *[early_partial[k] for k in exchange.names])
    dq, dk, dv = outs[:3]
    return dq, jnp.concatenate([dk, dv], axis=1), dict(zip(exchange.names, outs[3:]))


def _mem_fwd(name, proj, mkv, tq=512):
    s = proj.shape[0]
    tq = min(tq, s)
    qblk = SB_WIDTH // MEM_WIDTH

    def body(q_ref, kv_ref, o_ref):
        m_a, m_b = _head_masks((tq, LANES))
        mk_a, mk_b = _head_masks((kv_ref.shape[0], LANES))
        for p in range(MEM_WIDTH // LANES):
            q2 = q_ref[:, p * LANES:(p + 1) * LANES].astype(BF16)
            k2 = kv_ref[:, p * LANES:(p + 1) * LANES]
            v2 = kv_ref[:, MEM_WIDTH + p * LANES:MEM_WIDTH + (p + 1) * LANES]
            acc = jnp.zeros((tq, LANES), F32)
            for mq, mk in ((m_a, mk_a), (m_b, mk_b)):
                logits = _dot(_pick(mq, q2), k2, "nt") * ATT_SCALE
                e = jnp.exp(logits - jnp.max(logits, axis=-1, keepdims=True))
                prob = e / jnp.sum(e, axis=-1, keepdims=True)
                acc = acc + _dot(prob.astype(BF16), _pick(mk, v2), "nn")
            o_ref[:, p * LANES:(p + 1) * LANES] = acc.astype(o_ref.dtype)

    return pl.pallas_call(
        body, name=name, grid=(s // tq,),
        in_specs=[pl.BlockSpec((tq, MEM_WIDTH), lambda i: (i, qblk)), pl.BlockSpec(mkv.shape, lambda i: (0, 0))],
        out_specs=pl.BlockSpec((tq, MEM_WIDTH), lambda i: (i, 0)),
        out_shape=jax.ShapeDtypeStruct((s, MEM_WIDTH), BF16), compiler_params=_params("arbitrary"),
    )(proj, mkv)


def _mem_bwd(name, proj, mkv, dcat, tq=512):
    s = proj.shape[0]
    tq = min(tq, s)
    qblk = SB_WIDTH // MEM_WIDTH

    def body(q_ref, kv_ref, do_ref, dq_ref, dkv_ref):
        @pl.when(pl.program_id(0) == 0)
        def _():
            dkv_ref[...] = jnp.zeros_like(dkv_ref)

        m_a, m_b = _head_masks((tq, LANES))
        for p in range(MEM_WIDTH // LANES):
            ksl = slice(p * LANES, (p + 1) * LANES)
            vsl = slice(MEM_WIDTH + p * LANES, MEM_WIDTH + (p + 1) * LANES)
            q2, do2 = q_ref[:, ksl].astype(BF16), do_ref[:, ksl]
            k2, v2 = kv_ref[:, ksl], kv_ref[:, vsl]
            mk_a, mk_b = _head_masks(k2.shape)
            dq = jnp.zeros((tq, LANES), F32)
            dk = jnp.zeros(k2.shape, F32)
            dv = jnp.zeros(k2.shape, F32)
            for mq, mk in ((m_a, mk_a), (m_b, mk_b)):
                qh, doh = _pick(mq, q2), _pick(mq, do2)
                logits = _dot(qh, k2, "nt") * ATT_SCALE
                e = jnp.exp(logits - jnp.max(logits, axis=-1, keepdims=True))
                prob = e / jnp.sum(e, axis=-1, keepdims=True)
                dp = _dot(doh, v2, "nt")
                ds = prob * (dp - jnp.sum(dp * prob, axis=-1, keepdims=True)) * ATT_SCALE
                ds = ds.astype(BF16)
                dq = dq + _dot(ds, _pick(mk, k2), "nn")
                dk = dk + _dot(ds, qh, "tn")
                dv = dv + _dot(prob.astype(BF16), doh, "tn")
            dq_ref[:, ksl] = dq.astype(dq_ref.dtype)
            dkv_ref[:, ksl] += dk
            dkv_ref[:, vsl] += dv

    return pl.pallas_call(
        body, name=name, grid=(s // tq,),
        in_specs=[pl.BlockSpec((tq, MEM_WIDTH), lambda i: (i, qblk)), pl.BlockSpec(mkv.shape, lambda i: (0, 0)),
                  pl.BlockSpec((tq, MEM_WIDTH), lambda i: (i, qblk))],
        out_specs=[pl.BlockSpec((tq, MEM_WIDTH), lambda i: (i, 0)), pl.BlockSpec(mkv.shape, lambda i: (0, 0))],
        out_shape=[jax.ShapeDtypeStruct((s, MEM_WIDTH), BF16), jax.ShapeDtypeStruct(mkv.shape, F32)],
        compiler_params=_params("arbitrary"),
    )(proj, mkv, dcat)


def _gu_swiglu(name, h, w_gu, tn=256):
    t, d = h.shape
    f = w_gu.shape[1] // 2
    nb = f // tn

    def body(h_ref, wg_ref, wu_ref, g_ref, u_ref, a_ref):
        hh = h_ref[...]
        g = _dot(hh, wg_ref[...], "nn")
        u = _dot(hh, wu_ref[...], "nn")
        g_ref[...] = g.astype(BF16)
        u_ref[...] = u.astype(BF16)
        a_ref[...] = (g * jax.nn.sigmoid(g) * u).astype(BF16)

    out = pl.BlockSpec((t, tn), lambda j: (0, j))
    return pl.pallas_call(
        body, name=name, grid=(nb,),
        in_specs=[pl.BlockSpec((t, d), lambda j: (0, 0)), pl.BlockSpec((d, tn), lambda j: (0, j)),
                  pl.BlockSpec((d, tn), lambda j: (0, nb + j))],
        out_specs=[out, out, out], out_shape=[jax.ShapeDtypeStruct((t, f), BF16)] * 3,
        compiler_params=_params("arbitrary"),
    )(h, w_gu, w_gu)


def _down_dx_swiglu(name, dout_bf, w_down, gate, up, tm=256):
    t, d = dout_bf.shape
    f = w_down.shape[0]

    def body(do_ref, w_ref, g_ref, u_ref, o_ref):
        dact = _dot(do_ref[...], w_ref[...], "nt")
        g, u = g_ref[...].astype(F32), u_ref[...].astype(F32)
        sg = jax.nn.sigmoid(g)
        silu = g * sg
        o_ref[:, :f] = (dact * u * (sg + silu * (1.0 - sg))).astype(BF16)
        o_ref[:, f:] = (dact * silu).astype(BF16)

    row = lambda c: pl.BlockSpec((tm, c), lambda i: (i, 0))
    return pl.pallas_call(
        body, name=name, grid=(t // tm,),
        in_specs=[row(d), pl.BlockSpec((f, d), lambda i: (0, 0)), row(f), row(f)],
        out_specs=row(2 * f), out_shape=jax.ShapeDtypeStruct((t, 2 * f), BF16), compiler_params=_params("arbitrary"),
    )(dout_bf, w_down, gate, up)


def _ffn_fwd(tag, x, norm, w_gu, w_down):
    h = _rms_fwd(tag + "_ffn_norm", x, norm)
    gate, up, act = _gu_swiglu(tag + "_gu", h, w_gu)
    out = _mm(tag + "_down", act, w_down, "nn", F32, res=x, tn=1024)
    return out, (h, gate, up, act)


def _ffn_bwd(tag, x, norm, w_gu, w_down, saved, dout, dout_bf):
    h, gate, up, act = saved
    g_down = _mm(tag + "_down_dw", act, dout_bf, "tn", BF16, tm=256, tn=1024)
    dgu = _down_dx_swiglu(tag + "_down_dx", dout_bf, w_down, gate, up)
    g_gu = _mm(tag + "_gu_dw", h, dgu, "tn", BF16, tm=1024)
    dx, dx_bf, g_norm = _mm_nt_rms_bwd(tag + "_gu_dx", dgu, w_gu, x, norm, dout)
    return dx, dx_bf, g_gu, g_down, g_norm


def _mem_kv(tag, mem_n, w_mem_kv):
    return _mm(tag + "_memkv", mem_n, w_mem_kv, "nn", BF16)


def _mem_kv_bwd(tag, mem, mem_norm, mem_n, w_mem_kv, dmkv):
    dmkv = dmkv.astype(BF16)
    g_w = _mm(tag + "_memkv_dw", mem_n, dmkv, "tn", BF16)
    dmem_n = _mm(tag + "_memkv_dx", dmkv, w_mem_kv, "nt", F32)
    (g_norm,) = _rms_bwd(tag + "_memnorm_bwd", mem, mem_norm, dmem_n, want_dx=False)
    return g_w, g_norm


def _block_diag(w_group):
    z = jnp.zeros((POOL_GROUP, POOL_GROUP), w_group.dtype)
    return jnp.concatenate(
        [jnp.concatenate([w_group[g] if h == g else z for h in range(4)], axis=1) for g in range(4)], axis=0)


def _local_step(x, mem, target, w, late_shards, second):
    g = {}
    w = dict(w)
    mem_n = _rms_fwd("mem_norm", mem, w["mem_norm"])
    h_a = _rms_fwd("a_mix_norm", x, w["a_norm_mix"])
    proj_a = _mm("a_in", h_a, w["a_w_in"], "nn", F32, tn=1024)
    pooled = _pool("a_pool", proj_a, reverse=False)
    w_bd = _block_diag(w["a_w_group"])
    g_pre = _mm("a_group", pooled, w_bd, "nn", BF16, tn=768)
    mkv_a = _mem_kv("a", mem_n, w["a_w_mem_kv"])
    mem_a = _mem_fwd("a_mem_attn", proj_a, mkv_a)
    cat_a = _rowwise("a_cat", lambda gp, mo, sc: jnp.concatenate([gp.astype(F32) * sc, mo.astype(F32)], axis=1),
                     [g_pre, mem_a], [w["a_scale"]], [(1024, BF16)])[0]
    x1 = _mm("a_out", cat_a, w["a_w_out"], "nn", F32, res=x, tn=1024)
    w.update(_gather_finish(second[0], second[1], x1))
    x2, ffn_a = _ffn_fwd("a", x1, w["a_norm_ffn"], w["a_w_gu"], w["a_w_down"])
    h_k, h_b = _rms_fwd2("x2_norms", x2, w["kv_norm"], w["b_norm_mix"])
    kv = _mm("kv_proj", h_k, w["w_kv"], "nn", BF16, tn=1536)
    proj_b = _mm("b_q", h_b, w["b_w_q"], "nn", BF16, tn=1024)
    sb_out, tot, late = _sb_fwd(proj_b, kv, late_shards)
    w.update(late)
    mkv_b = _mem_kv("b", mem_n, w["b_w_mem_kv"])
    mem_b = _mem_fwd("b_mem_attn", proj_b, mkv_b)
    cat_b = jnp.concatenate([sb_out, mem_b], axis=1)
    x3 = _mm("b_out", cat_b, w["b_w_out"], "nn", F32, res=x2, tn=1024)
    x4, ffn_b = _ffn_fwd("b", x3, w["b_norm_ffn"], w["b_w_gu"], w["b_w_down"])

    d = x.shape[1]

    def head(xt, tt, gt):
        rstd = lax.rsqrt(jnp.mean(xt * xt, axis=-1, keepdims=True) + EPS)
        xhat = xt * rstd
        err = xhat * gt - tt
        loss = 0.5 * jnp.sum(jnp.sum(err * err, axis=1, keepdims=True), axis=0, keepdims=True) / d
        dy = err / d
        dxhat = dy * gt
        dx = rstd * (dxhat - xhat * jnp.mean(dxhat * xhat, axis=-1, keepdims=True))
        return dx, dx, jnp.sum(dy * xhat, axis=0, keepdims=True), jnp.broadcast_to(loss, (1, LANES))

    dx4, dx4_bf, g["final_norm"], loss = _rowwise(
        "loss_head", head, [x4, target], [w["final_norm"]], [(d, F32), (d, BF16)], [d, LANES])

    dx3, dx3_bf, g["b_w_gu"], g["b_w_down"], g["b_norm_ffn"] = _ffn_bwd(
        "b", x3, w["b_norm_ffn"], w["b_w_gu"], w["b_w_down"], ffn_b, dx4, dx4_bf)
    dcat_b = _mm("b_out_dx", dx3_bf, w["b_w_out"], "nt", BF16, tn=1024)
    g["b_w_out"] = _mm("b_out_dw", cat_b, dx3_bf, "tn", BF16, tm=1024, tn=1024)
    early_partial = _pair_sums("early", {k: g.pop(k) for k in EARLY})
    dq_sb, dkv, early_got = _sb_bwd(proj_b, kv, dcat_b, tot, early_partial)
    dq_mem_b, dmkv_b = _mem_bwd("b_mem_attn_bwd", proj_b, mkv_b, dcat_b)
    dproj_b = jnp.concatenate([dq_sb, dq_mem_b], axis=1)
    g["b_w_q"] = _mm("b_q_dw", h_b, dproj_b, "tn", BF16, tm=1024, tn=1024)
    dh_b = _mm("b_q_dx", dproj_b, w["b_w_q"], "nt", F32, tn=1024)
    g["b_w_mem_kv"], g_memnorm_b = _mem_kv_bwd("b", mem, w["mem_norm"], mem_n, w["b_w_mem_kv"], dmkv_b)
    g["w_kv"] = _mm("kv_proj_dw", h_k, dkv, "tn", BF16, tm=1024)
    dx2, dx2_bf, g["kv_norm"], g["b_norm_mix"] = _mm_nt_rms_bwd(
        "kv_proj_dx", dkv, w["w_kv"], x2, w["kv_norm"], dx3, other=(w["b_norm_mix"], dh_b))
    post_partial = _pair_sums("post_sb", {k: g.pop(k) for k in POST_SB})
    post_flight = _chip_exchange_start("post_sb", post_partial)
    norm_ffn_a = w["a_norm_ffn"] + post_flight[-1][0:1, 0:1]

    dx1, dx1_bf, g["a_w_gu"], g["a_w_down"], g["a_norm_ffn"] = _ffn_bwd(
        "a", x1, norm_ffn_a, w["a_w_gu"], w["a_w_down"], ffn_a, dx2, dx2_bf)
    mid_partial = _pair_sums("mid", {k: g.pop(k) for k in MID})
    mid_flight = _chip_exchange_start("mid", mid_partial)
    scale_a = w["a_scale"] + mid_flight[-1][0:1, 0:1]
    dcat_a = _mm("a_out_dx", dx1_bf, w["a_w_out"], "nt", BF16, tn=1024)
    g["a_w_out"] = _mm("a_out_dw", cat_a, dx1_bf, "tn", BF16, tm=1024, tn=1024)

    def scale_bwd(dc, gp, sc):
        dc, gp = dc.astype(F32), gp.astype(F32)
        return dc * sc, jnp.sum(dc * gp, axis=0, keepdims=True)

    dg_pre, g["a_scale"] = _rowwise("a_scale_bwd", scale_bwd, [(dcat_a, (SB_WIDTH, 0)), g_pre], [scale_a],
                                    [(SB_WIDTH, BF16)], [SB_WIDTH])
    g_bd = _mm("a_group_dw", pooled, dg_pre, "tn", F32, tm=768, tn=768)
    g["a_w_group"] = jnp.stack([g_bd[i * POOL_GROUP:(i + 1) * POOL_GROUP, i * POOL_GROUP:(i + 1) * POOL_GROUP]
                                for i in range(4)])
    dpooled = _mm("a_group_dx", dg_pre, w_bd, "nt", F32, tn=768)
    du_pool = _pool("a_pool_bwd", dpooled, reverse=True)
    dq_mem_a, dmkv_a = _mem_bwd("a_mem_attn_bwd", proj_a, mkv_a, dcat_a)
    dproj_a = jnp.concatenate([du_pool, dq_mem_a], axis=1)
    g["a_w_in"] = _mm("a_in_dw", h_a, dproj_a, "tn", BF16, tm=1024, tn=1024)
    grad_x, _, g["a_norm_mix"] = _mm_nt_rms_bwd("a_in_dx", dproj_a, w["a_w_in"], x, w["a_norm_mix"], dx1)
    g["a_w_mem_kv"], g_memnorm_a = _mem_kv_bwd("a", mem, w["mem_norm"], mem_n, w["a_w_mem_kv"], dmkv_a)
    g["mem_norm"] = g_memnorm_a + g_memnorm_b
    post_partial, post_got = _chip_exchange_wait("post_sb", post_partial, post_flight, g["a_norm_mix"])
    mid_partial, mid_got = _chip_exchange_wait("mid", mid_partial, mid_flight, g["a_norm_mix"])
    return (loss, grad_x, g, {**early_partial, **post_partial, **mid_partial}, {**early_got, **post_got, **mid_got})


ROW_SHARDED = ("a_w_in", "a_w_mem_kv", "a_w_out", "a_w_down", "b_w_q", "b_w_mem_kv", "b_w_out", "b_w_down")
COL_SHARDED = ("a_w_gu", "w_kv", "b_w_gu")
BIG = ("a_w_in", "a_w_mem_kv", "a_w_out", "a_w_gu", "a_w_down", "w_kv", "b_w_q", "b_w_mem_kv", "b_w_out", "b_w_gu",
       "b_w_down")
LATE = ("b_w_mem_kv", "b_w_out", "b_w_gu", "b_w_down")
EARLY = ("b_w_gu", "b_w_down", "b_w_out")
POST_SB = ("w_kv", "b_w_q", "b_w_mem_kv")
MID = ("a_w_gu", "a_w_down")
SECOND = ("a_w_gu", "a_w_down", "w_kv", "b_w_q")
N_CHIPS = 4
N_DEV = 8


def _position():
    x, y, c = lax.axis_index("x"), lax.axis_index("y"), lax.axis_index("c")
    other_chips = [(1 - x, y), (x, 1 - y), (1 - x, 1 - y)]
    return x, y, c, other_chips


def _remote(src, dst, send_sem, recv_sem, device):
    return pltpu.make_async_remote_copy(src_ref=src, dst_ref=dst, send_sem=send_sem, recv_sem=recv_sem,
                                        device_id=device, device_id_type=MESH)


def _comm_call(name, body, args, out_shape, n_remote, aliases=None):
    return pl.pallas_call(
        body, name=name, in_specs=[ANY] * len(args), out_specs=[ANY] * len(out_shape), out_shape=out_shape,
        scratch_shapes=[pltpu.SemaphoreType.DMA((n_remote,)), pltpu.SemaphoreType.DMA((n_remote,))],
        input_output_aliases=aliases or {},
    )(*args)


def _row_chunks(nrows, row_bytes, mult=16):
    assert nrows % mult == 0, (nrows, mult)
    per = max(mult, (COPY_BYTES // row_bytes) // mult * mult)
    return [(r0, min(per, nrows - r0)) for r0 in range(0, nrows, per)]


def _rows(ref, start, size, lead=()):
    if isinstance(start, int):
        return ref.at[(*lead, pl.ds(start, size))]
    return ref.at[(*lead, pl.ds(pl.multiple_of(start, 16), size))]


class _Copies:
    def __init__(self, send, recv):
        self.send, self.recv = send, recv
        self.n_remote = 0
        self.remotes = []

    def slot(self):
        self.n_remote += 1
        return self.n_remote - 1

    def remote(self, k, src, dst, device):
        cp = _remote(src, dst, self.send.at[k], self.recv.at[k], device)
        cp.start()
        self.remotes.append(cp)
        return cp

    def finish(self):
        for cp in self.remotes:
            cp.wait_send()


def _shard_cols(ref, row_sharded, cdim, chip):
    if row_sharded:
        return ref
    return ref.at[:, pl.ds(pl.multiple_of(chip * cdim, LANES), cdim)]


class _Gather:
    def __init__(self, shapes, mult=16):
        self.names = list(shapes)
        self.shapes = [tuple(shapes[k]) for k in self.names]
        self.row_sharded = [k in ROW_SHARDED for k in self.names]
        self.chunks = [(i, r0, size) for i, (r, cdim) in enumerate(self.shapes)
                       for r0, size in _row_chunks(r // 2, cdim * 2, mult)]
        self.n = len(self.names)

    def out_shape(self):
        return [jax.ShapeDtypeStruct((N_CHIPS * r, cdim) if rs else (r, N_CHIPS * cdim), BF16)
                for (r, cdim), rs in zip(self.shapes, self.row_sharded)]

    def scratch(self):
        n_remote, n_local = 6 * len(self.chunks), self.n + 2 * len(self.chunks)
        return [pltpu.SemaphoreType.DMA((n_remote,)), pltpu.SemaphoreType.DMA((n_remote,)),
                pltpu.SemaphoreType.DMA((n_local,))] + [pltpu.VMEM(s, BF16) for s in self.shapes]

    def _window(self, dst, i, chip, half, r0, size):
        r, cdim = self.shapes[i]
        base = (chip * r if self.row_sharded[i] else 0) + half * (r // 2) + r0
        return _rows(_shard_cols(dst[i], self.row_sharded[i], cdim, chip), base, size)

    def _mine(self, refs, i, half, r0, size):
        return _rows(refs[i], half * (self.shapes[i][0] // 2) + r0, size)

    def _sent(self, src, dst, sems, q, k, px, py, c, me):
        i, r0, size = self.chunks[q]
        return _remote(self._mine(src, i, c, r0, size), self._window(dst, i, me, c, r0, size),
                       sems[0].at[6 * q + k], sems[1].at[6 * q + k], (px, py, c))

    def start(self, src, dst, sems, vm):
        x, y, c, chips = _position()
        for q in range(len(self.chunks)):
            for k, (px, py) in enumerate(chips):
                self._sent(src, dst, sems, q, k, px, py, c, 2 * x + y).start()
        for i in range(self.n):
            pltpu.make_async_copy(src[i], vm[i], sems[2].at[i]).start()

    def start_ici(self, src, dst, sems):
        x, y, c, chips = _position()
        for q in range(len(self.chunks)):
            for k, (px, py) in enumerate(chips):
                self._sent(src, dst, sems, q, k, px, py, c, 2 * x + y).start()

    def wait_ici(self, src, dst, sems):
        x, y, c, chips = _position()
        for q, (i, r0, size) in enumerate(self.chunks):
            for k, (px, py) in enumerate(chips):
                landed = self._window(dst, i, 2 * px + py, c, r0, size)
                _remote(landed, landed, sems[0].at[6 * q + k], sems[1].at[6 * q + k], (px, py, c)).wait_recv()
        for q in range(len(self.chunks)):
            for k, (px, py) in enumerate(chips):
                self._sent(src, dst, sems, q, k, px, py, c, 2 * x + y).wait_send()

    def finish(self, src, dst, sems, vm, landed=None):
        x, y, c, chips = _position()
        me = 2 * x + y
        send, recv, loc = sems
        arrived = dst if landed is None else landed
        if landed is not None:
            for i in range(self.n):
                pltpu.make_async_copy(src[i], vm[i], loc.at[i]).start()
        for i in range(self.n):
            pltpu.make_async_copy(src[i], vm[i], loc.at[i]).wait()
        placed, forwarded = [], []
        for q, (i, r0, size) in enumerate(self.chunks):
            for half in range(2):
                cp = pltpu.make_async_copy(self._mine(vm, i, half, r0, size), self._window(dst, i, me, half, r0, size),
                                           loc.at[self.n + 2 * q + half])
                cp.start()
                placed.append(cp)
        for q, (i, r0, size) in enumerate(self.chunks):
            for k, (px, py) in enumerate(chips):
                if landed is None:
                    there = self._window(dst, i, 2 * px + py, c, r0, size)
                    _remote(there, there, send.at[6 * q + k], recv.at[6 * q + k], (px, py, c)).wait_recv()
                cp = _remote(self._window(arrived, i, 2 * px + py, c, r0, size), self._window(dst, i, 2 * px + py, c, r0, size),
                             send.at[6 * q + 3 + k], recv.at[6 * q + 3 + k], (x, y, 1 - c))
                cp.start()
                forwarded.append(cp)
        for q, (i, r0, size) in enumerate(self.chunks):
            for k, (px, py) in enumerate(chips):
                there = self._window(dst, i, 2 * px + py, 1 - c, r0, size)
                _remote(there, there, send.at[6 * q + 3 + k], recv.at[6 * q + 3 + k], (x, y, 1 - c)).wait_recv()
        if landed is None:
            for q in range(len(self.chunks)):
                for k, (px, py) in enumerate(chips):
                    self._sent(src, dst, sems, q, k, px, py, c, me).wait_send()
        for cp in forwarded:
            cp.wait_send()
        for cp in placed:
            cp.wait()


class _GatherRelay(_Gather):
    def __init__(self, shapes):
        super().__init__(shapes, mult=32)

    def scratch(self):
        n_remote, n_local = 8 * len(self.chunks), self.n + 2 * len(self.chunks)
        return [pltpu.SemaphoreType.DMA((n_remote,)), pltpu.SemaphoreType.DMA((n_remote,)),
                pltpu.SemaphoreType.DMA((n_local,))] + [pltpu.VMEM(s, BF16) for s in self.shapes]

    def run(self, src, dst, sems, vm):
        x, y, c, chips = _position()
        me, diag, sibling = 2 * x + y, 2 * (1 - x) + (1 - y), (x, y, 1 - c)
        nbrs = chips[:2]
        send, recv, loc = sems

        def copy(slot, ref, device):
            return _remote(ref, ref, send.at[slot], recv.at[slot], device)

        first = []
        for q, (i, r0, size) in enumerate(self.chunks):
            for k, (px, py) in enumerate(nbrs):
                cp = _remote(self._mine(src, i, c, r0, size), self._window(dst, i, me, c, r0, size),
                             send.at[8 * q + k], recv.at[8 * q + k], (px, py, c))
                cp.start()
                first.append(cp)
        for i in range(self.n):
            pltpu.make_async_copy(src[i], vm[i], loc.at[i]).start()
        for i in range(self.n):
            pltpu.make_async_copy(src[i], vm[i], loc.at[i]).wait()
        placed, passed = [], []
        for q, (i, r0, size) in enumerate(self.chunks):
            for half in range(2):
                cp = pltpu.make_async_copy(self._mine(vm, i, half, r0, size), self._window(dst, i, me, half, r0, size),
                                           loc.at[self.n + 2 * q + half])
                cp.start()
                placed.append(cp)
        for q, (i, r0, size) in enumerate(self.chunks):
            for k, (px, py) in enumerate(nbrs):
                landed = self._window(dst, i, 2 * px + py, c, r0, size)
                copy(8 * q + k, landed, (px, py, c)).wait_recv()
                piece = self._window(dst, i, 2 * px + py, c, r0 + k * (size // 2), size // 2)
                ox, oy = nbrs[1 - k]
                for cp in (copy(8 * q + 2 + k, piece, (ox, oy, c)), copy(8 * q + 4 + k, landed, sibling)):
                    cp.start()
                    passed.append(cp)
        for q, (i, r0, size) in enumerate(self.chunks):
            for k in range(2):
                piece = self._window(dst, i, diag, c, r0 + k * (size // 2), size // 2)
                ox, oy = nbrs[1 - k]
                copy(8 * q + 2 + k, piece, (ox, oy, c)).wait_recv()
                cp = copy(8 * q + 6 + k, piece, sibling)
                cp.start()
                passed.append(cp)
        for q, (i, r0, size) in enumerate(self.chunks):
            for k, (px, py) in enumerate(nbrs):
                copy(8 * q + 4 + k, self._window(dst, i, 2 * px + py, 1 - c, r0, size), sibling).wait_recv()
                copy(8 * q + 6 + k, self._window(dst, i, diag, 1 - c, r0 + k * (size // 2), size // 2), sibling).wait_recv()
        for cp in first + passed:
            cp.wait_send()
        for cp in placed:
            cp.wait()


def _gather_weights(shards):
    plan = _GatherRelay({k: v.shape for k, v in shards.items()})
    n = plan.n

    def body(*refs):
        plan.run(refs[:n], refs[n:2 * n], refs[2 * n:2 * n + 3], refs[2 * n + 3:])

    outs = pl.pallas_call(
        body, name="gather_weights", in_specs=[ANY] * n, out_specs=[ANY] * n, out_shape=plan.out_shape(),
        scratch_shapes=plan.scratch(), compiler_params=pltpu.CompilerParams(vmem_limit_bytes=VMEM_LIMIT),
    )(*[shards[k] for k in plan.names])
    return dict(zip(plan.names, outs))


def _gather_start(shards):
    plan = _Gather({k: v.shape for k, v in shards.items()})
    n = plan.n
    n_remote = 6 * len(plan.chunks)
    srcs = [pltpu.with_memory_space_constraint(shards[k], pltpu.HBM) for k in plan.names]
    lands = [pltpu.with_memory_space_constraint(lax.empty(s.shape, s.dtype), pltpu.HBM) for s in plan.out_shape()]

    def body(*refs):
        plan.start_ici(refs[:n], refs[n:2 * n], (refs[2 * n], refs[2 * n + 1]))
        refs[-1][...] = jnp.zeros_like(refs[-1])

    return pl.pallas_call(
        body, name="gather_start",
        out_shape=(pltpu.SemaphoreType.DMA((n_remote,)), pltpu.SemaphoreType.DMA((n_remote,)),
                   *[pltpu.HBM(a.shape, a.dtype) for a in srcs + lands], jax.ShapeDtypeStruct((8, LANES), F32)),
        in_specs=[HBM_SPEC] * (2 * n),
        out_specs=(SEM_SPEC, SEM_SPEC, *[HBM_SPEC] * (2 * n), pl.BlockSpec(memory_space=pltpu.VMEM)),
        input_output_aliases={i: 2 + i for i in range(2 * n)},
        compiler_params=pltpu.CompilerParams(has_side_effects=SIDE_EFFECT),
    )(*srcs, *lands)


def _gather_finish(shapes, flight, after):
    plan = _Gather(shapes)
    n = plan.n
    send, recv, thru = flight[0], flight[1], flight[2:2 + 2 * n]

    def wait_body(*refs):
        plan.wait_ici(refs[:n], refs[n:2 * n], (refs[2 * n], refs[2 * n + 1]))

    thru = pl.pallas_call(
        wait_body, name="gather_wait", out_shape=tuple(pltpu.HBM(t.shape, t.dtype) for t in thru),
        in_specs=[HBM_SPEC] * (2 * n) + [SEM_SPEC, SEM_SPEC, ANY], out_specs=[HBM_SPEC] * (2 * n),
        input_output_aliases={i: i for i in range(2 * n)},
        compiler_params=pltpu.CompilerParams(has_side_effects=SIDE_EFFECT),
    )(*thru, send, recv, after)

    def body(*refs):
        src, landed, dst = refs[:n], refs[n:2 * n], refs[2 * n:3 * n]
        plan.finish(src, dst, refs[3 * n:3 * n + 3], refs[3 * n + 3:], landed=landed)

    outs = pl.pallas_call(
        body, name="gather_finish", in_specs=[ANY] * (2 * n), out_specs=[ANY] * n, out_shape=plan.out_shape(),
        scratch_shapes=plan.scratch(), input_output_aliases={n + i: i for i in range(n)},
        compiler_params=pltpu.CompilerParams(vmem_limit_bytes=VMEM_LIMIT),
    )(*thru)
    return dict(zip(plan.names, outs))


def _scalar_grid_call(name, body, scalars, grid, in_specs, out_specs, out_shape, args):
    return pl.pallas_call(
        body, name=name, out_shape=out_shape,
        grid_spec=pltpu.PrefetchScalarGridSpec(num_scalar_prefetch=1, grid=grid, in_specs=in_specs, out_specs=out_specs),
        compiler_params=_params(*["arbitrary"] * len(grid)),
    )(scalars, *args)


def _pair_sum(name, g4, sib, where, tb=256):
    j, _, r, w = g4.shape
    tb = _row_tile(r, tb)

    def body(s_ref, g_ref, b_ref, o_ref):
        o_ref[...] = (g_ref[...].astype(F32) + b_ref[...].astype(F32)).astype(o_ref.dtype)

    blk = pl.BlockSpec((None, tb, w), lambda a, i, s: (a, i, 0))
    return _scalar_grid_call(
        name, body, where, (j, r // tb),
        [pl.BlockSpec((None, None, tb, w), lambda a, i, s: (a, s[0], i, 0)), blk], blk,
        jax.ShapeDtypeStruct((j, r, w), BF16), (g4, sib))


def _chip_sum(name, partial, got, where, row_sharded, tb=256):
    _, r, cdim = got.shape
    tb = _row_tile(r, tb)

    def body(s_ref, p_ref, g_ref, o_ref):
        acc = p_ref[...].astype(F32)
        for k in range(N_CHIPS - 1):
            acc = acc + g_ref[k].astype(F32)
        o_ref[...] = acc

    if row_sharded:
        own = pl.BlockSpec((None, tb, cdim), lambda i, s: (s[1], i, 0))
    else:
        own = pl.BlockSpec((None, tb, cdim), lambda i, s: (0, i, s[1]))
    return _scalar_grid_call(
        name, body, where, (r // tb,),
        [own, pl.BlockSpec((N_CHIPS - 1, tb, cdim), lambda i, s: (0, i, 0))],
        pl.BlockSpec((None, tb, cdim), lambda i, s: (s[0], i, 0)),
        jax.ShapeDtypeStruct((2, r, cdim), F32), (partial, got))


def _where():
    return jnp.stack([lax.axis_index("c"), 2 * lax.axis_index("x") + lax.axis_index("y")]).astype(jnp.int32)


def _grad_halves(name, g):
    rows, cols = g.shape
    if name in ROW_SHARDED:
        r = rows // N_CHIPS
        return g.reshape(N_CHIPS, 2, r // 2, cols), (r // 2, cols)
    return g.reshape(1, 2, rows // 2, cols), (rows // 2, cols // N_CHIPS)


def _pair_sums(tag, grads):
    names = list(grads)
    n = len(names)
    g4 = [_grad_halves(k, grads[k])[0] for k in names]
    plan = [[(j, r0, size) for j in range(g.shape[0]) for r0, size in _row_chunks(g.shape[2], g.shape[3] * 2)]
            for g in g4]
    out_shape = [jax.ShapeDtypeStruct((g.shape[0],) + g.shape[2:], BF16) for g in g4]

    def body(*refs):
        src, sib = refs[:n], refs[n:2 * n]
        cps = _Copies(*refs[2 * n:])
        x, y, c, _ = _position()
        waits = []
        for i in range(n):
            for j, r0, size in plan[i]:
                waits.append(cps.remote(cps.slot(), _rows(src[i], r0, size, lead=(j, 1 - c)),
                                        _rows(sib[i], r0, size, lead=(j,)), (x, y, 1 - c)))
        for cp in waits:
            cp.wait_recv()
        cps.finish()

    sibs = _comm_call("grads_pair_exchange_" + tag, body, g4, out_shape, sum(len(p) for p in plan))
    where = _where()
    return {k: _pair_sum(k + "_pair_sum", g, s, where) for k, g, s in zip(names, g4, sibs)}


class _ChipExchange:
    def __init__(self, partial, mult=16):
        self.names = list(partial)
        self.n = len(self.names)
        self.row_sharded = [k in ROW_SHARDED for k in self.names]
        self.half = []
        for k, rs in zip(self.names, self.row_sharded):
            _, r, w = partial[k].shape
            self.half.append((r, w) if rs else (r, w // N_CHIPS))
        self.chunks = [(i, r0, size) for i, (r, cdim) in enumerate(self.half)
                       for r0, size in _row_chunks(r, cdim * 2, mult)]

    def shard(self, src, i, chip, r0, size):
        if self.row_sharded[i]:
            return _rows(src[i], r0, size, lead=(chip,))
        return _rows(_shard_cols(src[i].at[0], False, self.half[i][1], chip), r0, size)

    def out_shape(self):
        return [jax.ShapeDtypeStruct((N_CHIPS - 1,) + s, BF16) for s in self.half]

    def scratch(self):
        n_remote = 3 * len(self.chunks)
        return [pltpu.SemaphoreType.DMA((n_remote,)), pltpu.SemaphoreType.DMA((n_remote,))]

    def _copies(self, src, dst, sems):
        x, y, c, chips = _position()
        for q, (i, r0, size) in enumerate(self.chunks):
            for k, (px, py) in enumerate(chips):
                yield _remote(self.shard(src, i, 2 * px + py, r0, size), _rows(dst[i], r0, size, lead=(k,)),
                              sems[0].at[3 * q + k], sems[1].at[3 * q + k], (px, py, c))

    def start(self, src, dst, sems):
        for cp in self._copies(src, dst, sems):
            cp.start()

    def finish(self, src, dst, sems):
        for cp in self._copies(src, dst, sems):
            cp.wait_recv()
        for cp in self._copies(src, dst, sems):
            cp.wait_send()


def _chip_exchange(partial):
    plan = _ChipExchange(partial)
    n = plan.n

    def body(*refs):
        parts = refs[:n], refs[n:2 * n], refs[2 * n:]
        plan.start(*parts)
        plan.finish(*parts)

    got = pl.pallas_call(
        body, name="grads_chip_exchange", in_specs=[ANY] * n, out_specs=[ANY] * n, out_shape=plan.out_shape(),
        scratch_shapes=plan.scratch(),
    )(*[partial[k] for k in plan.names])
    return dict(zip(plan.names, got))


HBM_SPEC = pl.BlockSpec(memory_space=pltpu.HBM)
SEM_SPEC = pl.BlockSpec(memory_space=pltpu.SEMAPHORE)
SIDE_EFFECT = pltpu.SideEffectType.DATAFLOW_SIDE_EFFECTING


def _chip_exchange_start(tag, partial):
    plan = _ChipExchange(partial)
    n = plan.n
    n_remote = 3 * len(plan.chunks)
    srcs = [pltpu.with_memory_space_constraint(partial[k], pltpu.HBM) for k in plan.names]
    lands = [pltpu.with_memory_space_constraint(lax.empty(s.shape, s.dtype), pltpu.HBM) for s in plan.out_shape()]

    def body(*refs):
        src, land = refs[:n], refs[n:2 * n]
        plan.start(src, land, (refs[2 * n], refs[2 * n + 1]))
        refs[-1][...] = jnp.zeros_like(refs[-1])

    return pl.pallas_call(
        body, name="grads_chip_exchange_start_" + tag,
        out_shape=(pltpu.SemaphoreType.DMA((n_remote,)), pltpu.SemaphoreType.DMA((n_remote,)),
                   *[pltpu.HBM(a.shape, a.dtype) for a in srcs + lands], jax.ShapeDtypeStruct((8, LANES), F32)),
        in_specs=[HBM_SPEC] * (2 * n),
        out_specs=(SEM_SPEC, SEM_SPEC, *[HBM_SPEC] * (2 * n), pl.BlockSpec(memory_space=pltpu.VMEM)),
        input_output_aliases={i: 2 + i for i in range(2 * n)},
        compiler_params=pltpu.CompilerParams(has_side_effects=SIDE_EFFECT),
    )(*srcs, *lands)


def _chip_exchange_wait(tag, partial, flight, after):
    plan = _ChipExchange(partial)
    n = plan.n
    send, recv, thru = flight[0], flight[1], flight[2:2 + 2 * n]

    def body(*refs):
        plan.finish(refs[:n], refs[n:2 * n], (refs[2 * n], refs[2 * n + 1]))

    outs = pl.pallas_call(
        body, name="grads_chip_exchange_wait_" + tag,
        out_shape=tuple(pltpu.HBM(t.shape, t.dtype) for t in thru),
        in_specs=[HBM_SPEC] * (2 * n) + [SEM_SPEC, SEM_SPEC, ANY], out_specs=[HBM_SPEC] * (2 * n),
        input_output_aliases={i: i for i in range(2 * n)},
        compiler_params=pltpu.CompilerParams(has_side_effects=SIDE_EFFECT),
    )(*thru, send, recv, after)
    return dict(zip(plan.names, outs[:n])), dict(zip(plan.names, outs[n:]))


def _finish_reduce(partial, got):
    names = list(partial)
    n = len(names)
    where = _where()
    halves = [_chip_sum(k + "_chip_sum", partial[k], got[k], where, k in ROW_SHARDED) for k in names]
    plan = [_row_chunks(h.shape[1], h.shape[2] * 4) for h in halves]
    out_shape = [jax.ShapeDtypeStruct(h.shape, F32) for h in halves]

    def body(*refs):
        src, dst = refs[:n], refs[n:2 * n]
        cps = _Copies(*refs[2 * n:])
        x, y, c, _ = _position()
        waits = []
        for i in range(n):
            for r0, size in plan[i]:
                waits.append(cps.remote(cps.slot(), _rows(src[i], r0, size, lead=(c,)), _rows(dst[i], r0, size, lead=(c,)),
                                        (x, y, 1 - c)))
        for cp in waits:
            cp.wait_recv()
        cps.finish()

    outs = _comm_call("grads_pair_share", body, halves, out_shape, sum(len(p) for p in plan),
                      aliases={i: i for i in range(n)})
    return {k: o.reshape(2 * o.shape[1], o.shape[2]) for k, o in zip(names, outs)}


def _all_reduce_small(name, v):
    rows, cols = v.shape
    h = rows // 2

    def body(v_ref, o_ref, sib, pair, buf, send, recv):
        x, y, c, chips = _position()
        me = 2 * x + y
        sibling = (x, y, 1 - c)
        cp = _remote(v_ref, sib, send.at[0], recv.at[0], sibling)
        cp.start()
        cp.wait()
        pair[...] = v_ref[...] + sib[...]
        mine = pl.ds(pl.multiple_of(c * h, 8), h)
        buf[me] = pair[mine, :]
        sends = [_remote(pair.at[mine], buf.at[me], send.at[1 + k], recv.at[1 + k], (px, py, c))
                 for k, (px, py) in enumerate(chips)]
        for cp in sends:
            cp.start()
        for k, (px, py) in enumerate(chips):
            _remote(pair.at[mine], buf.at[2 * px + py], send.at[1 + k], recv.at[1 + k], (px, py, c)).wait_recv()
        for cp in sends:
            cp.wait_send()
        o_ref[mine, :] = (buf[0] + buf[1]) + (buf[2] + buf[3])
        cp = _remote(o_ref.at[mine], o_ref.at[mine], send.at[4], recv.at[4], sibling)
        cp.start()
        cp.wait()

    vm = pl.BlockSpec(memory_space=pltpu.VMEM)
    return pl.pallas_call(
        body, name=name, in_specs=[vm], out_specs=vm, out_shape=jax.ShapeDtypeStruct(v.shape, F32),
        scratch_shapes=[pltpu.VMEM((rows, cols), F32), pltpu.VMEM((rows, cols), F32), pltpu.VMEM((N_CHIPS, h, cols), F32),
                        pltpu.SemaphoreType.DMA((5,)), pltpu.SemaphoreType.DMA((5,))],
        compiler_params=pltpu.CompilerParams(vmem_limit_bytes=VMEM_LIMIT),
    )(v)


def _adamw(name, w, g, m, v):
    def fn(wt, gt, mt, vt):
        mt = ADAM_B1 * mt + (1.0 - ADAM_B1) * gt
        vt = ADAM_B2 * vt + (1.0 - ADAM_B2) * (gt * gt)
        m_hat = mt / (1.0 - ADAM_B1 ** ADAM_STEP)
        v_hat = vt / (1.0 - ADAM_B2 ** ADAM_STEP)
        delta = -ADAM_LR * (m_hat / (jnp.sqrt(v_hat) + ADAM_EPS) + ADAM_WD * wt)
        return delta, mt, vt
    n = w.shape[1]
    return _rowwise(name, fn, [w, g, m, v], [], [(n, F32)] * 3, tb=256)


WEIGHTS = ("mem_norm", "a_norm_mix", "a_w_in", "a_w_group", "a_scale", "a_w_mem_kv", "a_w_out", "a_norm_ffn", "a_w_gu",
           "a_w_down", "kv_norm", "w_kv", "b_norm_mix", "b_w_q", "b_w_mem_kv", "b_w_out", "b_norm_ffn", "b_w_gu",
           "b_w_down", "final_norm")
REPLICATED_VECS = ("mem_norm", "kv_norm", "b_norm_mix", "b_norm_ffn", "final_norm")
SHARDED_VECS = ("a_norm_mix", "a_norm_ffn", "a_scale")
D_MODEL = 1024
GROUP_ROWS = 4 * POOL_GROUP * POOL_GROUP // D_MODEL


def _row(v):
    v = v.reshape(1, -1).astype(F32)
    return jnp.pad(v, ((0, 0), (0, D_MODEL - v.shape[1])))


def _pack_small(t):
    rows = [_row(t[k]) for k in REPLICATED_VECS]
    rows.append(_row(jnp.concatenate([t[k].reshape(-1) for k in SHARDED_VECS])))
    rows.append(jnp.zeros((2, D_MODEL), F32))
    rows.append(t["a_w_group"].reshape(GROUP_ROWS, D_MODEL).astype(F32))
    return jnp.concatenate(rows, axis=0)


def _unpack_small(p, like):
    out = {k: p[i, :].reshape(like[k].shape) for i, k in enumerate(REPLICATED_VECS)}
    off = 0
    for k in SHARDED_VECS:
        size = like[k].size
        out[k] = p[len(REPLICATED_VECS), off:off + size].reshape(like[k].shape)
        off += size
    out["a_w_group"] = p[len(REPLICATED_VECS) + 3:, :].reshape(like["a_w_group"].shape)
    return out


def kernel(x, mem, mem_norm, a_norm_mix, a_w_in, a_w_group, a_scale, a_w_mem_kv, a_w_out, a_norm_ffn, a_w_gu, a_w_down, kv_norm, w_kv, b_norm_mix, b_w_q, b_w_mem_kv, b_w_out, b_norm_ffn, b_w_gu, b_w_down, final_norm, loss_target, m_mem_norm, m_a_norm_mix, m_a_w_in, m_a_w_group, m_a_scale, m_a_w_mem_kv, m_a_w_out, m_a_norm_ffn, m_a_w_gu, m_a_w_down, m_kv_norm, m_w_kv, m_b_norm_mix, m_b_w_q, m_b_w_mem_kv, m_b_w_out, m_b_norm_ffn, m_b_w_gu, m_b_w_down, m_final_norm, v_mem_norm, v_a_norm_mix, v_a_w_in, v_a_w_group, v_a_scale, v_a_w_mem_kv, v_a_w_out, v_a_norm_ffn, v_a_w_gu, v_a_w_down, v_kv_norm, v_w_kv, v_b_norm_mix, v_b_w_q, v_b_w_mem_kv, v_b_w_out, v_b_norm_ffn, v_b_w_gu, v_b_w_down, v_final_norm):
    given = dict(locals())
    wl = {k: given[k] for k in WEIGHTS}
    ml = {k: given["m_" + k] for k in WEIGHTS}
    vl = {k: given["v_" + k] for k in WEIGHTS}
    chip = 2 * lax.axis_index("x") + lax.axis_index("y")

    def mat(a):
        return a.reshape(a.shape[-2], a.shape[-1])

    shards = {k: mat(wl[k]).astype(BF16) for k in BIG}
    second = _gather_start({k: shards[k] for k in SECOND})
    full = _gather_weights({k: shards[k] for k in BIG if k not in LATE + SECOND})
    gains = jnp.zeros((16, D_MODEL), F32)
    for i, k in enumerate(SHARDED_VECS):
        part = wl[k].reshape(1, -1)
        width = part.shape[1]
        gains = lax.dynamic_update_slice(gains, part, (i, chip * width))
    gains = _all_reduce_small("gains_all_gather", gains) * 0.5
    w = dict(full)
    for k in REPLICATED_VECS:
        w[k] = wl[k].reshape(1, D_MODEL)
    w["a_norm_mix"], w["a_norm_ffn"] = gains[0:1] + second[-1][0:1, 0:1], gains[1:2]
    w["a_scale"] = gains[2:3, :SB_WIDTH]
    w["a_w_group"] = wl["a_w_group"][0].astype(BF16)

    loss, grad_x, g, partial, got = _local_step(x[0], mem[0], loss_target[0], w, {k: shards[k] for k in LATE},
                                                ({k: shards[k].shape for k in SECOND}, second))

    rest = _pair_sums("late", {k: g[k] for k in BIG if k not in EARLY + POST_SB + MID})
    partial.update(rest)
    got.update(_chip_exchange(rest))
    red = _finish_reduce(partial, got)
    small = jnp.concatenate(
        [_row(g[k]) for k in REPLICATED_VECS] + [_row(g[k]) for k in SHARDED_VECS] + [_row(loss)]
        + [jnp.zeros((7, D_MODEL), F32), g["a_w_group"].reshape(GROUP_ROWS, D_MODEL)], axis=0)
    small = _all_reduce_small("small_grads_all_reduce", small)
    gs = {k: small[i] for i, k in enumerate(REPLICATED_VECS)}
    for i, k in enumerate(SHARDED_VECS):
        width = wl[k].shape[-1]
        gs[k] = lax.dynamic_slice(small[len(REPLICATED_VECS) + i], (chip * width,), (width,))
    gs["a_w_group"] = small[16:]
    total_loss = small[8, 0]

    out_g, out_d, out_m, out_v = {}, {}, {}, {}
    for k in BIG:
        shape = wl[k].shape
        out_g[k] = red[k].reshape(shape)
        d, nm, nv = _adamw(k + "_adamw", mat(wl[k]), red[k], mat(ml[k]), mat(vl[k]))
        out_d[k], out_m[k], out_v[k] = d.reshape(shape), nm.reshape(shape), nv.reshape(shape)
    small_names = REPLICATED_VECS + SHARDED_VECS + ("a_w_group",)
    d, nm, nv = _adamw("small_adamw", _pack_small(wl), _pack_small(gs), _pack_small(ml), _pack_small(vl))
    like = {k: wl[k] for k in small_names}
    for dst, p in ((out_d, d), (out_m, nm), (out_v, nv)):
        dst.update(_unpack_small(p, like))
    for k in small_names:
        out_g[k] = gs[k].reshape(wl[k].shape)

    return (total_loss, grad_x[None], *[out_g[k] for k in WEIGHTS], *[out_d[k] for k in WEIGHTS],
            *[out_m[k] for k in WEIGHTS], *[out_v[k] for k in WEIGHTS])
```

```python
import functools

import jax
import jax.numpy as jnp
from jax import lax
from jax.experimental import pallas as pl
from jax.experimental.pallas import tpu as pltpu

F32 = jnp.float32
BF16 = jnp.bfloat16

HEAD_DIM = 64
SB_WIDTH = 768
MEM_WIDTH = 256
POOL_WINDOWS = (2, 4, 8, 16)
POOL_GROUP = 192
POOL_HALO = 16
EPS = 1e-6
ATT_SCALE = HEAD_DIM ** -0.5
ADAM_LR, ADAM_B1, ADAM_B2, ADAM_EPS, ADAM_WD, ADAM_STEP = 0.001, 0.9, 0.999, 1e-08, 0.01, 10

LANES = 128
SB_TQ, SB_TK = 512, 512
VMEM_LIMIT = 56 * 1024 * 1024
MESH = pl.DeviceIdType.MESH
COPY_BYTES = 512 * 1024
ANY = pl.BlockSpec(memory_space=pl.ANY)


def _params(*sem):
    return pltpu.CompilerParams(dimension_semantics=sem, vmem_limit_bytes=VMEM_LIMIT)


def _tile(n, pref):
    if n <= pref:
        return n
    best = None
    for t in range(LANES, pref + 1, LANES):
        if n % t == 0:
            best = t
    assert best is not None, (n, pref)
    return best


def _row_tile(t, pref):
    if t <= pref:
        return t
    for tb in range(pref - pref % 16, 0, -16):
        if t % tb == 0:
            return tb
    raise ValueError((t, pref))


def _rowwise(name, fn, rows, vecs, row_outs, sum_outs=(), tb=512):
    norm_rows = []
    for r in rows:
        if isinstance(r, tuple):
            arr, (bc, cb) = r
        else:
            arr, (bc, cb) = r, (r.shape[1], 0)
        norm_rows.append((arr, bc, cb))
    t = norm_rows[0][0].shape[0]
    tb = _row_tile(t, tb)
    n_in, n_ro = len(norm_rows) + len(vecs), len(row_outs)

    def body(*refs):
        ins = [r[...] for r in refs[:n_in]]
        outs = fn(*ins)
        if not isinstance(outs, tuple):
            outs = (outs,)
        for o_ref, o in zip(refs[n_in:n_in + n_ro], outs[:n_ro]):
            o_ref[...] = o.astype(o_ref.dtype)
        for s_ref, s in zip(refs[n_in + n_ro:], outs[n_ro:]):
            @pl.when(pl.program_id(0) == 0)
            def _():
                s_ref[...] = jnp.zeros_like(s_ref)
            s_ref[...] += s

    in_specs = [pl.BlockSpec((tb, bc), functools.partial(lambda i, cb: (i, cb), cb=cb)) for _, bc, cb in norm_rows]
    in_specs += [pl.BlockSpec(v.shape, lambda i: (0, 0)) for v in vecs]
    out_specs = [pl.BlockSpec((tb, c), lambda i: (i, 0)) for c, _ in row_outs]
    out_specs += [pl.BlockSpec((1, c), lambda i: (0, 0)) for c in sum_outs]
    out_shape = [jax.ShapeDtypeStruct((t, c), d) for c, d in row_outs]
    out_shape += [jax.ShapeDtypeStruct((1, c), F32) for c in sum_outs]
    res = pl.pallas_call(
        body, name=name, grid=(t // tb,), in_specs=in_specs, out_specs=out_specs, out_shape=out_shape,
        compiler_params=_params("arbitrary"),
    )(*[a for a, _, _ in norm_rows], *vecs)
    return res


def _rms_fwd(name, x, g):
    def fn(xt, gt):
        rstd = lax.rsqrt(jnp.mean(xt * xt, axis=-1, keepdims=True) + EPS)
        return xt * rstd * gt
    return _rowwise(name, fn, [x], [g], [(x.shape[1], BF16)])[0]


def _rms_bwd(name, x, g, dh, dres=None, want_dx=True):
    has_res = dres is not None

    def fn(*a):
        if has_res:
            xt, dht, drt, gt = a
        else:
            xt, dht, gt = a
        rstd = lax.rsqrt(jnp.mean(xt * xt, axis=-1, keepdims=True) + EPS)
        xhat = xt * rstd
        dht = dht.astype(F32)
        dg = jnp.sum(dht * xhat, axis=0, keepdims=True)
        if not want_dx:
            return (dg,)
        dxhat = dht * gt
        dx = rstd * (dxhat - xhat * jnp.mean(dxhat * xhat, axis=-1, keepdims=True))
        if has_res:
            dx = dx + drt
        return dx, dx, dg

    d = x.shape[1]
    rows = [x, dh] + ([dres] if has_res else [])
    outs = [(d, F32), (d, BF16)] if want_dx else []
    return _rowwise(name, fn, rows, [g], outs, [d])


def _rms_fwd2(name, x, g1, g2):
    def fn(xt, g1t, g2t):
        xn = xt * lax.rsqrt(jnp.mean(xt * xt, axis=-1, keepdims=True) + EPS)
        return xn * g1t, xn * g2t
    return _rowwise(name, fn, [x], [g1, g2], [(x.shape[1], BF16)] * 2)


_DOT_DIMS = {"nn": ((1,), (0,)), "nt": ((1,), (1,)), "tn": ((0,), (0,))}


def _mm(name, a, b, mode, out_dtype, res=None, tm=512, tn=512):
    if mode == "nn":
        (m, k), (k2, n) = a.shape, b.shape
    elif mode == "nt":
        (m, k), (n, k2) = a.shape, b.shape
    else:
        (k, m), (k2, n) = a.shape, b.shape
    assert k == k2, (name, a.shape, b.shape)
    tm, tn = _tile(m, tm), _tile(n, tn)
    dims = (_DOT_DIMS[mode], ((), ()))
    has_res = res is not None

    def body(a_ref, b_ref, *rest):
        acc = lax.dot_general(a_ref[...], b_ref[...], dims, preferred_element_type=F32)
        if has_res:
            acc = acc + rest[0][...]
        rest[-1][...] = acc.astype(out_dtype)

    a_spec = pl.BlockSpec((k, tm), lambda i, j: (0, i)) if mode == "tn" else pl.BlockSpec((tm, k), lambda i, j: (i, 0))
    b_spec = pl.BlockSpec((tn, k), lambda i, j: (j, 0)) if mode == "nt" else pl.BlockSpec((k, tn), lambda i, j: (0, j))
    o_spec = pl.BlockSpec((tm, tn), lambda i, j: (i, j))
    in_specs, args = [a_spec, b_spec], [a, b]
    if has_res:
        in_specs.append(o_spec)
        args.append(res)
    return pl.pallas_call(
        body, name=name, grid=(m // tm, n // tn), in_specs=in_specs, out_specs=o_spec,
        out_shape=jax.ShapeDtypeStruct((m, n), out_dtype), compiler_params=_params("parallel", "arbitrary"),
    )(*args)


def _mm_nt_rms_bwd(name, a, b, x, g, dres, other=None, tm=256):
    m, k = a.shape
    d = b.shape[0]
    n_other = 0 if other is None else 2

    def body(a_ref, b_ref, x_ref, r_ref, g_ref, *rest):
        dx_ref, dxb_ref, dg_ref = rest[n_other:n_other + 3]
        dh = lax.dot_general(a_ref[...], b_ref[...], (_DOT_DIMS["nt"], ((), ())), preferred_element_type=F32)
        xt = x_ref[...]
        rstd = lax.rsqrt(jnp.mean(xt * xt, axis=-1, keepdims=True) + EPS)
        xhat = xt * rstd
        dxhat = dh * g_ref[...]
        if other is not None:
            dh2 = rest[1][...]
            dxhat = dxhat + dh2 * rest[0][...]
        dx = rstd * (dxhat - xhat * jnp.mean(dxhat * xhat, axis=-1, keepdims=True)) + r_ref[...]
        dx_ref[...] = dx
        dxb_ref[...] = dx.astype(BF16)

        @pl.when(pl.program_id(0) == 0)
        def _():
            for ref in rest[n_other + 2:]:
                ref[...] = jnp.zeros_like(ref)
        dg_ref[...] += jnp.sum(dh * xhat, axis=0, keepdims=True)
        if other is not None:
            rest[-1][...] += jnp.sum(dh2 * xhat, axis=0, keepdims=True)

    row = lambda c: pl.BlockSpec((tm, c), lambda i: (i, 0))
    whole = lambda shape: pl.BlockSpec(shape, lambda i: (0, 0))
    vec = jax.ShapeDtypeStruct((1, d), F32)
    extra_in = [] if other is None else [whole((1, d)), row(d)]
    extra_out = [] if other is None else [vec]
    return pl.pallas_call(
        body, name=name, grid=(m // tm,),
        in_specs=[row(k), whole((d, k)), row(d), row(d), whole((1, d))] + extra_in,
        out_specs=[row(d), row(d), whole((1, d))] + [whole((1, d))] * len(extra_out),
        out_shape=[jax.ShapeDtypeStruct((m, d), F32), jax.ShapeDtypeStruct((m, d), BF16), vec] + extra_out,
        compiler_params=_params("arbitrary"),
    )(a, b, x, dres, g, *(other or ()))


def _pool(name, u, reverse, tb=512):
    t = u.shape[0]
    tb = min(tb, t)
    nt = t // tb
    c = SB_WIDTH
    hpb = tb // POOL_HALO

    def body(cur_ref, halo_ref, o_ref):
        i = pl.program_id(0)
        cur = cur_ref[...].astype(F32)
        edge = (i == nt - 1) if reverse else (i == 0)
        halo = jnp.where(edge, 0.0, halo_ref[...].astype(F32))
        col = lax.broadcasted_iota(jnp.int32, (tb + POOL_HALO, c), 1)
        row = lax.broadcasted_iota(jnp.int32, (tb + POOL_HALO, c), 0)
        wcol = jnp.where(col < POOL_GROUP, 2, jnp.where(col < 2 * POOL_GROUP, 4, jnp.where(col < 3 * POOL_GROUP, 8, 16)))
        n = tb + POOL_HALO
        if reverse:
            ext = jnp.concatenate([cur, halo], axis=0)
            tpos = i * tb + row
            ext = ext / jnp.minimum(tpos + 1, wcol).astype(F32)
            shift = lambda a, k: pltpu.roll(a, n - k, 0)
        else:
            ext = jnp.concatenate([halo, cur], axis=0)
            shift = lambda a, k: pltpu.roll(a, k, 0)
        s2 = ext + shift(ext, 1)
        s4 = s2 + shift(s2, 2)
        s8 = s4 + shift(s4, 4)
        s16 = s8 + shift(s8, 8)
        win = jnp.where(wcol == 2, s2, jnp.where(wcol == 4, s4, jnp.where(wcol == 8, s8, s16)))
        if reverse:
            out = win[:tb] - cur
        else:
            tpos = i * tb + row[POOL_HALO:] - POOL_HALO
            out = win[POOL_HALO:] / jnp.minimum(tpos + 1, wcol[POOL_HALO:]).astype(F32) - cur
        o_ref[...] = out.astype(o_ref.dtype)

    if reverse:
        halo_map = lambda i: (jnp.minimum((i + 1) * hpb, t // POOL_HALO - 1), 0)
    else:
        halo_map = lambda i: (jnp.maximum(i * hpb - 1, 0), 0)
    return pl.pallas_call(
        body, name=name, grid=(nt,),
        in_specs=[pl.BlockSpec((tb, c), lambda i: (i, 0)), pl.BlockSpec((POOL_HALO, c), halo_map)],
        out_specs=pl.BlockSpec((tb, c), lambda i: (i, 0)),
        out_shape=jax.ShapeDtypeStruct((t, c), BF16), compiler_params=_params("arbitrary"),
    )(u, u)


def _head_masks(shape):
    lane = lax.broadcasted_iota(jnp.int32, shape, 1)
    return lane < HEAD_DIM, lane >= HEAD_DIM


def _pick(mask, a):
    return jnp.where(mask, a, jnp.zeros_like(a))


def _dot(a, b, mode):
    return lax.dot_general(a, b, (_DOT_DIMS[mode], ((), ())), preferred_element_type=F32)


def _dot_tri(a, tri, suffix):
    h = a.shape[1] // 2
    lo, hi = a[:, :h], a[:, h:]
    s_lo, s_hi = jnp.sum(lo, axis=1, keepdims=True), jnp.sum(hi, axis=1, keepdims=True)
    p_lo, p_hi = _dot(lo.astype(BF16), tri, "nn"), _dot(hi.astype(BF16), tri, "nn")
    if suffix:
        p_lo = p_lo + s_hi
    else:
        p_hi = p_hi + s_lo
    return jnp.concatenate([p_lo, p_hi], axis=1), s_lo + s_hi


def _log_gates(z):
    nz = -z
    l = jnp.log(1.0 + jnp.exp(jnp.minimum(z, nz)))
    ln = jnp.minimum(nz, 0.0) - l
    return ln, z + ln


def _sb_blocks(s, tq, tk):
    tq, tk = min(tq, s), min(tk, s)
    assert tq == tk and s % tk == 0 and tk % 64 == 0, (s, tq, tk)
    return tq, tk, tk // 2


def _strict_triangle(n, pred):
    return pred(lax.broadcasted_iota(jnp.int32, (n, n), 0), lax.broadcasted_iota(jnp.int32, (n, n), 1)).astype(BF16)


def _sb_fwd(proj, kv, late_shards, tq=SB_TQ, tk=SB_TK):
    s = proj.shape[0]
    tq, tk, th = _sb_blocks(s, tq, tk)
    npair = SB_WIDTH // LANES
    gather = _Gather({k: v.shape for k, v in late_shards.items()})
    ng = gather.n

    def body(q_ref, k_ref, v_ref, *rest):
        o_ref, tot_ref = rest[ng:ng + 2]
        comm = rest[:ng], rest[ng + 2:2 * ng + 2], rest[2 * ng + 2:2 * ng + 5], rest[2 * ng + 5:]

        @pl.when(pl.program_id(0) == 0)
        def _():
            gather.start(*comm)

        tri_gt = _strict_triangle(tk, lambda j, s_: j > s_)
        tri_gt_h = _strict_triangle(th, lambda j, s_: j > s_)
        seen = lax.broadcasted_iota(jnp.int32, (tq, th), 1) < lax.broadcasted_iota(jnp.int32, (tq, th), 0)
        m_a, m_b = _head_masks((tq, LANES))

        def block(qh, k2, v2, lane_mask, carry, acc, mask, tri):
            ln_full, lsz = _log_gates(_dot(qh, k2, "nt"))
            ln = ln_full if mask is None else jnp.where(mask, ln_full, 0.0)
            w = jnp.exp(lsz + _dot(ln.astype(BF16), tri, "nn"))
            if mask is not None:
                w = jnp.where(mask, w, 0.0)
            acc = acc + jnp.exp(carry) * _dot(w.astype(BF16), _pick(lane_mask[:v2.shape[0]], v2), "nn")
            return carry + jnp.sum(ln, axis=1, keepdims=True), acc

        def q_block(qi, _):
            q0 = pl.multiple_of(qi * tq, tq)
            q2 = q_ref[pl.ds(q0, tq), :] * ATT_SCALE
            qa, qb = _pick(m_a, q2), _pick(m_b, q2)

            def both(qa, qb, k0, size, ca, cb, acc, mask, tri):
                k2 = k_ref[pl.ds(k0, size), :]
                v2 = v_ref[pl.ds(k0, size), :]
                ca, acc = block(qa, k2, v2, m_a, ca, acc, mask, tri)
                cb, acc = block(qb, k2, v2, m_b, cb, acc, mask, tri)
                return ca, cb, acc

            zero_c = jnp.zeros((th, 1), F32)
            zero_o = jnp.zeros((th, LANES), F32)
            ca, cb, acc = both(qa[th:], qb[th:], pl.multiple_of(q0 + th, th), th, zero_c, zero_c, zero_o, seen[:th], tri_gt_h)
            ca, cb = jnp.concatenate([zero_c, ca], axis=0), jnp.concatenate([zero_c, cb], axis=0)
            carry = both(qa, qb, q0, th, ca, cb, jnp.concatenate([zero_o, acc], axis=0), seen, tri_gt_h)

            def k_block(k0, carry):
                return both(qa, qb, pl.multiple_of(k0, tk), tk, *carry, None, tri_gt)

            def two_blocks(step, carry):
                k0 = q0 - (2 * step + 1) * tk
                return k_block(k0 - tk, k_block(k0, carry))

            carry = lax.fori_loop(0, qi // 2, two_blocks, carry)
            ca, cb, acc = lax.fori_loop(0, qi % 2, lambda _, c_: k_block(0, c_), carry)
            o_ref[pl.ds(q0, tq), :] = acc.astype(o_ref.dtype)
            tot_ref[0, pl.ds(q0, tq), :] = jnp.broadcast_to(ca, (tq, LANES))
            tot_ref[1, pl.ds(q0, tq), :] = jnp.broadcast_to(cb, (tq, LANES))
            return 0

        lax.fori_loop(0, s // tq, q_block, 0)

        @pl.when(pl.program_id(0) == npair - 1)
        def _():
            gather.finish(*comm)

    outs = pl.pallas_call(
        body, name="sb_fwd", grid=(npair,),
        in_specs=[pl.BlockSpec((s, LANES), lambda p: (0, p)), pl.BlockSpec((s, LANES), lambda p: (0, p)),
                  pl.BlockSpec((s, LANES), lambda p: (0, npair + p))] + [ANY] * ng,
        out_specs=[pl.BlockSpec((s, LANES), lambda p: (0, p)), pl.BlockSpec((None, 2, s, LANES), lambda p: (p, 0, 0, 0))]
        + [ANY] * ng,
        out_shape=[jax.ShapeDtypeStruct((s, SB_WIDTH), BF16), jax.ShapeDtypeStruct((npair, 2, s, LANES), F32)]
        + gather.out_shape(),
        scratch_shapes=gather.scratch(), compiler_params=_params("arbitrary"),
    )(proj, kv, kv, *[late_shards[k] for k in gather.names])
    return outs[0], outs[1], dict(zip(gather.names, outs[2:]))


def _sb_bwd(proj, kv, dcat, tot, early_partial, tq=SB_TQ, tk=SB_TK):
    s = proj.shape[0]
    tq, tk, th = _sb_blocks(s, tq, tk)
    npair = SB_WIDTH // LANES
    exchange = _ChipExchange(early_partial)
    ne = exchange.n

    def body(q_ref, k_ref, v_ref, do_ref, tot_ref, *rest):
        dq_ref, dk_ref, dv_ref = rest[ne:ne + 3]
        dk_acc, dv_acc = rest[2 * ne + 3:2 * ne + 5]
        comm = rest[:ne], rest[ne + 3:2 * ne + 3], rest[2 * ne + 5:]

        @pl.when(pl.program_id(0) == 0)
        def _():
            exchange.start(*comm)

        tris = (_strict_triangle(tk // 2, lambda j, s_: j > s_), _strict_triangle(tk // 2, lambda j, s_: j < s_))
        tris_h = (_strict_triangle(th // 2, lambda j, s_: j > s_), _strict_triangle(th // 2, lambda j, s_: j < s_))
        seen = lax.broadcasted_iota(jnp.int32, (tq, th), 1) < lax.broadcasted_iota(jnp.int32, (tq, th), 0)
        m_a, m_b = _head_masks((tq, LANES))
        dk_acc[...] = jnp.zeros_like(dk_acc)
        dv_acc[...] = jnp.zeros_like(dv_acc)

        def block(qh, doh, k2, v2, lane_mask, tot_h, carry, mask, tri):
            c_ln, c_d, dq = carry
            ln_full, lsz = _log_gates(_dot(qh, k2, "nt"))
            ln = ln_full if mask is None else jnp.where(mask, ln_full, 0.0)
            inside, total = _dot_tri(ln, tri[0], True)
            c_ln = c_ln + total
            w = jnp.exp(lsz + ((tot_h - c_ln) + inside))
            if mask is not None:
                w = jnp.where(mask, w, 0.0)
            dlw = _dot(doh, v2, "nt") * w
            before, d_total = _dot_tri(dlw, tri[1], False)
            dz = dlw * jnp.exp(ln_full) - (before + c_d) * jnp.exp(lsz)
            if mask is not None:
                dz = jnp.where(mask, dz, 0.0)
            dz = dz.astype(BF16)
            dq = dq + _dot(dz, _pick(lane_mask[:k2.shape[0]], k2), "nn")
            dk = _dot(dz, qh, "tn")
            dv = _dot(w.astype(BF16), doh, "tn")
            carry = (c_ln, c_d + d_total, dq)
            return carry, dk, dv

        def q_block(qi, _):
            q0 = pl.multiple_of(qi * tq, tq)
            q2 = q_ref[pl.ds(q0, tq), :] * ATT_SCALE
            do2 = do_ref[pl.ds(q0, tq), :]
            qa, qb = _pick(m_a, q2), _pick(m_b, q2)
            doa, dob = _pick(m_a, do2), _pick(m_b, do2)
            tot_a = tot_ref[0, pl.ds(q0, tq), 0:1]
            tot_b = tot_ref[1, pl.ds(q0, tq), 0:1]
            zero_c = jnp.zeros((tq, 1), F32)
            zero_q = jnp.zeros((tq, LANES), F32)

            def both(rows, k0, size, ca, cb, mask, tri):
                k2 = k_ref[pl.ds(k0, size), :]
                v2 = v_ref[pl.ds(k0, size), :]
                ca, dka, dva = block(qa[rows], doa[rows], k2, v2, m_a, tot_a[rows], ca, mask, tri)
                cb, dkb, dvb = block(qb[rows], dob[rows], k2, v2, m_b, tot_b[rows], cb, mask, tri)
                dk_acc[pl.ds(k0, size), :] += dka + dkb
                dv_acc[pl.ds(k0, size), :] += dva + dvb
                return ca, cb

            def k_block(k0, carry):
                return both(slice(None), pl.multiple_of(k0, tk), tk, carry[0], carry[1], None, tris)

            def two_blocks(step, carry):
                return k_block((2 * step + 1) * tk, k_block(2 * step * tk, carry))

            init = ((zero_c, zero_c, zero_q), (zero_c, zero_c, zero_q))
            carry = lax.fori_loop(0, qi // 2, two_blocks, init)
            ca, cb = lax.fori_loop(0, qi % 2, lambda _, c_: k_block(q0 - tk, c_), carry)
            ca, cb = both(slice(None), q0, th, ca, cb, seen, tris_h)
            late = slice(th, tq)
            la, lb = both(late, pl.multiple_of(q0 + th, th), th, tuple(t[late] for t in ca), tuple(t[late] for t in cb),
                          seen[:th], tris_h)
            dq = jnp.concatenate([ca[2][:th] + cb[2][:th], la[2] + lb[2]], axis=0)
            dq_ref[pl.ds(q0, tq), :] = (dq * ATT_SCALE).astype(dq_ref.dtype)
            return 0

        lax.fori_loop(0, s // tq, q_block, 0)
        dk_ref[...] = dk_acc[...].astype(dk_ref.dtype)
        dv_ref[...] = dv_acc[...].astype(dv_ref.dtype)

        @pl.when(pl.program_id(0) == npair - 1)
        def _():
            exchange.finish(*comm)

    col = lambda off: pl.BlockSpec((s, LANES), functools.partial(lambda p, off: (0, off + p), off=off))
    outs = pl.pallas_call(
        body, name="sb_bwd", grid=(npair,),
        in_specs=[col(0), col(0), col(npair), col(0), pl.BlockSpec((None, 2, s, LANES), lambda p: (p, 0, 0, 0))]
        + [ANY] * ne,
        out_specs=[col(0), col(0), col(0)] + [ANY] * ne,
        out_shape=[jax.ShapeDtypeStruct((s, SB_WIDTH), BF16)] * 3 + exchange.out_shape(),
        scratch_shapes=[pltpu.VMEM((s, LANES), F32), pltpu.VMEM((s, LANES), F32)] + exchange.scratch(),
        compiler_params=_params("arbitrary"),
    )(proj, kv, kv, dcat, tot, *[early_partial[k] for k in exchange.names])
    dq, dk, dv = outs[:3]
    return dq, jnp.concatenate([dk, dv], axis=1), dict(zip(exchange.names, outs[3:]))


def _mem_fwd(name, proj, mkv, tq=512):
    s = proj.shape[0]
    tq = min(tq, s)
    qblk = SB_WIDTH // MEM_WIDTH

    def body(q_ref, kv_ref, o_ref):
        m_a, m_b = _head_masks((tq, LANES))
        mk_a, mk_b = _head_masks((kv_ref.shape[0], LANES))
        for p in range(MEM_WIDTH // LANES):
            q2 = q_ref[:, p * LANES:(p + 1) * LANES].astype(BF16)
            k2 = kv_ref[:, p * LANES:(p + 1) * LANES]
            v2 = kv_ref[:, MEM_WIDTH + p * LANES:MEM_WIDTH + (p + 1) * LANES]
            acc = jnp.zeros((tq, LANES), F32)
            for mq, mk in ((m_a, mk_a), (m_b, mk_b)):
                logits = _dot(_pick(mq, q2), k2, "nt") * ATT_SCALE
                e = jnp.exp(logits - jnp.max(logits, axis=-1, keepdims=True))
                prob = e / jnp.sum(e, axis=-1, keepdims=True)
                acc = acc + _dot(prob.astype(BF16), _pick(mk, v2), "nn")
            o_ref[:, p * LANES:(p + 1) * LANES] = acc.astype(o_ref.dtype)

    return pl.pallas_call(
        body, name=name, grid=(s // tq,),
        in_specs=[pl.BlockSpec((tq, MEM_WIDTH), lambda i: (i, qblk)), pl.BlockSpec(mkv.shape, lambda i: (0, 0))],
        out_specs=pl.BlockSpec((tq, MEM_WIDTH), lambda i: (i, 0)),
        out_shape=jax.ShapeDtypeStruct((s, MEM_WIDTH), BF16), compiler_params=_params("arbitrary"),
    )(proj, mkv)


def _mem_bwd(name, proj, mkv, dcat, tq=512):
    s = proj.shape[0]
    tq = min(tq, s)
    qblk = SB_WIDTH // MEM_WIDTH

    def body(q_ref, kv_ref, do_ref, dq_ref, dkv_ref):
        @pl.when(pl.program_id(0) == 0)
        def _():
            dkv_ref[...] = jnp.zeros_like(dkv_ref)

        m_a, m_b = _head_masks((tq, LANES))
        for p in range(MEM_WIDTH // LANES):
            ksl = slice(p * LANES, (p + 1) * LANES)
            vsl = slice(MEM_WIDTH + p * LANES, MEM_WIDTH + (p + 1) * LANES)
            q2, do2 = q_ref[:, ksl].astype(BF16), do_ref[:, ksl]
            k2, v2 = kv_ref[:, ksl], kv_ref[:, vsl]
            mk_a, mk_b = _head_masks(k2.shape)
            dq = jnp.zeros((tq, LANES), F32)
            dk = jnp.zeros(k2.shape, F32)
            dv = jnp.zeros(k2.shape, F32)
            for mq, mk in ((m_a, mk_a), (m_b, mk_b)):
                qh, doh = _pick(mq, q2), _pick(mq, do2)
                logits = _dot(qh, k2, "nt") * ATT_SCALE
                e = jnp.exp(logits - jnp.max(logits, axis=-1, keepdims=True))
                prob = e / jnp.sum(e, axis=-1, keepdims=True)
                dp = _dot(doh, v2, "nt")
                ds = prob * (dp - jnp.sum(dp * prob, axis=-1, keepdims=True)) * ATT_SCALE
                ds = ds.astype(BF16)
                dq = dq + _dot(ds, _pick(mk, k2), "nn")
                dk = dk + _dot(ds, qh, "tn")
                dv = dv + _dot(prob.astype(BF16), doh, "tn")
            dq_ref[:, ksl] = dq.astype(dq_ref.dtype)
            dkv_ref[:, ksl] += dk
            dkv_ref[:, vsl] += dv

    return pl.pallas_call(
        body, name=name, grid=(s // tq,),
        in_specs=[pl.BlockSpec((tq, MEM_WIDTH), lambda i: (i, qblk)), pl.BlockSpec(mkv.shape, lambda i: (0, 0)),
                  pl.BlockSpec((tq, MEM_WIDTH), lambda i: (i, qblk))],
        out_specs=[pl.BlockSpec((tq, MEM_WIDTH), lambda i: (i, 0)), pl.BlockSpec(mkv.shape, lambda i: (0, 0))],
        out_shape=[jax.ShapeDtypeStruct((s, MEM_WIDTH), BF16), jax.ShapeDtypeStruct(mkv.shape, F32)],
        compiler_params=_params("arbitrary"),
    )(proj, mkv, dcat)


def _gu_swiglu(name, h, w_gu, tn=256):
    t, d = h.shape
    f = w_gu.shape[1] // 2
    nb = f // tn

    def body(h_ref, wg_ref, wu_ref, g_ref, u_ref, a_ref):
        hh = h_ref[...]
        g = _dot(hh, wg_ref[...], "nn")
        u = _dot(hh, wu_ref[...], "nn")
        g_ref[...] = g.astype(BF16)
        u_ref[...] = u.astype(BF16)
        a_ref[...] = (g * jax.nn.sigmoid(g) * u).astype(BF16)

    out = pl.BlockSpec((t, tn), lambda j: (0, j))
    return pl.pallas_call(
        body, name=name, grid=(nb,),
        in_specs=[pl.BlockSpec((t, d), lambda j: (0, 0)), pl.BlockSpec((d, tn), lambda j: (0, j)),
                  pl.BlockSpec((d, tn), lambda j: (0, nb + j))],
        out_specs=[out, out, out], out_shape=[jax.ShapeDtypeStruct((t, f), BF16)] * 3,
        compiler_params=_params("arbitrary"),
    )(h, w_gu, w_gu)


def _down_dx_swiglu(name, dout_bf, w_down, gate, up, tm=256):
    t, d = dout_bf.shape
    f = w_down.shape[0]

    def body(do_ref, w_ref, g_ref, u_ref, o_ref):
        dact = _dot(do_ref[...], w_ref[...], "nt")
        g, u = g_ref[...].astype(F32), u_ref[...].astype(F32)
        sg = jax.nn.sigmoid(g)
        silu = g * sg
        o_ref[:, :f] = (dact * u * (sg + silu * (1.0 - sg))).astype(BF16)
        o_ref[:, f:] = (dact * silu).astype(BF16)

    row = lambda c: pl.BlockSpec((tm, c), lambda i: (i, 0))
    return pl.pallas_call(
        body, name=name, grid=(t // tm,),
        in_specs=[row(d), pl.BlockSpec((f, d), lambda i: (0, 0)), row(f), row(f)],
        out_specs=row(2 * f), out_shape=jax.ShapeDtypeStruct((t, 2 * f), BF16), compiler_params=_params("arbitrary"),
    )(dout_bf, w_down, gate, up)


def _ffn_fwd(tag, x, norm, w_gu, w_down):
    h = _rms_fwd(tag + "_ffn_norm", x, norm)
    gate, up, act = _gu_swiglu(tag + "_gu", h, w_gu)
    out = _mm(tag + "_down", act, w_down, "nn", F32, res=x, tn=1024)
    return out, (h, gate, up, act)


def _ffn_bwd(tag, x, norm, w_gu, w_down, saved, dout, dout_bf):
    h, gate, up, act = saved
    g_down = _mm(tag + "_down_dw", act, dout_bf, "tn", BF16, tm=256, tn=1024)
    dgu = _down_dx_swiglu(tag + "_down_dx", dout_bf, w_down, gate, up)
    g_gu = _mm(tag + "_gu_dw", h, dgu, "tn", BF16, tm=1024)
    dx, dx_bf, g_norm = _mm_nt_rms_bwd(tag + "_gu_dx", dgu, w_gu, x, norm, dout)
    return dx, dx_bf, g_gu, g_down, g_norm


def _mem_kv(tag, mem_n, w_mem_kv):
    return _mm(tag + "_memkv", mem_n, w_mem_kv, "nn", BF16)


def _mem_kv_bwd(tag, mem, mem_norm, mem_n, w_mem_kv, dmkv):
    dmkv = dmkv.astype(BF16)
    g_w = _mm(tag + "_memkv_dw", mem_n, dmkv, "tn", BF16)
    dmem_n = _mm(tag + "_memkv_dx", dmkv, w_mem_kv, "nt", F32)
    (g_norm,) = _rms_bwd(tag + "_memnorm_bwd", mem, mem_norm, dmem_n, want_dx=False)
    return g_w, g_norm


def _block_diag(w_group):
    z = jnp.zeros((POOL_GROUP, POOL_GROUP), w_group.dtype)
    return jnp.concatenate(
        [jnp.concatenate([w_group[g] if h == g else z for h in range(4)], axis=1) for g in range(4)], axis=0)


def _local_step(x, mem, target, w, late_shards, second):
    g = {}
    w = dict(w)
    mem_n = _rms_fwd("mem_norm", mem, w["mem_norm"])
    h_a = _rms_fwd("a_mix_norm", x, w["a_norm_mix"])
    proj_a = _mm("a_in", h_a, w["a_w_in"], "nn", F32, tn=1024)
    pooled = _pool("a_pool", proj_a, reverse=False)
    w_bd = _block_diag(w["a_w_group"])
    g_pre = _mm("a_group", pooled, w_bd, "nn", BF16, tn=768)
    mkv_a = _mem_kv("a", mem_n, w["a_w_mem_kv"])
    mem_a = _mem_fwd("a_mem_attn", proj_a, mkv_a)
    cat_a = _rowwise("a_cat", lambda gp, mo, sc: jnp.concatenate([gp.astype(F32) * sc, mo.astype(F32)], axis=1),
                     [g_pre, mem_a], [w["a_scale"]], [(1024, BF16)])[0]
    x1 = _mm("a_out", cat_a, w["a_w_out"], "nn", F32, res=x, tn=1024)
    w.update(_gather_finish(second[0], second[1], x1))
    x2, ffn_a = _ffn_fwd("a", x1, w["a_norm_ffn"], w["a_w_gu"], w["a_w_down"])
    h_k, h_b = _rms_fwd2("x2_norms", x2, w["kv_norm"], w["b_norm_mix"])
    kv = _mm("kv_proj", h_k, w["w_kv"], "nn", BF16, tn=1536)
    proj_b = _mm("b_q", h_b, w["b_w_q"], "nn", BF16, tn=1024)
    sb_out, tot, late = _sb_fwd(proj_b, kv, late_shards)
    w.update(late)
    mkv_b = _mem_kv("b", mem_n, w["b_w_mem_kv"])
    mem_b = _mem_fwd("b_mem_attn", proj_b, mkv_b)
    cat_b = jnp.concatenate([sb_out, mem_b], axis=1)
    x3 = _mm("b_out", cat_b, w["b_w_out"], "nn", F32, res=x2, tn=1024)
    x4, ffn_b = _ffn_fwd("b", x3, w["b_norm_ffn"], w["b_w_gu"], w["b_w_down"])

    d = x.shape[1]

    def head(xt, tt, gt):
        rstd = lax.rsqrt(jnp.mean(xt * xt, axis=-1, keepdims=True) + EPS)
        xhat = xt * rstd
        err = xhat * gt - tt
        loss = 0.5 * jnp.sum(jnp.sum(err * err, axis=1, keepdims=True), axis=0, keepdims=True) / d
        dy = err / d
        dxhat = dy * gt
        dx = rstd * (dxhat - xhat * jnp.mean(dxhat * xhat, axis=-1, keepdims=True))
        return dx, dx, jnp.sum(dy * xhat, axis=0, keepdims=True), jnp.broadcast_to(loss, (1, LANES))

    dx4, dx4_bf, g["final_norm"], loss = _rowwise(
        "loss_head", head, [x4, target], [w["final_norm"]], [(d, F32), (d, BF16)], [d, LANES])

    dx3, dx3_bf, g["b_w_gu"], g["b_w_down"], g["b_norm_ffn"] = _ffn_bwd(
        "b", x3, w["b_norm_ffn"], w["b_w_gu"], w["b_w_down"], ffn_b, dx4, dx4_bf)
    dcat_b = _mm("b_out_dx", dx3_bf, w["b_w_out"], "nt", BF16, tn=1024)
    g["b_w_out"] = _mm("b_out_dw", cat_b, dx3_bf, "tn", BF16, tm=1024, tn=1024)
    early_partial = _pair_sums("early", {k: g.pop(k) for k in EARLY})
    dq_sb, dkv, early_got = _sb_bwd(proj_b, kv, dcat_b, tot, early_partial)
    dq_mem_b, dmkv_b = _mem_bwd("b_mem_attn_bwd", proj_b, mkv_b, dcat_b)
    dproj_b = jnp.concatenate([dq_sb, dq_mem_b], axis=1)
    g["b_w_q"] = _mm("b_q_dw", h_b, dproj_b, "tn", BF16, tm=1024, tn=1024)
    dh_b = _mm("b_q_dx", dproj_b, w["b_w_q"], "nt", F32, tn=1024)
    g["b_w_mem_kv"], g_memnorm_b = _mem_kv_bwd("b", mem, w["mem_norm"], mem_n, w["b_w_mem_kv"], dmkv_b)
    g["w_kv"] = _mm("kv_proj_dw", h_k, dkv, "tn", BF16, tm=1024)
    dx2, dx2_bf, g["kv_norm"], g["b_norm_mix"] = _mm_nt_rms_bwd(
        "kv_proj_dx", dkv, w["w_kv"], x2, w["kv_norm"], dx3, other=(w["b_norm_mix"], dh_b))
    post_partial = _pair_sums("post_sb", {k: g.pop(k) for k in POST_SB})
    post_flight = _chip_exchange_start("post_sb", post_partial)
    norm_ffn_a = w["a_norm_ffn"] + post_flight[-1][0:1, 0:1]

    dx1, dx1_bf, g["a_w_gu"], g["a_w_down"], g["a_norm_ffn"] = _ffn_bwd(
        "a", x1, norm_ffn_a, w["a_w_gu"], w["a_w_down"], ffn_a, dx2, dx2_bf)
    g["a_w_out"] = _mm("a_out_dw", cat_a, dx1_bf, "tn", BF16, tm=1024, tn=1024)
    mid_partial = _pair_sums("mid", {k: g.pop(k) for k in MID})
    mid_flight = _chip_exchange_start("mid", mid_partial)
    scale_a = w["a_scale"] + mid_flight[-1][0:1, 0:1]
    dcat_a = _mm("a_out_dx", dx1_bf, w["a_w_out"], "nt", BF16, tn=1024)

    def scale_bwd(dc, gp, sc):
        dc, gp = dc.astype(F32), gp.astype(F32)
        return dc * sc, jnp.sum(dc * gp, axis=0, keepdims=True)

    dg_pre, g["a_scale"] = _rowwise("a_scale_bwd", scale_bwd, [(dcat_a, (SB_WIDTH, 0)), g_pre], [scale_a],
                                    [(SB_WIDTH, BF16)], [SB_WIDTH])
    g_bd = _mm("a_group_dw", pooled, dg_pre, "tn", F32, tm=768, tn=768)
    g["a_w_group"] = jnp.stack([g_bd[i * POOL_GROUP:(i + 1) * POOL_GROUP, i * POOL_GROUP:(i + 1) * POOL_GROUP]
                                for i in range(4)])
    dpooled = _mm("a_group_dx", dg_pre, w_bd, "nt", F32, tn=768)
    du_pool = _pool("a_pool_bwd", dpooled, reverse=True)
    dq_mem_a, dmkv_a = _mem_bwd("a_mem_attn_bwd", proj_a, mkv_a, dcat_a)
    dproj_a = jnp.concatenate([du_pool, dq_mem_a], axis=1)
    g["a_w_in"] = _mm("a_in_dw", h_a, dproj_a, "tn", BF16, tm=1024, tn=1024)
    grad_x, _, g["a_norm_mix"] = _mm_nt_rms_bwd("a_in_dx", dproj_a, w["a_w_in"], x, w["a_norm_mix"], dx1)
    g["a_w_mem_kv"], g_memnorm_a = _mem_kv_bwd("a", mem, w["mem_norm"], mem_n, w["a_w_mem_kv"], dmkv_a)
    g["mem_norm"] = g_memnorm_a + g_memnorm_b
    post_partial, post_got = _chip_exchange_wait("post_sb", post_partial, post_flight, g["a_norm_mix"])
    mid_partial, mid_got = _chip_exchange_wait("mid", mid_partial, mid_flight, g["a_norm_mix"])
    return (loss, grad_x, g, {**early_partial, **post_partial, **mid_partial}, {**early_got, **post_got, **mid_got})


ROW_SHARDED = ("a_w_in", "a_w_mem_kv", "a_w_out", "a_w_down", "b_w_q", "b_w_mem_kv", "b_w_out", "b_w_down")
COL_SHARDED = ("a_w_gu", "w_kv", "b_w_gu")
BIG = ("a_w_in", "a_w_mem_kv", "a_w_out", "a_w_gu", "a_w_down", "w_kv", "b_w_q", "b_w_mem_kv", "b_w_out", "b_w_gu",
       "b_w_down")
LATE = ("b_w_mem_kv", "b_w_out", "b_w_gu", "b_w_down")
EARLY = ("b_w_gu", "b_w_down", "b_w_out")
POST_SB = ("w_kv", "b_w_q", "b_w_mem_kv")
MID = ("a_w_gu", "a_w_down", "a_w_out")
SECOND = ("a_w_gu", "a_w_down", "w_kv", "b_w_q")
N_CHIPS = 4
N_DEV = 8


def _position():
    x, y, c = lax.axis_index("x"), lax.axis_index("y"), lax.axis_index("c")
    other_chips = [(1 - x, y), (x, 1 - y), (1 - x, 1 - y)]
    return x, y, c, other_chips


def _remote(src, dst, send_sem, recv_sem, device):
    return pltpu.make_async_remote_copy(src_ref=src, dst_ref=dst, send_sem=send_sem, recv_sem=recv_sem,
                                        device_id=device, device_id_type=MESH)


def _comm_call(name, body, args, out_shape, n_remote, aliases=None):
    return pl.pallas_call(
        body, name=name, in_specs=[ANY] * len(args), out_specs=[ANY] * len(out_shape), out_shape=out_shape,
        scratch_shapes=[pltpu.SemaphoreType.DMA((n_remote,)), pltpu.SemaphoreType.DMA((n_remote,))],
        input_output_aliases=aliases or {},
    )(*args)


def _row_chunks(nrows, row_bytes, mult=16):
    assert nrows % mult == 0, (nrows, mult)
    per = max(mult, (COPY_BYTES // row_bytes) // mult * mult)
    return [(r0, min(per, nrows - r0)) for r0 in range(0, nrows, per)]


def _rows(ref, start, size, lead=()):
    if isinstance(start, int):
        return ref.at[(*lead, pl.ds(start, size))]
    return ref.at[(*lead, pl.ds(pl.multiple_of(start, 16), size))]


class _Copies:
    def __init__(self, send, recv):
        self.send, self.recv = send, recv
        self.n_remote = 0
        self.remotes = []

    def slot(self):
        self.n_remote += 1
        return self.n_remote - 1

    def remote(self, k, src, dst, device):
        cp = _remote(src, dst, self.send.at[k], self.recv.at[k], device)
        cp.start()
        self.remotes.append(cp)
        return cp

    def finish(self):
        for cp in self.remotes:
            cp.wait_send()


def _shard_cols(ref, row_sharded, cdim, chip):
    if row_sharded:
        return ref
    return ref.at[:, pl.ds(pl.multiple_of(chip * cdim, LANES), cdim)]


class _Gather:
    def __init__(self, shapes, mult=16):
        self.names = list(shapes)
        self.shapes = [tuple(shapes[k]) for k in self.names]
        self.row_sharded = [k in ROW_SHARDED for k in self.names]
        self.chunks = [(i, r0, size) for i, (r, cdim) in enumerate(self.shapes)
                       for r0, size in _row_chunks(r // 2, cdim * 2, mult)]
        self.n = len(self.names)

    def out_shape(self):
        return [jax.ShapeDtypeStruct((N_CHIPS * r, cdim) if rs else (r, N_CHIPS * cdim), BF16)
                for (r, cdim), rs in zip(self.shapes, self.row_sharded)]

    def scratch(self):
        n_remote, n_local = 6 * len(self.chunks), self.n + 2 * len(self.chunks)
        return [pltpu.SemaphoreType.DMA((n_remote,)), pltpu.SemaphoreType.DMA((n_remote,)),
                pltpu.SemaphoreType.DMA((n_local,))] + [pltpu.VMEM(s, BF16) for s in self.shapes]

    def _window(self, dst, i, chip, half, r0, size):
        r, cdim = self.shapes[i]
        base = (chip * r if self.row_sharded[i] else 0) + half * (r // 2) + r0
        return _rows(_shard_cols(dst[i], self.row_sharded[i], cdim, chip), base, size)

    def _mine(self, refs, i, half, r0, size):
        return _rows(refs[i], half * (self.shapes[i][0] // 2) + r0, size)

    def _sent(self, src, dst, sems, q, k, px, py, c, me):
        i, r0, size = self.chunks[q]
        return _remote(self._mine(src, i, c, r0, size), self._window(dst, i, me, c, r0, size),
                       sems[0].at[6 * q + k], sems[1].at[6 * q + k], (px, py, c))

    def start(self, src, dst, sems, vm):
        x, y, c, chips = _position()
        for q in range(len(self.chunks)):
            for k, (px, py) in enumerate(chips):
                self._sent(src, dst, sems, q, k, px, py, c, 2 * x + y).start()
        for i in range(self.n):
            pltpu.make_async_copy(src[i], vm[i], sems[2].at[i]).start()

    def start_ici(self, src, dst, sems):
        x, y, c, chips = _position()
        for q in range(len(self.chunks)):
            for k, (px, py) in enumerate(chips):
                self._sent(src, dst, sems, q, k, px, py, c, 2 * x + y).start()

    def wait_ici(self, src, dst, sems):
        x, y, c, chips = _position()
        for q, (i, r0, size) in enumerate(self.chunks):
            for k, (px, py) in enumerate(chips):
                landed = self._window(dst, i, 2 * px + py, c, r0, size)
                _remote(landed, landed, sems[0].at[6 * q + k], sems[1].at[6 * q + k], (px, py, c)).wait_recv()
        for q in range(len(self.chunks)):
            for k, (px, py) in enumerate(chips):
                self._sent(src, dst, sems, q, k, px, py, c, 2 * x + y).wait_send()

    def finish(self, src, dst, sems, vm, landed=None):
        x, y, c, chips = _position()
        me = 2 * x + y
        send, recv, loc = sems
        arrived = dst if landed is None else landed
        if landed is not None:
            for i in range(self.n):
                pltpu.make_async_copy(src[i], vm[i], loc.at[i]).start()
        for i in range(self.n):
            pltpu.make_async_copy(src[i], vm[i], loc.at[i]).wait()
        placed, forwarded = [], []
        for q, (i, r0, size) in enumerate(self.chunks):
            for half in range(2):
                cp = pltpu.make_async_copy(self._mine(vm, i, half, r0, size), self._window(dst, i, me, half, r0, size),
                                           loc.at[self.n + 2 * q + half])
                cp.start()
                placed.append(cp)
        for q, (i, r0, size) in enumerate(self.chunks):
            for k, (px, py) in enumerate(chips):
                if landed is None:
                    there = self._window(dst, i, 2 * px + py, c, r0, size)
                    _remote(there, there, send.at[6 * q + k], recv.at[6 * q + k], (px, py, c)).wait_recv()
                cp = _remote(self._window(arrived, i, 2 * px + py, c, r0, size), self._window(dst, i, 2 * px + py, c, r0, size),
                             send.at[6 * q + 3 + k], recv.at[6 * q + 3 + k], (x, y, 1 - c))
                cp.start()
                forwarded.append(cp)
        for q, (i, r0, size) in enumerate(self.chunks):
            for k, (px, py) in enumerate(chips):
                there = self._window(dst, i, 2 * px + py, 1 - c, r0, size)
                _remote(there, there, send.at[6 * q + 3 + k], recv.at[6 * q + 3 + k], (x, y, 1 - c)).wait_recv()
        if landed is None:
            for q in range(len(self.chunks)):
                for k, (px, py) in enumerate(chips):
                    self._sent(src, dst, sems, q, k, px, py, c, me).wait_send()
        for cp in forwarded:
            cp.wait_send()
        for cp in placed:
            cp.wait()


class _GatherRelay(_Gather):
    def __init__(self, shapes):
        super().__init__(shapes, mult=32)

    def scratch(self):
        n_remote, n_local = 8 * len(self.chunks), self.n + 2 * len(self.chunks)
        return [pltpu.SemaphoreType.DMA((n_remote,)), pltpu.SemaphoreType.DMA((n_remote,)),
                pltpu.SemaphoreType.DMA((n_local,))] + [pltpu.VMEM(s, BF16) for s in self.shapes]

    def run(self, src, dst, sems, vm):
        x, y, c, chips = _position()
        me, diag, sibling = 2 * x + y, 2 * (1 - x) + (1 - y), (x, y, 1 - c)
        nbrs = chips[:2]
        send, recv, loc = sems

        def copy(slot, ref, device):
            return _remote(ref, ref, send.at[slot], recv.at[slot], device)

        first = []
        for q, (i, r0, size) in enumerate(self.chunks):
            for k, (px, py) in enumerate(nbrs):
                cp = _remote(self._mine(src, i, c, r0, size), self._window(dst, i, me, c, r0, size),
                             send.at[8 * q + k], recv.at[8 * q + k], (px, py, c))
                cp.start()
                first.append(cp)
        for i in range(self.n):
            pltpu.make_async_copy(src[i], vm[i], loc.at[i]).start()
        for i in range(self.n):
            pltpu.make_async_copy(src[i], vm[i], loc.at[i]).wait()
        placed, passed = [], []
        for q, (i, r0, size) in enumerate(self.chunks):
            for half in range(2):
                cp = pltpu.make_async_copy(self._mine(vm, i, half, r0, size), self._window(dst, i, me, half, r0, size),
                                           loc.at[self.n + 2 * q + half])
                cp.start()
                placed.append(cp)
        for q, (i, r0, size) in enumerate(self.chunks):
            for k, (px, py) in enumerate(nbrs):
                landed = self._window(dst, i, 2 * px + py, c, r0, size)
                copy(8 * q + k, landed, (px, py, c)).wait_recv()
                piece = self._window(dst, i, 2 * px + py, c, r0 + k * (size // 2), size // 2)
                ox, oy = nbrs[1 - k]
                for cp in (copy(8 * q + 2 + k, piece, (ox, oy, c)), copy(8 * q + 4 + k, landed, sibling)):
                    cp.start()
                    passed.append(cp)
        for q, (i, r0, size) in enumerate(self.chunks):
            for k in range(2):
                piece = self._window(dst, i, diag, c, r0 + k * (size // 2), size // 2)
                ox, oy = nbrs[1 - k]
                copy(8 * q + 2 + k, piece, (ox, oy, c)).wait_recv()
                cp = copy(8 * q + 6 + k, piece, sibling)
                cp.start()
                passed.append(cp)
        for q, (i, r0, size) in enumerate(self.chunks):
            for k, (px, py) in enumerate(nbrs):
                copy(8 * q + 4 + k, self._window(dst, i, 2 * px + py, 1 - c, r0, size), sibling).wait_recv()
                copy(8 * q + 6 + k, self._window(dst, i, diag, 1 - c, r0 + k * (size // 2), size // 2), sibling).wait_recv()
        for cp in first + passed:
            cp.wait_send()
        for cp in placed:
            cp.wait()


def _gather_weights(shards):
    plan = _GatherRelay({k: v.shape for k, v in shards.items()})
    n = plan.n

    def body(*refs):
        plan.run(refs[:n], refs[n:2 * n], refs[2 * n:2 * n + 3], refs[2 * n + 3:])

    outs = pl.pallas_call(
        body, name="gather_weights", in_specs=[ANY] * n, out_specs=[ANY] * n, out_shape=plan.out_shape(),
        scratch_shapes=plan.scratch(), compiler_params=pltpu.CompilerParams(vmem_limit_bytes=VMEM_LIMIT),
    )(*[shards[k] for k in plan.names])
    return dict(zip(plan.names, outs))


def _gather_start(shards):
    plan = _Gather({k: v.shape for k, v in shards.items()})
    n = plan.n
    n_remote = 6 * len(plan.chunks)
    srcs = [pltpu.with_memory_space_constraint(shards[k], pltpu.HBM) for k in plan.names]
    lands = [pltpu.with_memory_space_constraint(lax.empty(s.shape, s.dtype), pltpu.HBM) for s in plan.out_shape()]

    def body(*refs):
        plan.start_ici(refs[:n], refs[n:2 * n], (refs[2 * n], refs[2 * n + 1]))
        refs[-1][...] = jnp.zeros_like(refs[-1])

    return pl.pallas_call(
        body, name="gather_start",
        out_shape=(pltpu.SemaphoreType.DMA((n_remote,)), pltpu.SemaphoreType.DMA((n_remote,)),
                   *[pltpu.HBM(a.shape, a.dtype) for a in srcs + lands], jax.ShapeDtypeStruct((8, LANES), F32)),
        in_specs=[HBM_SPEC] * (2 * n),
        out_specs=(SEM_SPEC, SEM_SPEC, *[HBM_SPEC] * (2 * n), pl.BlockSpec(memory_space=pltpu.VMEM)),
        input_output_aliases={i: 2 + i for i in range(2 * n)},
        compiler_params=pltpu.CompilerParams(has_side_effects=SIDE_EFFECT),
    )(*srcs, *lands)


def _gather_finish(shapes, flight, after):
    plan = _Gather(shapes)
    n = plan.n
    send, recv, thru = flight[0], flight[1], flight[2:2 + 2 * n]

    def wait_body(*refs):
        plan.wait_ici(refs[:n], refs[n:2 * n], (refs[2 * n], refs[2 * n + 1]))

    thru = pl.pallas_call(
        wait_body, name="gather_wait", out_shape=tuple(pltpu.HBM(t.shape, t.dtype) for t in thru),
        in_specs=[HBM_SPEC] * (2 * n) + [SEM_SPEC, SEM_SPEC, ANY], out_specs=[HBM_SPEC] * (2 * n),
        input_output_aliases={i: i for i in range(2 * n)},
        compiler_params=pltpu.CompilerParams(has_side_effects=SIDE_EFFECT),
    )(*thru, send, recv, after)

    def body(*refs):
        src, landed, dst = refs[:n], refs[n:2 * n], refs[2 * n:3 * n]
        plan.finish(src, dst, refs[3 * n:3 * n + 3], refs[3 * n + 3:], landed=landed)

    outs = pl.pallas_call(
        body, name="gather_finish", in_specs=[ANY] * (2 * n), out_specs=[ANY] * n, out_shape=plan.out_shape(),
        scratch_shapes=plan.scratch(), input_output_aliases={n + i: i for i in range(n)},
        compiler_params=pltpu.CompilerParams(vmem_limit_bytes=VMEM_LIMIT),
    )(*thru)
    return dict(zip(plan.names, outs))


def _scalar_grid_call(name, body, scalars, grid, in_specs, out_specs, out_shape, args):
    return pl.pallas_call(
        body, name=name, out_shape=out_shape,
        grid_spec=pltpu.PrefetchScalarGridSpec(num_scalar_prefetch=1, grid=grid, in_specs=in_specs, out_specs=out_specs),
        compiler_params=_params(*["arbitrary"] * len(grid)),
    )(scalars, *args)


def _pair_sum(name, g4, sib, where, tb=256):
    j, _, r, w = g4.shape
    tb = _row_tile(r, tb)

    def body(s_ref, g_ref, b_ref, o_ref):
        o_ref[...] = (g_ref[...].astype(F32) + b_ref[...].astype(F32)).astype(o_ref.dtype)

    blk = pl.BlockSpec((None, tb, w), lambda a, i, s: (a, i, 0))
    return _scalar_grid_call(
        name, body, where, (j, r // tb),
        [pl.BlockSpec((None, None, tb, w), lambda a, i, s: (a, s[0], i, 0)), blk], blk,
        jax.ShapeDtypeStruct((j, r, w), BF16), (g4, sib))


def _chip_sum(name, partial, got, where, row_sharded, tb=256):
    _, r, cdim = got.shape
    tb = _row_tile(r, tb)

    def body(s_ref, p_ref, g_ref, o_ref):
        acc = p_ref[...].astype(F32)
        for k in range(N_CHIPS - 1):
            acc = acc + g_ref[k].astype(F32)
        o_ref[...] = acc

    if row_sharded:
        own = pl.BlockSpec((None, tb, cdim), lambda i, s: (s[1], i, 0))
    else:
        own = pl.BlockSpec((None, tb, cdim), lambda i, s: (0, i, s[1]))
    return _scalar_grid_call(
        name, body, where, (r // tb,),
        [own, pl.BlockSpec((N_CHIPS - 1, tb, cdim), lambda i, s: (0, i, 0))],
        pl.BlockSpec((None, tb, cdim), lambda i, s: (s[0], i, 0)),
        jax.ShapeDtypeStruct((2, r, cdim), F32), (partial, got))


def _where():
    return jnp.stack([lax.axis_index("c"), 2 * lax.axis_index("x") + lax.axis_index("y")]).astype(jnp.int32)


def _grad_halves(name, g):
    rows, cols = g.shape
    if name in ROW_SHARDED:
        r = rows // N_CHIPS
        return g.reshape(N_CHIPS, 2, r // 2, cols), (r // 2, cols)
    return g.reshape(1, 2, rows // 2, cols), (rows // 2, cols // N_CHIPS)


def _pair_sums(tag, grads):
    names = list(grads)
    n = len(names)
    g4 = [_grad_halves(k, grads[k])[0] for k in names]
    plan = [[(j, r0, size) for j in range(g.shape[0]) for r0, size in _row_chunks(g.shape[2], g.shape[3] * 2)]
            for g in g4]
    out_shape = [jax.ShapeDtypeStruct((g.shape[0],) + g.shape[2:], BF16) for g in g4]

    def body(*refs):
        src, sib = refs[:n], refs[n:2 * n]
        cps = _Copies(*refs[2 * n:])
        x, y, c, _ = _position()
        waits = []
        for i in range(n):
            for j, r0, size in plan[i]:
                waits.append(cps.remote(cps.slot(), _rows(src[i], r0, size, lead=(j, 1 - c)),
                                        _rows(sib[i], r0, size, lead=(j,)), (x, y, 1 - c)))
        for cp in waits:
            cp.wait_recv()
        cps.finish()

    sibs = _comm_call("grads_pair_exchange_" + tag, body, g4, out_shape, sum(len(p) for p in plan))
    where = _where()
    return {k: _pair_sum(k + "_pair_sum", g, s, where) for k, g, s in zip(names, g4, sibs)}


class _ChipExchange:
    def __init__(self, partial, mult=16):
        self.names = list(partial)
        self.n = len(self.names)
        self.row_sharded = [k in ROW_SHARDED for k in self.names]
        self.half = []
        for k, rs in zip(self.names, self.row_sharded):
            _, r, w = partial[k].shape
            self.half.append((r, w) if rs else (r, w // N_CHIPS))
        self.chunks = [(i, r0, size) for i, (r, cdim) in enumerate(self.half)
                       for r0, size in _row_chunks(r, cdim * 2, mult)]

    def shard(self, src, i, chip, r0, size):
        if self.row_sharded[i]:
            return _rows(src[i], r0, size, lead=(chip,))
        return _rows(_shard_cols(src[i].at[0], False, self.half[i][1], chip), r0, size)

    def out_shape(self):
        return [jax.ShapeDtypeStruct((N_CHIPS - 1,) + s, BF16) for s in self.half]

    def scratch(self):
        n_remote = 3 * len(self.chunks)
        return [pltpu.SemaphoreType.DMA((n_remote,)), pltpu.SemaphoreType.DMA((n_remote,))]

    def _copies(self, src, dst, sems):
        x, y, c, chips = _position()
        for q, (i, r0, size) in enumerate(self.chunks):
            for k, (px, py) in enumerate(chips):
                yield _remote(self.shard(src, i, 2 * px + py, r0, size), _rows(dst[i], r0, size, lead=(k,)),
                              sems[0].at[3 * q + k], sems[1].at[3 * q + k], (px, py, c))

    def start(self, src, dst, sems):
        for cp in self._copies(src, dst, sems):
            cp.start()

    def finish(self, src, dst, sems):
        for cp in self._copies(src, dst, sems):
            cp.wait_recv()
        for cp in self._copies(src, dst, sems):
            cp.wait_send()


def _chip_exchange(partial):
    plan = _ChipExchange(partial)
    n = plan.n

    def body(*refs):
        parts = refs[:n], refs[n:2 * n], refs[2 * n:]
        plan.start(*parts)
        plan.finish(*parts)

    got = pl.pallas_call(
        body, name="grads_chip_exchange", in_specs=[ANY] * n, out_specs=[ANY] * n, out_shape=plan.out_shape(),
        scratch_shapes=plan.scratch(),
    )(*[partial[k] for k in plan.names])
    return dict(zip(plan.names, got))


HBM_SPEC = pl.BlockSpec(memory_space=pltpu.HBM)
SEM_SPEC = pl.BlockSpec(memory_space=pltpu.SEMAPHORE)
SIDE_EFFECT = pltpu.SideEffectType.DATAFLOW_SIDE_EFFECTING


def _chip_exchange_start(tag, partial):
    plan = _ChipExchange(partial)
    n = plan.n
    n_remote = 3 * len(plan.chunks)
    srcs = [pltpu.with_memory_space_constraint(partial[k], pltpu.HBM) for k in plan.names]
    lands = [pltpu.with_memory_space_constraint(lax.empty(s.shape, s.dtype), pltpu.HBM) for s in plan.out_shape()]

    def body(*refs):
        src, land = refs[:n], refs[n:2 * n]
        plan.start(src, land, (refs[2 * n], refs[2 * n + 1]))
        refs[-1][...] = jnp.zeros_like(refs[-1])

    return pl.pallas_call(
        body, name="grads_chip_exchange_start_" + tag,
        out_shape=(pltpu.SemaphoreType.DMA((n_remote,)), pltpu.SemaphoreType.DMA((n_remote,)),
                   *[pltpu.HBM(a.shape, a.dtype) for a in srcs + lands], jax.ShapeDtypeStruct((8, LANES), F32)),
        in_specs=[HBM_SPEC] * (2 * n),
        out_specs=(SEM_SPEC, SEM_SPEC, *[HBM_SPEC] * (2 * n), pl.BlockSpec(memory_space=pltpu.VMEM)),
        input_output_aliases={i: 2 + i for i in range(2 * n)},
        compiler_params=pltpu.CompilerParams(has_side_effects=SIDE_EFFECT),
    )(*srcs, *lands)


def _chip_exchange_wait(tag, partial, flight, after):
    plan = _ChipExchange(partial)
    n = plan.n
    send, recv, thru = flight[0], flight[1], flight[2:2 + 2 * n]

    def body(*refs):
        plan.finish(refs[:n], refs[n:2 * n], (refs[2 * n], refs[2 * n + 1]))

    outs = pl.pallas_call(
        body, name="grads_chip_exchange_wait_" + tag,
        out_shape=tuple(pltpu.HBM(t.shape, t.dtype) for t in thru),
        in_specs=[HBM_SPEC] * (2 * n) + [SEM_SPEC, SEM_SPEC, ANY], out_specs=[HBM_SPEC] * (2 * n),
        input_output_aliases={i: i for i in range(2 * n)},
        compiler_params=pltpu.CompilerParams(has_side_effects=SIDE_EFFECT),
    )(*thru, send, recv, after)
    return dict(zip(plan.names, outs[:n])), dict(zip(plan.names, outs[n:]))


def _finish_reduce(partial, got):
    names = list(partial)
    n = len(names)
    where = _where()
    halves = [_chip_sum(k + "_chip_sum", partial[k], got[k], where, k in ROW_SHARDED) for k in names]
    plan = [_row_chunks(h.shape[1], h.shape[2] * 4) for h in halves]
    out_shape = [jax.ShapeDtypeStruct(h.shape, F32) for h in halves]

    def body(*refs):
        src, dst = refs[:n], refs[n:2 * n]
        cps = _Copies(*refs[2 * n:])
        x, y, c, _ = _position()
        waits = []
        for i in range(n):
            for r0, size in plan[i]:
                waits.append(cps.remote(cps.slot(), _rows(src[i], r0, size, lead=(c,)), _rows(dst[i], r0, size, lead=(c,)),
                                        (x, y, 1 - c)))
        for cp in waits:
            cp.wait_recv()
        cps.finish()

    outs = _comm_call("grads_pair_share", body, halves, out_shape, sum(len(p) for p in plan),
                      aliases={i: i for i in range(n)})
    return {k: o.reshape(2 * o.shape[1], o.shape[2]) for k, o in zip(names, outs)}


def _all_reduce_small(name, v):
    rows, cols = v.shape
    h = rows // 2

    def body(v_ref, o_ref, sib, pair, buf, send, recv):
        x, y, c, chips = _position()
        me = 2 * x + y
        sibling = (x, y, 1 - c)
        cp = _remote(v_ref, sib, send.at[0], recv.at[0], sibling)
        cp.start()
        cp.wait()
        pair[...] = v_ref[...] + sib[...]
        mine = pl.ds(pl.multiple_of(c * h, 8), h)
        buf[me] = pair[mine, :]
        sends = [_remote(pair.at[mine], buf.at[me], send.at[1 + k], recv.at[1 + k], (px, py, c))
                 for k, (px, py) in enumerate(chips)]
        for cp in sends:
            cp.start()
        for k, (px, py) in enumerate(chips):
            _remote(pair.at[mine], buf.at[2 * px + py], send.at[1 + k], recv.at[1 + k], (px, py, c)).wait_recv()
        for cp in sends:
            cp.wait_send()
        o_ref[mine, :] = (buf[0] + buf[1]) + (buf[2] + buf[3])
        cp = _remote(o_ref.at[mine], o_ref.at[mine], send.at[4], recv.at[4], sibling)
        cp.start()
        cp.wait()

    vm = pl.BlockSpec(memory_space=pltpu.VMEM)
    return pl.pallas_call(
        body, name=name, in_specs=[vm], out_specs=vm, out_shape=jax.ShapeDtypeStruct(v.shape, F32),
        scratch_shapes=[pltpu.VMEM((rows, cols), F32), pltpu.VMEM((rows, cols), F32), pltpu.VMEM((N_CHIPS, h, cols), F32),
                        pltpu.SemaphoreType.DMA((5,)), pltpu.SemaphoreType.DMA((5,))],
        compiler_params=pltpu.CompilerParams(vmem_limit_bytes=VMEM_LIMIT),
    )(v)


def _adamw(name, w, g, m, v):
    def fn(wt, gt, mt, vt):
        mt = ADAM_B1 * mt + (1.0 - ADAM_B1) * gt
        vt = ADAM_B2 * vt + (1.0 - ADAM_B2) * (gt * gt)
        m_hat = mt / (1.0 - ADAM_B1 ** ADAM_STEP)
        v_hat = vt / (1.0 - ADAM_B2 ** ADAM_STEP)
        delta = -ADAM_LR * (m_hat / (jnp.sqrt(v_hat) + ADAM_EPS) + ADAM_WD * wt)
        return delta, mt, vt
    n = w.shape[1]
    return _rowwise(name, fn, [w, g, m, v], [], [(n, F32)] * 3, tb=256)


WEIGHTS = ("mem_norm", "a_norm_mix", "a_w_in", "a_w_group", "a_scale", "a_w_mem_kv", "a_w_out", "a_norm_ffn", "a_w_gu",
           "a_w_down", "kv_norm", "w_kv", "b_norm_mix", "b_w_q", "b_w_mem_kv", "b_w_out", "b_norm_ffn", "b_w_gu",
           "b_w_down", "final_norm")
REPLICATED_VECS = ("mem_norm", "kv_norm", "b_norm_mix", "b_norm_ffn", "final_norm")
SHARDED_VECS = ("a_norm_mix", "a_norm_ffn", "a_scale")
D_MODEL = 1024
GROUP_ROWS = 4 * POOL_GROUP * POOL_GROUP // D_MODEL


def _row(v):
    v = v.reshape(1, -1).astype(F32)
    return jnp.pad(v, ((0, 0), (0, D_MODEL - v.shape[1])))


def _pack_small(t):
    rows = [_row(t[k]) for k in REPLICATED_VECS]
    rows.append(_row(jnp.concatenate([t[k].reshape(-1) for k in SHARDED_VECS])))
    rows.append(jnp.zeros((2, D_MODEL), F32))
    rows.append(t["a_w_group"].reshape(GROUP_ROWS, D_MODEL).astype(F32))
    return jnp.concatenate(rows, axis=0)


def _unpack_small(p, like):
    out = {k: p[i, :].reshape(like[k].shape) for i, k in enumerate(REPLICATED_VECS)}
    off = 0
    for k in SHARDED_VECS:
        size = like[k].size
        out[k] = p[len(REPLICATED_VECS), off:off + size].reshape(like[k].shape)
        off += size
    out["a_w_group"] = p[len(REPLICATED_VECS) + 3:, :].reshape(like["a_w_group"].shape)
    return out


def kernel(x, mem, mem_norm, a_norm_mix, a_w_in, a_w_group, a_scale, a_w_mem_kv, a_w_out, a_norm_ffn, a_w_gu, a_w_down, kv_norm, w_kv, b_norm_mix, b_w_q, b_w_mem_kv, b_w_out, b_norm_ffn, b_w_gu, b_w_down, final_norm, loss_target, m_mem_norm, m_a_norm_mix, m_a_w_in, m_a_w_group, m_a_scale, m_a_w_mem_kv, m_a_w_out, m_a_norm_ffn, m_a_w_gu, m_a_w_down, m_kv_norm, m_w_kv, m_b_norm_mix, m_b_w_q, m_b_w_mem_kv, m_b_w_out, m_b_norm_ffn, m_b_w_gu, m_b_w_down, m_final_norm, v_mem_norm, v_a_norm_mix, v_a_w_in, v_a_w_group, v_a_scale, v_a_w_mem_kv, v_a_w_out, v_a_norm_ffn, v_a_w_gu, v_a_w_down, v_kv_norm, v_w_kv, v_b_norm_mix, v_b_w_q, v_b_w_mem_kv, v_b_w_out, v_b_norm_ffn, v_b_w_gu, v_b_w_down, v_final_norm):
    given = dict(locals())
    wl = {k: given[k] for k in WEIGHTS}
    ml = {k: given["m_" + k] for k in WEIGHTS}
    vl = {k: given["v_" + k] for k in WEIGHTS}
    chip = 2 * lax.axis_index("x") + lax.axis_index("y")

    def mat(a):
        return a.reshape(a.shape[-2], a.shape[-1])

    shards = {k: mat(wl[k]).astype(BF16) for k in BIG}
    second = _gather_start({k: shards[k] for k in SECOND})
    full = _gather_weights({k: shards[k] for k in BIG if k not in LATE + SECOND})
    gains = jnp.zeros((16, D_MODEL), F32)
    for i, k in enumerate(SHARDED_VECS):
        part = wl[k].reshape(1, -1)
        width = part.shape[1]
        gains = lax.dynamic_update_slice(gains, part, (i, chip * width))
    gains = _all_reduce_small("gains_all_gather", gains) * 0.5
    w = dict(full)
    for k in REPLICATED_VECS:
        w[k] = wl[k].reshape(1, D_MODEL)
    w["a_norm_mix"], w["a_norm_ffn"] = gains[0:1] + second[-1][0:1, 0:1], gains[1:2]
    w["a_scale"] = gains[2:3, :SB_WIDTH]
    w["a_w_group"] = wl["a_w_group"][0].astype(BF16)

    loss, grad_x, g, partial, got = _local_step(x[0], mem[0], loss_target[0], w, {k: shards[k] for k in LATE},
                                                ({k: shards[k].shape for k in SECOND}, second))

    rest = _pair_sums("late", {k: g[k] for k in BIG if k not in EARLY + POST_SB + MID})
    partial.update(rest)
    got.update(_chip_exchange(rest))
    red = _finish_reduce(partial, got)
    small = jnp.concatenate(
        [_row(g[k]) for k in REPLICATED_VECS] + [_row(g[k]) for k in SHARDED_VECS] + [_row(loss)]
        + [jnp.zeros((7, D_MODEL), F32), g["a_w_group"].reshape(GROUP_ROWS, D_MODEL)], axis=0)
    small = _all_reduce_small("small_grads_all_reduce", small)
    gs = {k: small[i] for i, k in enumerate(REPLICATED_VECS)}
    for i, k in enumerate(SHARDED_VECS):
        width = wl[k].shape[-1]
        gs[k] = lax.dynamic_slice(small[len(REPLICATED_VECS) + i], (chip * width,), (width,))
    gs["a_w_group"] = small[16:]
    total_loss = small[8, 0]

    out_g, out_d, out_m, out_v = {}, {}, {}, {}
    for k in BIG:
        shape = wl[k].shape
        out_g[k] = red[k].reshape(shape)
        d, nm, nv = _adamw(k + "_adamw", mat(wl[k]), red[k], mat(ml[k]), mat(vl[k]))
        out_d[k], out_m[k], out_v[k] = d.reshape(shape), nm.reshape(shape), nv.reshape(shape)
    small_names = REPLICATED_VECS + SHARDED_VECS + ("a_w_group",)
    d, nm, nv = _adamw("small_adamw", _pack_small(wl), _pack_small(gs), _pack_small(ml), _pack_small(vl))
    like = {k: wl[k] for k in small_names}
    for dst, p in ((out_d, d), (out_m, nm), (out_v, nv)):
        dst.update(_unpack_small(p, like))
    for k in small_names:
        out_g[k] = gs[k].reshape(wl[k].shape)

    return (total_loss, grad_x[None], *[out_g[k] for k in WEIGHTS], *[out_d[k] for k in WEIGHTS],
            *[out_m[k] for k in WEIGHTS], *[out_v[k] for k in WEIGHTS])
```
